```python
import math
import jax, jax.numpy as jnp
from jax import lax
import numpy as np

D_MODEL = 2048
BATCH = 8
SEQ = 2048
DEPTH = 1
DEC_BATCH = 16
DEC_SEQ = 32
PAST_LEN = 1024

CHUNK = 64
N_META = 16
D_RNN = 2560
N_RNN_BLOCKS = 16
RNN_BLOCK = D_RNN // N_RNN_BLOCKS
CONV_W = 4
LRU_C = 8.0
N_HEADS_M = 8
DV_M = D_MODEL // N_HEADS_M
DQK_M = DV_M // 2
HQK = N_HEADS_M * DQK_M
HV = N_HEADS_M * DV_M
N_GROUPS = 4
EXPERTS_PER_GROUP = 4
N_EXPERTS = N_GROUPS * EXPERTS_PER_GROUP
TOP_K_IN_GROUP = 2
D_EXPERT = 512
IN_SIZES = (D_RNN, D_RNN, HQK, HQK, HV, HV, 2 * N_HEADS_M, D_MODEL, D_MODEL)
N_IN = 2 * D_RNN + 2 * HQK + 2 * HV + 2 * N_HEADS_M + 2 * D_MODEL
EPS = 1e-6

kernel_name = "hawk_mlstm_hmoe_streaming_step"

F32 = jnp.float32


def rmsnorm(x, w):
    xf = x.astype(F32)
    xf = xf * lax.rsqrt(jnp.mean(xf * xf, axis=-1, keepdims=True) + EPS)
    return (xf * w.astype(F32)).astype(x.dtype)


def rglru_branch(xr, gr, conv_buf, h0, conv_w, conv_b, a_w, a_b, x_w, x_b, lam):
    B, L, _ = xr.shape
    xp = jnp.concatenate([conv_buf.astype(xr.dtype), xr], axis=1)
    xc = conv_b.astype(xr.dtype) + conv_w[0] * xp[:, 0:L]
    for j in range(1, CONV_W):
        xc = xc + conv_w[j] * xp[:, j:j + L]
    new_buf = xp[:, L:]
    xb = xc.reshape(B, L, N_RNN_BLOCKS, RNN_BLOCK)
    r = jax.nn.sigmoid((jnp.einsum('blnc,ncd->blnd', xb, a_w).reshape(B, L, D_RNN) + a_b).astype(F32))
    i = jax.nn.sigmoid((jnp.einsum('blnc,ncd->blnd', xb, x_w).reshape(B, L, D_RNN) + x_b).astype(F32))
    log_a = -LRU_C * r * jax.nn.softplus(-lam.astype(F32))
    a = jnp.exp(log_a)
    u = jnp.sqrt(-jnp.expm1(2.0 * log_a)) * (i * xc.astype(F32))
    u = u.at[:, 0].add(a[:, 0] * h0.astype(F32))

    def combine(left, right):
        a1, b1 = left
        a2, b2 = right
        return a1 * a2, a2 * b1 + b2

    _, h = lax.associative_scan(combine, (a, u), axis=1)
    y = jax.nn.gelu(gr.astype(F32)) * h
    return y.astype(xr.dtype), new_buf, h[:, -1]


def mlstm_block(q, k, v, li, lf, C0, n0, m0):
    L = q.shape[1]
    b = jnp.cumsum(lf.astype(F32), axis=1).transpose(0, 2, 1)
    ig = li.astype(F32).transpose(0, 2, 1)
    m0 = m0.astype(F32)
    C0 = C0.astype(F32)
    n0 = n0.astype(F32)
    qf, kf, vf = q.astype(F32), k.astype(F32), v.astype(F32)
    causal = jnp.tril(jnp.ones((L, L), dtype=bool))
    log_w = jnp.where(causal, b[..., :, None] - b[..., None, :] + ig[..., None, :], -jnp.inf)
    log_inter = b + m0[..., None]
    m = jnp.maximum(log_inter, jnp.max(log_w, axis=-1))
    w_intra = jnp.exp(log_w - m[..., None])
    w_inter = jnp.exp(log_inter - m)
    s = jnp.einsum('bthk,bshk->bhts', qf, kf) * w_intra
    num = jnp.einsum('bhts,bshv->bthv', s, vf) + jnp.einsum('bht,bthk,bhkv->bthv', w_inter, qf, C0)
    den = jnp.sum(s, axis=-1) + w_inter * jnp.einsum('bthk,bhk->bht', qf, n0)
    denom = jnp.maximum(jnp.abs(den), jnp.exp(-m))
    h = num / denom.transpose(0, 2, 1)[..., None]
    m_end = m[..., -1]
    w_end = jnp.exp(b[..., -1:] - b + ig - m_end[..., None])
    decay = jnp.exp(b[..., -1] + m0 - m_end)
    C_new = decay[..., None, None] * C0 + jnp.einsum('bhs,bshk,bshv->bhkv', w_end, kf, vf)
    n_new = decay[..., None] * n0 + jnp.einsum('bhs,bshk->bhk', w_end, kf)
    return h, (C_new, n_new, m_end)


def mlstm_sequence(q, k, v, li, lf, C0, n0, m0, n_lead):
    h_lead, st = mlstm_block(q[:, :n_lead], k[:, :n_lead], v[:, :n_lead],
                             li[:, :n_lead], lf[:, :n_lead], C0, n0, m0)
    rest = q.shape[1] - n_lead
    if rest == 0:
        return h_lead, st
    n_blk = rest // CHUNK

    def to_blocks(a):
        a = a[:, n_lead:]
        a = a.reshape(a.shape[0], n_blk, CHUNK, *a.shape[2:])
        return jnp.moveaxis(a, 1, 0)

    def step(carry, blk):
        h, carry = mlstm_block(*blk, *carry)
        return carry, h

    blocks = (to_blocks(q), to_blocks(k), to_blocks(v), to_blocks(li), to_blocks(lf))
    st, hs = lax.scan(step, st, blocks)
    hs = jnp.moveaxis(hs, 0, 1).reshape(q.shape[0], rest, *hs.shape[3:])
    return jnp.concatenate([h_lead, hs], axis=1), st


def hier_moe(x, w_rg, b_rg, w_re, b_re, w_g, w_u, w_d):
    B, L, D = x.shape
    xf = x.reshape(B * L, D)
    g_logits = (xf @ w_rg + b_rg).astype(F32)
    g_prob = jax.nn.softmax(g_logits, axis=-1)
    g_idx = jnp.argmax(g_logits, axis=-1)
    g_w = jnp.max(g_prob, axis=-1, keepdims=True)
    e_logits = (xf @ w_re + b_re).astype(F32).reshape(-1, N_GROUPS, EXPERTS_PER_GROUP)
    e_in = jnp.einsum('tge,tg->te', e_logits, jax.nn.one_hot(g_idx, N_GROUPS, dtype=F32))
    top_v, top_i = lax.top_k(e_in, TOP_K_IN_GROUP)
    top_w = jax.nn.softmax(top_v, axis=-1) * g_w
    eid = g_idx[:, None] * EXPERTS_PER_GROUP + top_i
    gates = jnp.sum(jax.nn.one_hot(eid, N_EXPERTS, dtype=F32) * top_w[..., None], axis=1)
    out = jnp.zeros((B * L, D), F32)
    for e in range(N_EXPERTS):
        hdn = jax.nn.silu(xf @ w_g[e]) * (xf @ w_u[e])
        out = out + gates[:, e:e + 1] * (hdn @ w_d[e]).astype(F32)
    return out.astype(x.dtype).reshape(B, L, D)


def trunk_layer(x, conv_buf, h0, C0, n0, m0, n_lead, p):
    B, L, _ = x.shape
    xn = rmsnorm(x, p['norm1_w'])
    proj = xn @ p['w_in']
    cuts = []
    acc = 0
    for s in IN_SIZES[:-1]:
        acc += s
        cuts.append(acc)
    xr, gr, q, k, v, og, gif, mg_r, mg_m = jnp.split(proj, cuts, axis=-1)
    y_r, conv_new, h_new = rglru_branch(xr, gr, conv_buf, h0, p['conv_w'], p['conv_b'],
                                        p['rg_a_w'], p['rg_a_b'], p['rg_x_w'], p['rg_x_b'], p['rg_lambda'])
    q = q.reshape(B, L, N_HEADS_M, DQK_M)
    k = k.reshape(B, L, N_HEADS_M, DQK_M) * (DQK_M ** -0.5)
    v = v.reshape(B, L, N_HEADS_M, DV_M)
    gif = gif.astype(F32) + p['b_gates'].astype(F32)
    li = gif[..., :N_HEADS_M]
    lf = jax.nn.log_sigmoid(gif[..., N_HEADS_M:])
    hm, (C_new, n_new, m_new) = mlstm_sequence(q, k, v, li, lf, C0, n0, m0, n_lead)
    hm = hm * lax.rsqrt(jnp.mean(hm * hm, axis=-1, keepdims=True) + EPS)
    hm = hm * p['mlstm_norm_w'].astype(F32).reshape(N_HEADS_M, DV_M)
    y_m = (hm.reshape(B, L, HV) * jax.nn.sigmoid(og.astype(F32))).astype(x.dtype)
    mix = jax.nn.sigmoid(mg_r) * (y_r @ p['w_proj_rnn']) + jax.nn.sigmoid(mg_m) * (y_m @ p['w_proj_mlstm'])
    x = x + mix @ p['w_out']
    x = x + hier_moe(rmsnorm(x, p['norm2_w']), p['w_router_group'], p['b_router_group'],
                     p['w_router_expert'], p['b_router_expert'],
                     p['w_exp_gate'], p['w_exp_up'], p['w_exp_down'])
    dt = x.dtype
    return x, (conv_new.astype(dt), h_new.astype(dt), C_new.astype(dt), n_new.astype(dt), m_new.astype(dt))


def setup_inputs(seed: int = 0) -> dict:
    key = jax.random.key(seed)
    ks = jax.random.split(key, 40)

    def nrm(k, shape, s):
        return s * jax.random.normal(k, shape, F32)

    u = jax.random.uniform(ks[20], (DEPTH, D_RNN), F32, 0.9, 0.999)
    a0 = u ** (1.0 / LRU_C)
    rg_lambda = jnp.log(a0) - jnp.log1p(-a0)
    b_i = nrm(ks[21], (DEPTH, N_HEADS_M), 0.1)
    b_f = jnp.linspace(3.0, 6.0, N_HEADS_M, dtype=F32)[None] + nrm(ks[22], (DEPTH, N_HEADS_M), 0.1)
    return {
        "x_prompt": nrm(ks[0], (BATCH, SEQ, D_MODEL), 1.0),
        "x_sample": nrm(ks[1], (DEC_BATCH, DEC_SEQ, D_MODEL), 1.0),
        "state_rglru_conv": nrm(ks[2], (DEPTH, DEC_BATCH, CONV_W - 1, D_RNN), 1.0),
        "state_rglru_h": nrm(ks[3], (DEPTH, DEC_BATCH, D_RNN), 0.5),
        "state_mlstm_C": nrm(ks[4], (DEPTH, DEC_BATCH, N_HEADS_M, DQK_M, DV_M), 0.05),
        "state_mlstm_n": nrm(ks[5], (DEPTH, DEC_BATCH, N_HEADS_M, DQK_M), 0.05),
        "state_mlstm_m": nrm(ks[6], (DEPTH, DEC_BATCH, N_HEADS_M), 0.5),
        "meta_tokens": nrm(ks[7], (N_META, D_MODEL), 1.0),
        "norm1_w": 1.0 + nrm(ks[8], (DEPTH, D_MODEL), 0.02),
        "w_in": nrm(ks[9], (DEPTH, D_MODEL, N_IN), D_MODEL ** -0.5),
        "b_gates": jnp.concatenate([b_i, b_f], axis=-1),
        "conv_w": nrm(ks[10], (DEPTH, CONV_W, D_RNN), 0.5),
        "conv_b": nrm(ks[11], (DEPTH, D_RNN), 0.01),
        "rg_a_w": nrm(ks[12], (DEPTH, N_RNN_BLOCKS, RNN_BLOCK, RNN_BLOCK), RNN_BLOCK ** -0.5),
        "rg_a_b": nrm(ks[13], (DEPTH, D_RNN), 0.01),
        "rg_x_w": nrm(ks[14], (DEPTH, N_RNN_BLOCKS, RNN_BLOCK, RNN_BLOCK), RNN_BLOCK ** -0.5),
        "rg_x_b": nrm(ks[15], (DEPTH, D_RNN), 0.01),
        "rg_lambda": rg_lambda,
        "mlstm_norm_w": 1.0 + nrm(ks[16], (DEPTH, HV), 0.02),
        "w_proj_rnn": nrm(ks[17], (DEPTH, D_RNN, D_MODEL), D_RNN ** -0.5),
        "w_proj_mlstm": nrm(ks[18], (DEPTH, HV, D_MODEL), HV ** -0.5),
        "w_out": nrm(ks[19], (DEPTH, D_MODEL, D_MODEL), D_MODEL ** -0.5),
        "norm2_w": 1.0 + nrm(ks[23], (DEPTH, D_MODEL), 0.02),
        "w_router_group": nrm(ks[24], (DEPTH, D_MODEL, N_GROUPS), D_MODEL ** -0.5),
        "b_router_group": nrm(ks[25], (DEPTH, N_GROUPS), 0.01),
        "w_router_expert": nrm(ks[26], (DEPTH, D_MODEL, N_EXPERTS), D_MODEL ** -0.5),
        "b_router_expert": nrm(ks[27], (DEPTH, N_EXPERTS), 0.01),
        "w_exp_gate": nrm(ks[28], (DEPTH, N_EXPERTS, D_MODEL, D_EXPERT), D_MODEL ** -0.5),
        "w_exp_up": nrm(ks[29], (DEPTH, N_EXPERTS, D_MODEL, D_EXPERT), D_MODEL ** -0.5),
        "w_exp_down": nrm(ks[30], (DEPTH, N_EXPERTS, D_EXPERT, D_MODEL), D_EXPERT ** -0.5),
        "norm_f_w": 1.0 + nrm(ks[31], (D_MODEL,), 0.02),
    }


def reference(x_prompt, x_sample, state_rglru_conv, state_rglru_h, state_mlstm_C, state_mlstm_n,
              state_mlstm_m, meta_tokens, norm1_w, w_in, b_gates, conv_w, conv_b, rg_a_w, rg_a_b,
              rg_x_w, rg_x_b, rg_lambda, mlstm_norm_w, w_proj_rnn, w_proj_mlstm, w_out, norm2_w,
              w_router_group, b_router_group, w_router_expert, b_router_expert, w_exp_gate,
              w_exp_up, w_exp_down, norm_f_w):
    B = x_prompt.shape[0]
    dt = x_prompt.dtype
    xp = jnp.concatenate([jnp.broadcast_to(meta_tokens.astype(dt)[None], (B, N_META, D_MODEL)), x_prompt], axis=1)
    xs = x_sample
    n_lead_sample = x_sample.shape[1]
    p_conv, p_h, p_C, p_n, p_m = [], [], [], [], []
    s_conv, s_h, s_C, s_n, s_m = [], [], [], [], []
    for l in range(DEPTH):
        p = dict(norm1_w=norm1_w[l], w_in=w_in[l], b_gates=b_gates[l], conv_w=conv_w[l], conv_b=conv_b[l],
                 rg_a_w=rg_a_w[l], rg_a_b=rg_a_b[l], rg_x_w=rg_x_w[l], rg_x_b=rg_x_b[l],
                 rg_lambda=rg_lambda[l], mlstm_norm_w=mlstm_norm_w[l], w_proj_rnn=w_proj_rnn[l],
                 w_proj_mlstm=w_proj_mlstm[l], w_out=w_out[l], norm2_w=norm2_w[l],
                 w_router_group=w_router_group[l], b_router_group=b_router_group[l],
                 w_router_expert=w_router_expert[l], b_router_expert=b_router_expert[l],
                 w_exp_gate=w_exp_gate[l], w_exp_up=w_exp_up[l], w_exp_down=w_exp_down[l])
        xp, st_p = trunk_layer(xp,
                               jnp.zeros((B, CONV_W - 1, D_RNN), dt),
                               jnp.zeros((B, D_RNN), F32),
                               jnp.zeros((B, N_HEADS_M, DQK_M, DV_M), F32),
                               jnp.zeros((B, N_HEADS_M, DQK_M), F32),
                               jnp.zeros((B, N_HEADS_M), F32),
                               N_META, p)
        xs, st_s = trunk_layer(xs, state_rglru_conv[l], state_rglru_h[l], state_mlstm_C[l],
                               state_mlstm_n[l], state_mlstm_m[l], n_lead_sample, p)
        p_conv.append(st_p[0]); p_h.append(st_p[1]); p_C.append(st_p[2]); p_n.append(st_p[3]); p_m.append(st_p[4])
        s_conv.append(st_s[0]); s_h.append(st_s[1]); s_C.append(st_s[2]); s_n.append(st_s[3]); s_m.append(st_s[4])
    y_prompt = rmsnorm(xp, norm_f_w)[:, N_META:]
    y_sample = rmsnorm(xs, norm_f_w)
    return (y_prompt, y_sample,
            jnp.stack(p_conv), jnp.stack(p_h), jnp.stack(p_C), jnp.stack(p_n), jnp.stack(p_m),
            jnp.stack(s_conv), jnp.stack(s_h), jnp.stack(s_C), jnp.stack(s_n), jnp.stack(s_m))
```

```python
import functools

import jax
import jax.numpy as jnp
from jax import lax
from jax.experimental import pallas as pl
from jax.experimental.pallas import tpu as pltpu

F32 = jnp.float32
BF16 = jnp.bfloat16

D_MODEL = 2048
CHUNK = 64
N_META = 16
N_NULL = CHUNK - N_META
D_RNN = 2560
N_RNN_BLOCKS = 16
RNN_BLOCK = D_RNN // N_RNN_BLOCKS
RNN_SUPER = 640
N_SUPER = D_RNN // RNN_SUPER
CONV_W = 4
LRU_C = 8.0
N_HEADS_M = 8
DV_M = D_MODEL // N_HEADS_M
DQK_M = DV_M // 2
HQK = N_HEADS_M * DQK_M
HV = N_HEADS_M * DV_M
N_GROUPS = 4
EXPERTS_PER_GROUP = 4
N_EXPERTS = 16
D_EXPERT = 512
EPS = 1e-6
LANES = 128
GIF_OFF = 2 * D_RNN + 2 * HQK + 2 * HV
MG_OFF = GIF_OFF + 2 * N_HEADS_M
VMEM_LIMIT = 56 * 1024 * 1024
NEG_BIG = -1e30


def _cparams(sem):
    return pltpu.CompilerParams(dimension_semantics=sem, vmem_limit_bytes=VMEM_LIMIT)


def _rms(x, w):
    return x * lax.rsqrt(jnp.mean(x * x, axis=-1, keepdims=True) + EPS) * w


def _softplus(x):
    return jnp.maximum(x, 0.0) + jnp.log1p(jnp.exp(-jnp.abs(x)))


def _prep_kernel(x_ref, meta_ref, nw_ref, wg_ref, bg_ref, xn_ref, xt_ref, gif_ref, *, nb, has_head):
    ti = pl.program_id(0)
    nw = nw_ref[...]

    def emit(b, xb):
        xnb = _rms(xb, nw).astype(BF16)
        xn_ref[:, b * D_MODEL:(b + 1) * D_MODEL] = xnb
        xt_ref[:, b * D_MODEL:(b + 1) * D_MODEL] = xb
        gif_ref[:, b * LANES:(b + 1) * LANES] = (
            jnp.dot(xnb, wg_ref[...], preferred_element_type=F32) + bg_ref[...])

    if has_head:
        @pl.when(ti == 0)
        def _():
            head = jnp.concatenate([jnp.zeros((N_NULL, D_MODEL), F32), meta_ref[...]], axis=0)
            for b in range(nb):
                emit(b, head)

        @pl.when(ti > 0)
        def _():
            for b in range(nb):
                emit(b, x_ref[b])
    else:
        for b in range(nb):
            emit(b, x_ref[b])


def _prep(x, meta, norm_w, w_gif, b_gif, *, has_head):
    nb, seq, _ = x.shape
    tt = CHUNK if has_head else seq
    n_in_tiles = seq // tt
    n_tiles = n_in_tiles + (1 if has_head else 0)
    tl = n_tiles * tt
    if has_head:
        x_map = lambda ti: (0, jnp.maximum(ti - 1, 0), 0)
    else:
        x_map = lambda ti: (0, ti, 0)
    const = lambda ti: (0, 0)
    return pl.pallas_call(
        functools.partial(_prep_kernel, nb=nb, has_head=has_head),
        grid=(n_tiles,),
        in_specs=[pl.BlockSpec((nb, tt, D_MODEL), x_map),
                  pl.BlockSpec((N_META, D_MODEL), const),
                  pl.BlockSpec((1, D_MODEL), const),
                  pl.BlockSpec((D_MODEL, LANES), const),
                  pl.BlockSpec((1, LANES), const)],
        out_specs=[pl.BlockSpec((tt, nb * D_MODEL), lambda ti: (ti, 0)),
                   pl.BlockSpec((tt, nb * D_MODEL), lambda ti: (ti, 0)),
                   pl.BlockSpec((tt, nb * LANES), lambda ti: (ti, 0))],
        out_shape=[jax.ShapeDtypeStruct((tl, nb * D_MODEL), BF16),
                   jax.ShapeDtypeStruct((tl, nb * D_MODEL), F32),
                   jax.ShapeDtypeStruct((tl, nb * LANES), F32)],
        compiler_params=_cparams(("arbitrary",)),
        name="prep",
    )(x, meta, norm_w, w_gif, b_gif)


def _mm_kernel(*refs, has_gate, has_add):
    lhs_ref, w_ref = refs[0], refs[1]
    pos = 2
    gate_ref = add_ref = None
    if has_gate:
        gate_ref = refs[pos]
        pos += 1
    if has_add:
        add_ref = refs[pos]
        pos += 1
    out_ref, wbf_ref = refs[pos], refs[pos + 1]

    @pl.when(pl.program_id(1) == 0)
    def _():
        wbf_ref[...] = w_ref[...].astype(BF16)

    acc = jnp.dot(lhs_ref[...], wbf_ref[...], preferred_element_type=F32)
    if has_gate:
        acc = jax.nn.sigmoid(gate_ref[...].astype(F32)) * acc
    if has_add:
        acc = add_ref[...].astype(F32) + acc
    out_ref[...] = acc.astype(out_ref.dtype)


def _mm(lhs, w, *, col0, ncols, tn, tm, out_dtype, gate=None, gate_col0=0, add=None, name):
    rows, k = lhs.shape
    assert rows % tm == 0 and ncols % tn == 0 and col0 % tn == 0 and gate_col0 % tn == 0
    in_specs = [pl.BlockSpec((tm, k), lambda j, i: (i, 0)),
                pl.BlockSpec((k, tn), lambda j, i: (0, col0 // tn + j))]
    args = [lhs, w]
    if gate is not None:
        in_specs.append(pl.BlockSpec((tm, tn), lambda j, i: (i, gate_col0 // tn + j)))
        args.append(gate)
    if add is not None:
        in_specs.append(pl.BlockSpec((tm, tn), lambda j, i: (i, j)))
        args.append(add)
    return pl.pallas_call(
        functools.partial(_mm_kernel, has_gate=gate is not None, has_add=add is not None),
        grid=(ncols // tn, rows // tm),
        in_specs=in_specs,
        out_specs=pl.BlockSpec((tm, tn), lambda j, i: (i, j)),
        out_shape=jax.ShapeDtypeStruct((rows, ncols), out_dtype),
        scratch_shapes=[pltpu.VMEM((k, tn), BF16)],
        compiler_params=_cparams(("arbitrary", "arbitrary")),
        name=name,
    )(*args)


def _rglru_kernel(xr_ref, gr_ref, tail0_ref, h0_ref, cw_ref, cb_ref, ab_ref, xb_ref, lam_ref,
                  wa_ref, wx_ref, y_ref, hl_ref, tail_ref, xp_s, h_s, a_s, u_s, hs_s, *, nb, tt, n_null):
    ti = pl.program_id(1)
    rt = tt * nb
    tb = (CONV_W - 1) * nb

    @pl.when(ti == 0)
    def _():
        xp_s[0:tb, :] = tail0_ref[...]
        h_s[...] = h0_ref[...]

    xp_s[tb:tb + rt, :] = xr_ref[...]
    cw = cw_ref[...]
    xc = cb_ref[...] + cw[0:1, :] * xp_s[0:rt, :]
    for j in range(1, CONV_W):
        xc = xc + cw[j:j + 1, :] * xp_s[j * nb:j * nb + rt, :]
    xcb = xc.astype(BF16)
    r = jax.nn.sigmoid(jnp.dot(xcb, wa_ref[0], preferred_element_type=F32) + ab_ref[...])
    i = jax.nn.sigmoid(jnp.dot(xcb, wx_ref[0], preferred_element_type=F32) + xb_ref[...])
    log_a = (-LRU_C) * r * _softplus(-lam_ref[...])
    a = jnp.exp(log_a)
    a_s[...] = a
    u = jnp.sqrt(-jnp.tanh(log_a) * (a * a + 1.0)) * (i * xc)
    if n_null:
        row = lax.broadcasted_iota(jnp.int32, (rt, 1), 0)
        u = jnp.where((row >= n_null * nb) | (ti > 0), u, 0.0)
    u_s[...] = u

    def step(t, h):
        o = pl.multiple_of(t * nb, nb)
        h = a_s[pl.ds(o, nb), :] * h + u_s[pl.ds(o, nb), :]
        hs_s[pl.ds(o, nb), :] = h
        return h

    h = lax.fori_loop(0, tt, step, h_s[...], unroll=8)
    h_s[...] = h
    g = gr_ref[...].astype(F32)
    gelu = g * (0.5 * (1.0 + jnp.tanh(0.7978845608028654 * (g + 0.044715 * (g * g * g)))))
    y_ref[...] = (gelu * hs_s[...]).astype(BF16)
    xp_s[0:tb, :] = xp_s[rt:rt + tb, :]

    @pl.when(ti == pl.num_programs(1) - 1)
    def _():
        hl_ref[...] = h
        tail_ref[...] = xp_s[rt:rt + tb, :]


def _rglru(xr, gr, tail0, h0, cw, cb, ab, xb, lam, wa, wx, *, nb, tt, n_null):
    rows = xr.shape[0]
    rt = tt * nb
    tb = (CONV_W - 1) * nb
    blk = lambda s, ti: (ti, s)
    col = lambda s, ti: (0, s)
    return pl.pallas_call(
        functools.partial(_rglru_kernel, nb=nb, tt=tt, n_null=n_null),
        grid=(N_SUPER, rows // rt),
        in_specs=[pl.BlockSpec((rt, RNN_SUPER), blk),
                  pl.BlockSpec((rt, RNN_SUPER), blk),
                  pl.BlockSpec((tb, RNN_SUPER), col),
                  pl.BlockSpec((nb, RNN_SUPER), col),
                  pl.BlockSpec((CONV_W, RNN_SUPER), col),
                  pl.BlockSpec((1, RNN_SUPER), col),
                  pl.BlockSpec((1, RNN_SUPER), col),
                  pl.BlockSpec((1, RNN_SUPER), col),
                  pl.BlockSpec((1, RNN_SUPER), col),
                  pl.BlockSpec((1, RNN_SUPER, RNN_SUPER), lambda s, ti: (s, 0, 0)),
                  pl.BlockSpec((1, RNN_SUPER, RNN_SUPER), lambda s, ti: (s, 0, 0))],
        out_specs=[pl.BlockSpec((rt, RNN_SUPER), blk),
                   pl.BlockSpec((nb, RNN_SUPER), col),
                   pl.BlockSpec((tb, RNN_SUPER), col)],
        out_shape=[jax.ShapeDtypeStruct((rows, D_RNN), BF16),
                   jax.ShapeDtypeStruct((nb, D_RNN), F32),
                   jax.ShapeDtypeStruct((tb, D_RNN), F32)],
        scratch_shapes=[pltpu.VMEM((rt + tb, RNN_SUPER), F32),
                        pltpu.VMEM((nb, RNN_SUPER), F32),
                        pltpu.VMEM((rt, RNN_SUPER), F32),
                        pltpu.VMEM((rt, RNN_SUPER), F32),
                        pltpu.VMEM((rt, RNN_SUPER), F32)],
        compiler_params=_cparams(("arbitrary", "arbitrary")),
        name="rglru",
    )(xr, gr, tail0, h0, cw, cb, ab, xb, lam, wa, wx)


def _mlstm_kernel(q_ref, k_ref, v_ref, og_ref, gif_ref, c0_ref, n0_ref, m0_ref, nw_ref,
                  y_ref, co_ref, no_ref, mo_ref, c_s, n_s, m_s, *, L, n_null):
    c = pl.program_id(1)

    @pl.when(c == 0)
    def _():
        c_s[...] = c0_ref[0]
        n_s[...] = n0_ref[0]
        m_s[...] = m0_ref[0]

    g = gif_ref[...]
    lane = lax.broadcasted_iota(jnp.int32, (L, LANES), 1)
    row = lax.broadcasted_iota(jnp.int32, (L, LANES), 0)
    is_f = (lane >= N_HEADS_M) & (lane < 2 * N_HEADS_M)
    lf = jnp.where(is_f, jnp.minimum(g, 0.0) - jnp.log1p(jnp.exp(-jnp.abs(g))), 0.0)
    li = g
    if n_null:
        null = (row < n_null) & (c == 0)
        lf = jnp.where(null, 0.0, lf)
        li = jnp.where(null, NEG_BIG, li)
    rr = lax.broadcasted_iota(jnp.int32, (L, L), 0)
    cc = lax.broadcasted_iota(jnp.int32, (L, L), 1)
    causal = rr >= cc
    tri = causal.astype(F32)
    bfull = jnp.dot(tri, lf, preferred_element_type=F32, precision=lax.Precision.HIGHEST)
    b_t = bfull.T
    li_t = li.T
    m_all = m_s[...]
    m_new = m_all
    lane1 = lax.broadcasted_iota(jnp.int32, (1, LANES), 1)
    scale = DQK_M ** -0.5
    nw = nw_ref[...]
    dn_nt = (((1,), (1,)), ((), ()))
    dn_tn = (((0,), (0,)), ((), ()))
    for h in range(N_HEADS_M):
        bcol = bfull[:, N_HEADS_M + h:N_HEADS_M + h + 1]
        brow = b_t[N_HEADS_M + h:N_HEADS_M + h + 1, :]
        igrow = li_t[h:h + 1, :]
        igcol = li[:, h:h + 1]
        m0h = m_all[:, h:h + 1]
        log_w = jnp.where(causal, bcol - brow + igrow, -jnp.inf)
        log_inter = bcol + m0h
        mvec = jnp.maximum(log_inter, jnp.max(log_w, axis=1, keepdims=True))
        w_intra = jnp.exp(log_w - mvec)
        w_inter = jnp.exp(log_inter - mvec)
        qh = q_ref[:, h * DQK_M:(h + 1) * DQK_M]
        kh = k_ref[:, h * DQK_M:(h + 1) * DQK_M]
        vh = v_ref[:, h * DV_M:(h + 1) * DV_M]
        s = lax.dot_general(qh, kh, dn_nt, preferred_element_type=F32) * (scale * w_intra)
        ch = c_s[h]
        nh = n_s[h:h + 1, :]
        num = (jnp.dot(s.astype(BF16), vh, preferred_element_type=F32)
               + w_inter * jnp.dot(qh, ch.astype(BF16), preferred_element_type=F32))
        den = (jnp.sum(s, axis=1, keepdims=True)
               + w_inter * jnp.sum(qh.astype(F32) * nh, axis=1, keepdims=True))
        denom = jnp.maximum(jnp.abs(den), jnp.exp(-mvec))
        hh = num / denom
        hh = hh * lax.rsqrt(jnp.mean(hh * hh, axis=-1, keepdims=True) + EPS)
        hh = hh * nw[:, h * DV_M:(h + 1) * DV_M]
        og = og_ref[:, h * DV_M:(h + 1) * DV_M].astype(F32)
        y_ref[:, h * DV_M:(h + 1) * DV_M] = (hh * jax.nn.sigmoid(og)).astype(BF16)
        m_end = mvec[L - 1:L, :]
        b_end = bcol[L - 1:L, :]
        w_end = jnp.exp(b_end - bcol + igcol - m_end)
        decay = jnp.exp(b_end + m0h - m_end)
        kw = kh.astype(F32) * (w_end * scale)
        c_s[h] = decay * ch + lax.dot_general(kw.astype(BF16), vh, dn_tn, preferred_element_type=F32)
        n_s[h:h + 1, :] = decay * nh + jnp.sum(kw, axis=0, keepdims=True)
        m_new = jnp.where(lane1 == h, m_end, m_new)
    m_s[...] = m_new

    @pl.when(c == pl.num_programs(1) - 1)
    def _():
        co_ref[0] = c_s[...]
        no_ref[0] = n_s[...]
        mo_ref[0] = m_new


def _mlstm(qkvo, gif, c0, n0, m0, nw, *, nb, tl, L, n_null):
    nq = (2 * HQK + 2 * HV) // HQK
    nv = (2 * HQK + 2 * HV) // HV
    st = lambda b, c: (b, 0, 0)
    return pl.pallas_call(
        functools.partial(_mlstm_kernel, L=L, n_null=n_null),
        grid=(nb, tl // L),
        in_specs=[pl.BlockSpec((L, HQK), lambda b, c: (c, b * nq)),
                  pl.BlockSpec((L, HQK), lambda b, c: (c, b * nq + 1)),
                  pl.BlockSpec((L, HV), lambda b, c: (c, b * nv + 1)),
                  pl.BlockSpec((L, HV), lambda b, c: (c, b * nv + 2)),
                  pl.BlockSpec((L, LANES), lambda b, c: (c, b)),
                  pl.BlockSpec((1, N_HEADS_M, DQK_M, DV_M), lambda b, c: (b, 0, 0, 0)),
                  pl.BlockSpec((1, N_HEADS_M, DQK_M), st),
                  pl.BlockSpec((1, 1, LANES), st),
                  pl.BlockSpec((1, HV), lambda b, c: (0, 0))],
        out_specs=[pl.BlockSpec((L, HV), lambda b, c: (c, b)),
                   pl.BlockSpec((1, N_HEADS_M, DQK_M, DV_M), lambda b, c: (b, 0, 0, 0)),
                   pl.BlockSpec((1, N_HEADS_M, DQK_M), st),
                   pl.BlockSpec((1, 1, LANES), st)],
        out_shape=[jax.ShapeDtypeStruct((tl, nb * HV), BF16),
                   jax.ShapeDtypeStruct((nb, N_HEADS_M, DQK_M, DV_M), F32),
                   jax.ShapeDtypeStruct((nb, N_HEADS_M, DQK_M), F32),
                   jax.ShapeDtypeStruct((nb, 1, LANES), F32)],
        scratch_shapes=[pltpu.VMEM((N_HEADS_M, DQK_M, DV_M), F32),
                        pltpu.VMEM((N_HEADS_M, DQK_M), F32),
                        pltpu.VMEM((1, LANES), F32)],
        compiler_params=_cparams(("arbitrary", "arbitrary")),
        name="mlstm",
    )(qkvo, qkvo, qkvo, qkvo, gif, c0, n0, m0, nw)


def _route(logits):
    tm = logits.shape[0]
    lane = lax.broadcasted_iota(jnp.int32, (tm, LANES), 1)
    lanef = lane.astype(F32)
    gl = jnp.where(lane < N_GROUPS, logits, -jnp.inf)
    gmax = jnp.max(gl, axis=1, keepdims=True)
    gidx = jnp.min(jnp.where(gl == gmax, lanef, float(LANES)), axis=1, keepdims=True)
    gw = 1.0 / jnp.sum(jnp.exp(gl - gmax), axis=1, keepdims=True)
    lo = float(N_GROUPS) + float(EXPERTS_PER_GROUP) * gidx
    sel = (lanef >= lo) & (lanef < lo + float(EXPERTS_PER_GROUP))
    ev = jnp.where(sel, logits, -jnp.inf)
    v1 = jnp.max(ev, axis=1, keepdims=True)
    i1 = jnp.min(jnp.where(ev == v1, lanef, float(LANES)), axis=1, keepdims=True)
    ev2 = jnp.where(lanef == i1, -jnp.inf, ev)
    v2 = jnp.max(ev2, axis=1, keepdims=True)
    i2 = jnp.min(jnp.where(ev2 == v2, lanef, float(LANES)), axis=1, keepdims=True)
    t = jnp.exp(v2 - v1)
    w1 = 1.0 / (1.0 + t)
    w2 = t / (1.0 + t)
    return jnp.where(lanef == i1, w1 * gw, jnp.where(lanef == i2, w2 * gw, 0.0))


def _outproj_kernel(mix_ref, w_ref, x_ref, n2_ref, wr_ref, br_ref, x1_ref, xn2_ref, gates_ref):
    x1 = x_ref[...] + jnp.dot(mix_ref[...], w_ref[...], preferred_element_type=F32)
    x1_ref[...] = x1
    xn2 = _rms(x1, n2_ref[...]).astype(BF16)
    xn2_ref[...] = xn2
    logits = jnp.dot(xn2, wr_ref[...], preferred_element_type=F32) + br_ref[...]
    gates_ref[...] = _route(logits)


def _outproj(mix, w_out, x, n2, w_router, b_router, *, tm):
    rows = mix.shape[0]
    const = lambda i: (0, 0)
    rowblk = lambda i: (i, 0)
    return pl.pallas_call(
        _outproj_kernel,
        grid=(rows // tm,),
        in_specs=[pl.BlockSpec((tm, D_MODEL), rowblk),
                  pl.BlockSpec((D_MODEL, D_MODEL), const, pipeline_mode=pl.Buffered(1)),
                  pl.BlockSpec((tm, D_MODEL), rowblk),
                  pl.BlockSpec((1, D_MODEL), const),
                  pl.BlockSpec((D_MODEL, LANES), const),
                  pl.BlockSpec((1, LANES), const)],
        out_specs=[pl.BlockSpec((tm, D_MODEL), rowblk),
                   pl.BlockSpec((tm, D_MODEL), rowblk),
                   pl.BlockSpec((tm, LANES), rowblk)],
        out_shape=[jax.ShapeDtypeStruct((rows, D_MODEL), F32),
                   jax.ShapeDtypeStruct((rows, D_MODEL), BF16),
                   jax.ShapeDtypeStruct((rows, LANES), F32)],
        compiler_params=_cparams(("arbitrary",)),
        name="outproj",
    )(mix, w_out, x, n2, w_router, b_router)


def _moe_kernel(xn_ref, gates_ref, x1_ref, wg_ref, wu_ref, wd_ref, nf_ref, y_ref, acc_s):
    e = pl.program_id(1)

    @pl.when(e == 0)
    def _():
        acc_s[...] = jnp.zeros_like(acc_s)

    x = xn_ref[...]
    hg = jnp.dot(x, wg_ref[0], preferred_element_type=F32)
    hu = jnp.dot(x, wu_ref[0], preferred_element_type=F32)
    hdn = (hg * jax.nn.sigmoid(hg)) * hu
    yd = jnp.dot(hdn.astype(BF16), wd_ref[0], preferred_element_type=F32)
    gates = gates_ref[...]
    lane = lax.broadcasted_iota(jnp.int32, gates.shape, 1)
    gcol = jnp.sum(jnp.where(lane == e + N_GROUPS, gates, 0.0), axis=1, keepdims=True)
    acc_s[...] += gcol * yd

    @pl.when(e == pl.num_programs(1) - 1)
    def _():
        y_ref[...] = _rms(x1_ref[...] + acc_s[...], nf_ref[...])


def _moe(xn2, gates, x1, wg, wu, wd, nf, *, tm):
    rows = xn2.shape[0]
    rowblk = lambda i, e: (i, 0)
    wblk = lambda i, e: (e, 0, 0)
    return pl.pallas_call(
        _moe_kernel,
        grid=(rows // tm, N_EXPERTS),
        in_specs=[pl.BlockSpec((tm, D_MODEL), rowblk),
                  pl.BlockSpec((tm, LANES), rowblk),
                  pl.BlockSpec((tm, D_MODEL), rowblk),
                  pl.BlockSpec((1, D_MODEL, D_EXPERT), wblk),
                  pl.BlockSpec((1, D_MODEL, D_EXPERT), wblk),
                  pl.BlockSpec((1, D_EXPERT, D_MODEL), wblk),
                  pl.BlockSpec((1, D_MODEL), lambda i, e: (0, 0))],
        out_specs=pl.BlockSpec((tm, D_MODEL), rowblk),
        out_shape=jax.ShapeDtypeStruct((rows, D_MODEL), F32),
        scratch_shapes=[pltpu.VMEM((tm, D_MODEL), F32)],
        compiler_params=_cparams(("arbitrary", "arbitrary")),
        name="moe",
    )(xn2, gates, x1, wg, wu, wd, nf)


def _block_diag(w):
    per = RNN_SUPER // RNN_BLOCK
    w4 = w.reshape(N_SUPER, per, RNN_BLOCK, RNN_BLOCK)
    eye = jnp.eye(per, dtype=w.dtype)
    bd = jnp.einsum('spij,pq->spiqj', w4, eye)
    return bd.reshape(N_SUPER, RNN_SUPER, RNN_SUPER).astype(BF16)


def _stream(x, meta, conv0, h0, c0, n0, m0, p, *, has_head):
    nb, seq, _ = x.shape
    n_null = N_NULL if has_head else 0
    L = CHUNK if has_head else seq
    xn, xt, gif = _prep(x, meta, p['norm1_w'], p['w_gif'], p['b_gif'], has_head=has_head)
    tl = xn.shape[0]
    rows = tl * nb
    tm = rows // 16 if has_head else rows
    xn = xn.reshape(rows, D_MODEL)
    xt = xt.reshape(rows, D_MODEL)
    w_in = p['w_in']
    xr = _mm(xn, w_in, col0=0, ncols=D_RNN, tn=1280, tm=tm, out_dtype=F32, name="in_xr")
    gr = _mm(xn, w_in, col0=D_RNN, ncols=D_RNN, tn=1280, tm=tm, out_dtype=BF16, name="in_gr")
    qkvo = _mm(xn, w_in, col0=2 * D_RNN, ncols=2 * HQK + 2 * HV, tn=1024, tm=tm, out_dtype=BF16, name="in_qkvo")
    mg = _mm(xn, p['w_mg'], col0=0, ncols=2 * D_MODEL, tn=1024, tm=tm, out_dtype=BF16, name="in_mg")

    tt = L
    tail0 = jnp.transpose(conv0, (1, 0, 2)).reshape((CONV_W - 1) * nb, D_RNN)
    y_r, h_new, tail = _rglru(xr, gr, tail0, h0, p['conv_w'], p['conv_b'], p['rg_a_b'], p['rg_x_b'],
                              p['rg_lambda'], p['wa_bd'], p['wx_bd'], nb=nb, tt=tt, n_null=n_null)
    conv_new = jnp.transpose(tail.reshape(CONV_W - 1, nb, D_RNN), (1, 0, 2))

    m0p = jnp.pad(m0, ((0, 0), (0, LANES - N_HEADS_M))).reshape(nb, 1, LANES)
    y_m, c_new, n_new, m_new = _mlstm(qkvo.reshape(tl, nb * (2 * HQK + 2 * HV)), gif, c0, n0, m0p,
                                      p['mlstm_norm_w'], nb=nb, tl=tl, L=L, n_null=n_null)
    y_m = y_m.reshape(rows, HV)
    m_new = m_new[:, 0, :N_HEADS_M]

    p_r = _mm(y_r, p['w_proj_rnn'], col0=0, ncols=D_MODEL, tn=1024, tm=tm, out_dtype=F32,
              gate=mg, gate_col0=0, name="proj_rnn")
    mix = _mm(y_m, p['w_proj_mlstm'], col0=0, ncols=D_MODEL, tn=1024, tm=tm, out_dtype=BF16,
              gate=mg, gate_col0=D_MODEL, add=p_r, name="proj_mlstm")
    tm2 = rows // 32 if has_head else rows
    x1, xn2, gates = _outproj(mix, p['w_out'], xt, p['norm2_w'], p['w_router'], p['b_router'], tm=tm2)
    y = _moe(xn2, gates, x1, p['w_exp_gate'], p['w_exp_up'], p['w_exp_down'], p['norm_f_w'], tm=tm2)
    y = jnp.transpose(y.reshape(tl, nb, D_MODEL), (1, 0, 2))
    if has_head:
        y = y[:, CHUNK:]
    return y, (conv_new[None], h_new[None], c_new[None], n_new[None], m_new[None])


def kernel(x_prompt, x_sample, state_rglru_conv, state_rglru_h, state_mlstm_C, state_mlstm_n, state_mlstm_m, meta_tokens, norm1_w, w_in, b_gates, conv_w, conv_b, rg_a_w, rg_a_b, rg_x_w, rg_x_b, rg_lambda, mlstm_norm_w, w_proj_rnn, w_proj_mlstm, w_out, norm2_w, w_router_group, b_router_group, w_router_expert, b_router_expert, w_exp_gate, w_exp_up, w_exp_down, norm_f_w):
    l = 0
    w_in_l = w_in[l]
    pad_r = LANES - N_GROUPS - N_EXPERTS
    p = dict(
        norm1_w=norm1_w[l][None], w_in=w_in_l,
        w_gif=jnp.pad(w_in_l[:, GIF_OFF:MG_OFF], ((0, 0), (0, LANES - 2 * N_HEADS_M))).astype(BF16),
        b_gif=jnp.pad(b_gates[l], (0, LANES - 2 * N_HEADS_M))[None],
        w_mg=w_in_l[:, MG_OFF:],
        conv_w=conv_w[l], conv_b=conv_b[l][None], rg_a_b=rg_a_b[l][None], rg_x_b=rg_x_b[l][None],
        rg_lambda=rg_lambda[l][None], wa_bd=_block_diag(rg_a_w[l]), wx_bd=_block_diag(rg_x_w[l]),
        mlstm_norm_w=mlstm_norm_w[l][None],
        w_proj_rnn=w_proj_rnn[l], w_proj_mlstm=w_proj_mlstm[l], w_out=w_out[l].astype(BF16),
        norm2_w=norm2_w[l][None],
        w_router=jnp.pad(jnp.concatenate([w_router_group[l], w_router_expert[l]], axis=1),
                         ((0, 0), (0, pad_r))).astype(BF16),
        b_router=jnp.pad(jnp.concatenate([b_router_group[l], b_router_expert[l]]), (0, pad_r))[None],
        w_exp_gate=w_exp_gate[l].astype(BF16), w_exp_up=w_exp_up[l].astype(BF16),
        w_exp_down=w_exp_down[l].astype(BF16), norm_f_w=norm_f_w[None],
    )
    nbp = x_prompt.shape[0]
    dt = x_prompt.dtype
    y_p, st_p = _stream(
        x_prompt, meta_tokens,
        jnp.zeros((nbp, CONV_W - 1, D_RNN), dt), jnp.zeros((nbp, D_RNN), F32),
        jnp.zeros((nbp, N_HEADS_M, DQK_M, DV_M), F32), jnp.zeros((nbp, N_HEADS_M, DQK_M), F32),
        jnp.zeros((nbp, N_HEADS_M), F32), p, has_head=True)
    y_s, st_s = _stream(
        x_sample, meta_tokens, state_rglru_conv[l], state_rglru_h[l], state_mlstm_C[l],
        state_mlstm_n[l], state_mlstm_m[l], p, has_head=False)
    return (y_p, y_s) + st_p + st_s
```

```python
import functools

import jax
import jax.numpy as jnp
from jax import lax
from jax.experimental import pallas as pl
from jax.experimental.pallas import tpu as pltpu

F32 = jnp.float32
BF16 = jnp.bfloat16

D_MODEL = 2048
CHUNK = 64
N_META = 16
N_NULL = CHUNK - N_META
D_RNN = 2560
N_RNN_BLOCKS = 16
RNN_BLOCK = D_RNN // N_RNN_BLOCKS
RNN_SUPER = 640
N_SUPER = D_RNN // RNN_SUPER
CONV_W = 4
LRU_C = 8.0
N_HEADS_M = 8
DV_M = D_MODEL // N_HEADS_M
DQK_M = DV_M // 2
HQK = N_HEADS_M * DQK_M
HV = N_HEADS_M * DV_M
QKVO = 2 * HQK + 2 * HV
N_GROUPS = 4
EXPERTS_PER_GROUP = 4
N_EXPERTS = 16
D_EXPERT = 512
EPS = 1e-6
LANES = 128
GIF_OFF = 2 * D_RNN + QKVO
MG_OFF = GIF_OFF + 2 * N_HEADS_M
VMEM_LIMIT = 56 * 1024 * 1024
NEG_BIG = -1e30


def _cparams(sem):
    return pltpu.CompilerParams(dimension_semantics=sem, vmem_limit_bytes=VMEM_LIMIT)


def _rms(x, w):
    return x * lax.rsqrt(jnp.mean(x * x, axis=-1, keepdims=True) + EPS) * w


def _softplus(x):
    return jnp.maximum(x, 0.0) + jnp.log1p(jnp.exp(-jnp.abs(x)))


def _head_rows(meta_ref, nb):
    head = jnp.concatenate([jnp.zeros((N_NULL, D_MODEL), F32), meta_ref[...]], axis=0)
    return jnp.broadcast_to(head[None], (nb, CHUNK, D_MODEL)).reshape(nb * CHUNK, D_MODEL)


def _x_map(has_head):
    if has_head:
        return lambda ti: (0, jnp.maximum(ti - 1, 0), 0)
    return lambda ti: (0, ti, 0)


def _prep_kernel(x_ref, meta_ref, nw_ref, wg_ref, bg_ref, xn_ref, gif_ref, *, nb, tt, has_head):
    def emit(x):
        xnb = _rms(x, nw_ref[...]).astype(BF16)
        xn_ref[...] = xnb.reshape(nb, tt, D_MODEL)
        gif = jnp.dot(xnb, wg_ref[...], preferred_element_type=F32) + bg_ref[...]
        gif_ref[...] = gif.reshape(nb, tt, LANES)

    if has_head:
        @pl.when(pl.program_id(0) == 0)
        def _():
            emit(_head_rows(meta_ref, nb))

        @pl.when(pl.program_id(0) > 0)
        def _():
            emit(x_ref[...].reshape(nb * tt, D_MODEL))
    else:
        emit(x_ref[...].reshape(nb * tt, D_MODEL))


def _prep(x, meta, norm_w, w_gif, b_gif, *, has_head):
    nb, seq, _ = x.shape
    tt = CHUNK if has_head else seq
    n_tiles = seq // tt + (1 if has_head else 0)
    tl = n_tiles * tt
    const = lambda ti: (0, 0)
    return pl.pallas_call(
        functools.partial(_prep_kernel, nb=nb, tt=tt, has_head=has_head),
        grid=(n_tiles,),
        in_specs=[pl.BlockSpec((nb, tt, D_MODEL), _x_map(has_head)),
                  pl.BlockSpec((N_META, D_MODEL), const),
                  pl.BlockSpec((1, D_MODEL), const),
                  pl.BlockSpec((D_MODEL, LANES), const),
                  pl.BlockSpec((1, LANES), const)],
        out_specs=[pl.BlockSpec((nb, tt, D_MODEL), lambda ti: (0, ti, 0)),
                   pl.BlockSpec((nb, tt, LANES), lambda ti: (0, ti, 0))],
        out_shape=[jax.ShapeDtypeStruct((nb, tl, D_MODEL), BF16),
                   jax.ShapeDtypeStruct((nb, tl, LANES), F32)],
        compiler_params=_cparams(("arbitrary",)),
        name="prep",
    )(x, meta, norm_w, w_gif, b_gif)


def _mm_kernel(*refs, has_gate, has_add):
    lhs_ref, w_ref = refs[0], refs[1]
    pos = 2
    gate_ref = add_ref = None
    if has_gate:
        gate_ref = refs[pos]
        pos += 1
    if has_add:
        add_ref = refs[pos]
        pos += 1
    out_ref, wbf_ref = refs[pos], refs[pos + 1]
    nb, tt, k = lhs_ref.shape
    tn = out_ref.shape[-1]

    @pl.when(pl.program_id(1) == 0)
    def _():
        wbf_ref[...] = w_ref[...].astype(BF16)

    acc = jnp.dot(lhs_ref[...].reshape(nb * tt, k), wbf_ref[...], preferred_element_type=F32)
    if has_gate:
        acc = jax.nn.sigmoid(gate_ref[...].reshape(nb * tt, tn).astype(F32)) * acc
    if has_add:
        acc = add_ref[...].reshape(nb * tt, tn).astype(F32) + acc
    out_ref[...] = acc.astype(out_ref.dtype).reshape(nb, tt, tn)


def _mm(lhs, w, *, col0, ncols, tn, tt, out_dtype, gate=None, gate_col0=0, add=None, name):
    nb, tl, k = lhs.shape
    assert tl % tt == 0 and ncols % tn == 0 and col0 % tn == 0 and gate_col0 % tn == 0
    in_specs = [pl.BlockSpec((nb, tt, k), lambda j, i: (0, i, 0)),
                pl.BlockSpec((k, tn), lambda j, i: (0, col0 // tn + j))]
    args = [lhs, w]
    if gate is not None:
        in_specs.append(pl.BlockSpec((nb, tt, tn), lambda j, i: (0, i, gate_col0 // tn + j)))
        args.append(gate)
    if add is not None:
        in_specs.append(pl.BlockSpec((nb, tt, tn), lambda j, i: (0, i, j)))
        args.append(add)
    return pl.pallas_call(
        functools.partial(_mm_kernel, has_gate=gate is not None, has_add=add is not None),
        grid=(ncols // tn, tl // tt),
        in_specs=in_specs,
        out_specs=pl.BlockSpec((nb, tt, tn), lambda j, i: (0, i, j)),
        out_shape=jax.ShapeDtypeStruct((nb, tl, ncols), out_dtype),
        scratch_shapes=[pltpu.VMEM((k, tn), BF16)],
        compiler_params=_cparams(("arbitrary", "arbitrary")),
        name=name,
    )(*args)


def _rglru_kernel(xr_ref, gr_ref, tail0_ref, h0_ref, cw_ref, cb_ref, ab_ref, xb_ref, lam_ref,
                  wa_ref, wx_ref, y_ref, hl_ref, tail_ref, xp_s, h_s, a_s, u_s, hs_s, *, nb, tt, n_null):
    ti = pl.program_id(1)
    rt = tt * nb
    tb = (CONV_W - 1) * nb

    def time_major(x):
        return jnp.swapaxes(x, 0, 1).reshape(rt, x.shape[-1])

    @pl.when(ti == 0)
    def _():
        xp_s[0:tb, :] = tail0_ref[...]
        h_s[...] = h0_ref[...]

    xp_s[tb:tb + rt, :] = time_major(xr_ref[...])
    cw = cw_ref[...]
    xc = cb_ref[...] + cw[0:1, :] * xp_s[0:rt, :]
    for j in range(1, CONV_W):
        xc = xc + cw[j:j + 1, :] * xp_s[j * nb:j * nb + rt, :]
    xcb = xc.astype(BF16)
    r = jax.nn.sigmoid(jnp.dot(xcb, wa_ref[0], preferred_element_type=F32) + ab_ref[...])
    i = jax.nn.sigmoid(jnp.dot(xcb, wx_ref[0], preferred_element_type=F32) + xb_ref[...])
    log_a = (-LRU_C) * r * _softplus(-lam_ref[...])
    a = jnp.exp(log_a)
    a_s[...] = a
    u = jnp.sqrt(-jnp.tanh(log_a) * (a * a + 1.0)) * (i * xc)
    if n_null:
        row = lax.broadcasted_iota(jnp.int32, (rt, 1), 0)
        u = jnp.where((row >= n_null * nb) | (ti > 0), u, 0.0)
    u_s[...] = u

    def step(t, h):
        o = pl.multiple_of(t * nb, nb)
        h = a_s[pl.ds(o, nb), :] * h + u_s[pl.ds(o, nb), :]
        hs_s[pl.ds(o, nb), :] = h
        return h

    h = lax.fori_loop(0, tt, step, h_s[...], unroll=8)
    h_s[...] = h
    g = time_major(gr_ref[...].astype(F32))
    gelu = g * (0.5 * (1.0 + jnp.tanh(0.7978845608028654 * (g + 0.044715 * (g * g * g)))))
    y = gelu * hs_s[...]
    y_ref[...] = jnp.swapaxes(y.reshape(tt, nb, y.shape[-1]), 0, 1).astype(BF16)
    xp_s[0:tb, :] = xp_s[rt:rt + tb, :]

    @pl.when(ti == pl.num_programs(1) - 1)
    def _():
        hl_ref[...] = h
        tail_ref[...] = xp_s[rt:rt + tb, :]


def _rglru(xr, gr, tail0, h0, cw, cb, ab, xb, lam, wa, wx, *, tt, n_null):
    nb, tl, _ = xr.shape
    rt = tt * nb
    tb = (CONV_W - 1) * nb
    blk = lambda s, ti: (0, ti, s)
    col = lambda s, ti: (0, s)
    return pl.pallas_call(
        functools.partial(_rglru_kernel, nb=nb, tt=tt, n_null=n_null),
        grid=(N_SUPER, tl // tt),
        in_specs=[pl.BlockSpec((nb, tt, RNN_SUPER), blk),
                  pl.BlockSpec((nb, tt, RNN_SUPER), blk),
                  pl.BlockSpec((tb, RNN_SUPER), col),
                  pl.BlockSpec((nb, RNN_SUPER), col),
                  pl.BlockSpec((CONV_W, RNN_SUPER), col),
                  pl.BlockSpec((1, RNN_SUPER), col),
                  pl.BlockSpec((1, RNN_SUPER), col),
                  pl.BlockSpec((1, RNN_SUPER), col),
                  pl.BlockSpec((1, RNN_SUPER), col),
                  pl.BlockSpec((1, RNN_SUPER, RNN_SUPER), lambda s, ti: (s, 0, 0)),
                  pl.BlockSpec((1, RNN_SUPER, RNN_SUPER), lambda s, ti: (s, 0, 0))],
        out_specs=[pl.BlockSpec((nb, tt, RNN_SUPER), blk),
                   pl.BlockSpec((nb, RNN_SUPER), col),
                   pl.BlockSpec((tb, RNN_SUPER), col)],
        out_shape=[jax.ShapeDtypeStruct((nb, tl, D_RNN), BF16),
                   jax.ShapeDtypeStruct((nb, D_RNN), F32),
                   jax.ShapeDtypeStruct((tb, D_RNN), F32)],
        scratch_shapes=[pltpu.VMEM((rt + tb, RNN_SUPER), F32),
                        pltpu.VMEM((nb, RNN_SUPER), F32),
                        pltpu.VMEM((rt, RNN_SUPER), F32),
                        pltpu.VMEM((rt, RNN_SUPER), F32),
                        pltpu.VMEM((rt, RNN_SUPER), F32)],
        compiler_params=_cparams(("arbitrary", "arbitrary")),
        name="rglru",
    )(xr, gr, tail0, h0, cw, cb, ab, xb, lam, wa, wx)


def _mlstm_kernel(q_ref, k_ref, v_ref, og_ref, gif_ref, c0_ref, n0_ref, m0_ref, nw_ref,
                  y_ref, co_ref, no_ref, mo_ref, c_s, n_s, m_s, *, L, n_null):
    c = pl.program_id(1)

    @pl.when(c == 0)
    def _():
        c_s[...] = c0_ref[...]
        n_s[...] = n0_ref[...]
        m_s[...] = m0_ref[...]

    g = gif_ref[...]
    lane = lax.broadcasted_iota(jnp.int32, (L, LANES), 1)
    row = lax.broadcasted_iota(jnp.int32, (L, LANES), 0)
    is_f = (lane >= N_HEADS_M) & (lane < 2 * N_HEADS_M)
    lf = jnp.where(is_f, jnp.minimum(g, 0.0) - jnp.log1p(jnp.exp(-jnp.abs(g))), 0.0)
    li = g
    if n_null:
        null = (row < n_null) & (c == 0)
        lf = jnp.where(null, 0.0, lf)
        li = jnp.where(null, NEG_BIG, li)
    rr = lax.broadcasted_iota(jnp.int32, (L, L), 0)
    cc = lax.broadcasted_iota(jnp.int32, (L, L), 1)
    causal = rr >= cc
    tri = causal.astype(F32)
    bfull = jnp.dot(tri, lf, preferred_element_type=F32, precision=lax.Precision.HIGHEST)
    b_t = bfull.T
    li_t = li.T
    m_all = m_s[...]
    m_new = m_all
    lane1 = lax.broadcasted_iota(jnp.int32, (1, LANES), 1)
    scale = DQK_M ** -0.5
    nw = nw_ref[...]
    dn_nt = (((1,), (1,)), ((), ()))
    dn_tn = (((0,), (0,)), ((), ()))
    for h in range(N_HEADS_M):
        bcol = bfull[:, N_HEADS_M + h:N_HEADS_M + h + 1]
        brow = b_t[N_HEADS_M + h:N_HEADS_M + h + 1, :]
        igrow = li_t[h:h + 1, :]
        igcol = li[:, h:h + 1]
        m0h = m_all[:, h:h + 1]
        log_w = jnp.where(causal, bcol - brow + igrow, -jnp.inf)
        log_inter = bcol + m0h
        mvec = jnp.maximum(log_inter, jnp.max(log_w, axis=1, keepdims=True))
        w_intra = jnp.exp(log_w - mvec)
        w_inter = jnp.exp(log_inter - mvec)
        qh = q_ref[:, h * DQK_M:(h + 1) * DQK_M]
        kh = k_ref[:, h * DQK_M:(h + 1) * DQK_M]
        vh = v_ref[:, h * DV_M:(h + 1) * DV_M]
        s = lax.dot_general(qh, kh, dn_nt, preferred_element_type=F32) * (scale * w_intra)
        ch = c_s[h]
        nh = n_s[h:h + 1, :]
        num = (jnp.dot(s.astype(BF16), vh, preferred_element_type=F32)
               + w_inter * jnp.dot(qh, ch.astype(BF16), preferred_element_type=F32))
        den = (jnp.sum(s, axis=1, keepdims=True)
               + w_inter * jnp.sum(qh.astype(F32) * nh, axis=1, keepdims=True))
        denom = jnp.maximum(jnp.abs(den), jnp.exp(-mvec))
        hh = num / denom
        hh = hh * lax.rsqrt(jnp.mean(hh * hh, axis=-1, keepdims=True) + EPS)
        hh = hh * nw[:, h * DV_M:(h + 1) * DV_M]
        og = og_ref[:, h * DV_M:(h + 1) * DV_M].astype(F32)
        y_ref[:, h * DV_M:(h + 1) * DV_M] = (hh * jax.nn.sigmoid(og)).astype(BF16)
        m_end = mvec[L - 1:L, :]
        b_end = bcol[L - 1:L, :]
        w_end = jnp.exp(b_end - bcol + igcol - m_end)
        decay = jnp.exp(b_end + m0h - m_end)
        kw = kh.astype(F32) * (w_end * scale)
        c_s[h] = decay * ch + lax.dot_general(kw.astype(BF16), vh, dn_tn, preferred_element_type=F32)
        n_s[h:h + 1, :] = decay * nh + jnp.sum(kw, axis=0, keepdims=True)
        m_new = jnp.where(lane1 == h, m_end, m_new)
    m_s[...] = m_new

    @pl.when(c == pl.num_programs(1) - 1)
    def _():
        co_ref[...] = c_s[...]
        no_ref[...] = n_s[...]
        mo_ref[...] = m_new


def _mlstm(qkvo, gif, c0, n0, m0, nw, *, L, n_null):
    nb, tl, _ = qkvo.shape
    st = lambda b, c: (b, 0, 0)
    return pl.pallas_call(
        functools.partial(_mlstm_kernel, L=L, n_null=n_null),
        grid=(nb, tl // L),
        in_specs=[pl.BlockSpec((None, L, HQK), lambda b, c: (b, c, 0)),
                  pl.BlockSpec((None, L, HQK), lambda b, c: (b, c, 1)),
                  pl.BlockSpec((None, L, HV), lambda b, c: (b, c, 1)),
                  pl.BlockSpec((None, L, HV), lambda b, c: (b, c, 2)),
                  pl.BlockSpec((None, L, LANES), lambda b, c: (b, c, 0)),
                  pl.BlockSpec((None, N_HEADS_M, DQK_M, DV_M), lambda b, c: (b, 0, 0, 0)),
                  pl.BlockSpec((None, N_HEADS_M, DQK_M), st),
                  pl.BlockSpec((None, 1, LANES), st),
                  pl.BlockSpec((1, HV), lambda b, c: (0, 0))],
        out_specs=[pl.BlockSpec((None, L, HV), lambda b, c: (b, c, 0)),
                   pl.BlockSpec((None, N_HEADS_M, DQK_M, DV_M), lambda b, c: (b, 0, 0, 0)),
                   pl.BlockSpec((None, N_HEADS_M, DQK_M), st),
                   pl.BlockSpec((None, 1, LANES), st)],
        out_shape=[jax.ShapeDtypeStruct((nb, tl, HV), BF16),
                   jax.ShapeDtypeStruct((nb, N_HEADS_M, DQK_M, DV_M), F32),
                   jax.ShapeDtypeStruct((nb, N_HEADS_M, DQK_M), F32),
                   jax.ShapeDtypeStruct((nb, 1, LANES), F32)],
        scratch_shapes=[pltpu.VMEM((N_HEADS_M, DQK_M, DV_M), F32),
                        pltpu.VMEM((N_HEADS_M, DQK_M), F32),
                        pltpu.VMEM((1, LANES), F32)],
        compiler_params=_cparams(("arbitrary", "arbitrary")),
        name="mlstm",
    )(qkvo, qkvo, qkvo, qkvo, gif, c0, n0, m0, nw)


def _route(logits):
    tm = logits.shape[0]
    lane = lax.broadcasted_iota(jnp.int32, (tm, LANES), 1)
    lanef = lane.astype(F32)
    gl = jnp.where(lane < N_GROUPS, logits, -jnp.inf)
    gmax = jnp.max(gl, axis=1, keepdims=True)
    gidx = jnp.min(jnp.where(gl == gmax, lanef, float(LANES)), axis=1, keepdims=True)
    gw = 1.0 / jnp.sum(jnp.exp(gl - gmax), axis=1, keepdims=True)
    lo = float(N_GROUPS) + float(EXPERTS_PER_GROUP) * gidx
    sel = (lanef >= lo) & (lanef < lo + float(EXPERTS_PER_GROUP))
    ev = jnp.where(sel, logits, -jnp.inf)
    v1 = jnp.max(ev, axis=1, keepdims=True)
    i1 = jnp.min(jnp.where(ev == v1, lanef, float(LANES)), axis=1, keepdims=True)
    ev2 = jnp.where(lanef == i1, -jnp.inf, ev)
    v2 = jnp.max(ev2, axis=1, keepdims=True)
    i2 = jnp.min(jnp.where(ev2 == v2, lanef, float(LANES)), axis=1, keepdims=True)
    t = jnp.exp(v2 - v1)
    w1 = 1.0 / (1.0 + t)
    w2 = t / (1.0 + t)
    return jnp.where(lanef == i1, w1 * gw, jnp.where(lanef == i2, w2 * gw, 0.0))


def _outproj_kernel(mix_ref, w_ref, x_ref, meta_ref, n2_ref, wr_ref, br_ref, x1_ref, xn2_ref, gates_ref,
                    *, nb, tt, has_head):
    def emit(x):
        mix = mix_ref[...].reshape(nb * tt, D_MODEL)
        x1 = x + jnp.dot(mix, w_ref[...], preferred_element_type=F32)
        x1_ref[...] = x1.reshape(nb, tt, D_MODEL)
        xn2 = _rms(x1, n2_ref[...]).astype(BF16)
        xn2_ref[...] = xn2.reshape(nb, tt, D_MODEL)
        logits = jnp.dot(xn2, wr_ref[...], preferred_element_type=F32) + br_ref[...]
        gates_ref[...] = _route(logits).reshape(nb, tt, LANES)

    if has_head:
        @pl.when(pl.program_id(0) == 0)
        def _():
            emit(_head_rows(meta_ref, nb))

        @pl.when(pl.program_id(0) > 0)
        def _():
            emit(x_ref[...].reshape(nb * tt, D_MODEL))
    else:
        emit(x_ref[...].reshape(nb * tt, D_MODEL))


def _outproj(mix, w_out, x, meta, n2, w_router, b_router, *, tt, has_head):
    nb, tl, _ = mix.shape
    const = lambda i: (0, 0)
    blk = lambda i: (0, i, 0)
    return pl.pallas_call(
        functools.partial(_outproj_kernel, nb=nb, tt=tt, has_head=has_head),
        grid=(tl // tt,),
        in_specs=[pl.BlockSpec((nb, tt, D_MODEL), blk),
                  pl.BlockSpec((D_MODEL, D_MODEL), const, pipeline_mode=pl.Buffered(1)),
                  pl.BlockSpec((nb, tt, D_MODEL), _x_map(has_head)),
                  pl.BlockSpec((N_META, D_MODEL), const),
                  pl.BlockSpec((1, D_MODEL), const),
                  pl.BlockSpec((D_MODEL, LANES), const),
                  pl.BlockSpec((1, LANES), const)],
        out_specs=[pl.BlockSpec((nb, tt, D_MODEL), blk),
                   pl.BlockSpec((nb, tt, D_MODEL), blk),
                   pl.BlockSpec((nb, tt, LANES), blk)],
        out_shape=[jax.ShapeDtypeStruct((nb, tl, D_MODEL), F32),
                   jax.ShapeDtypeStruct((nb, tl, D_MODEL), BF16),
                   jax.ShapeDtypeStruct((nb, tl, LANES), F32)],
        compiler_params=_cparams(("arbitrary",)),
        name="outproj",
    )(mix, w_out, x, meta, n2, w_router, b_router)


def _moe_kernel(xn_ref, gates_ref, x1_ref, wg_ref, wu_ref, wd_ref, nf_ref, y_ref, acc_s, *, nb, tt):
    e = pl.program_id(1)
    rows = nb * tt

    @pl.when(e == 0)
    def _():
        acc_s[...] = jnp.zeros_like(acc_s)

    x = xn_ref[...].reshape(rows, D_MODEL)
    hg = jnp.dot(x, wg_ref[0], preferred_element_type=F32)
    hu = jnp.dot(x, wu_ref[0], preferred_element_type=F32)
    hdn = (hg * jax.nn.sigmoid(hg)) * hu
    yd = jnp.dot(hdn.astype(BF16), wd_ref[0], preferred_element_type=F32)
    gates = gates_ref[...].reshape(rows, LANES)
    lane = lax.broadcasted_iota(jnp.int32, gates.shape, 1)
    gcol = jnp.sum(jnp.where(lane == e + N_GROUPS, gates, 0.0), axis=1, keepdims=True)
    acc_s[...] += gcol * yd

    @pl.when(e == pl.num_programs(1) - 1)
    def _():
        x2 = x1_ref[...].reshape(rows, D_MODEL) + acc_s[...]
        y_ref[...] = _rms(x2, nf_ref[...]).reshape(nb, tt, D_MODEL)


def _moe(xn2, gates, x1, wg, wu, wd, nf, *, tt, has_head):
    nb, tl, _ = xn2.shape
    blk = lambda i, e: (0, i, 0)
    wblk = lambda i, e: (e, 0, 0)
    if has_head:
        out_map = lambda i, e: (0, jnp.maximum(i - 1, 0), 0)
        seq = tl - tt
    else:
        out_map = blk
        seq = tl
    return pl.pallas_call(
        functools.partial(_moe_kernel, nb=nb, tt=tt),
        grid=(tl // tt, N_EXPERTS),
        in_specs=[pl.BlockSpec((nb, tt, D_MODEL), blk),
                  pl.BlockSpec((nb, tt, LANES), blk),
                  pl.BlockSpec((nb, tt, D_MODEL), blk),
                  pl.BlockSpec((1, D_MODEL, D_EXPERT), wblk),
                  pl.BlockSpec((1, D_MODEL, D_EXPERT), wblk),
                  pl.BlockSpec((1, D_EXPERT, D_MODEL), wblk),
                  pl.BlockSpec((1, D_MODEL), lambda i, e: (0, 0))],
        out_specs=pl.BlockSpec((nb, tt, D_MODEL), out_map),
        out_shape=jax.ShapeDtypeStruct((nb, seq, D_MODEL), F32),
        scratch_shapes=[pltpu.VMEM((nb * tt, D_MODEL), F32)],
        compiler_params=_cparams(("arbitrary", "arbitrary")),
        name="moe",
    )(xn2, gates, x1, wg, wu, wd, nf)


def _block_diag(w):
    per = RNN_SUPER // RNN_BLOCK
    w4 = w.reshape(N_SUPER, per, RNN_BLOCK, RNN_BLOCK)
    eye = jnp.eye(per, dtype=w.dtype)
    bd = jnp.einsum('spij,pq->spiqj', w4, eye)
    return bd.reshape(N_SUPER, RNN_SUPER, RNN_SUPER).astype(BF16)


def _stream(x, meta, conv0, h0, c0, n0, m0, p, *, has_head):
    nb, seq, _ = x.shape
    n_null = N_NULL if has_head else 0
    tt = CHUNK if has_head else seq
    xn, gif = _prep(x, meta, p['norm1_w'], p['w_gif'], p['b_gif'], has_head=has_head)
    w_in = p['w_in']
    xr = _mm(xn, w_in, col0=0, ncols=D_RNN, tn=1280, tt=tt, out_dtype=F32, name="in_xr")
    gr = _mm(xn, w_in, col0=D_RNN, ncols=D_RNN, tn=1280, tt=tt, out_dtype=BF16, name="in_gr")
    qkvo = _mm(xn, w_in, col0=2 * D_RNN, ncols=QKVO, tn=1024, tt=tt, out_dtype=BF16, name="in_qkvo")
    mg = _mm(xn, p['w_mg'], col0=0, ncols=2 * D_MODEL, tn=1024, tt=tt, out_dtype=BF16, name="in_mg")

    tail0 = jnp.transpose(conv0, (1, 0, 2)).reshape((CONV_W - 1) * nb, D_RNN)
    y_r, h_new, tail = _rglru(xr, gr, tail0, h0, p['conv_w'], p['conv_b'], p['rg_a_b'], p['rg_x_b'],
                              p['rg_lambda'], p['wa_bd'], p['wx_bd'], tt=tt, n_null=n_null)
    conv_new = jnp.transpose(tail.reshape(CONV_W - 1, nb, D_RNN), (1, 0, 2))

    m0p = jnp.pad(m0, ((0, 0), (0, LANES - N_HEADS_M))).reshape(nb, 1, LANES)
    y_m, c_new, n_new, m_new = _mlstm(qkvo, gif, c0, n0, m0p, p['mlstm_norm_w'], L=tt, n_null=n_null)
    m_new = m_new[:, 0, :N_HEADS_M]

    p_r = _mm(y_r, p['w_proj_rnn'], col0=0, ncols=D_MODEL, tn=1024, tt=tt, out_dtype=F32,
              gate=mg, gate_col0=0, name="proj_rnn")
    mix = _mm(y_m, p['w_proj_mlstm'], col0=0, ncols=D_MODEL, tn=1024, tt=tt, out_dtype=BF16,
              gate=mg, gate_col0=D_MODEL, add=p_r, name="proj_mlstm")
    x1, xn2, gates = _outproj(mix, p['w_out'], x, meta, p['norm2_w'], p['w_router'], p['b_router'],
                              tt=tt, has_head=has_head)
    y = _moe(xn2, gates, x1, p['w_exp_gate'], p['w_exp_up'], p['w_exp_down'], p['norm_f_w'],
             tt=tt, has_head=has_head)
    return y, (conv_new[None], h_new[None], c_new[None], n_new[None], m_new[None])


def kernel(x_prompt, x_sample, state_rglru_conv, state_rglru_h, state_mlstm_C, state_mlstm_n, state_mlstm_m, meta_tokens, norm1_w, w_in, b_gates, conv_w, conv_b, rg_a_w, rg_a_b, rg_x_w, rg_x_b, rg_lambda, mlstm_norm_w, w_proj_rnn, w_proj_mlstm, w_out, norm2_w, w_router_group, b_router_group, w_router_expert, b_router_expert, w_exp_gate, w_exp_up, w_exp_down, norm_f_w):
    l = 0
    w_in_l = w_in[l]
    pad_r = LANES - N_GROUPS - N_EXPERTS
    p = dict(
        norm1_w=norm1_w[l][None], w_in=w_in_l,
        w_gif=jnp.pad(w_in_l[:, GIF_OFF:MG_OFF], ((0, 0), (0, LANES - 2 * N_HEADS_M))).astype(BF16),
        b_gif=jnp.pad(b_gates[l], (0, LANES - 2 * N_HEADS_M))[None],
        w_mg=w_in_l[:, MG_OFF:],
        conv_w=conv_w[l], conv_b=conv_b[l][None], rg_a_b=rg_a_b[l][None], rg_x_b=rg_x_b[l][None],
        rg_lambda=rg_lambda[l][None], wa_bd=_block_diag(rg_a_w[l]), wx_bd=_block_diag(rg_x_w[l]),
        mlstm_norm_w=mlstm_norm_w[l][None],
        w_proj_rnn=w_proj_rnn[l], w_proj_mlstm=w_proj_mlstm[l], w_out=w_out[l].astype(BF16),
        norm2_w=norm2_w[l][None],
        w_router=jnp.pad(jnp.concatenate([w_router_group[l], w_router_expert[l]], axis=1),
                         ((0, 0), (0, pad_r))).astype(BF16),
        b_router=jnp.pad(jnp.concatenate([b_router_group[l], b_router_expert[l]]), (0, pad_r))[None],
        w_exp_gate=w_exp_gate[l].astype(BF16), w_exp_up=w_exp_up[l].astype(BF16),
        w_exp_down=w_exp_down[l].astype(BF16), norm_f_w=norm_f_w[None],
    )
    nbp = x_prompt.shape[0]
    dt = x_prompt.dtype
    y_p, st_p = _stream(
        x_prompt, meta_tokens,
        jnp.zeros((nbp, CONV_W - 1, D_RNN), dt), jnp.zeros((nbp, D_RNN), F32),
        jnp.zeros((nbp, N_HEADS_M, DQK_M, DV_M), F32), jnp.zeros((nbp, N_HEADS_M, DQK_M), F32),
        jnp.zeros((nbp, N_HEADS_M), F32), p, has_head=True)
    y_s, st_s = _stream(
        x_sample, meta_tokens, state_rglru_conv[l], state_rglru_h[l], state_mlstm_C[l],
        state_mlstm_n[l], state_mlstm_m[l], p, has_head=False)
    return (y_p, y_s) + st_p + st_s
```

```python
import functools

import jax
import jax.numpy as jnp
from jax import lax
from jax.experimental import pallas as pl
from jax.experimental.pallas import tpu as pltpu

F32 = jnp.float32
BF16 = jnp.bfloat16

D_MODEL = 2048
CHUNK = 64
N_META = 16
N_NULL = CHUNK - N_META
D_RNN = 2560
N_RNN_BLOCKS = 16
RNN_BLOCK = D_RNN // N_RNN_BLOCKS
RNN_SUPER = 640
N_SUPER = D_RNN // RNN_SUPER
CONV_W = 4
LRU_C = 8.0
N_HEADS_M = 8
DV_M = D_MODEL // N_HEADS_M
DQK_M = DV_M // 2
HQK = N_HEADS_M * DQK_M
HV = N_HEADS_M * DV_M
QKVO = 2 * HQK + 2 * HV
N_GROUPS = 4
EXPERTS_PER_GROUP = 4
N_EXPERTS = 16
D_EXPERT = 512
PAIRS_PER_GROUP = 6
N_CLASSES = N_GROUPS * PAIRS_PER_GROUP
TM_E = 256
ROW_W = D_MODEL + 128
EPS = 1e-6
LANES = 128
GIF_OFF = 2 * D_RNN + QKVO
MG_OFF = GIF_OFF + 2 * N_HEADS_M
VMEM_LIMIT = 56 * 1024 * 1024
NEG_BIG = -1e30


def _cparams(sem):
    return pltpu.CompilerParams(dimension_semantics=sem, vmem_limit_bytes=VMEM_LIMIT)


def _rms(x, w):
    return x * lax.rsqrt(jnp.mean(x * x, axis=-1, keepdims=True) + EPS) * w


def _softplus(x):
    return jnp.maximum(x, 0.0) + jnp.log1p(jnp.exp(-jnp.abs(x)))


def _head_rows(meta_ref, nb):
    head = jnp.concatenate([jnp.zeros((N_NULL, D_MODEL), F32), meta_ref[...]], axis=0)
    return jnp.broadcast_to(head[None], (nb, CHUNK, D_MODEL)).reshape(nb * CHUNK, D_MODEL)


def _x_map(has_head):
    if has_head:
        return lambda ti: (0, jnp.maximum(ti - 1, 0), 0)
    return lambda ti: (0, ti, 0)


def _prep_kernel(x_ref, meta_ref, nw_ref, wg_ref, bg_ref, xn_ref, gif_ref, *, nb, tt, has_head):
    def emit(x):
        xnb = _rms(x, nw_ref[...]).astype(BF16)
        xn_ref[...] = xnb.reshape(nb, tt, D_MODEL)
        gif = jnp.dot(xnb, wg_ref[...], preferred_element_type=F32) + bg_ref[...]
        gif_ref[...] = gif.reshape(nb, tt, LANES)

    if has_head:
        @pl.when(pl.program_id(0) == 0)
        def _():
            emit(_head_rows(meta_ref, nb))

        @pl.when(pl.program_id(0) > 0)
        def _():
            emit(x_ref[...].reshape(nb * tt, D_MODEL))
    else:
        emit(x_ref[...].reshape(nb * tt, D_MODEL))


def _prep(x, meta, norm_w, w_gif, b_gif, *, has_head):
    nb, seq, _ = x.shape
    tt = CHUNK if has_head else seq
    n_tiles = seq // tt + (1 if has_head else 0)
    tl = n_tiles * tt
    const = lambda ti: (0, 0)
    return pl.pallas_call(
        functools.partial(_prep_kernel, nb=nb, tt=tt, has_head=has_head),
        grid=(n_tiles,),
        in_specs=[pl.BlockSpec((nb, tt, D_MODEL), _x_map(has_head)),
                  pl.BlockSpec((N_META, D_MODEL), const),
                  pl.BlockSpec((1, D_MODEL), const),
                  pl.BlockSpec((D_MODEL, LANES), const),
                  pl.BlockSpec((1, LANES), const)],
        out_specs=[pl.BlockSpec((nb, tt, D_MODEL), lambda ti: (0, ti, 0)),
                   pl.BlockSpec((nb, tt, LANES), lambda ti: (0, ti, 0))],
        out_shape=[jax.ShapeDtypeStruct((nb, tl, D_MODEL), BF16),
                   jax.ShapeDtypeStruct((nb, tl, LANES), F32)],
        compiler_params=_cparams(("arbitrary",)),
        name="prep",
    )(x, meta, norm_w, w_gif, b_gif)


def _mm_kernel(*refs, has_gate, has_add):
    lhs_ref, w_ref = refs[0], refs[1]
    pos = 2
    gate_ref = add_ref = None
    if has_gate:
        gate_ref = refs[pos]
        pos += 1
    if has_add:
        add_ref = refs[pos]
        pos += 1
    out_ref, wbf_ref = refs[pos], refs[pos + 1]
    nb, tt, k = lhs_ref.shape
    tn = out_ref.shape[-1]

    @pl.when(pl.program_id(1) == 0)
    def _():
        wbf_ref[...] = w_ref[...].astype(BF16)

    acc = jnp.dot(lhs_ref[...].reshape(nb * tt, k), wbf_ref[...], preferred_element_type=F32)
    if has_gate:
        acc = jax.nn.sigmoid(gate_ref[...].reshape(nb * tt, tn).astype(F32)) * acc
    if has_add:
        acc = add_ref[...].reshape(nb * tt, tn).astype(F32) + acc
    out_ref[...] = acc.astype(out_ref.dtype).reshape(nb, tt, tn)


def _mm(lhs, w, *, col0, ncols, tn, tt, out_dtype, gate=None, gate_col0=0, add=None, name):
    nb, tl, k = lhs.shape
    assert tl % tt == 0 and ncols % tn == 0 and col0 % tn == 0 and gate_col0 % tn == 0
    in_specs = [pl.BlockSpec((nb, tt, k), lambda j, i: (0, i, 0)),
                pl.BlockSpec((k, tn), lambda j, i: (0, col0 // tn + j))]
    args = [lhs, w]
    if gate is not None:
        in_specs.append(pl.BlockSpec((nb, tt, tn), lambda j, i: (0, i, gate_col0 // tn + j)))
        args.append(gate)
    if add is not None:
        in_specs.append(pl.BlockSpec((nb, tt, tn), lambda j, i: (0, i, j)))
        args.append(add)
    return pl.pallas_call(
        functools.partial(_mm_kernel, has_gate=gate is not None, has_add=add is not None),
        grid=(ncols // tn, tl // tt),
        in_specs=in_specs,
        out_specs=pl.BlockSpec((nb, tt, tn), lambda j, i: (0, i, j)),
        out_shape=jax.ShapeDtypeStruct((nb, tl, ncols), out_dtype),
        scratch_shapes=[pltpu.VMEM((k, tn), BF16)],
        compiler_params=_cparams(("arbitrary", "arbitrary")),
        name=name,
    )(*args)


def _rglru_kernel(xr_ref, gr_ref, tail0_ref, h0_ref, cw_ref, cb_ref, ab_ref, xb_ref, lam_ref,
                  wa_ref, wx_ref, y_ref, hl_ref, tail_ref, xp_s, h_s, a_s, u_s, hs_s, *, nb, tt, n_null):
    ti = pl.program_id(1)
    rt = tt * nb
    tb = (CONV_W - 1) * nb

    def time_major(x):
        return jnp.swapaxes(x, 0, 1).reshape(rt, x.shape[-1])

    @pl.when(ti == 0)
    def _():
        xp_s[0:tb, :] = tail0_ref[...]
        h_s[...] = h0_ref[...]

    xp_s[tb:tb + rt, :] = time_major(xr_ref[...])
    cw = cw_ref[...]
    xc = cb_ref[...] + cw[0:1, :] * xp_s[0:rt, :]
    for j in range(1, CONV_W):
        xc = xc + cw[j:j + 1, :] * xp_s[j * nb:j * nb + rt, :]
    xcb = xc.astype(BF16)
    r = jax.nn.sigmoid(jnp.dot(xcb, wa_ref[0], preferred_element_type=F32) + ab_ref[...])
    i = jax.nn.sigmoid(jnp.dot(xcb, wx_ref[0], preferred_element_type=F32) + xb_ref[...])
    log_a = (-LRU_C) * r * _softplus(-lam_ref[...])
    a = jnp.exp(log_a)
    a_s[...] = a
    u = jnp.sqrt(-jnp.tanh(log_a) * (a * a + 1.0)) * (i * xc)
    if n_null:
        row = lax.broadcasted_iota(jnp.int32, (rt, 1), 0)
        u = jnp.where((row >= n_null * nb) | (ti > 0), u, 0.0)
    u_s[...] = u

    def step(t, h):
        o = pl.multiple_of(t * nb, nb)
        h = a_s[pl.ds(o, nb), :] * h + u_s[pl.ds(o, nb), :]
        hs_s[pl.ds(o, nb), :] = h
        return h

    h = lax.fori_loop(0, tt, step, h_s[...], unroll=8)
    h_s[...] = h
    g = time_major(gr_ref[...].astype(F32))
    gelu = g * (0.5 * (1.0 + jnp.tanh(0.7978845608028654 * (g + 0.044715 * (g * g * g)))))
    y = gelu * hs_s[...]
    y_ref[...] = jnp.swapaxes(y.reshape(tt, nb, y.shape[-1]), 0, 1).astype(BF16)
    xp_s[0:tb, :] = xp_s[rt:rt + tb, :]

    @pl.when(ti == pl.num_programs(1) - 1)
    def _():
        hl_ref[...] = h
        tail_ref[...] = xp_s[rt:rt + tb, :]


def _rglru(xr, gr, tail0, h0, cw, cb, ab, xb, lam, wa, wx, *, tt, n_null):
    nb, tl, _ = xr.shape
    rt = tt * nb
    tb = (CONV_W - 1) * nb
    blk = lambda s, ti: (0, ti, s)
    col = lambda s, ti: (0, s)
    return pl.pallas_call(
        functools.partial(_rglru_kernel, nb=nb, tt=tt, n_null=n_null),
        grid=(N_SUPER, tl // tt),
        in_specs=[pl.BlockSpec((nb, tt, RNN_SUPER), blk),
                  pl.BlockSpec((nb, tt, RNN_SUPER), blk),
                  pl.BlockSpec((tb, RNN_SUPER), col),
                  pl.BlockSpec((nb, RNN_SUPER), col),
                  pl.BlockSpec((CONV_W, RNN_SUPER), col),
                  pl.BlockSpec((1, RNN_SUPER), col),
                  pl.BlockSpec((1, RNN_SUPER), col),
                  pl.BlockSpec((1, RNN_SUPER), col),
                  pl.BlockSpec((1, RNN_SUPER), col),
                  pl.BlockSpec((1, RNN_SUPER, RNN_SUPER), lambda s, ti: (s, 0, 0)),
                  pl.BlockSpec((1, RNN_SUPER, RNN_SUPER), lambda s, ti: (s, 0, 0))],
        out_specs=[pl.BlockSpec((nb, tt, RNN_SUPER), blk),
                   pl.BlockSpec((nb, RNN_SUPER), col),
                   pl.BlockSpec((tb, RNN_SUPER), col)],
        out_shape=[jax.ShapeDtypeStruct((nb, tl, D_RNN), BF16),
                   jax.ShapeDtypeStruct((nb, D_RNN), F32),
                   jax.ShapeDtypeStruct((tb, D_RNN), F32)],
        scratch_shapes=[pltpu.VMEM((rt + tb, RNN_SUPER), F32),
                        pltpu.VMEM((nb, RNN_SUPER), F32),
                        pltpu.VMEM((rt, RNN_SUPER), F32),
                        pltpu.VMEM((rt, RNN_SUPER), F32),
                        pltpu.VMEM((rt, RNN_SUPER), F32)],
        compiler_params=_cparams(("arbitrary", "arbitrary")),
        name="rglru",
    )(xr, gr, tail0, h0, cw, cb, ab, xb, lam, wa, wx)


def _mlstm_kernel(q_ref, k_ref, v_ref, og_ref, gif_ref, c0_ref, n0_ref, m0_ref, nw_ref,
                  y_ref, co_ref, no_ref, mo_ref, c_s, n_s, m_s, *, L, n_null):
    c = pl.program_id(1)

    @pl.when(c == 0)
    def _():
        c_s[...] = c0_ref[...]
        n_s[...] = n0_ref[...]
        m_s[...] = m0_ref[...]

    g = gif_ref[...]
    lane = lax.broadcasted_iota(jnp.int32, (L, LANES), 1)
    row = lax.broadcasted_iota(jnp.int32, (L, LANES), 0)
    is_f = (lane >= N_HEADS_M) & (lane < 2 * N_HEADS_M)
    lf = jnp.where(is_f, jnp.minimum(g, 0.0) - jnp.log1p(jnp.exp(-jnp.abs(g))), 0.0)
    li = g
    if n_null:
        null = (row < n_null) & (c == 0)
        lf = jnp.where(null, 0.0, lf)
        li = jnp.where(null, NEG_BIG, li)
    rr = lax.broadcasted_iota(jnp.int32, (L, L), 0)
    cc = lax.broadcasted_iota(jnp.int32, (L, L), 1)
    causal = rr >= cc
    tri = causal.astype(F32)
    bfull = jnp.dot(tri, lf, preferred_element_type=F32, precision=lax.Precision.HIGHEST)
    b_t = bfull.T
    li_t = li.T
    m_all = m_s[...]
    m_new = m_all
    lane1 = lax.broadcasted_iota(jnp.int32, (1, LANES), 1)
    scale = DQK_M ** -0.5
    nw = nw_ref[...]
    dn_nt = (((1,), (1,)), ((), ()))
    dn_tn = (((0,), (0,)), ((), ()))
    for h in range(N_HEADS_M):
        bcol = bfull[:, N_HEADS_M + h:N_HEADS_M + h + 1]
        brow = b_t[N_HEADS_M + h:N_HEADS_M + h + 1, :]
        igrow = li_t[h:h + 1, :]
        igcol = li[:, h:h + 1]
        m0h = m_all[:, h:h + 1]
        log_w = jnp.where(causal, bcol - brow + igrow, -jnp.inf)
        log_inter = bcol + m0h
        mvec = jnp.maximum(log_inter, jnp.max(log_w, axis=1, keepdims=True))
        w_intra = jnp.exp(log_w - mvec)
        w_inter = jnp.exp(log_inter - mvec)
        qh = q_ref[:, h * DQK_M:(h + 1) * DQK_M]
        kh = k_ref[:, h * DQK_M:(h + 1) * DQK_M]
        vh = v_ref[:, h * DV_M:(h + 1) * DV_M]
        s = lax.dot_general(qh, kh, dn_nt, preferred_element_type=F32) * (scale * w_intra)
        ch = c_s[h]
        nh = n_s[h:h + 1, :]
        num = (jnp.dot(s.astype(BF16), vh, preferred_element_type=F32)
               + w_inter * jnp.dot(qh, ch.astype(BF16), preferred_element_type=F32))
        den = (jnp.sum(s, axis=1, keepdims=True)
               + w_inter * jnp.sum(qh.astype(F32) * nh, axis=1, keepdims=True))
        denom = jnp.maximum(jnp.abs(den), jnp.exp(-mvec))
        hh = num / denom
        hh = hh * lax.rsqrt(jnp.mean(hh * hh, axis=-1, keepdims=True) + EPS)
        hh = hh * nw[:, h * DV_M:(h + 1) * DV_M]
        og = og_ref[:, h * DV_M:(h + 1) * DV_M].astype(F32)
        y_ref[:, h * DV_M:(h + 1) * DV_M] = (hh * jax.nn.sigmoid(og)).astype(BF16)
        m_end = mvec[L - 1:L, :]
        b_end = bcol[L - 1:L, :]
        w_end = jnp.exp(b_end - bcol + igcol - m_end)
        decay = jnp.exp(b_end + m0h - m_end)
        kw = kh.astype(F32) * (w_end * scale)
        c_s[h] = decay * ch + lax.dot_general(kw.astype(BF16), vh, dn_tn, preferred_element_type=F32)
        n_s[h:h + 1, :] = decay * nh + jnp.sum(kw, axis=0, keepdims=True)
        m_new = jnp.where(lane1 == h, m_end, m_new)
    m_s[...] = m_new

    @pl.when(c == pl.num_programs(1) - 1)
    def _():
        co_ref[...] = c_s[...]
        no_ref[...] = n_s[...]
        mo_ref[...] = m_new


def _mlstm(qkvo, gif, c0, n0, m0, nw, *, L, n_null):
    nb, tl, _ = qkvo.shape
    st = lambda b, c: (b, 0, 0)
    return pl.pallas_call(
        functools.partial(_mlstm_kernel, L=L, n_null=n_null),
        grid=(nb, tl // L),
        in_specs=[pl.BlockSpec((None, L, HQK), lambda b, c: (b, c, 0)),
                  pl.BlockSpec((None, L, HQK), lambda b, c: (b, c, 1)),
                  pl.BlockSpec((None, L, HV), lambda b, c: (b, c, 1)),
                  pl.BlockSpec((None, L, HV), lambda b, c: (b, c, 2)),
                  pl.BlockSpec((None, L, LANES), lambda b, c: (b, c, 0)),
                  pl.BlockSpec((None, N_HEADS_M, DQK_M, DV_M), lambda b, c: (b, 0, 0, 0)),
                  pl.BlockSpec((None, N_HEADS_M, DQK_M), st),
                  pl.BlockSpec((None, 1, LANES), st),
                  pl.BlockSpec((1, HV), lambda b, c: (0, 0))],
        out_specs=[pl.BlockSpec((None, L, HV), lambda b, c: (b, c, 0)),
                   pl.BlockSpec((None, N_HEADS_M, DQK_M, DV_M), lambda b, c: (b, 0, 0, 0)),
                   pl.BlockSpec((None, N_HEADS_M, DQK_M), st),
                   pl.BlockSpec((None, 1, LANES), st)],
        out_shape=[jax.ShapeDtypeStruct((nb, tl, HV), BF16),
                   jax.ShapeDtypeStruct((nb, N_HEADS_M, DQK_M, DV_M), F32),
                   jax.ShapeDtypeStruct((nb, N_HEADS_M, DQK_M), F32),
                   jax.ShapeDtypeStruct((nb, 1, LANES), F32)],
        scratch_shapes=[pltpu.VMEM((N_HEADS_M, DQK_M, DV_M), F32),
                        pltpu.VMEM((N_HEADS_M, DQK_M), F32),
                        pltpu.VMEM((1, LANES), F32)],
        compiler_params=_cparams(("arbitrary", "arbitrary")),
        name="mlstm",
    )(qkvo, qkvo, qkvo, qkvo, gif, c0, n0, m0, nw)


def _route(logits, dense):
    tm = logits.shape[0]
    lane = lax.broadcasted_iota(jnp.int32, (tm, LANES), 1)
    lanef = lane.astype(F32)
    gl = jnp.where(lane < N_GROUPS, logits, -jnp.inf)
    gmax = jnp.max(gl, axis=1, keepdims=True)
    gidx = jnp.min(jnp.where(gl == gmax, lanef, float(LANES)), axis=1, keepdims=True)
    gw = 1.0 / jnp.sum(jnp.exp(gl - gmax), axis=1, keepdims=True)
    lo = float(N_GROUPS) + float(EXPERTS_PER_GROUP) * gidx
    sel = (lanef >= lo) & (lanef < lo + float(EXPERTS_PER_GROUP))
    ev = jnp.where(sel, logits, -jnp.inf)
    v1 = jnp.max(ev, axis=1, keepdims=True)
    i1 = jnp.min(jnp.where(ev == v1, lanef, float(LANES)), axis=1, keepdims=True)
    ev2 = jnp.where(lanef == i1, -jnp.inf, ev)
    v2 = jnp.max(ev2, axis=1, keepdims=True)
    i2 = jnp.min(jnp.where(ev2 == v2, lanef, float(LANES)), axis=1, keepdims=True)
    t = jnp.exp(v2 - v1)
    w1 = gw / (1.0 + t)
    w2 = gw * (t / (1.0 + t))
    if dense:
        return jnp.where(lanef == i1, w1, jnp.where(lanef == i2, w2, 0.0))
    l1 = i1 - lo
    l2 = i2 - lo
    first = l1 < l2
    a = jnp.minimum(l1, l2)
    b = jnp.maximum(l1, l2)
    pair = jnp.where(a == 0.0, b - 1.0, jnp.where(a == 1.0, b + 1.0, 5.0))
    cls = float(PAIRS_PER_GROUP) * gidx + pair
    return jnp.where(lane == 0, cls,
                     jnp.where(lane == 1, jnp.where(first, w1, w2),
                               jnp.where(lane == 2, jnp.where(first, w2, w1), 0.0)))


def _outproj_kernel(mix_ref, w_ref, x_ref, meta_ref, n2_ref, wr_ref, br_ref, x1_ref, xn2_ref, gates_ref,
                    *, nb, tt, has_head, routed):
    def emit(x):
        mix = mix_ref[...].reshape(nb * tt, D_MODEL)
        x1 = x + jnp.dot(mix, w_ref[...], preferred_element_type=F32)
        x1_ref[...] = x1.reshape(nb, tt, D_MODEL)
        xn2 = _rms(x1, n2_ref[...]).astype(BF16)
        logits = jnp.dot(xn2, wr_ref[...], preferred_element_type=F32) + br_ref[...]
        info = _route(logits, dense=not routed)
        gates_ref[...] = info.reshape(nb, tt, LANES)
        if routed:
            xn2_ref[:, :, :D_MODEL] = xn2.astype(F32).reshape(nb, tt, D_MODEL)
            xn2_ref[:, :, D_MODEL:] = info.reshape(nb, tt, LANES)
        else:
            xn2_ref[...] = xn2.reshape(nb, tt, D_MODEL)

    if has_head:
        @pl.when(pl.program_id(0) == 0)
        def _():
            emit(_head_rows(meta_ref, nb))

        @pl.when(pl.program_id(0) > 0)
        def _():
            emit(x_ref[...].reshape(nb * tt, D_MODEL))
    else:
        emit(x_ref[...].reshape(nb * tt, D_MODEL))


def _outproj(mix, w_out, x, meta, n2, w_router, b_router, *, tt, has_head, routed):
    nb, tl, _ = mix.shape
    const = lambda i: (0, 0)
    blk = lambda i: (0, i, 0)
    xn2_w, xn2_dt = (ROW_W, F32) if routed else (D_MODEL, BF16)
    return pl.pallas_call(
        functools.partial(_outproj_kernel, nb=nb, tt=tt, has_head=has_head, routed=routed),
        grid=(tl // tt,),
        in_specs=[pl.BlockSpec((nb, tt, D_MODEL), blk),
                  pl.BlockSpec((D_MODEL, D_MODEL), const, pipeline_mode=pl.Buffered(1)),
                  pl.BlockSpec((nb, tt, D_MODEL), _x_map(has_head)),
                  pl.BlockSpec((N_META, D_MODEL), const),
                  pl.BlockSpec((1, D_MODEL), const),
                  pl.BlockSpec((D_MODEL, LANES), const),
                  pl.BlockSpec((1, LANES), const)],
        out_specs=[pl.BlockSpec((nb, tt, D_MODEL), blk),
                   pl.BlockSpec((nb, tt, xn2_w), blk),
                   pl.BlockSpec((nb, tt, LANES), blk)],
        out_shape=[jax.ShapeDtypeStruct((nb, tl, D_MODEL), F32),
                   jax.ShapeDtypeStruct((nb, tl, xn2_w), xn2_dt),
                   jax.ShapeDtypeStruct((nb, tl, LANES), F32)],
        compiler_params=_cparams(("arbitrary",)),
        name="outproj",
    )(mix, w_out, x, meta, n2, w_router, b_router)


def _moe_kernel(xn_ref, gates_ref, x1_ref, wg_ref, wu_ref, wd_ref, nf_ref, y_ref, acc_s, *, nb, tt):
    e = pl.program_id(1)
    rows = nb * tt

    @pl.when(e == 0)
    def _():
        acc_s[...] = jnp.zeros_like(acc_s)

    x = xn_ref[...].reshape(rows, D_MODEL)
    hg = jnp.dot(x, wg_ref[0], preferred_element_type=F32)
    hu = jnp.dot(x, wu_ref[0], preferred_element_type=F32)
    hdn = (hg * jax.nn.sigmoid(hg)) * hu
    yd = jnp.dot(hdn.astype(BF16), wd_ref[0], preferred_element_type=F32)
    gates = gates_ref[...].reshape(rows, LANES)
    lane = lax.broadcasted_iota(jnp.int32, gates.shape, 1)
    gcol = jnp.sum(jnp.where(lane == e + N_GROUPS, gates, 0.0), axis=1, keepdims=True)
    acc_s[...] += gcol * yd

    @pl.when(e == pl.num_programs(1) - 1)
    def _():
        x2 = x1_ref[...].reshape(rows, D_MODEL) + acc_s[...]
        y_ref[...] = _rms(x2, nf_ref[...]).reshape(nb, tt, D_MODEL)


def _moe(xn2, gates, x1, wg, wu, wd, nf, *, tt, has_head):
    nb, tl, _ = xn2.shape
    blk = lambda i, e: (0, i, 0)
    wblk = lambda i, e: (e, 0, 0)
    if has_head:
        out_map = lambda i, e: (0, jnp.maximum(i - 1, 0), 0)
        seq = tl - tt
    else:
        out_map = blk
        seq = tl
    return pl.pallas_call(
        functools.partial(_moe_kernel, nb=nb, tt=tt),
        grid=(tl // tt, N_EXPERTS),
        in_specs=[pl.BlockSpec((nb, tt, D_MODEL), blk),
                  pl.BlockSpec((nb, tt, LANES), blk),
                  pl.BlockSpec((nb, tt, D_MODEL), blk),
                  pl.BlockSpec((1, D_MODEL, D_EXPERT), wblk),
                  pl.BlockSpec((1, D_MODEL, D_EXPERT), wblk),
                  pl.BlockSpec((1, D_EXPERT, D_MODEL), wblk),
                  pl.BlockSpec((1, D_MODEL), lambda i, e: (0, 0))],
        out_specs=pl.BlockSpec((nb, tt, D_MODEL), out_map),
        out_shape=jax.ShapeDtypeStruct((nb, seq, D_MODEL), F32),
        scratch_shapes=[pltpu.VMEM((nb * tt, D_MODEL), F32)],
        compiler_params=_cparams(("arbitrary", "arbitrary")),
        name="moe",
    )(xn2, gates, x1, wg, wu, wd, nf)


_NT = (((1,), (1,)), ((), ()))


def _plan_kernel(ri_ref, dest_ref, tinfo_ref, cnt_s, carry_s, offs_s, *, rows):
    ph = pl.program_id(0)
    i = pl.program_id(1)
    ri = ri_ref[...].reshape(rows, LANES)
    lanef = lax.broadcasted_iota(jnp.int32, (rows, LANES), 1).astype(F32)
    oh = jnp.where(lanef == ri[:, 0:1], 1.0, 0.0)
    colsum = jnp.sum(oh, axis=0, keepdims=True)
    r128 = lax.broadcasted_iota(jnp.int32, (LANES, LANES), 0)
    c128 = lax.broadcasted_iota(jnp.int32, (LANES, LANES), 1)
    eye = jnp.where(r128 == c128, 1.0, 0.0).astype(BF16)

    @pl.when((ph == 0) & (i == 0))
    def _():
        cnt_s[...] = jnp.zeros_like(cnt_s)

    @pl.when(ph == 0)
    def _():
        cnt_s[...] += colsum

    @pl.when((ph == 1) & (i == 0))
    def _():
        tiles = jnp.floor((cnt_s[...] + float(TM_E - 1)) * (1.0 / TM_E))
        upper = jnp.where(r128 <= c128, 1.0, 0.0).astype(BF16)
        cum_incl = jnp.dot(jnp.broadcast_to(tiles, (8, LANES)).astype(BF16), upper, preferred_element_type=F32)
        offs_s[...] = (cum_incl[0:1, :] - tiles) * float(TM_E)
        carry_s[...] = jnp.zeros_like(carry_s)
        cum_col = lax.dot_general(eye, cum_incl.astype(BF16), _NT, preferred_element_type=F32)
        ended = (cum_col[:, 0:1] <= c128.astype(F32)) & (r128 < N_CLASSES)
        tcls = jnp.sum(jnp.where(ended, 1.0, 0.0), axis=0, keepdims=True)
        total = cum_incl[0:1, LANES - 1:LANES]
        row8 = lax.broadcasted_iota(jnp.int32, (8, LANES), 0)
        tinfo_ref[...] = jnp.where(row8 == 0, tcls, total).astype(jnp.int32)

    @pl.when(ph == 1)
    def _():
        tri = jnp.where(lax.broadcasted_iota(jnp.int32, (rows, rows), 0)
                        > lax.broadcasted_iota(jnp.int32, (rows, rows), 1), 1.0, 0.0).astype(BF16)
        rank = jnp.dot(tri, oh.astype(BF16), preferred_element_type=F32)
        dest = jnp.sum(oh * (rank + (offs_s[...] + carry_s[...])), axis=1, keepdims=True)
        carry_s[...] += colsum
        dhi = jnp.floor(dest * (1.0 / 256.0))
        dlo = dest - 256.0 * dhi
        digits = jnp.where(lanef == 0.0, dlo, jnp.where(lanef == 1.0, dhi, 0.0)).astype(BF16)
        dt = lax.dot_general(eye, digits, _NT, preferred_element_type=F32)
        dest_ref[...] = (dt[0:1, :] + 256.0 * dt[1:2, :]).astype(jnp.int32).reshape(1, 1, rows)


def _plan(rinfo, *, tt, skip):
    nb, tl, _ = rinfo.shape
    rows = nb * tt
    nt = tl // tt - skip
    assert (nt * rows) // TM_E + N_CLASSES <= LANES and rows % LANES == 0
    return pl.pallas_call(
        functools.partial(_plan_kernel, rows=rows),
        grid=(2, nt),
        in_specs=[pl.BlockSpec((nb, tt, LANES), lambda ph, i: (0, i + skip, 0))],
        out_specs=[pl.BlockSpec((1, 1, rows), lambda ph, i: (i * ph, 0, 0)),
                   pl.BlockSpec((8, LANES), lambda ph, i: (0, 0))],
        out_shape=[jax.ShapeDtypeStruct((nt, 1, rows), jnp.int32),
                   jax.ShapeDtypeStruct((8, LANES), jnp.int32)],
        scratch_shapes=[pltpu.VMEM((1, LANES), F32)] * 3,
        compiler_params=_cparams(("arbitrary", "arbitrary")),
        name="moe_plan",
    )(rinfo)


def _dispatch_kernel(tinfo_ref, dest_ref, x_ref, xs_ref, zbuf, zsem, sem, *, nb, tt):
    i = pl.program_id(0)

    @pl.when(i == 0)
    def _():
        zbuf[...] = jnp.zeros_like(zbuf)
        n_used = tinfo_ref[1, 0]

        def fill(start):
            def body(j, carry):
                last = (j >= n_used - 1) | (tinfo_ref[0, jnp.minimum(j + 1, LANES - 1)] != tinfo_ref[0, j])

                @pl.when(last)
                def _():
                    cp = pltpu.make_async_copy(
                        zbuf, xs_ref.at[pl.ds(pl.multiple_of(j * TM_E, TM_E), TM_E), :], zsem)
                    if start:
                        cp.start()
                    else:
                        cp.wait()
                return carry
            lax.fori_loop(0, xs_ref.shape[0] // TM_E, body, 0)

        fill(True)
        fill(False)

    def rows(start):
        for b in range(nb):
            def body(t, carry):
                d = dest_ref[0, 0, b * tt + t]
                cp = pltpu.make_async_copy(x_ref.at[b, pl.ds(t, 1), :], xs_ref.at[pl.ds(d, 1), :], sem)
                if start:
                    cp.start()
                else:
                    cp.wait()
                return carry
            lax.fori_loop(0, tt, body, 0, unroll=8)

    rows(True)
    rows(False)


def _dispatch(tinfo, dest, xrow, *, tt, skip):
    nb, tl, _ = xrow.shape
    nt = tl // tt - skip
    n_sorted = ((nt * nb * tt) // TM_E + N_CLASSES) * TM_E
    return pl.pallas_call(
        functools.partial(_dispatch_kernel, nb=nb, tt=tt),
        grid_spec=pltpu.PrefetchScalarGridSpec(
            num_scalar_prefetch=1,
            grid=(nt,),
            in_specs=[pl.BlockSpec((1, 1, nb * tt), lambda i, ti: (i, 0, 0), memory_space=pltpu.SMEM),
                      pl.BlockSpec((nb, tt, ROW_W), lambda i, ti: (0, i + skip, 0))],
            out_specs=pl.BlockSpec(memory_space=pl.ANY),
            scratch_shapes=[pltpu.VMEM((TM_E, ROW_W), F32),
                            pltpu.SemaphoreType.DMA(()),
                            pltpu.SemaphoreType.DMA(())]),
        out_shape=jax.ShapeDtypeStruct((n_sorted, ROW_W), F32),
        compiler_params=_cparams(("arbitrary",)),
        name="moe_dispatch",
    )(tinfo, dest, xrow)


def _class_expert(c, hi):
    g = c // PAIRS_PER_GROUP
    p = c - PAIRS_PER_GROUP * g
    ge3 = (p >= 3).astype(jnp.int32)
    ge5 = (p >= 5).astype(jnp.int32)
    local = (p + 1 - 2 * ge3 - ge5) if hi else (ge3 + ge5)
    return EXPERTS_PER_GROUP * g + local


def _experts_kernel(tinfo_ref, xs_ref, wga_ref, wua_ref, wda_ref, wgb_ref, wub_ref, wdb_ref, ys_ref):
    @pl.when(pl.program_id(0) < tinfo_ref[1, 0])
    def _():
        x = xs_ref[:, :D_MODEL].astype(BF16)

        def ffn(wg_ref, wu_ref, wd_ref):
            hg = jnp.dot(x, wg_ref[0], preferred_element_type=F32)
            hu = jnp.dot(x, wu_ref[0], preferred_element_type=F32)
            hdn = (hg * jax.nn.sigmoid(hg)) * hu
            return jnp.dot(hdn.astype(BF16), wd_ref[0], preferred_element_type=F32)

        ys_ref[...] = (xs_ref[:, D_MODEL + 1:D_MODEL + 2] * ffn(wga_ref, wua_ref, wda_ref)
                       + xs_ref[:, D_MODEL + 2:D_MODEL + 3] * ffn(wgb_ref, wub_ref, wdb_ref))

    @pl.when(pl.program_id(0) >= tinfo_ref[1, 0])
    def _():
        ys_ref[...] = jnp.zeros_like(ys_ref)


def _experts(tinfo, xs, wg, wu, wd):
    n_sorted = xs.shape[0]
    nt = n_sorted // TM_E

    def tile(j, ti):
        return jnp.minimum(j, ti[1, 0] - 1)

    def wmap(hi):
        return lambda j, ti: (_class_expert(ti[0, tile(j, ti)], hi), 0, 0)

    up = lambda hi: pl.BlockSpec((1, D_MODEL, D_EXPERT), wmap(hi))
    down = lambda hi: pl.BlockSpec((1, D_EXPERT, D_MODEL), wmap(hi))
    return pl.pallas_call(
        _experts_kernel,
        grid_spec=pltpu.PrefetchScalarGridSpec(
            num_scalar_prefetch=1,
            grid=(nt,),
            in_specs=[pl.BlockSpec((TM_E, ROW_W), lambda j, ti: (tile(j, ti), 0)),
                      up(0), up(0), down(0), up(1), up(1), down(1)],
            out_specs=pl.BlockSpec((TM_E, D_MODEL), lambda j, ti: (j, 0))),
        out_shape=jax.ShapeDtypeStruct((n_sorted, D_MODEL), F32),
        compiler_params=_cparams(("arbitrary",)),
        name="moe_experts",
    )(tinfo, xs, wg, wu, wd, wg, wu, wd)


def _combine_kernel(dcur_ref, dnxt_ref, x1_ref, nf_ref, ys_ref, y_ref, buf, sem, *, nb, tt):
    i = pl.program_id(0)
    slot = i % 2

    def gather(dref, s, start):
        for b in range(nb):
            def body(t, carry):
                d = dref[0, 0, b * tt + t]
                cp = pltpu.make_async_copy(ys_ref.at[pl.ds(d, 1), :], buf.at[s, b, pl.ds(t, 1), :], sem.at[s])
                if start:
                    cp.start()
                else:
                    cp.wait()
                return carry
            lax.fori_loop(0, tt, body, 0, unroll=8)

    @pl.when(i == 0)
    def _():
        gather(dcur_ref, 0, True)

    @pl.when(i + 1 < pl.num_programs(0))
    def _():
        gather(dnxt_ref, 1 - slot, True)

    gather(dcur_ref, slot, False)
    x2 = x1_ref[...].reshape(nb * tt, D_MODEL) + buf[slot].reshape(nb * tt, D_MODEL)
    y_ref[...] = _rms(x2, nf_ref[...]).reshape(nb, tt, D_MODEL)


def _combine(dest, x1, nf, ys, *, tt, skip):
    nb, tl, _ = x1.shape
    nt = tl // tt - skip
    dspec = lambda f: pl.BlockSpec((1, 1, nb * tt), f, memory_space=pltpu.SMEM)
    return pl.pallas_call(
        functools.partial(_combine_kernel, nb=nb, tt=tt),
        grid=(nt,),
        in_specs=[dspec(lambda i: (i, 0, 0)),
                  dspec(lambda i: (jnp.minimum(i + 1, nt - 1), 0, 0)),
                  pl.BlockSpec((nb, tt, D_MODEL), lambda i: (0, i + skip, 0)),
                  pl.BlockSpec((1, D_MODEL), lambda i: (0, 0)),
                  pl.BlockSpec(memory_space=pl.ANY)],
        out_specs=pl.BlockSpec((nb, tt, D_MODEL), lambda i: (0, i, 0)),
        out_shape=jax.ShapeDtypeStruct((nb, nt * tt, D_MODEL), F32),
        scratch_shapes=[pltpu.VMEM((2, nb, tt, D_MODEL), F32),
                        pltpu.SemaphoreType.DMA((2,))],
        compiler_params=_cparams(("arbitrary",)),
        name="moe_combine",
    )(dest, dest, x1, nf, ys)


def _block_diag(w):
    per = RNN_SUPER // RNN_BLOCK
    w4 = w.reshape(N_SUPER, per, RNN_BLOCK, RNN_BLOCK)
    eye = jnp.eye(per, dtype=w.dtype)
    bd = jnp.einsum('spij,pq->spiqj', w4, eye)
    return bd.reshape(N_SUPER, RNN_SUPER, RNN_SUPER).astype(BF16)


def _stream(x, meta, conv0, h0, c0, n0, m0, p, *, has_head):
    nb, seq, _ = x.shape
    n_null = N_NULL if has_head else 0
    tt = CHUNK if has_head else seq
    xn, gif = _prep(x, meta, p['norm1_w'], p['w_gif'], p['b_gif'], has_head=has_head)
    w_in = p['w_in']
    xr = _mm(xn, w_in, col0=0, ncols=D_RNN, tn=1280, tt=tt, out_dtype=F32, name="in_xr")
    gr = _mm(xn, w_in, col0=D_RNN, ncols=D_RNN, tn=1280, tt=tt, out_dtype=BF16, name="in_gr")
    qkvo = _mm(xn, w_in, col0=2 * D_RNN, ncols=QKVO, tn=1024, tt=tt, out_dtype=BF16, name="in_qkvo")
    mg = _mm(xn, p['w_mg'], col0=0, ncols=2 * D_MODEL, tn=1024, tt=tt, out_dtype=BF16, name="in_mg")

    tail0 = jnp.transpose(conv0, (1, 0, 2)).reshape((CONV_W - 1) * nb, D_RNN)
    y_r, h_new, tail = _rglru(xr, gr, tail0, h0, p['conv_w'], p['conv_b'], p['rg_a_b'], p['rg_x_b'],
                              p['rg_lambda'], p['wa_bd'], p['wx_bd'], tt=tt, n_null=n_null)
    conv_new = jnp.transpose(tail.reshape(CONV_W - 1, nb, D_RNN), (1, 0, 2))

    m0p = jnp.pad(m0, ((0, 0), (0, LANES - N_HEADS_M))).reshape(nb, 1, LANES)
    y_m, c_new, n_new, m_new = _mlstm(qkvo, gif, c0, n0, m0p, p['mlstm_norm_w'], L=tt, n_null=n_null)
    m_new = m_new[:, 0, :N_HEADS_M]

    p_r = _mm(y_r, p['w_proj_rnn'], col0=0, ncols=D_MODEL, tn=1024, tt=tt, out_dtype=F32,
              gate=mg, gate_col0=0, name="proj_rnn")
    mix = _mm(y_m, p['w_proj_mlstm'], col0=0, ncols=D_MODEL, tn=1024, tt=tt, out_dtype=BF16,
              gate=mg, gate_col0=D_MODEL, add=p_r, name="proj_mlstm")
    routed = has_head
    x1, xn2, gates = _outproj(mix, p['w_out'], x, meta, p['norm2_w'], p['w_router'], p['b_router'],
                              tt=tt, has_head=has_head, routed=routed)
    if routed:
        skip = 1
        dest, tinfo = _plan(gates, tt=tt, skip=skip)
        xs = _dispatch(tinfo, dest, xn2, tt=tt, skip=skip)
        ys = _experts(tinfo, xs, p['w_exp_gate'], p['w_exp_up'], p['w_exp_down'])
        y = _combine(dest, x1, p['norm_f_w'], ys, tt=tt, skip=skip)
    else:
        y = _moe(xn2, gates, x1, p['w_exp_gate'], p['w_exp_up'], p['w_exp_down'], p['norm_f_w'],
                 tt=tt, has_head=has_head)
    return y, (conv_new[None], h_new[None], c_new[None], n_new[None], m_new[None])


def kernel(x_prompt, x_sample, state_rglru_conv, state_rglru_h, state_mlstm_C, state_mlstm_n, state_mlstm_m, meta_tokens, norm1_w, w_in, b_gates, conv_w, conv_b, rg_a_w, rg_a_b, rg_x_w, rg_x_b, rg_lambda, mlstm_norm_w, w_proj_rnn, w_proj_mlstm, w_out, norm2_w, w_router_group, b_router_group, w_router_expert, b_router_expert, w_exp_gate, w_exp_up, w_exp_down, norm_f_w):
    l = 0
    w_in_l = w_in[l]
    pad_r = LANES - N_GROUPS - N_EXPERTS
    p = dict(
        norm1_w=norm1_w[l][None], w_in=w_in_l,
        w_gif=jnp.pad(w_in_l[:, GIF_OFF:MG_OFF], ((0, 0), (0, LANES - 2 * N_HEADS_M))).astype(BF16),
        b_gif=jnp.pad(b_gates[l], (0, LANES - 2 * N_HEADS_M))[None],
        w_mg=w_in_l[:, MG_OFF:],
        conv_w=conv_w[l], conv_b=conv_b[l][None], rg_a_b=rg_a_b[l][None], rg_x_b=rg_x_b[l][None],
        rg_lambda=rg_lambda[l][None], wa_bd=_block_diag(rg_a_w[l]), wx_bd=_block_diag(rg_x_w[l]),
        mlstm_norm_w=mlstm_norm_w[l][None],
        w_proj_rnn=w_proj_rnn[l], w_proj_mlstm=w_proj_mlstm[l], w_out=w_out[l].astype(BF16),
        norm2_w=norm2_w[l][None],
        w_router=jnp.pad(jnp.concatenate([w_router_group[l], w_router_expert[l]], axis=1),
                         ((0, 0), (0, pad_r))).astype(BF16),
        b_router=jnp.pad(jnp.concatenate([b_router_group[l], b_router_expert[l]]), (0, pad_r))[None],
        w_exp_gate=w_exp_gate[l].astype(BF16), w_exp_up=w_exp_up[l].astype(BF16),
        w_exp_down=w_exp_down[l].astype(BF16), norm_f_w=norm_f_w[None],
    )
    nbp = x_prompt.shape[0]
    dt = x_prompt.dtype
    y_p, st_p = _stream(
        x_prompt, meta_tokens,
        jnp.zeros((nbp, CONV_W - 1, D_RNN), dt), jnp.zeros((nbp, D_RNN), F32),
        jnp.zeros((nbp, N_HEADS_M, DQK_M, DV_M), F32), jnp.zeros((nbp, N_HEADS_M, DQK_M), F32),
        jnp.zeros((nbp, N_HEADS_M), F32), p, has_head=True)
    y_s, st_s = _stream(
        x_sample, meta_tokens, state_rglru_conv[l], state_rglru_h[l], state_mlstm_C[l],
        state_mlstm_n[l], state_mlstm_m[l], p, has_head=False)
    return (y_p, y_s) + st_p + st_s
```

```python
import functools

import jax
import jax.numpy as jnp
from jax import lax
from jax.experimental import pallas as pl
from jax.experimental.pallas import tpu as pltpu

F32 = jnp.float32
BF16 = jnp.bfloat16

D_MODEL = 2048
CHUNK = 64
N_META = 16
N_NULL = CHUNK - N_META
D_RNN = 2560
N_RNN_BLOCKS = 16
RNN_BLOCK = D_RNN // N_RNN_BLOCKS
RNN_SUPER = 640
N_SUPER = D_RNN // RNN_SUPER
CONV_W = 4
LRU_C = 8.0
N_HEADS_M = 8
DV_M = D_MODEL // N_HEADS_M
DQK_M = DV_M // 2
HQK = N_HEADS_M * DQK_M
HV = N_HEADS_M * DV_M
QKVO = 2 * HQK + 2 * HV
N_GROUPS = 4
EXPERTS_PER_GROUP = 4
N_EXPERTS = 16
D_EXPERT = 512
PAIRS_PER_GROUP = 6
N_CLASSES = N_GROUPS * PAIRS_PER_GROUP
MM_TILES = 3
MLSTM_NBK = 2
TM_E = 256
ROW_W = D_MODEL + 128
EPS = 1e-6
LANES = 128
GIF_OFF = 2 * D_RNN + QKVO
MG_OFF = GIF_OFF + 2 * N_HEADS_M
VMEM_LIMIT = 56 * 1024 * 1024
NEG_BIG = -1e30


def _cparams(sem):
    return pltpu.CompilerParams(dimension_semantics=sem, vmem_limit_bytes=VMEM_LIMIT)


def _rms(x, w):
    return x * lax.rsqrt(jnp.mean(x * x, axis=-1, keepdims=True) + EPS) * w


def _softplus(x):
    return jnp.maximum(x, 0.0) + jnp.log1p(jnp.exp(-jnp.abs(x)))


def _head_rows(meta_ref, nb):
    head = jnp.concatenate([jnp.zeros((N_NULL, D_MODEL), F32), meta_ref[...]], axis=0)
    return jnp.broadcast_to(head[None], (nb, CHUNK, D_MODEL)).reshape(nb * CHUNK, D_MODEL)


def _x_map(has_head):
    if has_head:
        return lambda ti: (0, jnp.maximum(ti - 1, 0), 0)
    return lambda ti: (0, ti, 0)


def _prep_kernel(x_ref, meta_ref, nw_ref, wg_ref, bg_ref, xn_ref, gif_ref, *, nb, tt, has_head):
    def emit(x):
        xnb = _rms(x, nw_ref[...]).astype(BF16)
        xn_ref[...] = xnb.reshape(nb, tt, D_MODEL)
        gif = jnp.dot(xnb, wg_ref[...], preferred_element_type=F32) + bg_ref[...]
        gif_ref[...] = gif.reshape(nb, tt, LANES)

    if has_head:
        @pl.when(pl.program_id(0) == 0)
        def _():
            emit(_head_rows(meta_ref, nb))

        @pl.when(pl.program_id(0) > 0)
        def _():
            emit(x_ref[...].reshape(nb * tt, D_MODEL))
    else:
        emit(x_ref[...].reshape(nb * tt, D_MODEL))


def _prep(x, meta, norm_w, w_gif, b_gif, *, has_head):
    nb, seq, _ = x.shape
    tt = CHUNK if has_head else seq
    n_tiles = seq // tt + (1 if has_head else 0)
    tl = n_tiles * tt
    const = lambda ti: (0, 0)
    return pl.pallas_call(
        functools.partial(_prep_kernel, nb=nb, tt=tt, has_head=has_head),
        grid=(n_tiles,),
        in_specs=[pl.BlockSpec((nb, tt, D_MODEL), _x_map(has_head)),
                  pl.BlockSpec((N_META, D_MODEL), const),
                  pl.BlockSpec((1, D_MODEL), const),
                  pl.BlockSpec((D_MODEL, LANES), const),
                  pl.BlockSpec((1, LANES), const)],
        out_specs=[pl.BlockSpec((nb, tt, D_MODEL), lambda ti: (0, ti, 0)),
                   pl.BlockSpec((nb, tt, LANES), lambda ti: (0, ti, 0))],
        out_shape=[jax.ShapeDtypeStruct((nb, tl, D_MODEL), BF16),
                   jax.ShapeDtypeStruct((nb, tl, LANES), F32)],
        compiler_params=_cparams(("arbitrary",)),
        name="prep",
    )(x, meta, norm_w, w_gif, b_gif)


def _mm_kernel(*refs, has_gate, has_add, w_t):
    lhs_ref, w_ref = refs[0], refs[1]
    pos = 2
    gate_ref = add_ref = None
    if has_gate:
        gate_ref = refs[pos]
        pos += 1
    if has_add:
        add_ref = refs[pos]
        pos += 1
    out_ref, wbf_ref = refs[pos], refs[pos + 1]
    nb, tt, k = lhs_ref.shape
    tn = out_ref.shape[-1]

    @pl.when(pl.program_id(1) == 0)
    def _():
        w = w_ref[...]
        wbf_ref[...] = (w.T if w_t else w).astype(BF16)

    acc = jnp.dot(lhs_ref[...].reshape(nb * tt, k), wbf_ref[...], preferred_element_type=F32)
    if has_gate:
        acc = jax.nn.sigmoid(gate_ref[...].reshape(nb * tt, tn).astype(F32)) * acc
    if has_add:
        acc = add_ref[...].reshape(nb * tt, tn).astype(F32) + acc
    out_ref[...] = acc.astype(out_ref.dtype).reshape(nb, tt, tn)


def _mm(lhs, w, *, col0, ncols, tn, tt, out_dtype, gate=None, gate_col0=0, add=None, w_t=False, name):
    nb, tl, k = lhs.shape
    assert tl % tt == 0 and ncols % tn == 0 and col0 % tn == 0 and gate_col0 % tn == 0
    if w_t:
        w_spec = pl.BlockSpec((tn, k), lambda j, i: (col0 // tn + j, 0))
    else:
        w_spec = pl.BlockSpec((k, tn), lambda j, i: (0, col0 // tn + j))
    in_specs = [pl.BlockSpec((nb, tt, k), lambda j, i: (0, i, 0)), w_spec]
    args = [lhs, w]
    if gate is not None:
        in_specs.append(pl.BlockSpec((nb, tt, tn), lambda j, i: (0, i, gate_col0 // tn + j)))
        args.append(gate)
    if add is not None:
        in_specs.append(pl.BlockSpec((nb, tt, tn), lambda j, i: (0, i, j)))
        args.append(add)
    return pl.pallas_call(
        functools.partial(_mm_kernel, has_gate=gate is not None, has_add=add is not None, w_t=w_t),
        grid=(ncols // tn, tl // tt),
        in_specs=in_specs,
        out_specs=pl.BlockSpec((nb, tt, tn), lambda j, i: (0, i, j)),
        out_shape=jax.ShapeDtypeStruct((nb, tl, ncols), out_dtype),
        scratch_shapes=[pltpu.VMEM((k, tn), BF16)],
        compiler_params=_cparams(("arbitrary", "arbitrary")),
        name=name,
    )(*args)


def _rglru_kernel(xr_ref, gr_ref, tail0_ref, h0_ref, cw_ref, cb_ref, ab_ref, xb_ref, lam_ref,
                  wa_ref, wx_ref, y_ref, hl_ref, tail_ref, xp_s, h_s, a_s, u_s, hs_s, *, nb, tt, n_null):
    ti = pl.program_id(1)
    rt = tt * nb
    tb = (CONV_W - 1) * nb

    def time_major(x):
        return jnp.swapaxes(x, 0, 1).reshape(rt, x.shape[-1])

    @pl.when(ti == 0)
    def _():
        xp_s[0:tb, :] = tail0_ref[...]
        h_s[...] = h0_ref[...]

    xp_s[tb:tb + rt, :] = time_major(xr_ref[...])
    cw = cw_ref[...]
    xc = cb_ref[...] + cw[0:1, :] * xp_s[0:rt, :]
    for j in range(1, CONV_W):
        xc = xc + cw[j:j + 1, :] * xp_s[j * nb:j * nb + rt, :]
    xcb = xc.astype(BF16)
    r = jax.nn.sigmoid(jnp.dot(xcb, wa_ref[0], preferred_element_type=F32) + ab_ref[...])
    i = jax.nn.sigmoid(jnp.dot(xcb, wx_ref[0], preferred_element_type=F32) + xb_ref[...])
    log_a = (-LRU_C) * r * _softplus(-lam_ref[...])
    a = jnp.exp(log_a)
    a_s[...] = a
    u = jnp.sqrt(-jnp.tanh(log_a) * (a * a + 1.0)) * (i * xc)
    if n_null:
        row = lax.broadcasted_iota(jnp.int32, (rt, 1), 0)
        u = jnp.where((row >= n_null * nb) | (ti > 0), u, 0.0)
    u_s[...] = u

    def step(t, h):
        o = pl.multiple_of(t * nb, nb)
        h = a_s[pl.ds(o, nb), :] * h + u_s[pl.ds(o, nb), :]
        hs_s[pl.ds(o, nb), :] = h
        return h

    h = lax.fori_loop(0, tt, step, h_s[...], unroll=8)
    h_s[...] = h
    g = time_major(gr_ref[...].astype(F32))
    gelu = g * (0.5 * (1.0 + jnp.tanh(0.7978845608028654 * (g + 0.044715 * (g * g * g)))))
    y = gelu * hs_s[...]
    y_ref[...] = jnp.swapaxes(y.reshape(tt, nb, y.shape[-1]), 0, 1).astype(BF16)
    xp_s[0:tb, :] = xp_s[rt:rt + tb, :]

    @pl.when(ti == pl.num_programs(1) - 1)
    def _():
        hl_ref[...] = h
        tail_ref[...] = xp_s[rt:rt + tb, :]


def _rglru(xr, gr, tail0, h0, cw, cb, ab, xb, lam, wa, wx, *, tt, n_null):
    nb, tl, _ = xr.shape
    rt = tt * nb
    tb = (CONV_W - 1) * nb
    blk = lambda s, ti: (0, ti, s)
    col = lambda s, ti: (0, s)
    return pl.pallas_call(
        functools.partial(_rglru_kernel, nb=nb, tt=tt, n_null=n_null),
        grid=(N_SUPER, tl // tt),
        in_specs=[pl.BlockSpec((nb, tt, RNN_SUPER), blk),
                  pl.BlockSpec((nb, tt, RNN_SUPER), blk),
                  pl.BlockSpec((tb, RNN_SUPER), col),
                  pl.BlockSpec((nb, RNN_SUPER), col),
                  pl.BlockSpec((CONV_W, RNN_SUPER), col),
                  pl.BlockSpec((1, RNN_SUPER), col),
                  pl.BlockSpec((1, RNN_SUPER), col),
                  pl.BlockSpec((1, RNN_SUPER), col),
                  pl.BlockSpec((1, RNN_SUPER), col),
                  pl.BlockSpec((1, RNN_SUPER, RNN_SUPER), lambda s, ti: (s, 0, 0)),
                  pl.BlockSpec((1, RNN_SUPER, RNN_SUPER), lambda s, ti: (s, 0, 0))],
        out_specs=[pl.BlockSpec((nb, tt, RNN_SUPER), blk),
                   pl.BlockSpec((nb, RNN_SUPER), col),
                   pl.BlockSpec((tb, RNN_SUPER), col)],
        out_shape=[jax.ShapeDtypeStruct((nb, tl, D_RNN), BF16),
                   jax.ShapeDtypeStruct((nb, D_RNN), F32),
                   jax.ShapeDtypeStruct((tb, D_RNN), F32)],
        scratch_shapes=[pltpu.VMEM((rt + tb, RNN_SUPER), F32),
                        pltpu.VMEM((nb, RNN_SUPER), F32),
                        pltpu.VMEM((rt, RNN_SUPER), F32),
                        pltpu.VMEM((rt, RNN_SUPER), F32),
                        pltpu.VMEM((rt, RNN_SUPER), F32)],
        compiler_params=_cparams(("arbitrary", "arbitrary")),
        name="rglru",
    )(xr, gr, tail0, h0, cw, cb, ab, xb, lam, wa, wx)


def _mlstm_batch(c, q_ref, k_ref, v_ref, og_ref, gif_ref, nw_ref, y_ref, c_s, n_s, m_s, *, L, n_null):
    g = gif_ref[...]
    lane = lax.broadcasted_iota(jnp.int32, (L, LANES), 1)
    row = lax.broadcasted_iota(jnp.int32, (L, LANES), 0)
    is_f = (lane >= N_HEADS_M) & (lane < 2 * N_HEADS_M)
    lf = jnp.where(is_f, jnp.minimum(g, 0.0) - jnp.log1p(jnp.exp(-jnp.abs(g))), 0.0)
    li = g
    if n_null:
        null = (row < n_null) & (c == 0)
        lf = jnp.where(null, 0.0, lf)
        li = jnp.where(null, NEG_BIG, li)
    rr = lax.broadcasted_iota(jnp.int32, (L, L), 0)
    cc = lax.broadcasted_iota(jnp.int32, (L, L), 1)
    causal = rr >= cc
    tri = causal.astype(F32)
    bfull = jnp.dot(tri, lf, preferred_element_type=F32, precision=lax.Precision.HIGHEST)
    b_t = bfull.T
    li_t = li.T
    m_all = m_s[...]
    m_new = m_all
    lane1 = lax.broadcasted_iota(jnp.int32, (1, LANES), 1)
    scale = DQK_M ** -0.5
    nw = nw_ref[...]
    dn_nt = (((1,), (1,)), ((), ()))
    dn_tn = (((0,), (0,)), ((), ()))
    for h in range(N_HEADS_M):
        bcol = bfull[:, N_HEADS_M + h:N_HEADS_M + h + 1]
        brow = b_t[N_HEADS_M + h:N_HEADS_M + h + 1, :]
        igrow = li_t[h:h + 1, :]
        igcol = li[:, h:h + 1]
        m0h = m_all[:, h:h + 1]
        log_w = jnp.where(causal, bcol - brow + igrow, -jnp.inf)
        log_inter = bcol + m0h
        mvec = jnp.maximum(log_inter, jnp.max(log_w, axis=1, keepdims=True))
        w_intra = jnp.exp(log_w - mvec)
        w_inter = jnp.exp(log_inter - mvec)
        qh = q_ref[:, h * DQK_M:(h + 1) * DQK_M]
        kh = k_ref[:, h * DQK_M:(h + 1) * DQK_M]
        vh = v_ref[:, h * DV_M:(h + 1) * DV_M]
        s = lax.dot_general(qh, kh, dn_nt, preferred_element_type=F32) * (scale * w_intra)
        ch = c_s[h]
        nh = n_s[h:h + 1, :]
        num = (jnp.dot(s.astype(BF16), vh, preferred_element_type=F32)
               + w_inter * jnp.dot(qh, ch.astype(BF16), preferred_element_type=F32))
        den = (jnp.sum(s, axis=1, keepdims=True)
               + w_inter * jnp.sum(qh.astype(F32) * nh, axis=1, keepdims=True))
        denom = jnp.maximum(jnp.abs(den), jnp.exp(-mvec))
        hh = num / denom
        hh = hh * lax.rsqrt(jnp.mean(hh * hh, axis=-1, keepdims=True) + EPS)
        hh = hh * nw[:, h * DV_M:(h + 1) * DV_M]
        og = og_ref[:, h * DV_M:(h + 1) * DV_M].astype(F32)
        y_ref[:, h * DV_M:(h + 1) * DV_M] = (hh * jax.nn.sigmoid(og)).astype(BF16)
        m_end = mvec[L - 1:L, :]
        b_end = bcol[L - 1:L, :]
        w_end = jnp.exp(b_end - bcol + igcol - m_end)
        decay = jnp.exp(b_end + m0h - m_end)
        kw = kh.astype(F32) * (w_end * scale)
        c_s[h] = decay * ch + lax.dot_general(kw.astype(BF16), vh, dn_tn, preferred_element_type=F32)
        n_s[h:h + 1, :] = decay * nh + jnp.sum(kw, axis=0, keepdims=True)
        m_new = jnp.where(lane1 == h, m_end, m_new)
    m_s[...] = m_new


def _mlstm_kernel(q_ref, k_ref, v_ref, og_ref, gif_ref, c0_ref, n0_ref, m0_ref, nw_ref,
                  y_ref, co_ref, no_ref, mo_ref, c_s, n_s, m_s, *, L, n_null, nbk):
    c = pl.program_id(1)

    @pl.when(c == 0)
    def _():
        c_s[...] = c0_ref[...]
        n_s[...] = n0_ref[...]
        m_s[...] = m0_ref[...]

    for bi in range(nbk):
        _mlstm_batch(c, q_ref.at[bi], k_ref.at[bi], v_ref.at[bi], og_ref.at[bi], gif_ref.at[bi], nw_ref,
                     y_ref.at[bi], c_s.at[bi], n_s.at[bi], m_s.at[bi], L=L, n_null=n_null)

    @pl.when(c == pl.num_programs(1) - 1)
    def _():
        co_ref[...] = c_s[...]
        no_ref[...] = n_s[...]
        mo_ref[...] = m_s[...]


def _mlstm(qkvo, gif, c0, n0, m0, nw, *, L, n_null, nbk):
    nb, tl, _ = qkvo.shape
    assert nb % nbk == 0
    st = lambda b, c: (b, 0, 0)
    return pl.pallas_call(
        functools.partial(_mlstm_kernel, L=L, n_null=n_null, nbk=nbk),
        grid=(nb // nbk, tl // L),
        in_specs=[pl.BlockSpec((nbk, L, HQK), lambda b, c: (b, c, 0)),
                  pl.BlockSpec((nbk, L, HQK), lambda b, c: (b, c, 1)),
                  pl.BlockSpec((nbk, L, HV), lambda b, c: (b, c, 1)),
                  pl.BlockSpec((nbk, L, HV), lambda b, c: (b, c, 2)),
                  pl.BlockSpec((nbk, L, LANES), lambda b, c: (b, c, 0)),
                  pl.BlockSpec((nbk, N_HEADS_M, DQK_M, DV_M), lambda b, c: (b, 0, 0, 0)),
                  pl.BlockSpec((nbk, N_HEADS_M, DQK_M), st),
                  pl.BlockSpec((nbk, 1, LANES), st),
                  pl.BlockSpec((1, HV), lambda b, c: (0, 0))],
        out_specs=[pl.BlockSpec((nbk, L, HV), lambda b, c: (b, c, 0)),
                   pl.BlockSpec((nbk, N_HEADS_M, DQK_M, DV_M), lambda b, c: (b, 0, 0, 0)),
                   pl.BlockSpec((nbk, N_HEADS_M, DQK_M), st),
                   pl.BlockSpec((nbk, 1, LANES), st)],
        out_shape=[jax.ShapeDtypeStruct((nb, tl, HV), BF16),
                   jax.ShapeDtypeStruct((nb, N_HEADS_M, DQK_M, DV_M), F32),
                   jax.ShapeDtypeStruct((nb, N_HEADS_M, DQK_M), F32),
                   jax.ShapeDtypeStruct((nb, 1, LANES), F32)],
        scratch_shapes=[pltpu.VMEM((nbk, N_HEADS_M, DQK_M, DV_M), F32),
                        pltpu.VMEM((nbk, N_HEADS_M, DQK_M), F32),
                        pltpu.VMEM((nbk, 1, LANES), F32)],
        compiler_params=_cparams(("arbitrary", "arbitrary")),
        name="mlstm",
    )(qkvo, qkvo, qkvo, qkvo, gif, c0, n0, m0, nw)


def _route(logits, dense):
    tm = logits.shape[0]
    lane = lax.broadcasted_iota(jnp.int32, (tm, LANES), 1)
    lanef = lane.astype(F32)
    gl = jnp.where(lane < N_GROUPS, logits, -jnp.inf)
    gmax = jnp.max(gl, axis=1, keepdims=True)
    gidx = jnp.min(jnp.where(gl == gmax, lanef, float(LANES)), axis=1, keepdims=True)
    gw = 1.0 / jnp.sum(jnp.exp(gl - gmax), axis=1, keepdims=True)
    lo = float(N_GROUPS) + float(EXPERTS_PER_GROUP) * gidx
    sel = (lanef >= lo) & (lanef < lo + float(EXPERTS_PER_GROUP))
    ev = jnp.where(sel, logits, -jnp.inf)
    v1 = jnp.max(ev, axis=1, keepdims=True)
    i1 = jnp.min(jnp.where(ev == v1, lanef, float(LANES)), axis=1, keepdims=True)
    ev2 = jnp.where(lanef == i1, -jnp.inf, ev)
    v2 = jnp.max(ev2, axis=1, keepdims=True)
    i2 = jnp.min(jnp.where(ev2 == v2, lanef, float(LANES)), axis=1, keepdims=True)
    t = jnp.exp(v2 - v1)
    w1 = gw / (1.0 + t)
    w2 = gw * (t / (1.0 + t))
    if dense:
        return jnp.where(lanef == i1, w1, jnp.where(lanef == i2, w2, 0.0))
    l1 = i1 - lo
    l2 = i2 - lo
    first = l1 < l2
    a = jnp.minimum(l1, l2)
    b = jnp.maximum(l1, l2)
    pair = jnp.where(a == 0.0, b - 1.0, jnp.where(a == 1.0, b + 1.0, 5.0))
    cls = float(PAIRS_PER_GROUP) * gidx + pair
    return jnp.where(lane == 0, cls,
                     jnp.where(lane == 1, jnp.where(first, w1, w2),
                               jnp.where(lane == 2, jnp.where(first, w2, w1), 0.0)))


def _outproj_kernel(mix_ref, w_ref, x_ref, meta_ref, n2_ref, wr_ref, br_ref, x1_ref, xn2_ref, gates_ref,
                    *, nb, tt, has_head, routed):
    def emit(x):
        mix = mix_ref[...].reshape(nb * tt, D_MODEL)
        x1 = x + jnp.dot(mix, w_ref[...], preferred_element_type=F32)
        x1_ref[...] = x1.reshape(nb, tt, D_MODEL)
        xn2 = _rms(x1, n2_ref[...]).astype(BF16)
        logits = jnp.dot(xn2, wr_ref[...], preferred_element_type=F32) + br_ref[...]
        info = _route(logits, dense=not routed)
        gates_ref[...] = info.reshape(nb, tt, LANES)
        if routed:
            xn2_ref[:, :, :D_MODEL] = xn2.astype(F32).reshape(nb, tt, D_MODEL)
            xn2_ref[:, :, D_MODEL:] = info.reshape(nb, tt, LANES)
        else:
            xn2_ref[...] = xn2.reshape(nb, tt, D_MODEL)

    if has_head:
        @pl.when(pl.program_id(0) == 0)
        def _():
            emit(_head_rows(meta_ref, nb))

        @pl.when(pl.program_id(0) > 0)
        def _():
            emit(x_ref[...].reshape(nb * tt, D_MODEL))
    else:
        emit(x_ref[...].reshape(nb * tt, D_MODEL))


def _outproj(mix, w_out, x, meta, n2, w_router, b_router, *, tt, has_head, routed):
    nb, tl, _ = mix.shape
    const = lambda i: (0, 0)
    blk = lambda i: (0, i, 0)
    xn2_w, xn2_dt = (ROW_W, F32) if routed else (D_MODEL, BF16)
    return pl.pallas_call(
        functools.partial(_outproj_kernel, nb=nb, tt=tt, has_head=has_head, routed=routed),
        grid=(tl // tt,),
        in_specs=[pl.BlockSpec((nb, tt, D_MODEL), blk),
                  pl.BlockSpec((D_MODEL, D_MODEL), const, pipeline_mode=pl.Buffered(1)),
                  pl.BlockSpec((nb, tt, D_MODEL), _x_map(has_head)),
                  pl.BlockSpec((N_META, D_MODEL), const),
                  pl.BlockSpec((1, D_MODEL), const),
                  pl.BlockSpec((D_MODEL, LANES), const),
                  pl.BlockSpec((1, LANES), const)],
        out_specs=[pl.BlockSpec((nb, tt, D_MODEL), blk),
                   pl.BlockSpec((nb, tt, xn2_w), blk),
                   pl.BlockSpec((nb, tt, LANES), blk)],
        out_shape=[jax.ShapeDtypeStruct((nb, tl, D_MODEL), F32),
                   jax.ShapeDtypeStruct((nb, tl, xn2_w), xn2_dt),
                   jax.ShapeDtypeStruct((nb, tl, LANES), F32)],
        compiler_params=_cparams(("arbitrary",)),
        name="outproj",
    )(mix, w_out, x, meta, n2, w_router, b_router)


def _moe_kernel(xn_ref, gates_ref, x1_ref, wg_ref, wu_ref, wd_ref, nf_ref, y_ref, acc_s, *, nb, tt):
    e = pl.program_id(1)
    rows = nb * tt

    @pl.when(e == 0)
    def _():
        acc_s[...] = jnp.zeros_like(acc_s)

    x = xn_ref[...].reshape(rows, D_MODEL)
    hg = jnp.dot(x, wg_ref[0], preferred_element_type=F32)
    hu = jnp.dot(x, wu_ref[0], preferred_element_type=F32)
    hdn = (hg * jax.nn.sigmoid(hg)) * hu
    yd = jnp.dot(hdn.astype(BF16), wd_ref[0], preferred_element_type=F32)
    gates = gates_ref[...].reshape(rows, LANES)
    lane = lax.broadcasted_iota(jnp.int32, gates.shape, 1)
    gcol = jnp.sum(jnp.where(lane == e + N_GROUPS, gates, 0.0), axis=1, keepdims=True)
    acc_s[...] += gcol * yd

    @pl.when(e == pl.num_programs(1) - 1)
    def _():
        x2 = x1_ref[...].reshape(rows, D_MODEL) + acc_s[...]
        y_ref[...] = _rms(x2, nf_ref[...]).reshape(nb, tt, D_MODEL)


def _moe(xn2, gates, x1, wg, wu, wd, nf, *, tt, has_head):
    nb, tl, _ = xn2.shape
    blk = lambda i, e: (0, i, 0)
    wblk = lambda i, e: (e, 0, 0)
    if has_head:
        out_map = lambda i, e: (0, jnp.maximum(i - 1, 0), 0)
        seq = tl - tt
    else:
        out_map = blk
        seq = tl
    return pl.pallas_call(
        functools.partial(_moe_kernel, nb=nb, tt=tt),
        grid=(tl // tt, N_EXPERTS),
        in_specs=[pl.BlockSpec((nb, tt, D_MODEL), blk),
                  pl.BlockSpec((nb, tt, LANES), blk),
                  pl.BlockSpec((nb, tt, D_MODEL), blk),
                  pl.BlockSpec((1, D_MODEL, D_EXPERT), wblk),
                  pl.BlockSpec((1, D_MODEL, D_EXPERT), wblk),
                  pl.BlockSpec((1, D_EXPERT, D_MODEL), wblk),
                  pl.BlockSpec((1, D_MODEL), lambda i, e: (0, 0))],
        out_specs=pl.BlockSpec((nb, tt, D_MODEL), out_map),
        out_shape=jax.ShapeDtypeStruct((nb, seq, D_MODEL), F32),
        scratch_shapes=[pltpu.VMEM((nb * tt, D_MODEL), F32)],
        compiler_params=_cparams(("arbitrary", "arbitrary")),
        name="moe",
    )(xn2, gates, x1, wg, wu, wd, nf)


_NT = (((1,), (1,)), ((), ()))


def _plan_kernel(ri_ref, dest_ref, tinfo_ref, cnt_s, carry_s, offs_s, *, rows):
    ph = pl.program_id(0)
    i = pl.program_id(1)
    ri = ri_ref[...].reshape(rows, LANES)
    lanef = lax.broadcasted_iota(jnp.int32, (rows, LANES), 1).astype(F32)
    oh = jnp.where(lanef == ri[:, 0:1], 1.0, 0.0)
    colsum = jnp.sum(oh, axis=0, keepdims=True)
    r128 = lax.broadcasted_iota(jnp.int32, (LANES, LANES), 0)
    c128 = lax.broadcasted_iota(jnp.int32, (LANES, LANES), 1)
    eye = jnp.where(r128 == c128, 1.0, 0.0).astype(BF16)

    @pl.when((ph == 0) & (i == 0))
    def _():
        cnt_s[...] = jnp.zeros_like(cnt_s)

    @pl.when(ph == 0)
    def _():
        cnt_s[...] += colsum

    @pl.when((ph == 1) & (i == 0))
    def _():
        tiles = jnp.floor((cnt_s[...] + float(TM_E - 1)) * (1.0 / TM_E))
        upper = jnp.where(r128 <= c128, 1.0, 0.0).astype(BF16)
        cum_incl = jnp.dot(jnp.broadcast_to(tiles, (8, LANES)).astype(BF16), upper, preferred_element_type=F32)
        offs_s[...] = (cum_incl[0:1, :] - tiles) * float(TM_E)
        carry_s[...] = jnp.zeros_like(carry_s)
        cum_col = lax.dot_general(eye, cum_incl.astype(BF16), _NT, preferred_element_type=F32)
        ended = (cum_col[:, 0:1] <= c128.astype(F32)) & (r128 < N_CLASSES)
        tcls = jnp.sum(jnp.where(ended, 1.0, 0.0), axis=0, keepdims=True)
        total = cum_incl[0:1, LANES - 1:LANES]
        row8 = lax.broadcasted_iota(jnp.int32, (8, LANES), 0)
        tinfo_ref[...] = jnp.where(row8 == 0, tcls, total).astype(jnp.int32)

    @pl.when(ph == 1)
    def _():
        tri = jnp.where(lax.broadcasted_iota(jnp.int32, (rows, rows), 0)
                        > lax.broadcasted_iota(jnp.int32, (rows, rows), 1), 1.0, 0.0).astype(BF16)
        rank = jnp.dot(tri, oh.astype(BF16), preferred_element_type=F32)
        dest = jnp.sum(oh * (rank + (offs_s[...] + carry_s[...])), axis=1, keepdims=True)
        carry_s[...] += colsum
        dhi = jnp.floor(dest * (1.0 / 256.0))
        dlo = dest - 256.0 * dhi
        digits = jnp.where(lanef == 0.0, dlo, jnp.where(lanef == 1.0, dhi, 0.0)).astype(BF16)
        dt = lax.dot_general(eye, digits, _NT, preferred_element_type=F32)
        dest_ref[...] = (dt[0:1, :] + 256.0 * dt[1:2, :]).astype(jnp.int32).reshape(1, 1, rows)


def _plan(rinfo, *, tt, skip):
    nb, tl, _ = rinfo.shape
    rows = nb * tt
    nt = tl // tt - skip
    assert (nt * rows) // TM_E + N_CLASSES <= LANES and rows % LANES == 0
    return pl.pallas_call(
        functools.partial(_plan_kernel, rows=rows),
        grid=(2, nt),
        in_specs=[pl.BlockSpec((nb, tt, LANES), lambda ph, i: (0, i + skip, 0))],
        out_specs=[pl.BlockSpec((1, 1, rows), lambda ph, i: (i * ph, 0, 0)),
                   pl.BlockSpec((8, LANES), lambda ph, i: (0, 0))],
        out_shape=[jax.ShapeDtypeStruct((nt, 1, rows), jnp.int32),
                   jax.ShapeDtypeStruct((8, LANES), jnp.int32)],
        scratch_shapes=[pltpu.VMEM((1, LANES), F32)] * 3,
        compiler_params=_cparams(("arbitrary", "arbitrary")),
        name="moe_plan",
    )(rinfo)


def _dispatch_kernel(tinfo_ref, dest_ref, x_ref, xs_ref, zbuf, zsem, sem, *, nb, tt):
    i = pl.program_id(0)

    @pl.when(i == 0)
    def _():
        zbuf[...] = jnp.zeros_like(zbuf)
        n_used = tinfo_ref[1, 0]

        def fill(start):
            def body(j, carry):
                last = (j >= n_used - 1) | (tinfo_ref[0, jnp.minimum(j + 1, LANES - 1)] != tinfo_ref[0, j])

                @pl.when(last)
                def _():
                    cp = pltpu.make_async_copy(
                        zbuf, xs_ref.at[pl.ds(pl.multiple_of(j * TM_E, TM_E), TM_E), :], zsem)
                    if start:
                        cp.start()
                    else:
                        cp.wait()
                return carry
            lax.fori_loop(0, xs_ref.shape[0] // TM_E, body, 0)

        fill(True)
        fill(False)

    def rows(start):
        for b in range(nb):
            def body(t, carry):
                d = dest_ref[0, 0, b * tt + t]
                cp = pltpu.make_async_copy(x_ref.at[b, pl.ds(t, 1), :], xs_ref.at[pl.ds(d, 1), :], sem)
                if start:
                    cp.start()
                else:
                    cp.wait()
                return carry
            lax.fori_loop(0, tt, body, 0, unroll=8)

    rows(True)
    rows(False)


def _dispatch(tinfo, dest, xrow, *, tt, skip):
    nb, tl, _ = xrow.shape
    nt = tl // tt - skip
    n_sorted = ((nt * nb * tt) // TM_E + N_CLASSES) * TM_E
    return pl.pallas_call(
        functools.partial(_dispatch_kernel, nb=nb, tt=tt),
        grid_spec=pltpu.PrefetchScalarGridSpec(
            num_scalar_prefetch=1,
            grid=(nt,),
            in_specs=[pl.BlockSpec((1, 1, nb * tt), lambda i, ti: (i, 0, 0), memory_space=pltpu.SMEM),
                      pl.BlockSpec((nb, tt, ROW_W), lambda i, ti: (0, i + skip, 0))],
            out_specs=pl.BlockSpec(memory_space=pl.ANY),
            scratch_shapes=[pltpu.VMEM((TM_E, ROW_W), F32),
                            pltpu.SemaphoreType.DMA(()),
                            pltpu.SemaphoreType.DMA(())]),
        out_shape=jax.ShapeDtypeStruct((n_sorted, ROW_W), F32),
        compiler_params=_cparams(("arbitrary",)),
        name="moe_dispatch",
    )(tinfo, dest, xrow)


def _class_expert(c, hi):
    g = c // PAIRS_PER_GROUP
    p = c - PAIRS_PER_GROUP * g
    ge3 = (p >= 3).astype(jnp.int32)
    ge5 = (p >= 5).astype(jnp.int32)
    local = (p + 1 - 2 * ge3 - ge5) if hi else (ge3 + ge5)
    return EXPERTS_PER_GROUP * g + local


def _experts_kernel(tinfo_ref, xs_ref, wga_ref, wua_ref, wda_ref, wgb_ref, wub_ref, wdb_ref, ys_ref):
    @pl.when(pl.program_id(0) < tinfo_ref[1, 0])
    def _():
        x = xs_ref[:, :D_MODEL].astype(BF16)

        def ffn(wg_ref, wu_ref, wd_ref):
            hg = jnp.dot(x, wg_ref[0], preferred_element_type=F32)
            hu = jnp.dot(x, wu_ref[0], preferred_element_type=F32)
            hdn = (hg * jax.nn.sigmoid(hg)) * hu
            return jnp.dot(hdn.astype(BF16), wd_ref[0], preferred_element_type=F32)

        ys_ref[...] = (xs_ref[:, D_MODEL + 1:D_MODEL + 2] * ffn(wga_ref, wua_ref, wda_ref)
                       + xs_ref[:, D_MODEL + 2:D_MODEL + 3] * ffn(wgb_ref, wub_ref, wdb_ref))

    @pl.when(pl.program_id(0) >= tinfo_ref[1, 0])
    def _():
        ys_ref[...] = jnp.zeros_like(ys_ref)


def _experts(tinfo, xs, wg, wu, wd):
    n_sorted = xs.shape[0]
    nt = n_sorted // TM_E

    def tile(j, ti):
        return jnp.minimum(j, ti[1, 0] - 1)

    def wmap(hi):
        return lambda j, ti: (_class_expert(ti[0, tile(j, ti)], hi), 0, 0)

    up = lambda hi: pl.BlockSpec((1, D_MODEL, D_EXPERT), wmap(hi))
    down = lambda hi: pl.BlockSpec((1, D_EXPERT, D_MODEL), wmap(hi))
    return pl.pallas_call(
        _experts_kernel,
        grid_spec=pltpu.PrefetchScalarGridSpec(
            num_scalar_prefetch=1,
            grid=(nt,),
            in_specs=[pl.BlockSpec((TM_E, ROW_W), lambda j, ti: (tile(j, ti), 0)),
                      up(0), up(0), down(0), up(1), up(1), down(1)],
            out_specs=pl.BlockSpec((TM_E, D_MODEL), lambda j, ti: (j, 0))),
        out_shape=jax.ShapeDtypeStruct((n_sorted, D_MODEL), F32),
        compiler_params=_cparams(("arbitrary",)),
        name="moe_experts",
    )(tinfo, xs, wg, wu, wd, wg, wu, wd)


def _combine_kernel(dcur_ref, dnxt_ref, x1_ref, nf_ref, ys_ref, y_ref, buf, sem, *, nb, tt):
    i = pl.program_id(0)
    slot = i % 2

    def gather(dref, s, start):
        for b in range(nb):
            def body(t, carry):
                d = dref[0, 0, b * tt + t]
                cp = pltpu.make_async_copy(ys_ref.at[pl.ds(d, 1), :], buf.at[s, b, pl.ds(t, 1), :], sem.at[s])
                if start:
                    cp.start()
                else:
                    cp.wait()
                return carry
            lax.fori_loop(0, tt, body, 0, unroll=8)

    @pl.when(i == 0)
    def _():
        gather(dcur_ref, 0, True)

    @pl.when(i + 1 < pl.num_programs(0))
    def _():
        gather(dnxt_ref, 1 - slot, True)

    gather(dcur_ref, slot, False)
    x2 = x1_ref[...].reshape(nb * tt, D_MODEL) + buf[slot].reshape(nb * tt, D_MODEL)
    y_ref[...] = _rms(x2, nf_ref[...]).reshape(nb, tt, D_MODEL)


def _combine(dest, x1, nf, ys, *, tt, skip):
    nb, tl, _ = x1.shape
    nt = tl // tt - skip
    dspec = lambda f: pl.BlockSpec((1, 1, nb * tt), f, memory_space=pltpu.SMEM)
    return pl.pallas_call(
        functools.partial(_combine_kernel, nb=nb, tt=tt),
        grid=(nt,),
        in_specs=[dspec(lambda i: (i, 0, 0)),
                  dspec(lambda i: (jnp.minimum(i + 1, nt - 1), 0, 0)),
                  pl.BlockSpec((nb, tt, D_MODEL), lambda i: (0, i + skip, 0)),
                  pl.BlockSpec((1, D_MODEL), lambda i: (0, 0)),
                  pl.BlockSpec(memory_space=pl.ANY)],
        out_specs=pl.BlockSpec((nb, tt, D_MODEL), lambda i: (0, i, 0)),
        out_shape=jax.ShapeDtypeStruct((nb, nt * tt, D_MODEL), F32),
        scratch_shapes=[pltpu.VMEM((2, nb, tt, D_MODEL), F32),
                        pltpu.SemaphoreType.DMA((2,))],
        compiler_params=_cparams(("arbitrary",)),
        name="moe_combine",
    )(dest, dest, x1, nf, ys)


def _block_diag(w):
    per = RNN_SUPER // RNN_BLOCK
    w4 = w.reshape(N_SUPER, per, RNN_BLOCK, RNN_BLOCK)
    eye = jnp.eye(per, dtype=w.dtype)
    bd = jnp.einsum('spij,pq->spiqj', w4, eye)
    return bd.reshape(N_SUPER, RNN_SUPER, RNN_SUPER).astype(BF16)


def _stream(x, meta, conv0, h0, c0, n0, m0, p, *, has_head):
    nb, seq, _ = x.shape
    n_null = N_NULL if has_head else 0
    tt = CHUNK if has_head else seq
    xn, gif = _prep(x, meta, p['norm1_w'], p['w_gif'], p['b_gif'], has_head=has_head)
    w_in = p['w_in']
    tt_mm = MM_TILES * CHUNK if has_head else seq
    xr = _mm(xn, w_in, col0=0, ncols=D_RNN, tn=512, tt=tt_mm, out_dtype=F32, w_t=True, name="in_xr")
    gr = _mm(xn, w_in, col0=D_RNN, ncols=D_RNN, tn=512, tt=tt_mm, out_dtype=BF16, w_t=True, name="in_gr")
    qkvo = _mm(xn, w_in, col0=2 * D_RNN, ncols=QKVO, tn=1024, tt=tt_mm, out_dtype=BF16, w_t=True,
               name="in_qkvo")
    mg = _mm(xn, p['w_mg'], col0=0, ncols=2 * D_MODEL, tn=1024, tt=tt_mm, out_dtype=BF16, w_t=True,
             name="in_mg")

    tail0 = jnp.transpose(conv0, (1, 0, 2)).reshape((CONV_W - 1) * nb, D_RNN)
    y_r, h_new, tail = _rglru(xr, gr, tail0, h0, p['conv_w'], p['conv_b'], p['rg_a_b'], p['rg_x_b'],
                              p['rg_lambda'], p['wa_bd'], p['wx_bd'], tt=tt, n_null=n_null)
    conv_new = jnp.transpose(tail.reshape(CONV_W - 1, nb, D_RNN), (1, 0, 2))

    m0p = jnp.pad(m0, ((0, 0), (0, LANES - N_HEADS_M))).reshape(nb, 1, LANES)
    y_m, c_new, n_new, m_new = _mlstm(qkvo, gif, c0, n0, m0p, p['mlstm_norm_w'], L=tt, n_null=n_null,
                                      nbk=MLSTM_NBK)
    m_new = m_new[:, 0, :N_HEADS_M]

    p_r = _mm(y_r, p['w_proj_rnn'], col0=0, ncols=D_MODEL, tn=512, tt=tt_mm, out_dtype=F32,
              gate=mg, gate_col0=0, name="proj_rnn")
    mix = _mm(y_m, p['w_proj_mlstm'], col0=0, ncols=D_MODEL, tn=512, tt=tt_mm, out_dtype=BF16,
              gate=mg, gate_col0=D_MODEL, add=p_r, name="proj_mlstm")
    routed = has_head
    x1, xn2, gates = _outproj(mix, p['w_out'], x, meta, p['norm2_w'], p['w_router'], p['b_router'],
                              tt=tt, has_head=has_head, routed=routed)
    if routed:
        skip = 1
        dest, tinfo = _plan(gates, tt=tt, skip=skip)
        xs = _dispatch(tinfo, dest, xn2, tt=tt, skip=skip)
        ys = _experts(tinfo, xs, p['w_exp_gate'], p['w_exp_up'], p['w_exp_down'])
        y = _combine(dest, x1, p['norm_f_w'], ys, tt=tt, skip=skip)
    else:
        y = _moe(xn2, gates, x1, p['w_exp_gate'], p['w_exp_up'], p['w_exp_down'], p['norm_f_w'],
                 tt=tt, has_head=has_head)
    return y, (conv_new[None], h_new[None], c_new[None], n_new[None], m_new[None])


def kernel(x_prompt, x_sample, state_rglru_conv, state_rglru_h, state_mlstm_C, state_mlstm_n, state_mlstm_m, meta_tokens, norm1_w, w_in, b_gates, conv_w, conv_b, rg_a_w, rg_a_b, rg_x_w, rg_x_b, rg_lambda, mlstm_norm_w, w_proj_rnn, w_proj_mlstm, w_out, norm2_w, w_router_group, b_router_group, w_router_expert, b_router_expert, w_exp_gate, w_exp_up, w_exp_down, norm_f_w):
    l = 0
    w_in_t = jnp.swapaxes(w_in[l], 0, 1)
    pad_r = LANES - N_GROUPS - N_EXPERTS
    p = dict(
        norm1_w=norm1_w[l][None], w_in=w_in_t,
        w_gif=jnp.pad(w_in_t[GIF_OFF:MG_OFF].T, ((0, 0), (0, LANES - 2 * N_HEADS_M))).astype(BF16),
        b_gif=jnp.pad(b_gates[l], (0, LANES - 2 * N_HEADS_M))[None],
        w_mg=w_in_t[MG_OFF:],
        conv_w=conv_w[l], conv_b=conv_b[l][None], rg_a_b=rg_a_b[l][None], rg_x_b=rg_x_b[l][None],
        rg_lambda=rg_lambda[l][None], wa_bd=_block_diag(rg_a_w[l]), wx_bd=_block_diag(rg_x_w[l]),
        mlstm_norm_w=mlstm_norm_w[l][None],
        w_proj_rnn=w_proj_rnn[l], w_proj_mlstm=w_proj_mlstm[l], w_out=w_out[l].astype(BF16),
        norm2_w=norm2_w[l][None],
        w_router=jnp.pad(jnp.concatenate([w_router_group[l], w_router_expert[l]], axis=1),
                         ((0, 0), (0, pad_r))).astype(BF16),
        b_router=jnp.pad(jnp.concatenate([b_router_group[l], b_router_expert[l]]), (0, pad_r))[None],
        w_exp_gate=w_exp_gate[l].astype(BF16), w_exp_up=w_exp_up[l].astype(BF16),
        w_exp_down=w_exp_down[l].astype(BF16), norm_f_w=norm_f_w[None],
    )
    nbp = x_prompt.shape[0]
    dt = x_prompt.dtype
    y_p, st_p = _stream(
        x_prompt, meta_tokens,
        jnp.zeros((nbp, CONV_W - 1, D_RNN), dt), jnp.zeros((nbp, D_RNN), F32),
        jnp.zeros((nbp, N_HEADS_M, DQK_M, DV_M), F32), jnp.zeros((nbp, N_HEADS_M, DQK_M), F32),
        jnp.zeros((nbp, N_HEADS_M), F32), p, has_head=True)
    y_s, st_s = _stream(
        x_sample, meta_tokens, state_rglru_conv[l], state_rglru_h[l], state_mlstm_C[l],
        state_mlstm_n[l], state_mlstm_m[l], p, has_head=False)
    return (y_p, y_s) + st_p + st_s
```

```python
import functools

import jax
import jax.numpy as jnp
from jax import lax
from jax.experimental import pallas as pl
from jax.experimental.pallas import tpu as pltpu

F32 = jnp.float32
BF16 = jnp.bfloat16

D_MODEL = 2048
CHUNK = 64
N_META = 16
N_NULL = CHUNK - N_META
D_RNN = 2560
N_RNN_BLOCKS = 16
RNN_BLOCK = D_RNN // N_RNN_BLOCKS
RNN_SUPER = 640
N_SUPER = D_RNN // RNN_SUPER
CONV_W = 4
LRU_C = 8.0
N_HEADS_M = 8
DV_M = D_MODEL // N_HEADS_M
DQK_M = DV_M // 2
HQK = N_HEADS_M * DQK_M
HV = N_HEADS_M * DV_M
QKVO = 2 * HQK + 2 * HV
N_GROUPS = 4
EXPERTS_PER_GROUP = 4
N_EXPERTS = 16
D_EXPERT = 512
PAIRS_PER_GROUP = 6
N_CLASSES = N_GROUPS * PAIRS_PER_GROUP
MM_TILES = 3
MLSTM_NBK = 2
TM_E = 256
ROW_W = D_MODEL + 128
EPS = 1e-6
LANES = 128
GIF_OFF = 2 * D_RNN + QKVO
MG_OFF = GIF_OFF + 2 * N_HEADS_M
VMEM_LIMIT = 56 * 1024 * 1024
NEG_BIG = -1e30


def _cparams(sem):
    return pltpu.CompilerParams(dimension_semantics=sem, vmem_limit_bytes=VMEM_LIMIT)


def _rms(x, w):
    return x * lax.rsqrt(jnp.mean(x * x, axis=-1, keepdims=True) + EPS) * w


def _softplus(x):
    return jnp.maximum(x, 0.0) + jnp.log1p(jnp.exp(-jnp.abs(x)))


def _head_rows(meta_ref, nb):
    head = jnp.concatenate([jnp.zeros((N_NULL, D_MODEL), F32), meta_ref[...]], axis=0)
    return jnp.broadcast_to(head[None], (nb, CHUNK, D_MODEL)).reshape(nb * CHUNK, D_MODEL)


def _x_map(has_head):
    if has_head:
        return lambda ti: (0, jnp.maximum(ti - 1, 0), 0)
    return lambda ti: (0, ti, 0)


def _prep_kernel(x_ref, meta_ref, nw_ref, wg_ref, bg_ref, xn_ref, gif_ref, *, nb, tt, has_head):
    def emit(x):
        xnb = _rms(x, nw_ref[...]).astype(BF16)
        xn_ref[...] = xnb.reshape(nb, tt, D_MODEL)
        gif = jnp.dot(xnb, wg_ref[...], preferred_element_type=F32) + bg_ref[...]
        gif_ref[...] = gif.reshape(nb, tt, LANES)

    if has_head:
        @pl.when(pl.program_id(0) == 0)
        def _():
            emit(_head_rows(meta_ref, nb))

        @pl.when(pl.program_id(0) > 0)
        def _():
            emit(x_ref[...].reshape(nb * tt, D_MODEL))
    else:
        emit(x_ref[...].reshape(nb * tt, D_MODEL))


def _prep(x, meta, norm_w, w_gif, b_gif, *, has_head):
    nb, seq, _ = x.shape
    tt = CHUNK if has_head else seq
    n_tiles = seq // tt + (1 if has_head else 0)
    tl = n_tiles * tt
    const = lambda ti: (0, 0)
    return pl.pallas_call(
        functools.partial(_prep_kernel, nb=nb, tt=tt, has_head=has_head),
        grid=(n_tiles,),
        in_specs=[pl.BlockSpec((nb, tt, D_MODEL), _x_map(has_head)),
                  pl.BlockSpec((N_META, D_MODEL), const),
                  pl.BlockSpec((1, D_MODEL), const),
                  pl.BlockSpec((D_MODEL, LANES), const),
                  pl.BlockSpec((1, LANES), const)],
        out_specs=[pl.BlockSpec((nb, tt, D_MODEL), lambda ti: (0, ti, 0)),
                   pl.BlockSpec((nb, tt, LANES), lambda ti: (0, ti, 0))],
        out_shape=[jax.ShapeDtypeStruct((nb, tl, D_MODEL), BF16),
                   jax.ShapeDtypeStruct((nb, tl, LANES), F32)],
        compiler_params=_cparams(("arbitrary",)),
        name="prep",
    )(x, meta, norm_w, w_gif, b_gif)


def _gelu_tanh(g):
    return g * (0.5 * (1.0 + jnp.tanh(0.7978845608028654 * (g + 0.044715 * (g * g * g)))))


def _mm_kernel(*refs, has_gate, has_add, w_t, gelu):
    lhs_ref, w_ref = refs[0], refs[1]
    pos = 2
    gate_ref = add_ref = None
    if has_gate:
        gate_ref = refs[pos]
        pos += 1
    if has_add:
        add_ref = refs[pos]
        pos += 1
    out_ref, wbf_ref = refs[pos], refs[pos + 1]
    nb, tt, k = lhs_ref.shape
    tn = out_ref.shape[-1]

    @pl.when(pl.program_id(1) == 0)
    def _():
        w = w_ref[...]
        wbf_ref[...] = (w.T if w_t else w).astype(BF16)

    acc = jnp.dot(lhs_ref[...].reshape(nb * tt, k), wbf_ref[...], preferred_element_type=F32)
    if gelu:
        acc = _gelu_tanh(acc)
    if has_gate:
        acc = jax.nn.sigmoid(gate_ref[...].reshape(nb * tt, tn).astype(F32)) * acc
    if has_add:
        acc = add_ref[...].reshape(nb * tt, tn).astype(F32) + acc
    out_ref[...] = acc.astype(out_ref.dtype).reshape(nb, tt, tn)


def _mm(lhs, w, *, col0, ncols, tn, tt, out_dtype, gate=None, gate_col0=0, add=None, w_t=False, gelu=False,
        name):
    nb, tl, k = lhs.shape
    assert tl % tt == 0 and ncols % tn == 0 and col0 % tn == 0 and gate_col0 % tn == 0
    if w_t:
        w_spec = pl.BlockSpec((tn, k), lambda j, i: (col0 // tn + j, 0))
    else:
        w_spec = pl.BlockSpec((k, tn), lambda j, i: (0, col0 // tn + j))
    in_specs = [pl.BlockSpec((nb, tt, k), lambda j, i: (0, i, 0)), w_spec]
    args = [lhs, w]
    if gate is not None:
        in_specs.append(pl.BlockSpec((nb, tt, tn), lambda j, i: (0, i, gate_col0 // tn + j)))
        args.append(gate)
    if add is not None:
        in_specs.append(pl.BlockSpec((nb, tt, tn), lambda j, i: (0, i, j)))
        args.append(add)
    return pl.pallas_call(
        functools.partial(_mm_kernel, has_gate=gate is not None, has_add=add is not None, w_t=w_t, gelu=gelu),
        grid=(ncols // tn, tl // tt),
        in_specs=in_specs,
        out_specs=pl.BlockSpec((nb, tt, tn), lambda j, i: (0, i, j)),
        out_shape=jax.ShapeDtypeStruct((nb, tl, ncols), out_dtype),
        scratch_shapes=[pltpu.VMEM((k, tn), BF16)],
        compiler_params=_cparams(("arbitrary", "arbitrary")),
        name=name,
    )(*args)


def _rglru_kernel(xr_ref, gr_ref, tail0_ref, h0_ref, cw_ref, cb_ref, ab_ref, xb_ref, lam_ref,
                  wa_ref, wx_ref, y_ref, hl_ref, tail_ref, xp_s, h_s, a_s, u_s, hs_s, *, nb, tt, n_null):
    ti = pl.program_id(1)
    rt = tt * nb
    tb = (CONV_W - 1) * nb

    def time_major(x):
        return jnp.swapaxes(x, 0, 1).reshape(rt, x.shape[-1])

    @pl.when(ti == 0)
    def _():
        xp_s[0:tb, :] = tail0_ref[...]
        h_s[...] = h0_ref[...]

    xp_s[tb:tb + rt, :] = time_major(xr_ref[...])
    cw = cw_ref[...]
    xc = cb_ref[...] + cw[0:1, :] * xp_s[0:rt, :]
    for j in range(1, CONV_W):
        xc = xc + cw[j:j + 1, :] * xp_s[j * nb:j * nb + rt, :]
    xcb = xc.astype(BF16)
    r = jax.nn.sigmoid(jnp.dot(xcb, wa_ref[0], preferred_element_type=F32) + ab_ref[...])
    i = jax.nn.sigmoid(jnp.dot(xcb, wx_ref[0], preferred_element_type=F32) + xb_ref[...])
    log_a = r * ((-LRU_C) * _softplus(-lam_ref[...]))
    a = jnp.exp(log_a)
    a_s[...] = a
    u = jnp.sqrt(-jnp.tanh(log_a) * (a * a + 1.0)) * (i * xc)
    if n_null:
        row = lax.broadcasted_iota(jnp.int32, (rt, 1), 0)
        u = jnp.where((row >= n_null * nb) | (ti > 0), u, 0.0)
    u_s[...] = u

    def step(t, h):
        o = pl.multiple_of(t * nb, nb)
        h = a_s[pl.ds(o, nb), :] * h + u_s[pl.ds(o, nb), :]
        hs_s[pl.ds(o, nb), :] = h
        return h

    h = lax.fori_loop(0, tt, step, h_s[...], unroll=8)
    h_s[...] = h
    h_bm = jnp.swapaxes(hs_s[...].reshape(tt, nb, hs_s.shape[-1]), 0, 1)
    y_ref[...] = (gr_ref[...].astype(F32) * h_bm).astype(BF16)
    xp_s[0:tb, :] = xp_s[rt:rt + tb, :]

    @pl.when(ti == pl.num_programs(1) - 1)
    def _():
        hl_ref[...] = h
        tail_ref[...] = xp_s[rt:rt + tb, :]


def _rglru(xr, gr, tail0, h0, cw, cb, ab, xb, lam, wa, wx, *, tt, n_null):
    nb, tl, _ = xr.shape
    rt = tt * nb
    tb = (CONV_W - 1) * nb
    blk = lambda s, ti: (0, ti, s)
    col = lambda s, ti: (0, s)
    return pl.pallas_call(
        functools.partial(_rglru_kernel, nb=nb, tt=tt, n_null=n_null),
        grid=(N_SUPER, tl // tt),
        in_specs=[pl.BlockSpec((nb, tt, RNN_SUPER), blk),
                  pl.BlockSpec((nb, tt, RNN_SUPER), blk),
                  pl.BlockSpec((tb, RNN_SUPER), col),
                  pl.BlockSpec((nb, RNN_SUPER), col),
                  pl.BlockSpec((CONV_W, RNN_SUPER), col),
                  pl.BlockSpec((1, RNN_SUPER), col),
                  pl.BlockSpec((1, RNN_SUPER), col),
                  pl.BlockSpec((1, RNN_SUPER), col),
                  pl.BlockSpec((1, RNN_SUPER), col),
                  pl.BlockSpec((1, RNN_SUPER, RNN_SUPER), lambda s, ti: (s, 0, 0)),
                  pl.BlockSpec((1, RNN_SUPER, RNN_SUPER), lambda s, ti: (s, 0, 0))],
        out_specs=[pl.BlockSpec((nb, tt, RNN_SUPER), blk),
                   pl.BlockSpec((nb, RNN_SUPER), col),
                   pl.BlockSpec((tb, RNN_SUPER), col)],
        out_shape=[jax.ShapeDtypeStruct((nb, tl, D_RNN), BF16),
                   jax.ShapeDtypeStruct((nb, D_RNN), F32),
                   jax.ShapeDtypeStruct((tb, D_RNN), F32)],
        scratch_shapes=[pltpu.VMEM((rt + tb, RNN_SUPER), F32),
                        pltpu.VMEM((nb, RNN_SUPER), F32),
                        pltpu.VMEM((rt, RNN_SUPER), F32),
                        pltpu.VMEM((rt, RNN_SUPER), F32),
                        pltpu.VMEM((rt, RNN_SUPER), F32)],
        compiler_params=_cparams(("arbitrary", "arbitrary")),
        name="rglru",
    )(xr, gr, tail0, h0, cw, cb, ab, xb, lam, wa, wx)


def _mlstm_batch(c, q_ref, k_ref, v_ref, og_ref, gif_ref, nw_ref, y_ref, c_s, n_s, m_s, *, L, n_null):
    g = gif_ref[...]
    lane = lax.broadcasted_iota(jnp.int32, (L, LANES), 1)
    row = lax.broadcasted_iota(jnp.int32, (L, LANES), 0)
    is_f = (lane >= N_HEADS_M) & (lane < 2 * N_HEADS_M)
    lf = jnp.where(is_f, jnp.minimum(g, 0.0) - jnp.log1p(jnp.exp(-jnp.abs(g))), 0.0)
    li = g
    if n_null:
        null = (row < n_null) & (c == 0)
        lf = jnp.where(null, 0.0, lf)
        li = jnp.where(null, NEG_BIG, li)
    rr = lax.broadcasted_iota(jnp.int32, (L, L), 0)
    cc = lax.broadcasted_iota(jnp.int32, (L, L), 1)
    causal = rr >= cc
    tri = causal.astype(F32)
    bfull = jnp.dot(tri, lf, preferred_element_type=F32, precision=lax.Precision.HIGHEST)
    b_t = bfull.T
    li_t = li.T
    m_all = m_s[...]
    m_new = m_all
    lane1 = lax.broadcasted_iota(jnp.int32, (1, LANES), 1)
    scale = DQK_M ** -0.5
    nw = nw_ref[...]
    dn_nt = (((1,), (1,)), ((), ()))
    dn_tn = (((0,), (0,)), ((), ()))
    for h in range(N_HEADS_M):
        bcol = bfull[:, N_HEADS_M + h:N_HEADS_M + h + 1]
        brow = b_t[N_HEADS_M + h:N_HEADS_M + h + 1, :]
        igrow = li_t[h:h + 1, :]
        igcol = li[:, h:h + 1]
        m0h = m_all[:, h:h + 1]
        log_w = jnp.where(causal, bcol - brow + igrow, -jnp.inf)
        log_inter = bcol + m0h
        mvec = jnp.maximum(log_inter, jnp.max(log_w, axis=1, keepdims=True))
        w_intra = jnp.exp(log_w - mvec)
        w_inter = jnp.exp(log_inter - mvec)
        qh = q_ref[:, h * DQK_M:(h + 1) * DQK_M]
        kh = k_ref[:, h * DQK_M:(h + 1) * DQK_M]
        vh = v_ref[:, h * DV_M:(h + 1) * DV_M]
        s = lax.dot_general(qh, kh, dn_nt, preferred_element_type=F32) * (scale * w_intra)
        ch = c_s[h]
        nh = n_s[h:h + 1, :]
        num = (jnp.dot(s.astype(BF16), vh, preferred_element_type=F32)
               + w_inter * jnp.dot(qh, ch.astype(BF16), preferred_element_type=F32))
        den = (jnp.sum(s, axis=1, keepdims=True)
               + w_inter * jnp.sum(qh.astype(F32) * nh, axis=1, keepdims=True))
        denom = jnp.maximum(jnp.abs(den), jnp.exp(-mvec))
        hh = num / denom
        hh = hh * lax.rsqrt(jnp.mean(hh * hh, axis=-1, keepdims=True) + EPS)
        hh = hh * nw[:, h * DV_M:(h + 1) * DV_M]
        og = og_ref[:, h * DV_M:(h + 1) * DV_M].astype(F32)
        y_ref[:, h * DV_M:(h + 1) * DV_M] = (hh * jax.nn.sigmoid(og)).astype(BF16)
        m_end = mvec[L - 1:L, :]
        b_end = bcol[L - 1:L, :]
        w_end = jnp.exp(b_end - bcol + igcol - m_end)
        decay = jnp.exp(b_end + m0h - m_end)
        kw = kh.astype(F32) * (w_end * scale)
        c_s[h] = decay * ch + lax.dot_general(kw.astype(BF16), vh, dn_tn, preferred_element_type=F32)
        n_s[h:h + 1, :] = decay * nh + jnp.sum(kw, axis=0, keepdims=True)
        m_new = jnp.where(lane1 == h, m_end, m_new)
    m_s[...] = m_new


def _mlstm_kernel(q_ref, k_ref, v_ref, og_ref, gif_ref, c0_ref, n0_ref, m0_ref, nw_ref,
                  y_ref, co_ref, no_ref, mo_ref, c_s, n_s, m_s, *, L, n_null, nbk):
    c = pl.program_id(1)

    @pl.when(c == 0)
    def _():
        c_s[...] = c0_ref[...]
        n_s[...] = n0_ref[...]
        m_s[...] = m0_ref[...]

    for bi in range(nbk):
        _mlstm_batch(c, q_ref.at[bi], k_ref.at[bi], v_ref.at[bi], og_ref.at[bi], gif_ref.at[bi], nw_ref,
                     y_ref.at[bi], c_s.at[bi], n_s.at[bi], m_s.at[bi], L=L, n_null=n_null)

    @pl.when(c == pl.num_programs(1) - 1)
    def _():
        co_ref[...] = c_s[...]
        no_ref[...] = n_s[...]
        mo_ref[...] = m_s[...]


def _mlstm(qkvo, gif, c0, n0, m0, nw, *, L, n_null, nbk):
    nb, tl, _ = qkvo.shape
    assert nb % nbk == 0
    st = lambda b, c: (b, 0, 0)
    return pl.pallas_call(
        functools.partial(_mlstm_kernel, L=L, n_null=n_null, nbk=nbk),
        grid=(nb // nbk, tl // L),
        in_specs=[pl.BlockSpec((nbk, L, HQK), lambda b, c: (b, c, 0)),
                  pl.BlockSpec((nbk, L, HQK), lambda b, c: (b, c, 1)),
                  pl.BlockSpec((nbk, L, HV), lambda b, c: (b, c, 1)),
                  pl.BlockSpec((nbk, L, HV), lambda b, c: (b, c, 2)),
                  pl.BlockSpec((nbk, L, LANES), lambda b, c: (b, c, 0)),
                  pl.BlockSpec((nbk, N_HEADS_M, DQK_M, DV_M), lambda b, c: (b, 0, 0, 0)),
                  pl.BlockSpec((nbk, N_HEADS_M, DQK_M), st),
                  pl.BlockSpec((nbk, 1, LANES), st),
                  pl.BlockSpec((1, HV), lambda b, c: (0, 0))],
        out_specs=[pl.BlockSpec((nbk, L, HV), lambda b, c: (b, c, 0)),
                   pl.BlockSpec((nbk, N_HEADS_M, DQK_M, DV_M), lambda b, c: (b, 0, 0, 0)),
                   pl.BlockSpec((nbk, N_HEADS_M, DQK_M), st),
                   pl.BlockSpec((nbk, 1, LANES), st)],
        out_shape=[jax.ShapeDtypeStruct((nb, tl, HV), BF16),
                   jax.ShapeDtypeStruct((nb, N_HEADS_M, DQK_M, DV_M), F32),
                   jax.ShapeDtypeStruct((nb, N_HEADS_M, DQK_M), F32),
                   jax.ShapeDtypeStruct((nb, 1, LANES), F32)],
        scratch_shapes=[pltpu.VMEM((nbk, N_HEADS_M, DQK_M, DV_M), F32),
                        pltpu.VMEM((nbk, N_HEADS_M, DQK_M), F32),
                        pltpu.VMEM((nbk, 1, LANES), F32)],
        compiler_params=_cparams(("arbitrary", "arbitrary")),
        name="mlstm",
    )(qkvo, qkvo, qkvo, qkvo, gif, c0, n0, m0, nw)


DVA = DV_M + 128


def _mlstm_gates(c, gif_ref, m_s, *, L, n_null):
    hm = N_HEADS_M
    g = gif_ref[...]
    if L < LANES:
        g = jnp.concatenate([g, jnp.zeros((LANES - L, LANES), F32)], axis=0)
    g_t = g.T
    lane = lax.broadcasted_iota(jnp.int32, (hm, LANES), 1)
    valid = lane < L
    li = g_t[0:hm, :]
    lfr = g_t[hm:2 * hm, :]
    lf = jnp.where(valid, jnp.minimum(lfr, 0.0) - jnp.log1p(jnp.exp(-jnp.abs(lfr))), 0.0)
    if n_null:
        null = (lane < n_null) & (c == 0)
        lf = jnp.where(null, 0.0, lf)
        li = jnp.where(null, NEG_BIG, li)
    r128 = lax.broadcasted_iota(jnp.int32, (LANES, LANES), 0)
    c128 = lax.broadcasted_iota(jnp.int32, (LANES, LANES), 1)
    upper = jnp.where(r128 <= c128, 1.0, 0.0)
    b = jnp.dot(lf, upper, preferred_element_type=F32, precision=lax.Precision.HIGHEST)
    a = li - b
    cm = jnp.where(valid, a, -jnp.inf)
    sh = 1
    while sh < L:
        cm = jnp.maximum(cm, jnp.where(lane >= sh, pltpu.roll(cm, sh, axis=1), -jnp.inf))
        sh *= 2
    m0 = m_s[...]
    big_m = jnp.maximum(cm, m0)
    m_last = jnp.max(jnp.where(valid, big_m, -jnp.inf), axis=1, keepdims=True)
    b_last = jnp.sum(lf, axis=1, keepdims=True)
    scale = DQK_M ** -0.5
    e_mv = jnp.exp(-(b + big_m))
    w_int = jnp.exp(m0 - big_m)
    w_end = jnp.exp(a - m_last) * scale
    decay = jnp.exp(m0 - m_last)
    m_s[...] = jnp.broadcast_to(b_last + m_last, (hm, LANES))
    cols = jnp.concatenate([big_m, e_mv, w_int, w_end, jnp.zeros((LANES - 4 * hm, LANES), F32)], axis=0).T
    return a, cols, decay


def _mlstm_heads(gates, q_ref, k_ref, v_ref, og_ref, nw_ref, y_ref, c_s, *, L):
    hm = N_HEADS_M
    scale = DQK_M ** -0.5
    rr = lax.broadcasted_iota(jnp.int32, (L, L), 0)
    cc = lax.broadcasted_iota(jnp.int32, (L, L), 1)
    causal = rr >= cc
    ones_col = jnp.where(lax.broadcasted_iota(jnp.int32, (L, 128), 1) == 0, 1.0, 0.0).astype(BF16)
    nw = nw_ref[...]
    dn_nt = (((1,), (1,)), ((), ()))
    dn_tn = (((0,), (0,)), ((), ()))
    bh = [(bi, h) for bi in range(len(gates)) for h in range(hm)]
    col = lambda bi, k, h: gates[bi][1][0:L, k * hm + h:k * hm + h + 1]
    qs = [q_ref[bi, :, h * DQK_M:(h + 1) * DQK_M] for bi, h in bh]
    ks = [k_ref[bi, :, h * DQK_M:(h + 1) * DQK_M] for bi, h in bh]
    v_augs = [jnp.concatenate([v_ref[bi, :, h * DV_M:(h + 1) * DV_M], ones_col], axis=1) for bi, h in bh]
    c_augs = [c_s[bi, h] for bi, h in bh]
    s_raw = [lax.dot_general(q, k, dn_nt, preferred_element_type=F32) for q, k in zip(qs, ks)]
    inter = [jnp.dot((qs[i].astype(F32) * col(bi, 2, h)).astype(BF16), c_augs[i].astype(BF16),
                     preferred_element_type=F32) for i, (bi, h) in enumerate(bh)]
    upd = [lax.dot_general((ks[i].astype(F32) * col(bi, 3, h)).astype(BF16), v_augs[i], dn_tn,
                           preferred_element_type=F32) for i, (bi, h) in enumerate(bh)]
    s_w = [(s_raw[i] * (scale * jnp.where(causal, jnp.exp(gates[bi][0][h:h + 1, 0:L] - col(bi, 0, h)), 0.0))
            ).astype(BF16) for i, (bi, h) in enumerate(bh)]
    res = [jnp.dot(s_w[i], v_augs[i], preferred_element_type=F32) + inter[i] for i in range(len(bh))]
    for i, (bi, h) in enumerate(bh):
        c_s[bi, h] = gates[bi][2][h:h + 1, 0:1] * c_augs[i] + upd[i]
    for i, (bi, h) in enumerate(bh):
        num = res[i][:, :DV_M]
        denom = jnp.maximum(jnp.abs(res[i][:, DV_M:DV_M + 1]), col(bi, 1, h))
        hh = num / denom
        hh = hh * lax.rsqrt(jnp.mean(hh * hh, axis=-1, keepdims=True) + EPS)
        hh = hh * nw[:, h * DV_M:(h + 1) * DV_M]
        og = og_ref[bi, :, h * DV_M:(h + 1) * DV_M].astype(F32)
        y_ref[bi, :, h * DV_M:(h + 1) * DV_M] = (hh * jax.nn.sigmoid(og)).astype(BF16)


def _mlstm_fused_kernel(q_ref, k_ref, v_ref, og_ref, gif_ref, c0_ref, m0_ref, nw_ref,
                        y_ref, co_ref, mo_ref, c_s, m_s, *, L, n_null, nbk):
    c = pl.program_id(1)

    @pl.when(c == 0)
    def _():
        c_s[...] = c0_ref[...]
        m_s[...] = m0_ref[...]

    gates = [_mlstm_gates(c, gif_ref.at[bi], m_s.at[bi], L=L, n_null=n_null) for bi in range(nbk)]
    _mlstm_heads(gates, q_ref, k_ref, v_ref, og_ref, nw_ref, y_ref, c_s, L=L)

    @pl.when(c == pl.num_programs(1) - 1)
    def _():
        co_ref[...] = c_s[...]
        mo_ref[...] = m_s[...]


def _mlstm_fused(qkvo, gif, c0, n0, m0, nw, *, L, n_null, nbk):
    nb, tl, _ = qkvo.shape
    assert nb % nbk == 0
    hm = N_HEADS_M
    c_aug0 = jnp.concatenate([c0, n0[..., None], jnp.zeros((nb, hm, DQK_M, DVA - DV_M - 1), F32)], axis=-1)
    m_rep0 = jnp.broadcast_to(m0[..., None], (nb, hm, LANES))
    st4 = lambda b, c: (b, 0, 0, 0)
    st3 = lambda b, c: (b, 0, 0)
    y, c_aug, m_rep = pl.pallas_call(
        functools.partial(_mlstm_fused_kernel, L=L, n_null=n_null, nbk=nbk),
        grid=(nb // nbk, tl // L),
        in_specs=[pl.BlockSpec((nbk, L, HQK), lambda b, c: (b, c, 0)),
                  pl.BlockSpec((nbk, L, HQK), lambda b, c: (b, c, 1)),
                  pl.BlockSpec((nbk, L, HV), lambda b, c: (b, c, 1)),
                  pl.BlockSpec((nbk, L, HV), lambda b, c: (b, c, 2)),
                  pl.BlockSpec((nbk, L, LANES), lambda b, c: (b, c, 0)),
                  pl.BlockSpec((nbk, hm, DQK_M, DVA), st4),
                  pl.BlockSpec((nbk, hm, LANES), st3),
                  pl.BlockSpec((1, HV), lambda b, c: (0, 0))],
        out_specs=[pl.BlockSpec((nbk, L, HV), lambda b, c: (b, c, 0)),
                   pl.BlockSpec((nbk, hm, DQK_M, DVA), st4),
                   pl.BlockSpec((nbk, hm, LANES), st3)],
        out_shape=[jax.ShapeDtypeStruct((nb, tl, HV), BF16),
                   jax.ShapeDtypeStruct((nb, hm, DQK_M, DVA), F32),
                   jax.ShapeDtypeStruct((nb, hm, LANES), F32)],
        scratch_shapes=[pltpu.VMEM((nbk, hm, DQK_M, DVA), F32),
                        pltpu.VMEM((nbk, hm, LANES), F32)],
        compiler_params=_cparams(("arbitrary", "arbitrary")),
        name="mlstm",
    )(qkvo, qkvo, qkvo, qkvo, gif, c_aug0, m_rep0, nw)
    return y, c_aug[..., :DV_M], c_aug[..., DV_M], m_rep[..., 0]


def _route(logits, dense):
    tm = logits.shape[0]
    lane = lax.broadcasted_iota(jnp.int32, (tm, LANES), 1)
    lanef = lane.astype(F32)
    gl = jnp.where(lane < N_GROUPS, logits, -jnp.inf)
    gmax = jnp.max(gl, axis=1, keepdims=True)
    gidx = jnp.min(jnp.where(gl == gmax, lanef, float(LANES)), axis=1, keepdims=True)
    gw = 1.0 / jnp.sum(jnp.exp(gl - gmax), axis=1, keepdims=True)
    lo = float(N_GROUPS) + float(EXPERTS_PER_GROUP) * gidx
    sel = (lanef >= lo) & (lanef < lo + float(EXPERTS_PER_GROUP))
    ev = jnp.where(sel, logits, -jnp.inf)
    v1 = jnp.max(ev, axis=1, keepdims=True)
    i1 = jnp.min(jnp.where(ev == v1, lanef, float(LANES)), axis=1, keepdims=True)
    ev2 = jnp.where(lanef == i1, -jnp.inf, ev)
    v2 = jnp.max(ev2, axis=1, keepdims=True)
    i2 = jnp.min(jnp.where(ev2 == v2, lanef, float(LANES)), axis=1, keepdims=True)
    t = jnp.exp(v2 - v1)
    w1 = gw / (1.0 + t)
    w2 = gw * (t / (1.0 + t))
    if dense:
        return jnp.where(lanef == i1, w1, jnp.where(lanef == i2, w2, 0.0))
    l1 = i1 - lo
    l2 = i2 - lo
    first = l1 < l2
    a = jnp.minimum(l1, l2)
    b = jnp.maximum(l1, l2)
    pair = jnp.where(a == 0.0, b - 1.0, jnp.where(a == 1.0, b + 1.0, 5.0))
    cls = float(PAIRS_PER_GROUP) * gidx + pair
    return jnp.where(lane == 0, cls,
                     jnp.where(lane == 1, jnp.where(first, w1, w2),
                               jnp.where(lane == 2, jnp.where(first, w2, w1), 0.0)))


def _outproj_kernel(mix_ref, w_ref, x_ref, meta_ref, n2_ref, wr_ref, br_ref, x1_ref, xn2_ref, gates_ref,
                    *, nb, tt, has_head, routed):
    def emit(x):
        mix = mix_ref[...].reshape(nb * tt, D_MODEL)
        x1 = x + jnp.dot(mix, w_ref[...], preferred_element_type=F32)
        x1_ref[...] = x1.reshape(nb, tt, D_MODEL)
        xn2 = _rms(x1, n2_ref[...]).astype(BF16)
        logits = jnp.dot(xn2, wr_ref[...], preferred_element_type=F32) + br_ref[...]
        info = _route(logits, dense=not routed)
        gates_ref[...] = info.reshape(nb, tt, LANES)
        if routed:
            xn2_ref[:, :, :D_MODEL] = xn2.astype(F32).reshape(nb, tt, D_MODEL)
            xn2_ref[:, :, D_MODEL:] = info.reshape(nb, tt, LANES)
        else:
            xn2_ref[...] = xn2.reshape(nb, tt, D_MODEL)

    if has_head:
        @pl.when(pl.program_id(0) == 0)
        def _():
            emit(_head_rows(meta_ref, nb))

        @pl.when(pl.program_id(0) > 0)
        def _():
            emit(x_ref[...].reshape(nb * tt, D_MODEL))
    else:
        emit(x_ref[...].reshape(nb * tt, D_MODEL))


def _outproj(mix, w_out, x, meta, n2, w_router, b_router, *, tt, has_head, routed):
    nb, tl, _ = mix.shape
    const = lambda i: (0, 0)
    blk = lambda i: (0, i, 0)
    xn2_w, xn2_dt = (ROW_W, F32) if routed else (D_MODEL, BF16)
    return pl.pallas_call(
        functools.partial(_outproj_kernel, nb=nb, tt=tt, has_head=has_head, routed=routed),
        grid=(tl // tt,),
        in_specs=[pl.BlockSpec((nb, tt, D_MODEL), blk),
                  pl.BlockSpec((D_MODEL, D_MODEL), const, pipeline_mode=pl.Buffered(1)),
                  pl.BlockSpec((nb, tt, D_MODEL), _x_map(has_head)),
                  pl.BlockSpec((N_META, D_MODEL), const),
                  pl.BlockSpec((1, D_MODEL), const),
                  pl.BlockSpec((D_MODEL, LANES), const),
                  pl.BlockSpec((1, LANES), const)],
        out_specs=[pl.BlockSpec((nb, tt, D_MODEL), blk),
                   pl.BlockSpec((nb, tt, xn2_w), blk),
                   pl.BlockSpec((nb, tt, LANES), blk)],
        out_shape=[jax.ShapeDtypeStruct((nb, tl, D_MODEL), F32),
                   jax.ShapeDtypeStruct((nb, tl, xn2_w), xn2_dt),
                   jax.ShapeDtypeStruct((nb, tl, LANES), F32)],
        compiler_params=_cparams(("arbitrary",)),
        name="outproj",
    )(mix, w_out, x, meta, n2, w_router, b_router)


def _moe_kernel(xn_ref, gates_ref, x1_ref, wg_ref, wu_ref, wd_ref, nf_ref, y_ref, acc_s, *, nb, tt):
    e = pl.program_id(1)
    rows = nb * tt

    @pl.when(e == 0)
    def _():
        acc_s[...] = jnp.zeros_like(acc_s)

    x = xn_ref[...].reshape(rows, D_MODEL)
    hg = jnp.dot(x, wg_ref[0], preferred_element_type=F32)
    hu = jnp.dot(x, wu_ref[0], preferred_element_type=F32)
    hdn = (hg * jax.nn.sigmoid(hg)) * hu
    yd = jnp.dot(hdn.astype(BF16), wd_ref[0], preferred_element_type=F32)
    gates = gates_ref[...].reshape(rows, LANES)
    lane = lax.broadcasted_iota(jnp.int32, gates.shape, 1)
    gcol = jnp.sum(jnp.where(lane == e + N_GROUPS, gates, 0.0), axis=1, keepdims=True)
    acc_s[...] += gcol * yd

    @pl.when(e == pl.num_programs(1) - 1)
    def _():
        x2 = x1_ref[...].reshape(rows, D_MODEL) + acc_s[...]
        y_ref[...] = _rms(x2, nf_ref[...]).reshape(nb, tt, D_MODEL)


def _moe(xn2, gates, x1, wg, wu, wd, nf, *, tt, has_head):
    nb, tl, _ = xn2.shape
    blk = lambda i, e: (0, i, 0)
    wblk = lambda i, e: (e, 0, 0)
    if has_head:
        out_map = lambda i, e: (0, jnp.maximum(i - 1, 0), 0)
        seq = tl - tt
    else:
        out_map = blk
        seq = tl
    return pl.pallas_call(
        functools.partial(_moe_kernel, nb=nb, tt=tt),
        grid=(tl // tt, N_EXPERTS),
        in_specs=[pl.BlockSpec((nb, tt, D_MODEL), blk),
                  pl.BlockSpec((nb, tt, LANES), blk),
                  pl.BlockSpec((nb, tt, D_MODEL), blk),
                  pl.BlockSpec((1, D_MODEL, D_EXPERT), wblk),
                  pl.BlockSpec((1, D_MODEL, D_EXPERT), wblk),
                  pl.BlockSpec((1, D_EXPERT, D_MODEL), wblk),
                  pl.BlockSpec((1, D_MODEL), lambda i, e: (0, 0))],
        out_specs=pl.BlockSpec((nb, tt, D_MODEL), out_map),
        out_shape=jax.ShapeDtypeStruct((nb, seq, D_MODEL), F32),
        scratch_shapes=[pltpu.VMEM((nb * tt, D_MODEL), F32)],
        compiler_params=_cparams(("arbitrary", "arbitrary")),
        name="moe",
    )(xn2, gates, x1, wg, wu, wd, nf)


_NT = (((1,), (1,)), ((), ()))


def _plan_kernel(ri_ref, dest_ref, tinfo_ref, cnt_s, carry_s, offs_s, *, rows):
    ph = pl.program_id(0)
    i = pl.program_id(1)
    ri = ri_ref[...].reshape(rows, LANES)
    lanef = lax.broadcasted_iota(jnp.int32, (rows, LANES), 1).astype(F32)
    oh = jnp.where(lanef == ri[:, 0:1], 1.0, 0.0)
    colsum = jnp.sum(oh, axis=0, keepdims=True)
    r128 = lax.broadcasted_iota(jnp.int32, (LANES, LANES), 0)
    c128 = lax.broadcasted_iota(jnp.int32, (LANES, LANES), 1)
    eye = jnp.where(r128 == c128, 1.0, 0.0).astype(BF16)

    @pl.when((ph == 0) & (i == 0))
    def _():
        cnt_s[...] = jnp.zeros_like(cnt_s)

    @pl.when(ph == 0)
    def _():
        cnt_s[...] += colsum

    @pl.when((ph == 1) & (i == 0))
    def _():
        tiles = jnp.floor((cnt_s[...] + float(TM_E - 1)) * (1.0 / TM_E))
        upper = jnp.where(r128 <= c128, 1.0, 0.0).astype(BF16)
        cum_incl = jnp.dot(jnp.broadcast_to(tiles, (8, LANES)).astype(BF16), upper, preferred_element_type=F32)
        offs_s[...] = (cum_incl[0:1, :] - tiles) * float(TM_E)
        carry_s[...] = jnp.zeros_like(carry_s)
        cum_col = lax.dot_general(eye, cum_incl.astype(BF16), _NT, preferred_element_type=F32)
        ended = (cum_col[:, 0:1] <= c128.astype(F32)) & (r128 < N_CLASSES)
        tcls = jnp.sum(jnp.where(ended, 1.0, 0.0), axis=0, keepdims=True)
        total = cum_incl[0:1, LANES - 1:LANES]
        row8 = lax.broadcasted_iota(jnp.int32, (8, LANES), 0)
        tinfo_ref[...] = jnp.where(row8 == 0, tcls, total).astype(jnp.int32)

    @pl.when(ph == 1)
    def _():
        tri = jnp.where(lax.broadcasted_iota(jnp.int32, (rows, rows), 0)
                        > lax.broadcasted_iota(jnp.int32, (rows, rows), 1), 1.0, 0.0).astype(BF16)
        rank = jnp.dot(tri, oh.astype(BF16), preferred_element_type=F32)
        dest = jnp.sum(oh * (rank + (offs_s[...] + carry_s[...])), axis=1, keepdims=True)
        carry_s[...] += colsum
        dhi = jnp.floor(dest * (1.0 / 256.0))
        dlo = dest - 256.0 * dhi
        digits = jnp.where(lanef == 0.0, dlo, jnp.where(lanef == 1.0, dhi, 0.0)).astype(BF16)
        dt = lax.dot_general(eye, digits, _NT, preferred_element_type=F32)
        dest_ref[...] = (dt[0:1, :] + 256.0 * dt[1:2, :]).astype(jnp.int32).reshape(1, 1, rows)


def _plan(rinfo, *, tt, skip):
    nb, tl, _ = rinfo.shape
    rows = nb * tt
    nt = tl // tt - skip
    assert (nt * rows) // TM_E + N_CLASSES <= LANES and rows % LANES == 0
    return pl.pallas_call(
        functools.partial(_plan_kernel, rows=rows),
        grid=(2, nt),
        in_specs=[pl.BlockSpec((nb, tt, LANES), lambda ph, i: (0, i + skip, 0))],
        out_specs=[pl.BlockSpec((1, 1, rows), lambda ph, i: (i * ph, 0, 0)),
                   pl.BlockSpec((8, LANES), lambda ph, i: (0, 0))],
        out_shape=[jax.ShapeDtypeStruct((nt, 1, rows), jnp.int32),
                   jax.ShapeDtypeStruct((8, LANES), jnp.int32)],
        scratch_shapes=[pltpu.VMEM((1, LANES), F32)] * 3,
        compiler_params=_cparams(("arbitrary", "arbitrary")),
        name="moe_plan",
    )(rinfo)


def _dispatch_kernel(tinfo_ref, dest_ref, x_ref, xs_ref, zbuf, zsem, sem, *, nb, tt):
    i = pl.program_id(0)

    @pl.when(i == 0)
    def _():
        zbuf[...] = jnp.zeros_like(zbuf)
        n_used = tinfo_ref[1, 0]

        def fill(start):
            def body(j, carry):
                last = (j >= n_used - 1) | (tinfo_ref[0, jnp.minimum(j + 1, LANES - 1)] != tinfo_ref[0, j])

                @pl.when(last)
                def _():
                    cp = pltpu.make_async_copy(
                        zbuf, xs_ref.at[pl.ds(pl.multiple_of(j * TM_E, TM_E), TM_E), :], zsem)
                    if start:
                        cp.start()
                    else:
                        cp.wait()
                return carry
            lax.fori_loop(0, xs_ref.shape[0] // TM_E, body, 0)

        fill(True)
        fill(False)

    def rows(start):
        for b in range(nb):
            def body(t2, carry):
                for par in range(2):
                    t = 2 * t2 + par
                    d = dest_ref[0, 0, b * tt + t]
                    cp = pltpu.make_async_copy(x_ref.at[b, pl.ds(t, 1), :], xs_ref.at[pl.ds(d, 1), :], sem)
                    if start:
                        cp.start(priority=par)
                    else:
                        cp.wait()
                return carry
            lax.fori_loop(0, tt // 2, body, 0, unroll=4)

    rows(True)
    rows(False)


def _dispatch(tinfo, dest, xrow, *, tt, skip):
    nb, tl, _ = xrow.shape
    nt = tl // tt - skip
    n_sorted = ((nt * nb * tt) // TM_E + N_CLASSES) * TM_E
    return pl.pallas_call(
        functools.partial(_dispatch_kernel, nb=nb, tt=tt),
        grid_spec=pltpu.PrefetchScalarGridSpec(
            num_scalar_prefetch=1,
            grid=(nt,),
            in_specs=[pl.BlockSpec((1, 1, nb * tt), lambda i, ti: (i, 0, 0), memory_space=pltpu.SMEM),
                      pl.BlockSpec((nb, tt, ROW_W), lambda i, ti: (0, i + skip, 0))],
            out_specs=pl.BlockSpec(memory_space=pl.ANY),
            scratch_shapes=[pltpu.VMEM((TM_E, ROW_W), F32),
                            pltpu.SemaphoreType.DMA(()),
                            pltpu.SemaphoreType.DMA(())]),
        out_shape=jax.ShapeDtypeStruct((n_sorted, ROW_W), F32),
        compiler_params=_cparams(("arbitrary",)),
        name="moe_dispatch",
    )(tinfo, dest, xrow)


def _class_expert(c, hi):
    g = c // PAIRS_PER_GROUP
    p = c - PAIRS_PER_GROUP * g
    ge3 = (p >= 3).astype(jnp.int32)
    ge5 = (p >= 5).astype(jnp.int32)
    local = (p + 1 - 2 * ge3 - ge5) if hi else (ge3 + ge5)
    return EXPERTS_PER_GROUP * g + local


def _experts_kernel(tinfo_ref, xs_ref, wga_ref, wua_ref, wda_ref, wgb_ref, wub_ref, wdb_ref, ys_ref):
    @pl.when(pl.program_id(0) < tinfo_ref[1, 0])
    def _():
        x = xs_ref[:, :D_MODEL].astype(BF16)

        def ffn(wg_ref, wu_ref, wd_ref):
            hg = jnp.dot(x, wg_ref[0], preferred_element_type=F32)
            hu = jnp.dot(x, wu_ref[0], preferred_element_type=F32)
            hdn = (hg * jax.nn.sigmoid(hg)) * hu
            return jnp.dot(hdn.astype(BF16), wd_ref[0], preferred_element_type=F32)

        ys_ref[...] = (xs_ref[:, D_MODEL + 1:D_MODEL + 2] * ffn(wga_ref, wua_ref, wda_ref)
                       + xs_ref[:, D_MODEL + 2:D_MODEL + 3] * ffn(wgb_ref, wub_ref, wdb_ref))

    @pl.when(pl.program_id(0) >= tinfo_ref[1, 0])
    def _():
        ys_ref[...] = jnp.zeros_like(ys_ref)


def _experts(tinfo, xs, wg, wu, wd):
    n_sorted = xs.shape[0]
    nt = n_sorted // TM_E

    def tile(j, ti):
        return jnp.minimum(j, ti[1, 0] - 1)

    def wmap(hi):
        return lambda j, ti: (_class_expert(ti[0, tile(j, ti)], hi), 0, 0)

    up = lambda hi: pl.BlockSpec((1, D_MODEL, D_EXPERT), wmap(hi))
    down = lambda hi: pl.BlockSpec((1, D_EXPERT, D_MODEL), wmap(hi))
    return pl.pallas_call(
        _experts_kernel,
        grid_spec=pltpu.PrefetchScalarGridSpec(
            num_scalar_prefetch=1,
            grid=(nt,),
            in_specs=[pl.BlockSpec((TM_E, ROW_W), lambda j, ti: (tile(j, ti), 0)),
                      up(0), up(0), down(0), up(1), up(1), down(1)],
            out_specs=pl.BlockSpec((TM_E, D_MODEL), lambda j, ti: (j, 0))),
        out_shape=jax.ShapeDtypeStruct((n_sorted, D_MODEL), F32),
        compiler_params=_cparams(("arbitrary",)),
        name="moe_experts",
    )(tinfo, xs, wg, wu, wd, wg, wu, wd)


def _combine_kernel(dcur_ref, dnxt_ref, x1_ref, nf_ref, ys_ref, y_ref, buf, sem, *, nb, tt):
    i = pl.program_id(0)
    slot = i % 2

    def gather(dref, s, start):
        for b in range(nb):
            def body(t2, carry):
                for par in range(2):
                    t = 2 * t2 + par
                    d = dref[0, 0, b * tt + t]
                    cp = pltpu.make_async_copy(ys_ref.at[pl.ds(d, 1), :], buf.at[s, b, pl.ds(t, 1), :],
                                               sem.at[s])
                    if start:
                        cp.start(priority=par)
                    else:
                        cp.wait()
                return carry
            lax.fori_loop(0, tt // 2, body, 0, unroll=4)

    @pl.when(i == 0)
    def _():
        gather(dcur_ref, 0, True)

    @pl.when(i + 1 < pl.num_programs(0))
    def _():
        gather(dnxt_ref, 1 - slot, True)

    gather(dcur_ref, slot, False)
    x2 = x1_ref[...].reshape(nb * tt, D_MODEL) + buf[slot].reshape(nb * tt, D_MODEL)
    y_ref[...] = _rms(x2, nf_ref[...]).reshape(nb, tt, D_MODEL)


def _combine(dest, x1, nf, ys, *, tt, skip):
    nb, tl, _ = x1.shape
    nt = tl // tt - skip
    dspec = lambda f: pl.BlockSpec((1, 1, nb * tt), f, memory_space=pltpu.SMEM)
    return pl.pallas_call(
        functools.partial(_combine_kernel, nb=nb, tt=tt),
        grid=(nt,),
        in_specs=[dspec(lambda i: (i, 0, 0)),
                  dspec(lambda i: (jnp.minimum(i + 1, nt - 1), 0, 0)),
                  pl.BlockSpec((nb, tt, D_MODEL), lambda i: (0, i + skip, 0)),
                  pl.BlockSpec((1, D_MODEL), lambda i: (0, 0)),
                  pl.BlockSpec(memory_space=pl.ANY)],
        out_specs=pl.BlockSpec((nb, tt, D_MODEL), lambda i: (0, i, 0)),
        out_shape=jax.ShapeDtypeStruct((nb, nt * tt, D_MODEL), F32),
        scratch_shapes=[pltpu.VMEM((2, nb, tt, D_MODEL), F32),
                        pltpu.SemaphoreType.DMA((2,))],
        compiler_params=_cparams(("arbitrary",)),
        name="moe_combine",
    )(dest, dest, x1, nf, ys)


def _block_diag(w):
    per = RNN_SUPER // RNN_BLOCK
    w4 = w.reshape(N_SUPER, per, RNN_BLOCK, RNN_BLOCK)
    eye = jnp.eye(per, dtype=w.dtype)
    bd = jnp.einsum('spij,pq->spiqj', w4, eye)
    return bd.reshape(N_SUPER, RNN_SUPER, RNN_SUPER).astype(BF16)


def _stream(x, meta, conv0, h0, c0, n0, m0, p, *, has_head):
    nb, seq, _ = x.shape
    n_null = N_NULL if has_head else 0
    tt = CHUNK if has_head else seq
    xn, gif = _prep(x, meta, p['norm1_w'], p['w_gif'], p['b_gif'], has_head=has_head)
    w_in = p['w_in']
    tt_mm = MM_TILES * CHUNK if has_head else seq
    xr = _mm(xn, w_in, col0=0, ncols=D_RNN, tn=512, tt=tt_mm, out_dtype=F32, w_t=True, name="in_xr")
    gr = _mm(xn, w_in, col0=D_RNN, ncols=D_RNN, tn=512, tt=tt_mm, out_dtype=BF16, w_t=True, gelu=True,
             name="in_gr")
    qkvo = _mm(xn, w_in, col0=2 * D_RNN, ncols=QKVO, tn=1024, tt=tt_mm, out_dtype=BF16, w_t=True,
               name="in_qkvo")
    mg = _mm(xn, p['w_mg'], col0=0, ncols=2 * D_MODEL, tn=1024, tt=tt_mm, out_dtype=BF16, w_t=True,
             name="in_mg")

    tail0 = jnp.transpose(conv0, (1, 0, 2)).reshape((CONV_W - 1) * nb, D_RNN)
    y_r, h_new, tail = _rglru(xr, gr, tail0, h0, p['conv_w'], p['conv_b'], p['rg_a_b'], p['rg_x_b'],
                              p['rg_lambda'], p['wa_bd'], p['wx_bd'], tt=tt, n_null=n_null)
    conv_new = jnp.transpose(tail.reshape(CONV_W - 1, nb, D_RNN), (1, 0, 2))

    y_m, c_new, n_new, m_new = _mlstm_fused(qkvo, gif, c0, n0, m0, p['mlstm_norm_w'], L=tt, n_null=n_null,
                                            nbk=MLSTM_NBK)

    p_r = _mm(y_r, p['w_proj_rnn'], col0=0, ncols=D_MODEL, tn=1024, tt=tt, out_dtype=F32,
              gate=mg, gate_col0=0, name="proj_rnn")
    mix = _mm(y_m, p['w_proj_mlstm'], col0=0, ncols=D_MODEL, tn=1024, tt=tt, out_dtype=BF16,
              gate=mg, gate_col0=D_MODEL, add=p_r, name="proj_mlstm")
    routed = has_head
    x1, xn2, gates = _outproj(mix, p['w_out'], x, meta, p['norm2_w'], p['w_router'], p['b_router'],
                              tt=tt, has_head=has_head, routed=routed)
    if routed:
        skip = 1
        dest, tinfo = _plan(gates, tt=tt, skip=skip)
        xs = _dispatch(tinfo, dest, xn2, tt=tt, skip=skip)
        ys = _experts(tinfo, xs, p['w_exp_gate'], p['w_exp_up'], p['w_exp_down'])
        y = _combine(dest, x1, p['norm_f_w'], ys, tt=tt, skip=skip)
    else:
        y = _moe(xn2, gates, x1, p['w_exp_gate'], p['w_exp_up'], p['w_exp_down'], p['norm_f_w'],
                 tt=tt, has_head=has_head)
    return y, (conv_new[None], h_new[None], c_new[None], n_new[None], m_new[None])


def kernel(x_prompt, x_sample, state_rglru_conv, state_rglru_h, state_mlstm_C, state_mlstm_n, state_mlstm_m, meta_tokens, norm1_w, w_in, b_gates, conv_w, conv_b, rg_a_w, rg_a_b, rg_x_w, rg_x_b, rg_lambda, mlstm_norm_w, w_proj_rnn, w_proj_mlstm, w_out, norm2_w, w_router_group, b_router_group, w_router_expert, b_router_expert, w_exp_gate, w_exp_up, w_exp_down, norm_f_w):
    l = 0
    w_in_t = jnp.swapaxes(w_in[l], 0, 1)
    pad_r = LANES - N_GROUPS - N_EXPERTS
    p = dict(
        norm1_w=norm1_w[l][None], w_in=w_in_t,
        w_gif=jnp.pad(w_in_t[GIF_OFF:MG_OFF].T, ((0, 0), (0, LANES - 2 * N_HEADS_M))).astype(BF16),
        b_gif=jnp.pad(b_gates[l], (0, LANES - 2 * N_HEADS_M))[None],
        w_mg=w_in_t[MG_OFF:],
        conv_w=conv_w[l], conv_b=conv_b[l][None], rg_a_b=rg_a_b[l][None], rg_x_b=rg_x_b[l][None],
        rg_lambda=rg_lambda[l][None], wa_bd=_block_diag(rg_a_w[l]), wx_bd=_block_diag(rg_x_w[l]),
        mlstm_norm_w=mlstm_norm_w[l][None],
        w_proj_rnn=w_proj_rnn[l], w_proj_mlstm=w_proj_mlstm[l], w_out=w_out[l].astype(BF16),
        norm2_w=norm2_w[l][None],
        w_router=jnp.pad(jnp.concatenate([w_router_group[l], w_router_expert[l]], axis=1),
                         ((0, 0), (0, pad_r))).astype(BF16),
        b_router=jnp.pad(jnp.concatenate([b_router_group[l], b_router_expert[l]]), (0, pad_r))[None],
        w_exp_gate=w_exp_gate[l].astype(BF16), w_exp_up=w_exp_up[l].astype(BF16),
        w_exp_down=w_exp_down[l].astype(BF16), norm_f_w=norm_f_w[None],
    )
    nbp = x_prompt.shape[0]
    dt = x_prompt.dtype
    y_p, st_p = _stream(
        x_prompt, meta_tokens,
        jnp.zeros((nbp, CONV_W - 1, D_RNN), dt), jnp.zeros((nbp, D_RNN), F32),
        jnp.zeros((nbp, N_HEADS_M, DQK_M, DV_M), F32), jnp.zeros((nbp, N_HEADS_M, DQK_M), F32),
        jnp.zeros((nbp, N_HEADS_M), F32), p, has_head=True)
    y_s, st_s = _stream(
        x_sample, meta_tokens, state_rglru_conv[l], state_rglru_h[l], state_mlstm_C[l],
        state_mlstm_n[l], state_mlstm_m[l], p, has_head=False)
    return (y_p, y_s) + st_p + st_s
```

```python
import functools

import jax
import jax.numpy as jnp
from jax import lax
from jax.experimental import pallas as pl
from jax.experimental.pallas import tpu as pltpu

F32 = jnp.float32
BF16 = jnp.bfloat16

D_MODEL = 2048
CHUNK = 64
N_META = 16
N_NULL = CHUNK - N_META
D_RNN = 2560
N_RNN_BLOCKS = 16
RNN_BLOCK = D_RNN // N_RNN_BLOCKS
RNN_SUPER = 640
N_SUPER = D_RNN // RNN_SUPER
CONV_W = 4
LRU_C = 8.0
N_HEADS_M = 8
DV_M = D_MODEL // N_HEADS_M
DQK_M = DV_M // 2
HQK = N_HEADS_M * DQK_M
HV = N_HEADS_M * DV_M
QKVO = 2 * HQK + 2 * HV
N_GROUPS = 4
EXPERTS_PER_GROUP = 4
N_EXPERTS = 16
D_EXPERT = 512
PAIRS_PER_GROUP = 6
N_CLASSES = N_GROUPS * PAIRS_PER_GROUP
MM_TILES = 3
MLSTM_NBK = 2
TM_E = 256
ROW_W = D_MODEL + 128
EPS = 1e-6
LANES = 128
GIF_OFF = 2 * D_RNN + QKVO
MG_OFF = GIF_OFF + 2 * N_HEADS_M
VMEM_LIMIT = 56 * 1024 * 1024
NEG_BIG = -1e30


def _cparams(sem):
    return pltpu.CompilerParams(dimension_semantics=sem, vmem_limit_bytes=VMEM_LIMIT)


def _rms(x, w):
    return x * lax.rsqrt(jnp.mean(x * x, axis=-1, keepdims=True) + EPS) * w


def _softplus(x):
    return jnp.maximum(x, 0.0) + jnp.log1p(jnp.exp(-jnp.abs(x)))


def _head_rows(meta_ref, nb):
    head = jnp.concatenate([jnp.zeros((N_NULL, D_MODEL), F32), meta_ref[...]], axis=0)
    return jnp.broadcast_to(head[None], (nb, CHUNK, D_MODEL)).reshape(nb * CHUNK, D_MODEL)


def _x_map(has_head):
    if has_head:
        return lambda ti: (0, jnp.maximum(ti - 1, 0), 0)
    return lambda ti: (0, ti, 0)


def _prep_kernel(x_ref, meta_ref, nw_ref, wg_ref, bg_ref, xn_ref, gif_ref, *, nb, tt, has_head):
    def emit(x):
        xnb = _rms(x, nw_ref[...]).astype(BF16)
        xn_ref[...] = xnb.reshape(nb, tt, D_MODEL)
        gif = jnp.dot(xnb, wg_ref[...], preferred_element_type=F32) + bg_ref[...]
        gif_ref[...] = gif.reshape(nb, tt, LANES)

    if has_head:
        @pl.when(pl.program_id(0) == 0)
        def _():
            emit(_head_rows(meta_ref, nb))

        @pl.when(pl.program_id(0) > 0)
        def _():
            emit(x_ref[...].reshape(nb * tt, D_MODEL))
    else:
        emit(x_ref[...].reshape(nb * tt, D_MODEL))


def _prep(x, meta, norm_w, w_gif, b_gif, *, has_head):
    nb, seq, _ = x.shape
    tt = CHUNK if has_head else seq
    n_tiles = seq // tt + (1 if has_head else 0)
    tl = n_tiles * tt
    const = lambda ti: (0, 0)
    return pl.pallas_call(
        functools.partial(_prep_kernel, nb=nb, tt=tt, has_head=has_head),
        grid=(n_tiles,),
        in_specs=[pl.BlockSpec((nb, tt, D_MODEL), _x_map(has_head)),
                  pl.BlockSpec((N_META, D_MODEL), const),
                  pl.BlockSpec((1, D_MODEL), const),
                  pl.BlockSpec((D_MODEL, LANES), const),
                  pl.BlockSpec((1, LANES), const)],
        out_specs=[pl.BlockSpec((nb, tt, D_MODEL), lambda ti: (0, ti, 0)),
                   pl.BlockSpec((nb, tt, LANES), lambda ti: (0, ti, 0))],
        out_shape=[jax.ShapeDtypeStruct((nb, tl, D_MODEL), BF16),
                   jax.ShapeDtypeStruct((nb, tl, LANES), F32)],
        compiler_params=_cparams(("arbitrary",)),
        name="prep",
    )(x, meta, norm_w, w_gif, b_gif)


def _gelu_tanh(g):
    return g * (0.5 * (1.0 + jnp.tanh(0.7978845608028654 * (g + 0.044715 * (g * g * g)))))


def _mm_kernel(*refs, has_gate, has_add, w_t, gelu):
    lhs_ref, w_ref = refs[0], refs[1]
    pos = 2
    gate_ref = add_ref = None
    if has_gate:
        gate_ref = refs[pos]
        pos += 1
    if has_add:
        add_ref = refs[pos]
        pos += 1
    out_ref, wbf_ref = refs[pos], refs[pos + 1]
    nb, tt, k = lhs_ref.shape
    tn = out_ref.shape[-1]

    @pl.when(pl.program_id(1) == 0)
    def _():
        w = w_ref[...]
        wbf_ref[...] = (w.T if w_t else w).astype(BF16)

    acc = jnp.dot(lhs_ref[...].reshape(nb * tt, k), wbf_ref[...], preferred_element_type=F32)
    if gelu:
        acc = _gelu_tanh(acc)
    if has_gate:
        acc = jax.nn.sigmoid(gate_ref[...].reshape(nb * tt, tn).astype(F32)) * acc
    if has_add:
        acc = add_ref[...].reshape(nb * tt, tn).astype(F32) + acc
    out_ref[...] = acc.astype(out_ref.dtype).reshape(nb, tt, tn)


def _mm(lhs, w, *, col0, ncols, tn, tt, out_dtype, gate=None, gate_col0=0, add=None, w_t=False, gelu=False,
        name):
    nb, tl, k = lhs.shape
    assert tl % tt == 0 and ncols % tn == 0 and col0 % tn == 0 and gate_col0 % tn == 0
    if w_t:
        w_spec = pl.BlockSpec((tn, k), lambda j, i: (col0 // tn + j, 0))
    else:
        w_spec = pl.BlockSpec((k, tn), lambda j, i: (0, col0 // tn + j))
    in_specs = [pl.BlockSpec((nb, tt, k), lambda j, i: (0, i, 0)), w_spec]
    args = [lhs, w]
    if gate is not None:
        in_specs.append(pl.BlockSpec((nb, tt, tn), lambda j, i: (0, i, gate_col0 // tn + j)))
        args.append(gate)
    if add is not None:
        in_specs.append(pl.BlockSpec((nb, tt, tn), lambda j, i: (0, i, j)))
        args.append(add)
    return pl.pallas_call(
        functools.partial(_mm_kernel, has_gate=gate is not None, has_add=add is not None, w_t=w_t, gelu=gelu),
        grid=(ncols // tn, tl // tt),
        in_specs=in_specs,
        out_specs=pl.BlockSpec((nb, tt, tn), lambda j, i: (0, i, j)),
        out_shape=jax.ShapeDtypeStruct((nb, tl, ncols), out_dtype),
        scratch_shapes=[pltpu.VMEM((k, tn), BF16)],
        compiler_params=_cparams(("arbitrary", "arbitrary")),
        name=name,
    )(*args)


def _merge_kernel(yr_ref, ym_ref, wr_ref, wm_ref, gr_ref, gm_ref, out_ref, wr_bf, wm_bf):
    nb, tt, tn = out_ref.shape
    rows = nb * tt

    @pl.when(pl.program_id(1) == 0)
    def _():
        wr_bf[...] = wr_ref[...].astype(BF16)
        wm_bf[...] = wm_ref[...].astype(BF16)

    pr = jnp.dot(yr_ref[...].reshape(rows, D_RNN), wr_bf[...], preferred_element_type=F32)
    pm = jnp.dot(ym_ref[...].reshape(rows, HV), wm_bf[...], preferred_element_type=F32)
    mix = (jax.nn.sigmoid(gr_ref[...].reshape(rows, tn).astype(F32)) * pr
           + jax.nn.sigmoid(gm_ref[...].reshape(rows, tn).astype(F32)) * pm)
    out_ref[...] = mix.astype(BF16).reshape(nb, tt, tn)


def _merge(y_r, y_m, w_r, w_m, mg, *, tn, tt):
    nb, tl, _ = y_r.shape
    nj = D_MODEL // tn
    return pl.pallas_call(
        _merge_kernel,
        grid=(nj, tl // tt),
        in_specs=[pl.BlockSpec((nb, tt, D_RNN), lambda j, i: (0, i, 0)),
                  pl.BlockSpec((nb, tt, HV), lambda j, i: (0, i, 0)),
                  pl.BlockSpec((D_RNN, tn), lambda j, i: (0, j)),
                  pl.BlockSpec((HV, tn), lambda j, i: (0, j)),
                  pl.BlockSpec((nb, tt, tn), lambda j, i: (0, i, j)),
                  pl.BlockSpec((nb, tt, tn), lambda j, i: (0, i, nj + j))],
        out_specs=pl.BlockSpec((nb, tt, tn), lambda j, i: (0, i, j)),
        out_shape=jax.ShapeDtypeStruct((nb, tl, D_MODEL), BF16),
        scratch_shapes=[pltpu.VMEM((D_RNN, tn), BF16), pltpu.VMEM((HV, tn), BF16)],
        compiler_params=_cparams(("arbitrary", "arbitrary")),
        name="merge",
    )(y_r, y_m, w_r, w_m, mg, mg)


def _rglru_kernel(xr_ref, gr_ref, tail0_ref, h0_ref, cw_ref, cb_ref, ab_ref, xb_ref, lam_ref,
                  wa_ref, wx_ref, y_ref, hl_ref, tail_ref, xp_s, h_s, a_s, u_s, hs_s, *, nb, tt, n_null):
    ti = pl.program_id(1)
    rt = tt * nb
    tb = (CONV_W - 1) * nb

    def time_major(x):
        return jnp.swapaxes(x, 0, 1).reshape(rt, x.shape[-1])

    @pl.when(ti == 0)
    def _():
        xp_s[0:tb, :] = tail0_ref[...]
        h_s[...] = h0_ref[...]

    xp_s[tb:tb + rt, :] = time_major(xr_ref[...])
    cw = cw_ref[...]
    xc = cb_ref[...] + cw[0:1, :] * xp_s[0:rt, :]
    for j in range(1, CONV_W):
        xc = xc + cw[j:j + 1, :] * xp_s[j * nb:j * nb + rt, :]
    xcb = xc.astype(BF16)
    r = jax.nn.sigmoid(jnp.dot(xcb, wa_ref[0], preferred_element_type=F32) + ab_ref[...])
    i = jax.nn.sigmoid(jnp.dot(xcb, wx_ref[0], preferred_element_type=F32) + xb_ref[...])
    log_a = r * ((-LRU_C) * _softplus(-lam_ref[...]))
    a = jnp.exp(log_a)
    a_s[...] = a
    u = jnp.sqrt(-jnp.tanh(log_a) * (a * a + 1.0)) * (i * xc)
    if n_null:
        row = lax.broadcasted_iota(jnp.int32, (rt, 1), 0)
        u = jnp.where((row >= n_null * nb) | (ti > 0), u, 0.0)
    u_s[...] = u

    def step(t, h):
        o = pl.multiple_of(t * nb, nb)
        h = a_s[pl.ds(o, nb), :] * h + u_s[pl.ds(o, nb), :]
        hs_s[pl.ds(o, nb), :] = h
        return h

    h = lax.fori_loop(0, tt, step, h_s[...], unroll=8)
    h_s[...] = h
    h_bm = jnp.swapaxes(hs_s[...].reshape(tt, nb, hs_s.shape[-1]), 0, 1)
    y_ref[...] = (gr_ref[...].astype(F32) * h_bm).astype(BF16)
    xp_s[0:tb, :] = xp_s[rt:rt + tb, :]

    @pl.when(ti == pl.num_programs(1) - 1)
    def _():
        hl_ref[...] = h
        tail_ref[...] = xp_s[rt:rt + tb, :]


def _rglru(xr, gr, tail0, h0, cw, cb, ab, xb, lam, wa, wx, *, tt, n_null):
    nb, tl, _ = xr.shape
    rt = tt * nb
    tb = (CONV_W - 1) * nb
    blk = lambda s, ti: (0, ti, s)
    col = lambda s, ti: (0, s)
    return pl.pallas_call(
        functools.partial(_rglru_kernel, nb=nb, tt=tt, n_null=n_null),
        grid=(N_SUPER, tl // tt),
        in_specs=[pl.BlockSpec((nb, tt, RNN_SUPER), blk),
                  pl.BlockSpec((nb, tt, RNN_SUPER), blk),
                  pl.BlockSpec((tb, RNN_SUPER), col),
                  pl.BlockSpec((nb, RNN_SUPER), col),
                  pl.BlockSpec((CONV_W, RNN_SUPER), col),
                  pl.BlockSpec((1, RNN_SUPER), col),
                  pl.BlockSpec((1, RNN_SUPER), col),
                  pl.BlockSpec((1, RNN_SUPER), col),
                  pl.BlockSpec((1, RNN_SUPER), col),
                  pl.BlockSpec((1, RNN_SUPER, RNN_SUPER), lambda s, ti: (s, 0, 0)),
                  pl.BlockSpec((1, RNN_SUPER, RNN_SUPER), lambda s, ti: (s, 0, 0))],
        out_specs=[pl.BlockSpec((nb, tt, RNN_SUPER), blk),
                   pl.BlockSpec((nb, RNN_SUPER), col),
                   pl.BlockSpec((tb, RNN_SUPER), col)],
        out_shape=[jax.ShapeDtypeStruct((nb, tl, D_RNN), BF16),
                   jax.ShapeDtypeStruct((nb, D_RNN), F32),
                   jax.ShapeDtypeStruct((tb, D_RNN), F32)],
        scratch_shapes=[pltpu.VMEM((rt + tb, RNN_SUPER), F32),
                        pltpu.VMEM((nb, RNN_SUPER), F32),
                        pltpu.VMEM((rt, RNN_SUPER), F32),
                        pltpu.VMEM((rt, RNN_SUPER), F32),
                        pltpu.VMEM((rt, RNN_SUPER), F32)],
        compiler_params=_cparams(("arbitrary", "arbitrary")),
        name="rglru",
    )(xr, gr, tail0, h0, cw, cb, ab, xb, lam, wa, wx)


def _mlstm_batch(c, q_ref, k_ref, v_ref, og_ref, gif_ref, nw_ref, y_ref, c_s, n_s, m_s, *, L, n_null):
    g = gif_ref[...]
    lane = lax.broadcasted_iota(jnp.int32, (L, LANES), 1)
    row = lax.broadcasted_iota(jnp.int32, (L, LANES), 0)
    is_f = (lane >= N_HEADS_M) & (lane < 2 * N_HEADS_M)
    lf = jnp.where(is_f, jnp.minimum(g, 0.0) - jnp.log1p(jnp.exp(-jnp.abs(g))), 0.0)
    li = g
    if n_null:
        null = (row < n_null) & (c == 0)
        lf = jnp.where(null, 0.0, lf)
        li = jnp.where(null, NEG_BIG, li)
    rr = lax.broadcasted_iota(jnp.int32, (L, L), 0)
    cc = lax.broadcasted_iota(jnp.int32, (L, L), 1)
    causal = rr >= cc
    tri = causal.astype(F32)
    bfull = jnp.dot(tri, lf, preferred_element_type=F32, precision=lax.Precision.HIGHEST)
    b_t = bfull.T
    li_t = li.T
    m_all = m_s[...]
    m_new = m_all
    lane1 = lax.broadcasted_iota(jnp.int32, (1, LANES), 1)
    scale = DQK_M ** -0.5
    nw = nw_ref[...]
    dn_nt = (((1,), (1,)), ((), ()))
    dn_tn = (((0,), (0,)), ((), ()))
    for h in range(N_HEADS_M):
        bcol = bfull[:, N_HEADS_M + h:N_HEADS_M + h + 1]
        brow = b_t[N_HEADS_M + h:N_HEADS_M + h + 1, :]
        igrow = li_t[h:h + 1, :]
        igcol = li[:, h:h + 1]
        m0h = m_all[:, h:h + 1]
        log_w = jnp.where(causal, bcol - brow + igrow, -jnp.inf)
        log_inter = bcol + m0h
        mvec = jnp.maximum(log_inter, jnp.max(log_w, axis=1, keepdims=True))
        w_intra = jnp.exp(log_w - mvec)
        w_inter = jnp.exp(log_inter - mvec)
        qh = q_ref[:, h * DQK_M:(h + 1) * DQK_M]
        kh = k_ref[:, h * DQK_M:(h + 1) * DQK_M]
        vh = v_ref[:, h * DV_M:(h + 1) * DV_M]
        s = lax.dot_general(qh, kh, dn_nt, preferred_element_type=F32) * (scale * w_intra)
        ch = c_s[h]
        nh = n_s[h:h + 1, :]
        num = (jnp.dot(s.astype(BF16), vh, preferred_element_type=F32)
               + w_inter * jnp.dot(qh, ch.astype(BF16), preferred_element_type=F32))
        den = (jnp.sum(s, axis=1, keepdims=True)
               + w_inter * jnp.sum(qh.astype(F32) * nh, axis=1, keepdims=True))
        denom = jnp.maximum(jnp.abs(den), jnp.exp(-mvec))
        hh = num / denom
        hh = hh * lax.rsqrt(jnp.mean(hh * hh, axis=-1, keepdims=True) + EPS)
        hh = hh * nw[:, h * DV_M:(h + 1) * DV_M]
        og = og_ref[:, h * DV_M:(h + 1) * DV_M].astype(F32)
        y_ref[:, h * DV_M:(h + 1) * DV_M] = (hh * jax.nn.sigmoid(og)).astype(BF16)
        m_end = mvec[L - 1:L, :]
        b_end = bcol[L - 1:L, :]
        w_end = jnp.exp(b_end - bcol + igcol - m_end)
        decay = jnp.exp(b_end + m0h - m_end)
        kw = kh.astype(F32) * (w_end * scale)
        c_s[h] = decay * ch + lax.dot_general(kw.astype(BF16), vh, dn_tn, preferred_element_type=F32)
        n_s[h:h + 1, :] = decay * nh + jnp.sum(kw, axis=0, keepdims=True)
        m_new = jnp.where(lane1 == h, m_end, m_new)
    m_s[...] = m_new


def _mlstm_kernel(q_ref, k_ref, v_ref, og_ref, gif_ref, c0_ref, n0_ref, m0_ref, nw_ref,
                  y_ref, co_ref, no_ref, mo_ref, c_s, n_s, m_s, *, L, n_null, nbk):
    c = pl.program_id(1)

    @pl.when(c == 0)
    def _():
        c_s[...] = c0_ref[...]
        n_s[...] = n0_ref[...]
        m_s[...] = m0_ref[...]

    for bi in range(nbk):
        _mlstm_batch(c, q_ref.at[bi], k_ref.at[bi], v_ref.at[bi], og_ref.at[bi], gif_ref.at[bi], nw_ref,
                     y_ref.at[bi], c_s.at[bi], n_s.at[bi], m_s.at[bi], L=L, n_null=n_null)

    @pl.when(c == pl.num_programs(1) - 1)
    def _():
        co_ref[...] = c_s[...]
        no_ref[...] = n_s[...]
        mo_ref[...] = m_s[...]


def _mlstm(qkvo, gif, c0, n0, m0, nw, *, L, n_null, nbk):
    nb, tl, _ = qkvo.shape
    assert nb % nbk == 0
    st = lambda b, c: (b, 0, 0)
    return pl.pallas_call(
        functools.partial(_mlstm_kernel, L=L, n_null=n_null, nbk=nbk),
        grid=(nb // nbk, tl // L),
        in_specs=[pl.BlockSpec((nbk, L, HQK), lambda b, c: (b, c, 0)),
                  pl.BlockSpec((nbk, L, HQK), lambda b, c: (b, c, 1)),
                  pl.BlockSpec((nbk, L, HV), lambda b, c: (b, c, 1)),
                  pl.BlockSpec((nbk, L, HV), lambda b, c: (b, c, 2)),
                  pl.BlockSpec((nbk, L, LANES), lambda b, c: (b, c, 0)),
                  pl.BlockSpec((nbk, N_HEADS_M, DQK_M, DV_M), lambda b, c: (b, 0, 0, 0)),
                  pl.BlockSpec((nbk, N_HEADS_M, DQK_M), st),
                  pl.BlockSpec((nbk, 1, LANES), st),
                  pl.BlockSpec((1, HV), lambda b, c: (0, 0))],
        out_specs=[pl.BlockSpec((nbk, L, HV), lambda b, c: (b, c, 0)),
                   pl.BlockSpec((nbk, N_HEADS_M, DQK_M, DV_M), lambda b, c: (b, 0, 0, 0)),
                   pl.BlockSpec((nbk, N_HEADS_M, DQK_M), st),
                   pl.BlockSpec((nbk, 1, LANES), st)],
        out_shape=[jax.ShapeDtypeStruct((nb, tl, HV), BF16),
                   jax.ShapeDtypeStruct((nb, N_HEADS_M, DQK_M, DV_M), F32),
                   jax.ShapeDtypeStruct((nb, N_HEADS_M, DQK_M), F32),
                   jax.ShapeDtypeStruct((nb, 1, LANES), F32)],
        scratch_shapes=[pltpu.VMEM((nbk, N_HEADS_M, DQK_M, DV_M), F32),
                        pltpu.VMEM((nbk, N_HEADS_M, DQK_M), F32),
                        pltpu.VMEM((nbk, 1, LANES), F32)],
        compiler_params=_cparams(("arbitrary", "arbitrary")),
        name="mlstm",
    )(qkvo, qkvo, qkvo, qkvo, gif, c0, n0, m0, nw)


DVA = DV_M + 128


def _mlstm_gates(c, gif_ref, m_s, *, L, n_null):
    hm = N_HEADS_M
    g = gif_ref[...]
    if L < LANES:
        g = jnp.concatenate([g, jnp.zeros((LANES - L, LANES), F32)], axis=0)
    g_t = g.T
    lane = lax.broadcasted_iota(jnp.int32, (hm, LANES), 1)
    valid = lane < L
    li = g_t[0:hm, :]
    lfr = g_t[hm:2 * hm, :]
    lf = jnp.where(valid, jnp.minimum(lfr, 0.0) - jnp.log1p(jnp.exp(-jnp.abs(lfr))), 0.0)
    if n_null:
        null = (lane < n_null) & (c == 0)
        lf = jnp.where(null, 0.0, lf)
        li = jnp.where(null, NEG_BIG, li)
    r128 = lax.broadcasted_iota(jnp.int32, (LANES, LANES), 0)
    c128 = lax.broadcasted_iota(jnp.int32, (LANES, LANES), 1)
    upper = jnp.where(r128 <= c128, 1.0, 0.0)
    b = jnp.dot(lf, upper, preferred_element_type=F32, precision=lax.Precision.HIGHEST)
    a = li - b
    cm = jnp.where(valid, a, -jnp.inf)
    sh = 1
    while sh < L:
        cm = jnp.maximum(cm, jnp.where(lane >= sh, pltpu.roll(cm, sh, axis=1), -jnp.inf))
        sh *= 2
    m0 = m_s[...]
    big_m = jnp.maximum(cm, m0)
    m_last = jnp.max(jnp.where(valid, big_m, -jnp.inf), axis=1, keepdims=True)
    b_last = jnp.sum(lf, axis=1, keepdims=True)
    scale = DQK_M ** -0.5
    e_mv = jnp.exp(-(b + big_m))
    w_int = jnp.exp(m0 - big_m)
    w_end = jnp.exp(a - m_last) * scale
    decay = jnp.exp(m0 - m_last)
    m_s[...] = jnp.broadcast_to(b_last + m_last, (hm, LANES))
    cols = jnp.concatenate([big_m, e_mv, w_int, w_end, jnp.zeros((LANES - 4 * hm, LANES), F32)], axis=0).T
    return a, cols, decay


def _mlstm_heads(gates, q_ref, k_ref, v_ref, og_ref, nw_ref, y_ref, c_s, *, L):
    hm = N_HEADS_M
    scale = DQK_M ** -0.5
    rr = lax.broadcasted_iota(jnp.int32, (L, L), 0)
    cc = lax.broadcasted_iota(jnp.int32, (L, L), 1)
    causal = rr >= cc
    ones_col = jnp.where(lax.broadcasted_iota(jnp.int32, (L, 128), 1) == 0, 1.0, 0.0).astype(BF16)
    nw = nw_ref[...]
    dn_nt = (((1,), (1,)), ((), ()))
    dn_tn = (((0,), (0,)), ((), ()))
    bh = [(bi, h) for bi in range(len(gates)) for h in range(hm)]
    col = lambda bi, k, h: gates[bi][1][0:L, k * hm + h:k * hm + h + 1]
    qs = [q_ref[bi, :, h * DQK_M:(h + 1) * DQK_M] for bi, h in bh]
    ks = [k_ref[bi, :, h * DQK_M:(h + 1) * DQK_M] for bi, h in bh]
    v_augs = [jnp.concatenate([v_ref[bi, :, h * DV_M:(h + 1) * DV_M], ones_col], axis=1) for bi, h in bh]
    c_augs = [c_s[bi, h] for bi, h in bh]
    s_raw = [lax.dot_general(q, k, dn_nt, preferred_element_type=F32) for q, k in zip(qs, ks)]
    inter = [jnp.dot((qs[i].astype(F32) * col(bi, 2, h)).astype(BF16), c_augs[i].astype(BF16),
                     preferred_element_type=F32) for i, (bi, h) in enumerate(bh)]
    upd = [lax.dot_general((ks[i].astype(F32) * col(bi, 3, h)).astype(BF16), v_augs[i], dn_tn,
                           preferred_element_type=F32) for i, (bi, h) in enumerate(bh)]
    s_w = [(s_raw[i] * (scale * jnp.where(causal, jnp.exp(gates[bi][0][h:h + 1, 0:L] - col(bi, 0, h)), 0.0))
            ).astype(BF16) for i, (bi, h) in enumerate(bh)]
    res = [jnp.dot(s_w[i], v_augs[i], preferred_element_type=F32) + inter[i] for i in range(len(bh))]
    for i, (bi, h) in enumerate(bh):
        c_s[bi, h] = gates[bi][2][h:h + 1, 0:1] * c_augs[i] + upd[i]
    for i, (bi, h) in enumerate(bh):
        num = res[i][:, :DV_M]
        denom = jnp.maximum(jnp.abs(res[i][:, DV_M:DV_M + 1]), col(bi, 1, h))
        hh = num / denom
        hh = hh * lax.rsqrt(jnp.mean(hh * hh, axis=-1, keepdims=True) + EPS)
        hh = hh * nw[:, h * DV_M:(h + 1) * DV_M]
        og = og_ref[bi, :, h * DV_M:(h + 1) * DV_M].astype(F32)
        y_ref[bi, :, h * DV_M:(h + 1) * DV_M] = (hh * jax.nn.sigmoid(og)).astype(BF16)


def _mlstm_fused_kernel(q_ref, k_ref, v_ref, og_ref, gif_ref, c0_ref, n0_ref, m0_ref, nw_ref,
                        y_ref, co_ref, no_ref, mo_ref, c_s, m_s, *, L, n_null, nbk):
    c = pl.program_id(1)
    bh = [(bi, h) for bi in range(nbk) for h in range(N_HEADS_M)]

    @pl.when(c == 0)
    def _():
        m_s[...] = m0_ref[...]
        for bi, h in bh:
            c_s[bi, h, :, :DV_M] = c0_ref[bi, h]
            n_tile = jnp.concatenate([n0_ref[bi, h:h + 1, :], jnp.zeros((DQK_M - 1, DQK_M), F32)], axis=0)
            c_s[bi, h, :, DV_M:] = n_tile.T

    gates = [_mlstm_gates(c, gif_ref.at[bi], m_s.at[bi], L=L, n_null=n_null) for bi in range(nbk)]
    _mlstm_heads(gates, q_ref, k_ref, v_ref, og_ref, nw_ref, y_ref, c_s, L=L)

    @pl.when(c == pl.num_programs(1) - 1)
    def _():
        mo_ref[...] = m_s[...]
        for bi, h in bh:
            co_ref[bi, h] = c_s[bi, h, :, :DV_M]
            no_ref[bi, h:h + 1, :] = c_s[bi, h, :, DV_M:].T[0:1, :]


def _mlstm_fused(qkvo, gif, c0, n0, m0, nw, *, L, n_null, nbk):
    nb, tl, _ = qkvo.shape
    assert nb % nbk == 0
    hm = N_HEADS_M
    assert DQK_M == LANES and DVA - DV_M == DQK_M
    m_rep0 = jnp.broadcast_to(m0[..., None], (nb, hm, LANES))
    st4 = lambda b, c: (b, 0, 0, 0)
    st3 = lambda b, c: (b, 0, 0)
    y, c_new, n_new, m_rep = pl.pallas_call(
        functools.partial(_mlstm_fused_kernel, L=L, n_null=n_null, nbk=nbk),
        grid=(nb // nbk, tl // L),
        in_specs=[pl.BlockSpec((nbk, L, HQK), lambda b, c: (b, c, 0)),
                  pl.BlockSpec((nbk, L, HQK), lambda b, c: (b, c, 1)),
                  pl.BlockSpec((nbk, L, HV), lambda b, c: (b, c, 1)),
                  pl.BlockSpec((nbk, L, HV), lambda b, c: (b, c, 2)),
                  pl.BlockSpec((nbk, L, LANES), lambda b, c: (b, c, 0)),
                  pl.BlockSpec((nbk, hm, DQK_M, DV_M), st4),
                  pl.BlockSpec((nbk, hm, DQK_M), st3),
                  pl.BlockSpec((nbk, hm, LANES), st3),
                  pl.BlockSpec((1, HV), lambda b, c: (0, 0))],
        out_specs=[pl.BlockSpec((nbk, L, HV), lambda b, c: (b, c, 0)),
                   pl.BlockSpec((nbk, hm, DQK_M, DV_M), st4),
                   pl.BlockSpec((nbk, hm, DQK_M), st3),
                   pl.BlockSpec((nbk, hm, LANES), st3)],
        out_shape=[jax.ShapeDtypeStruct((nb, tl, HV), BF16),
                   jax.ShapeDtypeStruct((nb, hm, DQK_M, DV_M), F32),
                   jax.ShapeDtypeStruct((nb, hm, DQK_M), F32),
                   jax.ShapeDtypeStruct((nb, hm, LANES), F32)],
        scratch_shapes=[pltpu.VMEM((nbk, hm, DQK_M, DVA), F32),
                        pltpu.VMEM((nbk, hm, LANES), F32)],
        compiler_params=_cparams(("arbitrary", "arbitrary")),
        name="mlstm",
    )(qkvo, qkvo, qkvo, qkvo, gif, c0, n0, m_rep0, nw)
    return y, c_new, n_new, m_rep[..., 0]


def _route(logits, dense):
    tm = logits.shape[0]
    lane = lax.broadcasted_iota(jnp.int32, (tm, LANES), 1)
    lanef = lane.astype(F32)
    gl = jnp.where(lane < N_GROUPS, logits, -jnp.inf)
    gmax = jnp.max(gl, axis=1, keepdims=True)
    gidx = jnp.min(jnp.where(gl == gmax, lanef, float(LANES)), axis=1, keepdims=True)
    gw = 1.0 / jnp.sum(jnp.exp(gl - gmax), axis=1, keepdims=True)
    lo = float(N_GROUPS) + float(EXPERTS_PER_GROUP) * gidx
    sel = (lanef >= lo) & (lanef < lo + float(EXPERTS_PER_GROUP))
    ev = jnp.where(sel, logits, -jnp.inf)
    v1 = jnp.max(ev, axis=1, keepdims=True)
    i1 = jnp.min(jnp.where(ev == v1, lanef, float(LANES)), axis=1, keepdims=True)
    ev2 = jnp.where(lanef == i1, -jnp.inf, ev)
    v2 = jnp.max(ev2, axis=1, keepdims=True)
    i2 = jnp.min(jnp.where(ev2 == v2, lanef, float(LANES)), axis=1, keepdims=True)
    t = jnp.exp(v2 - v1)
    w1 = gw / (1.0 + t)
    w2 = gw * (t / (1.0 + t))
    if dense:
        return jnp.where(lanef == i1, w1, jnp.where(lanef == i2, w2, 0.0))
    l1 = i1 - lo
    l2 = i2 - lo
    first = l1 < l2
    a = jnp.minimum(l1, l2)
    b = jnp.maximum(l1, l2)
    pair = jnp.where(a == 0.0, b - 1.0, jnp.where(a == 1.0, b + 1.0, 5.0))
    cls = float(PAIRS_PER_GROUP) * gidx + pair
    return jnp.where(lane == 0, cls,
                     jnp.where(lane == 1, jnp.where(first, w1, w2),
                               jnp.where(lane == 2, jnp.where(first, w2, w1), 0.0)))


def _outproj_kernel(mix_ref, w_ref, x_ref, meta_ref, n2_ref, wr_ref, br_ref, x1_ref, xn2_ref, gates_ref,
                    *, nb, tt, has_head, routed):
    def emit(x):
        mix = mix_ref[...].reshape(nb * tt, D_MODEL)
        x1 = x + jnp.dot(mix, w_ref[...], preferred_element_type=F32)
        x1_ref[...] = x1.reshape(nb, tt, D_MODEL)
        xn2 = _rms(x1, n2_ref[...]).astype(BF16)
        logits = jnp.dot(xn2, wr_ref[...], preferred_element_type=F32) + br_ref[...]
        info = _route(logits, dense=not routed)
        gates_ref[...] = info.reshape(nb, tt, LANES)
        if routed:
            xn2_ref[:, :, :D_MODEL] = xn2.astype(F32).reshape(nb, tt, D_MODEL)
            xn2_ref[:, :, D_MODEL:] = info.reshape(nb, tt, LANES)
        else:
            xn2_ref[...] = xn2.reshape(nb, tt, D_MODEL)

    if has_head:
        @pl.when(pl.program_id(0) == 0)
        def _():
            emit(_head_rows(meta_ref, nb))

        @pl.when(pl.program_id(0) > 0)
        def _():
            emit(x_ref[...].reshape(nb * tt, D_MODEL))
    else:
        emit(x_ref[...].reshape(nb * tt, D_MODEL))


def _outproj(mix, w_out, x, meta, n2, w_router, b_router, *, tt, has_head, routed):
    nb, tl, _ = mix.shape
    const = lambda i: (0, 0)
    blk = lambda i: (0, i, 0)
    xn2_w, xn2_dt = (ROW_W, F32) if routed else (D_MODEL, BF16)
    return pl.pallas_call(
        functools.partial(_outproj_kernel, nb=nb, tt=tt, has_head=has_head, routed=routed),
        grid=(tl // tt,),
        in_specs=[pl.BlockSpec((nb, tt, D_MODEL), blk),
                  pl.BlockSpec((D_MODEL, D_MODEL), const, pipeline_mode=pl.Buffered(1)),
                  pl.BlockSpec((nb, tt, D_MODEL), _x_map(has_head)),
                  pl.BlockSpec((N_META, D_MODEL), const),
                  pl.BlockSpec((1, D_MODEL), const),
                  pl.BlockSpec((D_MODEL, LANES), const),
                  pl.BlockSpec((1, LANES), const)],
        out_specs=[pl.BlockSpec((nb, tt, D_MODEL), blk),
                   pl.BlockSpec((nb, tt, xn2_w), blk),
                   pl.BlockSpec((nb, tt, LANES), blk)],
        out_shape=[jax.ShapeDtypeStruct((nb, tl, D_MODEL), F32),
                   jax.ShapeDtypeStruct((nb, tl, xn2_w), xn2_dt),
                   jax.ShapeDtypeStruct((nb, tl, LANES), F32)],
        compiler_params=_cparams(("arbitrary",)),
        name="outproj",
    )(mix, w_out, x, meta, n2, w_router, b_router)


def _moe_kernel(xn_ref, gates_ref, x1_ref, wg_ref, wu_ref, wd_ref, nf_ref, y_ref, acc_s, *, nb, tt):
    e = pl.program_id(1)
    rows = nb * tt

    @pl.when(e == 0)
    def _():
        acc_s[...] = jnp.zeros_like(acc_s)

    x = xn_ref[...].reshape(rows, D_MODEL)
    hg = jnp.dot(x, wg_ref[0], preferred_element_type=F32)
    hu = jnp.dot(x, wu_ref[0], preferred_element_type=F32)
    hdn = (hg * jax.nn.sigmoid(hg)) * hu
    yd = jnp.dot(hdn.astype(BF16), wd_ref[0], preferred_element_type=F32)
    gates = gates_ref[...].reshape(rows, LANES)
    lane = lax.broadcasted_iota(jnp.int32, gates.shape, 1)
    gcol = jnp.sum(jnp.where(lane == e + N_GROUPS, gates, 0.0), axis=1, keepdims=True)
    acc_s[...] += gcol * yd

    @pl.when(e == pl.num_programs(1) - 1)
    def _():
        x2 = x1_ref[...].reshape(rows, D_MODEL) + acc_s[...]
        y_ref[...] = _rms(x2, nf_ref[...]).reshape(nb, tt, D_MODEL)


def _moe(xn2, gates, x1, wg, wu, wd, nf, *, tt, has_head):
    nb, tl, _ = xn2.shape
    blk = lambda i, e: (0, i, 0)
    wblk = lambda i, e: (e, 0, 0)
    if has_head:
        out_map = lambda i, e: (0, jnp.maximum(i - 1, 0), 0)
        seq = tl - tt
    else:
        out_map = blk
        seq = tl
    return pl.pallas_call(
        functools.partial(_moe_kernel, nb=nb, tt=tt),
        grid=(tl // tt, N_EXPERTS),
        in_specs=[pl.BlockSpec((nb, tt, D_MODEL), blk),
                  pl.BlockSpec((nb, tt, LANES), blk),
                  pl.BlockSpec((nb, tt, D_MODEL), blk),
                  pl.BlockSpec((1, D_MODEL, D_EXPERT), wblk),
                  pl.BlockSpec((1, D_MODEL, D_EXPERT), wblk),
                  pl.BlockSpec((1, D_EXPERT, D_MODEL), wblk),
                  pl.BlockSpec((1, D_MODEL), lambda i, e: (0, 0))],
        out_specs=pl.BlockSpec((nb, tt, D_MODEL), out_map),
        out_shape=jax.ShapeDtypeStruct((nb, seq, D_MODEL), F32),
        scratch_shapes=[pltpu.VMEM((nb * tt, D_MODEL), F32)],
        compiler_params=_cparams(("arbitrary", "arbitrary")),
        name="moe",
    )(xn2, gates, x1, wg, wu, wd, nf)


_NT = (((1,), (1,)), ((), ()))


def _plan_kernel(ri_ref, dest_ref, tinfo_ref, cnt_s, carry_s, offs_s, *, rows):
    ph = pl.program_id(0)
    i = pl.program_id(1)
    ri = ri_ref[...].reshape(rows, LANES)
    lanef = lax.broadcasted_iota(jnp.int32, (rows, LANES), 1).astype(F32)
    oh = jnp.where(lanef == ri[:, 0:1], 1.0, 0.0)
    colsum = jnp.sum(oh, axis=0, keepdims=True)
    r128 = lax.broadcasted_iota(jnp.int32, (LANES, LANES), 0)
    c128 = lax.broadcasted_iota(jnp.int32, (LANES, LANES), 1)
    eye = jnp.where(r128 == c128, 1.0, 0.0).astype(BF16)

    @pl.when((ph == 0) & (i == 0))
    def _():
        cnt_s[...] = jnp.zeros_like(cnt_s)

    @pl.when(ph == 0)
    def _():
        cnt_s[...] += colsum

    @pl.when((ph == 1) & (i == 0))
    def _():
        tiles = jnp.floor((cnt_s[...] + float(TM_E - 1)) * (1.0 / TM_E))
        upper = jnp.where(r128 <= c128, 1.0, 0.0).astype(BF16)
        cum_incl = jnp.dot(jnp.broadcast_to(tiles, (8, LANES)).astype(BF16), upper, preferred_element_type=F32)
        offs_s[...] = (cum_incl[0:1, :] - tiles) * float(TM_E)
        carry_s[...] = jnp.zeros_like(carry_s)
        cum_col = lax.dot_general(eye, cum_incl.astype(BF16), _NT, preferred_element_type=F32)
        ended = (cum_col[:, 0:1] <= c128.astype(F32)) & (r128 < N_CLASSES)
        tcls = jnp.sum(jnp.where(ended, 1.0, 0.0), axis=0, keepdims=True)
        total = cum_incl[0:1, LANES - 1:LANES]
        row8 = lax.broadcasted_iota(jnp.int32, (8, LANES), 0)
        tinfo_ref[...] = jnp.where(row8 == 0, tcls, total).astype(jnp.int32)

    @pl.when(ph == 1)
    def _():
        tri = jnp.where(lax.broadcasted_iota(jnp.int32, (rows, rows), 0)
                        > lax.broadcasted_iota(jnp.int32, (rows, rows), 1), 1.0, 0.0).astype(BF16)
        rank = jnp.dot(tri, oh.astype(BF16), preferred_element_type=F32)
        dest = jnp.sum(oh * (rank + (offs_s[...] + carry_s[...])), axis=1, keepdims=True)
        carry_s[...] += colsum
        dhi = jnp.floor(dest * (1.0 / 256.0))
        dlo = dest - 256.0 * dhi
        digits = jnp.where(lanef == 0.0, dlo, jnp.where(lanef == 1.0, dhi, 0.0)).astype(BF16)
        dt = lax.dot_general(eye, digits, _NT, preferred_element_type=F32)
        dest_ref[...] = (dt[0:1, :] + 256.0 * dt[1:2, :]).astype(jnp.int32).reshape(1, 1, rows)


def _plan(rinfo, *, tt, skip):
    nb, tl, _ = rinfo.shape
    rows = nb * tt
    nt = tl // tt - skip
    assert (nt * rows) // TM_E + N_CLASSES <= LANES and rows % LANES == 0
    return pl.pallas_call(
        functools.partial(_plan_kernel, rows=rows),
        grid=(2, nt),
        in_specs=[pl.BlockSpec((nb, tt, LANES), lambda ph, i: (0, i + skip, 0))],
        out_specs=[pl.BlockSpec((1, 1, rows), lambda ph, i: (i * ph, 0, 0)),
                   pl.BlockSpec((8, LANES), lambda ph, i: (0, 0))],
        out_shape=[jax.ShapeDtypeStruct((nt, 1, rows), jnp.int32),
                   jax.ShapeDtypeStruct((8, LANES), jnp.int32)],
        scratch_shapes=[pltpu.VMEM((1, LANES), F32)] * 3,
        compiler_params=_cparams(("arbitrary", "arbitrary")),
        name="moe_plan",
    )(rinfo)


def _dispatch_kernel(tinfo_ref, dest_ref, x_ref, xs_ref, zbuf, zsem, sem, *, nb, tt):
    i = pl.program_id(0)

    @pl.when(i == 0)
    def _():
        zbuf[...] = jnp.zeros_like(zbuf)
        n_used = tinfo_ref[1, 0]

        def fill(start):
            def body(j, carry):
                last = (j >= n_used - 1) | (tinfo_ref[0, jnp.minimum(j + 1, LANES - 1)] != tinfo_ref[0, j])

                @pl.when(last)
                def _():
                    cp = pltpu.make_async_copy(
                        zbuf, xs_ref.at[pl.ds(pl.multiple_of(j * TM_E, TM_E), TM_E), :], zsem)
                    if start:
                        cp.start()
                    else:
                        cp.wait()
                return carry
            lax.fori_loop(0, xs_ref.shape[0] // TM_E, body, 0)

        fill(True)
        fill(False)

    def rows(start):
        for b in range(nb):
            def body(t2, carry):
                for par in range(2):
                    t = 2 * t2 + par
                    d = dest_ref[0, 0, b * tt + t]
                    cp = pltpu.make_async_copy(x_ref.at[b, pl.ds(t, 1), :], xs_ref.at[pl.ds(d, 1), :], sem)
                    if start:
                        cp.start(priority=par)
                    else:
                        cp.wait()
                return carry
            lax.fori_loop(0, tt // 2, body, 0, unroll=4)

    rows(True)
    whole = xs_ref.at[pl.ds(0, nb * tt), :]
    pltpu.make_async_copy(whole, whole, sem).wait()


def _dispatch(tinfo, dest, xrow, *, tt, skip):
    nb, tl, _ = xrow.shape
    nt = tl // tt - skip
    n_sorted = ((nt * nb * tt) // TM_E + N_CLASSES) * TM_E
    return pl.pallas_call(
        functools.partial(_dispatch_kernel, nb=nb, tt=tt),
        grid_spec=pltpu.PrefetchScalarGridSpec(
            num_scalar_prefetch=1,
            grid=(nt,),
            in_specs=[pl.BlockSpec((1, 1, nb * tt), lambda i, ti: (i, 0, 0), memory_space=pltpu.SMEM),
                      pl.BlockSpec((nb, tt, ROW_W), lambda i, ti: (0, i + skip, 0))],
            out_specs=pl.BlockSpec(memory_space=pl.ANY),
            scratch_shapes=[pltpu.VMEM((TM_E, ROW_W), F32),
                            pltpu.SemaphoreType.DMA(()),
                            pltpu.SemaphoreType.DMA(())]),
        out_shape=jax.ShapeDtypeStruct((n_sorted, ROW_W), F32),
        compiler_params=_cparams(("arbitrary",)),
        name="moe_dispatch",
    )(tinfo, dest, xrow)


def _class_expert(c, hi):
    g = c // PAIRS_PER_GROUP
    p = c - PAIRS_PER_GROUP * g
    ge3 = (p >= 3).astype(jnp.int32)
    ge5 = (p >= 5).astype(jnp.int32)
    local = (p + 1 - 2 * ge3 - ge5) if hi else (ge3 + ge5)
    return EXPERTS_PER_GROUP * g + local


def _experts_kernel(tinfo_ref, xs_ref, wga_ref, wua_ref, wda_ref, wgb_ref, wub_ref, wdb_ref, ys_ref):
    @pl.when(pl.program_id(0) < tinfo_ref[1, 0])
    def _():
        x = xs_ref[:, :D_MODEL].astype(BF16)

        def ffn(wg_ref, wu_ref, wd_ref):
            hg = jnp.dot(x, wg_ref[0], preferred_element_type=F32)
            hu = jnp.dot(x, wu_ref[0], preferred_element_type=F32)
            hdn = (hg * jax.nn.sigmoid(hg)) * hu
            return jnp.dot(hdn.astype(BF16), wd_ref[0], preferred_element_type=F32)

        ys_ref[...] = (xs_ref[:, D_MODEL + 1:D_MODEL + 2] * ffn(wga_ref, wua_ref, wda_ref)
                       + xs_ref[:, D_MODEL + 2:D_MODEL + 3] * ffn(wgb_ref, wub_ref, wdb_ref))

    @pl.when(pl.program_id(0) >= tinfo_ref[1, 0])
    def _():
        ys_ref[...] = jnp.zeros_like(ys_ref)


def _experts(tinfo, xs, wg, wu, wd):
    n_sorted = xs.shape[0]
    nt = n_sorted // TM_E

    def tile(j, ti):
        return jnp.minimum(j, ti[1, 0] - 1)

    def wmap(hi):
        return lambda j, ti: (_class_expert(ti[0, tile(j, ti)], hi), 0, 0)

    up = lambda hi: pl.BlockSpec((1, D_MODEL, D_EXPERT), wmap(hi))
    down = lambda hi: pl.BlockSpec((1, D_EXPERT, D_MODEL), wmap(hi))
    return pl.pallas_call(
        _experts_kernel,
        grid_spec=pltpu.PrefetchScalarGridSpec(
            num_scalar_prefetch=1,
            grid=(nt,),
            in_specs=[pl.BlockSpec((TM_E, ROW_W), lambda j, ti: (tile(j, ti), 0)),
                      up(0), up(0), down(0), up(1), up(1), down(1)],
            out_specs=pl.BlockSpec((TM_E, D_MODEL), lambda j, ti: (j, 0))),
        out_shape=jax.ShapeDtypeStruct((n_sorted, D_MODEL), F32),
        compiler_params=_cparams(("arbitrary",)),
        name="moe_experts",
    )(tinfo, xs, wg, wu, wd, wg, wu, wd)


def _combine_kernel(dcur_ref, dnxt_ref, x1_ref, nf_ref, ys_ref, y_ref, buf, sem, *, nb, tt):
    i = pl.program_id(0)
    slot = i % 2

    def gather(dref, s, start):
        for b in range(nb):
            def body(t2, carry):
                for par in range(2):
                    t = 2 * t2 + par
                    d = dref[0, 0, b * tt + t]
                    cp = pltpu.make_async_copy(ys_ref.at[pl.ds(d, 1), :], buf.at[s, b, pl.ds(t, 1), :],
                                               sem.at[s])
                    if start:
                        cp.start(priority=par)
                    else:
                        cp.wait()
                return carry
            lax.fori_loop(0, tt // 2, body, 0, unroll=4)

    @pl.when(i == 0)
    def _():
        gather(dcur_ref, 0, True)

    @pl.when(i + 1 < pl.num_programs(0))
    def _():
        gather(dnxt_ref, 1 - slot, True)

    whole = ys_ref.at[pl.ds(0, nb * tt), :]
    pltpu.make_async_copy(whole, whole, sem.at[slot]).wait()
    x2 = x1_ref[...].reshape(nb * tt, D_MODEL) + buf[slot].reshape(nb * tt, D_MODEL)
    y_ref[...] = _rms(x2, nf_ref[...]).reshape(nb, tt, D_MODEL)


def _combine(dest, x1, nf, ys, *, tt, skip):
    nb, tl, _ = x1.shape
    nt = tl // tt - skip
    dspec = lambda f: pl.BlockSpec((1, 1, nb * tt), f, memory_space=pltpu.SMEM)
    return pl.pallas_call(
        functools.partial(_combine_kernel, nb=nb, tt=tt),
        grid=(nt,),
        in_specs=[dspec(lambda i: (i, 0, 0)),
                  dspec(lambda i: (jnp.minimum(i + 1, nt - 1), 0, 0)),
                  pl.BlockSpec((nb, tt, D_MODEL), lambda i: (0, i + skip, 0)),
                  pl.BlockSpec((1, D_MODEL), lambda i: (0, 0)),
                  pl.BlockSpec(memory_space=pl.ANY)],
        out_specs=pl.BlockSpec((nb, tt, D_MODEL), lambda i: (0, i, 0)),
        out_shape=jax.ShapeDtypeStruct((nb, nt * tt, D_MODEL), F32),
        scratch_shapes=[pltpu.VMEM((2, nb, tt, D_MODEL), F32),
                        pltpu.SemaphoreType.DMA((2,))],
        compiler_params=_cparams(("arbitrary",)),
        name="moe_combine",
    )(dest, dest, x1, nf, ys)


def _block_diag(w):
    per = RNN_SUPER // RNN_BLOCK
    w4 = w.reshape(N_SUPER, per, RNN_BLOCK, RNN_BLOCK)
    eye = jnp.eye(per, dtype=w.dtype)
    bd = jnp.einsum('spij,pq->spiqj', w4, eye)
    return bd.reshape(N_SUPER, RNN_SUPER, RNN_SUPER).astype(BF16)


def _stream(x, meta, conv0, h0, c0, n0, m0, p, *, has_head):
    nb, seq, _ = x.shape
    n_null = N_NULL if has_head else 0
    tt = CHUNK if has_head else seq
    xn, gif = _prep(x, meta, p['norm1_w'], p['w_gif'], p['b_gif'], has_head=has_head)
    w_in = p['w_in']
    tt_mm = MM_TILES * CHUNK if has_head else seq
    xr = _mm(xn, w_in, col0=0, ncols=D_RNN, tn=512, tt=tt_mm, out_dtype=F32, w_t=True, name="in_xr")
    gr = _mm(xn, w_in, col0=D_RNN, ncols=D_RNN, tn=512, tt=tt_mm, out_dtype=BF16, w_t=True, gelu=True,
             name="in_gr")
    qkvo = _mm(xn, w_in, col0=2 * D_RNN, ncols=QKVO, tn=1024, tt=tt_mm, out_dtype=BF16, w_t=True,
               name="in_qkvo")
    mg = _mm(xn, p['w_mg'], col0=0, ncols=2 * D_MODEL, tn=1024, tt=tt_mm, out_dtype=BF16, w_t=True,
             name="in_mg")

    tail0 = jnp.transpose(conv0, (1, 0, 2)).reshape((CONV_W - 1) * nb, D_RNN)
    y_r, h_new, tail = _rglru(xr, gr, tail0, h0, p['conv_w'], p['conv_b'], p['rg_a_b'], p['rg_x_b'],
                              p['rg_lambda'], p['wa_bd'], p['wx_bd'], tt=tt, n_null=n_null)
    conv_new = jnp.transpose(tail.reshape(CONV_W - 1, nb, D_RNN), (1, 0, 2))

    y_m, c_new, n_new, m_new = _mlstm_fused(qkvo, gif, c0, n0, m0, p['mlstm_norm_w'], L=tt, n_null=n_null,
                                            nbk=MLSTM_NBK)

    mix = _merge(y_r, y_m, p['w_proj_rnn'], p['w_proj_mlstm'], mg, tn=512, tt=tt)
    routed = has_head
    x1, xn2, gates = _outproj(mix, p['w_out'], x, meta, p['norm2_w'], p['w_router'], p['b_router'],
                              tt=tt, has_head=has_head, routed=routed)
    if routed:
        skip = 1
        dest, tinfo = _plan(gates, tt=tt, skip=skip)
        xs = _dispatch(tinfo, dest, xn2, tt=tt, skip=skip)
        ys = _experts(tinfo, xs, p['w_exp_gate'], p['w_exp_up'], p['w_exp_down'])
        y = _combine(dest, x1, p['norm_f_w'], ys, tt=tt, skip=skip)
    else:
        y = _moe(xn2, gates, x1, p['w_exp_gate'], p['w_exp_up'], p['w_exp_down'], p['norm_f_w'],
                 tt=tt, has_head=has_head)
    return y, (conv_new[None], h_new[None], c_new[None], n_new[None], m_new[None])


def kernel(x_prompt, x_sample, state_rglru_conv, state_rglru_h, state_mlstm_C, state_mlstm_n, state_mlstm_m, meta_tokens, norm1_w, w_in, b_gates, conv_w, conv_b, rg_a_w, rg_a_b, rg_x_w, rg_x_b, rg_lambda, mlstm_norm_w, w_proj_rnn, w_proj_mlstm, w_out, norm2_w, w_router_group, b_router_group, w_router_expert, b_router_expert, w_exp_gate, w_exp_up, w_exp_down, norm_f_w):
    l = 0
    w_in_t = jnp.swapaxes(w_in[l], 0, 1)
    pad_r = LANES - N_GROUPS - N_EXPERTS
    p = dict(
        norm1_w=norm1_w[l][None], w_in=w_in_t,
        w_gif=jnp.pad(w_in_t[GIF_OFF:MG_OFF].T, ((0, 0), (0, LANES - 2 * N_HEADS_M))).astype(BF16),
        b_gif=jnp.pad(b_gates[l], (0, LANES - 2 * N_HEADS_M))[None],
        w_mg=w_in_t[MG_OFF:],
        conv_w=conv_w[l], conv_b=conv_b[l][None], rg_a_b=rg_a_b[l][None], rg_x_b=rg_x_b[l][None],
        rg_lambda=rg_lambda[l][None], wa_bd=_block_diag(rg_a_w[l]), wx_bd=_block_diag(rg_x_w[l]),
        mlstm_norm_w=mlstm_norm_w[l][None],
        w_proj_rnn=w_proj_rnn[l], w_proj_mlstm=w_proj_mlstm[l], w_out=w_out[l].astype(BF16),
        norm2_w=norm2_w[l][None],
        w_router=jnp.pad(jnp.concatenate([w_router_group[l], w_router_expert[l]], axis=1),
                         ((0, 0), (0, pad_r))).astype(BF16),
        b_router=jnp.pad(jnp.concatenate([b_router_group[l], b_router_expert[l]]), (0, pad_r))[None],
        w_exp_gate=w_exp_gate[l].astype(BF16), w_exp_up=w_exp_up[l].astype(BF16),
        w_exp_down=w_exp_down[l].astype(BF16), norm_f_w=norm_f_w[None],
    )
    nbp = x_prompt.shape[0]
    dt = x_prompt.dtype
    y_p, st_p = _stream(
        x_prompt, meta_tokens,
        jnp.zeros((nbp, CONV_W - 1, D_RNN), dt), jnp.zeros((nbp, D_RNN), F32),
        jnp.zeros((nbp, N_HEADS_M, DQK_M, DV_M), F32), jnp.zeros((nbp, N_HEADS_M, DQK_M), F32),
        jnp.zeros((nbp, N_HEADS_M), F32), p, has_head=True)
    y_s, st_s = _stream(
        x_sample, meta_tokens, state_rglru_conv[l], state_rglru_h[l], state_mlstm_C[l],
        state_mlstm_n[l], state_mlstm_m[l], p, has_head=False)
    return (y_p, y_s) + st_p + st_s
```

```python
import functools

import jax
import jax.numpy as jnp
from jax import lax
from jax.experimental import pallas as pl
from jax.experimental.pallas import tpu as pltpu

F32 = jnp.float32
BF16 = jnp.bfloat16

D_MODEL = 2048
CHUNK = 64
N_META = 16
N_NULL = CHUNK - N_META
D_RNN = 2560
N_RNN_BLOCKS = 16
RNN_BLOCK = D_RNN // N_RNN_BLOCKS
RNN_SUPER = 640
N_SUPER = D_RNN // RNN_SUPER
CONV_W = 4
LRU_C = 8.0
N_HEADS_M = 8
DV_M = D_MODEL // N_HEADS_M
DQK_M = DV_M // 2
HQK = N_HEADS_M * DQK_M
HV = N_HEADS_M * DV_M
QKVO = 2 * HQK + 2 * HV
N_GROUPS = 4
EXPERTS_PER_GROUP = 4
N_EXPERTS = 16
D_EXPERT = 512
PAIRS_PER_GROUP = 6
N_CLASSES = N_GROUPS * PAIRS_PER_GROUP
MM_TILES = 3
MLSTM_NBK = 2
TM_E = 256
ROW_W = D_MODEL + 128
EPS = 1e-6
LANES = 128
GIF_OFF = 2 * D_RNN + QKVO
MG_OFF = GIF_OFF + 2 * N_HEADS_M
VMEM_LIMIT = 56 * 1024 * 1024
NEG_BIG = -1e30


def _cparams(sem):
    return pltpu.CompilerParams(dimension_semantics=sem, vmem_limit_bytes=VMEM_LIMIT)


def _rms(x, w):
    return x * lax.rsqrt(jnp.mean(x * x, axis=-1, keepdims=True) + EPS) * w


def _softplus(x):
    return jnp.maximum(x, 0.0) + jnp.log1p(jnp.exp(-jnp.abs(x)))


def _head_rows(meta_ref, nb):
    head = jnp.concatenate([jnp.zeros((N_NULL, D_MODEL), F32), meta_ref[...]], axis=0)
    return jnp.broadcast_to(head[None], (nb, CHUNK, D_MODEL)).reshape(nb * CHUNK, D_MODEL)


def _x_map(has_head):
    if has_head:
        return lambda ti: (0, jnp.maximum(ti - 1, 0), 0)
    return lambda ti: (0, ti, 0)


def _prep_kernel(x_ref, meta_ref, nw_ref, wg_ref, bg_ref, xn_ref, gif_ref, *, nb, tt, has_head):
    def emit(x):
        xnb = _rms(x, nw_ref[...]).astype(BF16)
        xn_ref[...] = xnb.reshape(nb, tt, D_MODEL)
        gif = jnp.dot(xnb, wg_ref[...], preferred_element_type=F32) + bg_ref[...]
        gif_ref[...] = gif.reshape(nb, tt, LANES)

    if has_head:
        @pl.when(pl.program_id(0) == 0)
        def _():
            emit(_head_rows(meta_ref, nb))

        @pl.when(pl.program_id(0) > 0)
        def _():
            emit(x_ref[...].reshape(nb * tt, D_MODEL))
    else:
        emit(x_ref[...].reshape(nb * tt, D_MODEL))


def _prep(x, meta, norm_w, w_gif, b_gif, *, has_head):
    nb, seq, _ = x.shape
    tt = CHUNK if has_head else seq
    n_tiles = seq // tt + (1 if has_head else 0)
    tl = n_tiles * tt
    const = lambda ti: (0, 0)
    return pl.pallas_call(
        functools.partial(_prep_kernel, nb=nb, tt=tt, has_head=has_head),
        grid=(n_tiles,),
        in_specs=[pl.BlockSpec((nb, tt, D_MODEL), _x_map(has_head)),
                  pl.BlockSpec((N_META, D_MODEL), const),
                  pl.BlockSpec((1, D_MODEL), const),
                  pl.BlockSpec((D_MODEL, LANES), const),
                  pl.BlockSpec((1, LANES), const)],
        out_specs=[pl.BlockSpec((nb, tt, D_MODEL), lambda ti: (0, ti, 0)),
                   pl.BlockSpec((nb, tt, LANES), lambda ti: (0, ti, 0))],
        out_shape=[jax.ShapeDtypeStruct((nb, tl, D_MODEL), BF16),
                   jax.ShapeDtypeStruct((nb, tl, LANES), F32)],
        compiler_params=_cparams(("arbitrary",)),
        name="prep",
    )(x, meta, norm_w, w_gif, b_gif)


def _gelu_tanh(g):
    return g * (0.5 * (1.0 + jnp.tanh(0.7978845608028654 * (g + 0.044715 * (g * g * g)))))


def _mm_kernel(*refs, has_gate, has_add, w_t, gelu):
    lhs_ref, w_ref = refs[0], refs[1]
    pos = 2
    gate_ref = add_ref = None
    if has_gate:
        gate_ref = refs[pos]
        pos += 1
    if has_add:
        add_ref = refs[pos]
        pos += 1
    out_ref, wbf_ref = refs[pos], refs[pos + 1]
    nb, tt, k = lhs_ref.shape
    tn = out_ref.shape[-1]

    @pl.when(pl.program_id(1) == 0)
    def _():
        w = w_ref[...]
        wbf_ref[...] = (w.T if w_t else w).astype(BF16)

    acc = jnp.dot(lhs_ref[...].reshape(nb * tt, k), wbf_ref[...], preferred_element_type=F32)
    if gelu:
        acc = _gelu_tanh(acc)
    if has_gate:
        acc = jax.nn.sigmoid(gate_ref[...].reshape(nb * tt, tn).astype(F32)) * acc
    if has_add:
        acc = add_ref[...].reshape(nb * tt, tn).astype(F32) + acc
    out_ref[...] = acc.astype(out_ref.dtype).reshape(nb, tt, tn)


def _mm(lhs, w, *, col0, ncols, tn, tt, out_dtype, gate=None, gate_col0=0, add=None, w_t=False, gelu=False,
        name):
    nb, tl, k = lhs.shape
    assert tl % tt == 0 and ncols % tn == 0 and col0 % tn == 0 and gate_col0 % tn == 0
    if w_t:
        w_spec = pl.BlockSpec((tn, k), lambda j, i: (col0 // tn + j, 0))
    else:
        w_spec = pl.BlockSpec((k, tn), lambda j, i: (0, col0 // tn + j))
    in_specs = [pl.BlockSpec((nb, tt, k), lambda j, i: (0, i, 0)), w_spec]
    args = [lhs, w]
    if gate is not None:
        in_specs.append(pl.BlockSpec((nb, tt, tn), lambda j, i: (0, i, gate_col0 // tn + j)))
        args.append(gate)
    if add is not None:
        in_specs.append(pl.BlockSpec((nb, tt, tn), lambda j, i: (0, i, j)))
        args.append(add)
    return pl.pallas_call(
        functools.partial(_mm_kernel, has_gate=gate is not None, has_add=add is not None, w_t=w_t, gelu=gelu),
        grid=(ncols // tn, tl // tt),
        in_specs=in_specs,
        out_specs=pl.BlockSpec((nb, tt, tn), lambda j, i: (0, i, j)),
        out_shape=jax.ShapeDtypeStruct((nb, tl, ncols), out_dtype),
        scratch_shapes=[pltpu.VMEM((k, tn), BF16)],
        compiler_params=_cparams(("arbitrary", "arbitrary")),
        name=name,
    )(*args)


def _merge_kernel(yr_ref, ym_ref, wr_ref, wm_ref, gr_ref, gm_ref, out_ref, wr_bf, wm_bf):
    nb, tt, tn = out_ref.shape
    rows = nb * tt

    @pl.when(pl.program_id(1) == 0)
    def _():
        wr_bf[...] = wr_ref[...].astype(BF16)
        wm_bf[...] = wm_ref[...].astype(BF16)

    pr = jnp.dot(yr_ref[...].reshape(rows, D_RNN), wr_bf[...], preferred_element_type=F32)
    pm = jnp.dot(ym_ref[...].reshape(rows, HV), wm_bf[...], preferred_element_type=F32)
    mix = (jax.nn.sigmoid(gr_ref[...].reshape(rows, tn).astype(F32)) * pr
           + jax.nn.sigmoid(gm_ref[...].reshape(rows, tn).astype(F32)) * pm)
    out_ref[...] = mix.astype(BF16).reshape(nb, tt, tn)


def _merge(y_r, y_m, w_r, w_m, mg, *, tn, tt):
    nb, tl, _ = y_r.shape
    nj = D_MODEL // tn
    return pl.pallas_call(
        _merge_kernel,
        grid=(nj, tl // tt),
        in_specs=[pl.BlockSpec((nb, tt, D_RNN), lambda j, i: (0, i, 0)),
                  pl.BlockSpec((nb, tt, HV), lambda j, i: (0, i, 0)),
                  pl.BlockSpec((D_RNN, tn), lambda j, i: (0, j)),
                  pl.BlockSpec((HV, tn), lambda j, i: (0, j)),
                  pl.BlockSpec((nb, tt, tn), lambda j, i: (0, i, j)),
                  pl.BlockSpec((nb, tt, tn), lambda j, i: (0, i, nj + j))],
        out_specs=pl.BlockSpec((nb, tt, tn), lambda j, i: (0, i, j)),
        out_shape=jax.ShapeDtypeStruct((nb, tl, D_MODEL), BF16),
        scratch_shapes=[pltpu.VMEM((D_RNN, tn), BF16), pltpu.VMEM((HV, tn), BF16)],
        compiler_params=_cparams(("arbitrary", "arbitrary")),
        name="merge",
    )(y_r, y_m, w_r, w_m, mg, mg)


def _rglru_kernel(xr_ref, gr_ref, tail0_ref, h0_ref, cw_ref, cb_ref, ab_ref, xb_ref, lam_ref,
                  wa_ref, wx_ref, y_ref, hl_ref, tail_ref, xp_s, h_s, a_s, u_s, hs_s, *, nb, tt, n_null):
    ti = pl.program_id(1)
    rt = tt * nb
    tb = (CONV_W - 1) * nb

    def time_major(x):
        return jnp.swapaxes(x, 0, 1).reshape(rt, x.shape[-1])

    @pl.when(ti == 0)
    def _():
        xp_s[0:tb, :] = tail0_ref[...]
        h_s[...] = h0_ref[...]

    xp_s[tb:tb + rt, :] = time_major(xr_ref[...])
    cw = cw_ref[...]
    xc = cb_ref[...] + cw[0:1, :] * xp_s[0:rt, :]
    for j in range(1, CONV_W):
        xc = xc + cw[j:j + 1, :] * xp_s[j * nb:j * nb + rt, :]
    xcb = xc.astype(BF16)
    r = jax.nn.sigmoid(jnp.dot(xcb, wa_ref[0], preferred_element_type=F32) + ab_ref[...])
    i = jax.nn.sigmoid(jnp.dot(xcb, wx_ref[0], preferred_element_type=F32) + xb_ref[...])
    log_a = r * ((-LRU_C) * _softplus(-lam_ref[...]))
    a = jnp.exp(log_a)
    a_s[...] = a
    u = jnp.sqrt(-jnp.tanh(log_a) * (a * a + 1.0)) * (i * xc)
    if n_null:
        row = lax.broadcasted_iota(jnp.int32, (rt, 1), 0)
        u = jnp.where((row >= n_null * nb) | (ti > 0), u, 0.0)
    u_s[...] = u

    def step(t, h):
        o = pl.multiple_of(t * nb, nb)
        h = a_s[pl.ds(o, nb), :] * h + u_s[pl.ds(o, nb), :]
        hs_s[pl.ds(o, nb), :] = h
        return h

    h = lax.fori_loop(0, tt, step, h_s[...], unroll=8)
    h_s[...] = h
    h_bm = jnp.swapaxes(hs_s[...].reshape(tt, nb, hs_s.shape[-1]), 0, 1)
    y_ref[...] = (gr_ref[...].astype(F32) * h_bm).astype(BF16)
    xp_s[0:tb, :] = xp_s[rt:rt + tb, :]

    @pl.when(ti == pl.num_programs(1) - 1)
    def _():
        hl_ref[...] = h
        tail_ref[...] = xp_s[rt:rt + tb, :]


def _rglru(xr, gr, tail0, h0, cw, cb, ab, xb, lam, wa, wx, *, tt, n_null):
    nb, tl, _ = xr.shape
    rt = tt * nb
    tb = (CONV_W - 1) * nb
    blk = lambda s, ti: (0, ti, s)
    col = lambda s, ti: (0, s)
    return pl.pallas_call(
        functools.partial(_rglru_kernel, nb=nb, tt=tt, n_null=n_null),
        grid=(N_SUPER, tl // tt),
        in_specs=[pl.BlockSpec((nb, tt, RNN_SUPER), blk),
                  pl.BlockSpec((nb, tt, RNN_SUPER), blk),
                  pl.BlockSpec((tb, RNN_SUPER), col),
                  pl.BlockSpec((nb, RNN_SUPER), col),
                  pl.BlockSpec((CONV_W, RNN_SUPER), col),
                  pl.BlockSpec((1, RNN_SUPER), col),
                  pl.BlockSpec((1, RNN_SUPER), col),
                  pl.BlockSpec((1, RNN_SUPER), col),
                  pl.BlockSpec((1, RNN_SUPER), col),
                  pl.BlockSpec((1, RNN_SUPER, RNN_SUPER), lambda s, ti: (s, 0, 0)),
                  pl.BlockSpec((1, RNN_SUPER, RNN_SUPER), lambda s, ti: (s, 0, 0))],
        out_specs=[pl.BlockSpec((nb, tt, RNN_SUPER), blk),
                   pl.BlockSpec((nb, RNN_SUPER), col),
                   pl.BlockSpec((tb, RNN_SUPER), col)],
        out_shape=[jax.ShapeDtypeStruct((nb, tl, D_RNN), BF16),
                   jax.ShapeDtypeStruct((nb, D_RNN), F32),
                   jax.ShapeDtypeStruct((tb, D_RNN), F32)],
        scratch_shapes=[pltpu.VMEM((rt + tb, RNN_SUPER), F32),
                        pltpu.VMEM((nb, RNN_SUPER), F32),
                        pltpu.VMEM((rt, RNN_SUPER), F32),
                        pltpu.VMEM((rt, RNN_SUPER), F32),
                        pltpu.VMEM((rt, RNN_SUPER), F32)],
        compiler_params=_cparams(("arbitrary", "arbitrary")),
        name="rglru",
    )(xr, gr, tail0, h0, cw, cb, ab, xb, lam, wa, wx)


DVA = DV_M + 128


def _mlstm_gates(c, gif_ref, m_s, *, L, n_null):
    hm = N_HEADS_M
    g = gif_ref[...]
    if L < LANES:
        g = jnp.concatenate([g, jnp.zeros((LANES - L, LANES), F32)], axis=0)
    g_t = g.T
    lane = lax.broadcasted_iota(jnp.int32, (hm, LANES), 1)
    valid = lane < L
    li = g_t[0:hm, :]
    lfr = g_t[hm:2 * hm, :]
    lf = jnp.where(valid, jnp.minimum(lfr, 0.0) - jnp.log1p(jnp.exp(-jnp.abs(lfr))), 0.0)
    if n_null:
        null = (lane < n_null) & (c == 0)
        lf = jnp.where(null, 0.0, lf)
        li = jnp.where(null, NEG_BIG, li)
    r128 = lax.broadcasted_iota(jnp.int32, (LANES, LANES), 0)
    c128 = lax.broadcasted_iota(jnp.int32, (LANES, LANES), 1)
    upper = jnp.where(r128 <= c128, 1.0, 0.0)
    b = jnp.dot(lf, upper, preferred_element_type=F32, precision=lax.Precision.HIGHEST)
    a = li - b
    cm = jnp.where(valid, a, -jnp.inf)
    sh = 1
    while sh < L:
        cm = jnp.maximum(cm, jnp.where(lane >= sh, pltpu.roll(cm, sh, axis=1), -jnp.inf))
        sh *= 2
    m0 = m_s[...]
    big_m = jnp.maximum(cm, m0)
    m_last = jnp.max(jnp.where(valid, big_m, -jnp.inf), axis=1, keepdims=True)
    b_last = jnp.sum(lf, axis=1, keepdims=True)
    scale = DQK_M ** -0.5
    e_mv = jnp.exp(-(b + big_m))
    w_int = jnp.exp(m0 - big_m)
    w_end = jnp.exp(a - m_last) * scale
    decay = jnp.exp(m0 - m_last)
    m_s[...] = jnp.broadcast_to(b_last + m_last, (hm, LANES))
    cols = jnp.concatenate([big_m, e_mv, w_int, w_end, jnp.zeros((LANES - 4 * hm, LANES), F32)], axis=0).T
    return a, cols, decay


def _mlstm_heads(gates, q_ref, k_ref, v_ref, og_ref, nw_ref, y_ref, c_s, *, L):
    hm = N_HEADS_M
    scale = DQK_M ** -0.5
    rr = lax.broadcasted_iota(jnp.int32, (L, L), 0)
    cc = lax.broadcasted_iota(jnp.int32, (L, L), 1)
    causal = rr >= cc
    ones_col = jnp.where(lax.broadcasted_iota(jnp.int32, (L, 128), 1) == 0, 1.0, 0.0).astype(BF16)
    nw = nw_ref[...]
    dn_nt = (((1,), (1,)), ((), ()))
    dn_tn = (((0,), (0,)), ((), ()))
    bh = [(bi, h) for bi in range(len(gates)) for h in range(hm)]
    col = lambda bi, k, h: gates[bi][1][0:L, k * hm + h:k * hm + h + 1]
    qs = [q_ref[bi, :, h * DQK_M:(h + 1) * DQK_M] for bi, h in bh]
    ks = [k_ref[bi, :, h * DQK_M:(h + 1) * DQK_M] for bi, h in bh]
    v_augs = [jnp.concatenate([v_ref[bi, :, h * DV_M:(h + 1) * DV_M], ones_col], axis=1) for bi, h in bh]
    c_augs = [c_s[bi, h] for bi, h in bh]
    s_raw = [lax.dot_general(q, k, dn_nt, preferred_element_type=F32) for q, k in zip(qs, ks)]
    inter = [jnp.dot((qs[i].astype(F32) * col(bi, 2, h)).astype(BF16), c_augs[i].astype(BF16),
                     preferred_element_type=F32) for i, (bi, h) in enumerate(bh)]
    upd = [lax.dot_general((ks[i].astype(F32) * col(bi, 3, h)).astype(BF16), v_augs[i], dn_tn,
                           preferred_element_type=F32) for i, (bi, h) in enumerate(bh)]
    s_w = [(s_raw[i] * (scale * jnp.where(causal, jnp.exp(gates[bi][0][h:h + 1, 0:L] - col(bi, 0, h)), 0.0))
            ).astype(BF16) for i, (bi, h) in enumerate(bh)]
    res = [jnp.dot(s_w[i], v_augs[i], preferred_element_type=F32) + inter[i] for i in range(len(bh))]
    for i, (bi, h) in enumerate(bh):
        c_s[bi, h] = gates[bi][2][h:h + 1, 0:1] * c_augs[i] + upd[i]
    for i, (bi, h) in enumerate(bh):
        num = res[i][:, :DV_M]
        denom = jnp.maximum(jnp.abs(res[i][:, DV_M:DV_M + 1]), col(bi, 1, h))
        hh = num / denom
        hh = hh * lax.rsqrt(jnp.mean(hh * hh, axis=-1, keepdims=True) + EPS)
        hh = hh * nw[:, h * DV_M:(h + 1) * DV_M]
        og = og_ref[bi, :, h * DV_M:(h + 1) * DV_M].astype(F32)
        y_ref[bi, :, h * DV_M:(h + 1) * DV_M] = (hh * jax.nn.sigmoid(og)).astype(BF16)


def _mlstm_fused_kernel(q_ref, k_ref, v_ref, og_ref, gif_ref, c0_ref, n0_ref, m0_ref, nw_ref,
                        y_ref, co_ref, no_ref, mo_ref, c_s, m_s, *, L, n_null, nbk):
    c = pl.program_id(1)
    bh = [(bi, h) for bi in range(nbk) for h in range(N_HEADS_M)]

    @pl.when(c == 0)
    def _():
        m_s[...] = m0_ref[...]
        for bi, h in bh:
            c_s[bi, h, :, :DV_M] = c0_ref[bi, h]
            n_tile = jnp.concatenate([n0_ref[bi, h:h + 1, :], jnp.zeros((DQK_M - 1, DQK_M), F32)], axis=0)
            c_s[bi, h, :, DV_M:] = n_tile.T

    gates = [_mlstm_gates(c, gif_ref.at[bi], m_s.at[bi], L=L, n_null=n_null) for bi in range(nbk)]
    _mlstm_heads(gates, q_ref, k_ref, v_ref, og_ref, nw_ref, y_ref, c_s, L=L)

    @pl.when(c == pl.num_programs(1) - 1)
    def _():
        mo_ref[...] = m_s[...]
        for bi, h in bh:
            co_ref[bi, h] = c_s[bi, h, :, :DV_M]
            no_ref[bi, h:h + 1, :] = c_s[bi, h, :, DV_M:].T[0:1, :]


def _mlstm_fused(qkvo, gif, c0, n0, m0, nw, *, L, n_null, nbk):
    nb, tl, _ = qkvo.shape
    assert nb % nbk == 0
    hm = N_HEADS_M
    assert DQK_M == LANES and DVA - DV_M == DQK_M
    m_rep0 = jnp.broadcast_to(m0[..., None], (nb, hm, LANES))
    st4 = lambda b, c: (b, 0, 0, 0)
    st3 = lambda b, c: (b, 0, 0)
    y, c_new, n_new, m_rep = pl.pallas_call(
        functools.partial(_mlstm_fused_kernel, L=L, n_null=n_null, nbk=nbk),
        grid=(nb // nbk, tl // L),
        in_specs=[pl.BlockSpec((nbk, L, HQK), lambda b, c: (b, c, 0)),
                  pl.BlockSpec((nbk, L, HQK), lambda b, c: (b, c, 1)),
                  pl.BlockSpec((nbk, L, HV), lambda b, c: (b, c, 1)),
                  pl.BlockSpec((nbk, L, HV), lambda b, c: (b, c, 2)),
                  pl.BlockSpec((nbk, L, LANES), lambda b, c: (b, c, 0)),
                  pl.BlockSpec((nbk, hm, DQK_M, DV_M), st4),
                  pl.BlockSpec((nbk, hm, DQK_M), st3),
                  pl.BlockSpec((nbk, hm, LANES), st3),
                  pl.BlockSpec((1, HV), lambda b, c: (0, 0))],
        out_specs=[pl.BlockSpec((nbk, L, HV), lambda b, c: (b, c, 0)),
                   pl.BlockSpec((nbk, hm, DQK_M, DV_M), st4),
                   pl.BlockSpec((nbk, hm, DQK_M), st3),
                   pl.BlockSpec((nbk, hm, LANES), st3)],
        out_shape=[jax.ShapeDtypeStruct((nb, tl, HV), BF16),
                   jax.ShapeDtypeStruct((nb, hm, DQK_M, DV_M), F32),
                   jax.ShapeDtypeStruct((nb, hm, DQK_M), F32),
                   jax.ShapeDtypeStruct((nb, hm, LANES), F32)],
        scratch_shapes=[pltpu.VMEM((nbk, hm, DQK_M, DVA), F32),
                        pltpu.VMEM((nbk, hm, LANES), F32)],
        compiler_params=_cparams(("arbitrary", "arbitrary")),
        name="mlstm",
    )(qkvo, qkvo, qkvo, qkvo, gif, c0, n0, m_rep0, nw)
    return y, c_new, n_new, m_rep[..., 0]


def _route(logits, dense):
    tm = logits.shape[0]
    lane = lax.broadcasted_iota(jnp.int32, (tm, LANES), 1)
    lanef = lane.astype(F32)
    gl = jnp.where(lane < N_GROUPS, logits, -jnp.inf)
    gmax = jnp.max(gl, axis=1, keepdims=True)
    gidx = jnp.min(jnp.where(gl == gmax, lanef, float(LANES)), axis=1, keepdims=True)
    gw = 1.0 / jnp.sum(jnp.exp(gl - gmax), axis=1, keepdims=True)
    lo = float(N_GROUPS) + float(EXPERTS_PER_GROUP) * gidx
    sel = (lanef >= lo) & (lanef < lo + float(EXPERTS_PER_GROUP))
    ev = jnp.where(sel, logits, -jnp.inf)
    v1 = jnp.max(ev, axis=1, keepdims=True)
    i1 = jnp.min(jnp.where(ev == v1, lanef, float(LANES)), axis=1, keepdims=True)
    ev2 = jnp.where(lanef == i1, -jnp.inf, ev)
    v2 = jnp.max(ev2, axis=1, keepdims=True)
    i2 = jnp.min(jnp.where(ev2 == v2, lanef, float(LANES)), axis=1, keepdims=True)
    t = jnp.exp(v2 - v1)
    w1 = gw / (1.0 + t)
    w2 = gw * (t / (1.0 + t))
    if dense:
        return jnp.where(lanef == i1, w1, jnp.where(lanef == i2, w2, 0.0))
    l1 = i1 - lo
    l2 = i2 - lo
    first = l1 < l2
    a = jnp.minimum(l1, l2)
    b = jnp.maximum(l1, l2)
    pair = jnp.where(a == 0.0, b - 1.0, jnp.where(a == 1.0, b + 1.0, 5.0))
    cls = float(PAIRS_PER_GROUP) * gidx + pair
    return jnp.where(lane == 0, cls,
                     jnp.where(lane == 1, jnp.where(first, w1, w2),
                               jnp.where(lane == 2, jnp.where(first, w2, w1), 0.0)))


def _outproj_kernel(mix_ref, w_ref, x_ref, meta_ref, n2_ref, wr_ref, br_ref, x1_ref, xn2_ref, gates_ref,
                    *, nb, tt, has_head, routed):
    def emit(x):
        mix = mix_ref[...].reshape(nb * tt, D_MODEL)
        x1 = x + jnp.dot(mix, w_ref[...], preferred_element_type=F32)
        x1_ref[...] = x1.reshape(nb, tt, D_MODEL)
        xn2 = _rms(x1, n2_ref[...]).astype(BF16)
        logits = jnp.dot(xn2, wr_ref[...], preferred_element_type=F32) + br_ref[...]
        info = _route(logits, dense=not routed)
        gates_ref[...] = info.reshape(nb, tt, LANES)
        if routed:
            xn2_ref[:, :, :D_MODEL] = xn2.astype(F32).reshape(nb, tt, D_MODEL)
            xn2_ref[:, :, D_MODEL:] = info.reshape(nb, tt, LANES)
        else:
            xn2_ref[...] = xn2.reshape(nb, tt, D_MODEL)

    if has_head:
        @pl.when(pl.program_id(0) == 0)
        def _():
            emit(_head_rows(meta_ref, nb))

        @pl.when(pl.program_id(0) > 0)
        def _():
            emit(x_ref[...].reshape(nb * tt, D_MODEL))
    else:
        emit(x_ref[...].reshape(nb * tt, D_MODEL))


def _outproj(mix, w_out, x, meta, n2, w_router, b_router, *, tt, has_head, routed):
    nb, tl, _ = mix.shape
    const = lambda i: (0, 0)
    blk = lambda i: (0, i, 0)
    xn2_w, xn2_dt = (ROW_W, F32) if routed else (D_MODEL, BF16)
    return pl.pallas_call(
        functools.partial(_outproj_kernel, nb=nb, tt=tt, has_head=has_head, routed=routed),
        grid=(tl // tt,),
        in_specs=[pl.BlockSpec((nb, tt, D_MODEL), blk),
                  pl.BlockSpec((D_MODEL, D_MODEL), const, pipeline_mode=pl.Buffered(1)),
                  pl.BlockSpec((nb, tt, D_MODEL), _x_map(has_head)),
                  pl.BlockSpec((N_META, D_MODEL), const),
                  pl.BlockSpec((1, D_MODEL), const),
                  pl.BlockSpec((D_MODEL, LANES), const),
                  pl.BlockSpec((1, LANES), const)],
        out_specs=[pl.BlockSpec((nb, tt, D_MODEL), blk),
                   pl.BlockSpec((nb, tt, xn2_w), blk),
                   pl.BlockSpec((nb, tt, LANES), blk)],
        out_shape=[jax.ShapeDtypeStruct((nb, tl, D_MODEL), F32),
                   jax.ShapeDtypeStruct((nb, tl, xn2_w), xn2_dt),
                   jax.ShapeDtypeStruct((nb, tl, LANES), F32)],
        compiler_params=_cparams(("arbitrary",)),
        name="outproj",
    )(mix, w_out, x, meta, n2, w_router, b_router)


def _moe_kernel(xn_ref, gates_ref, x1_ref, wg_ref, wu_ref, wd_ref, nf_ref, y_ref, acc_s, *, nb, tt):
    e = pl.program_id(1)
    rows = nb * tt

    @pl.when(e == 0)
    def _():
        acc_s[...] = jnp.zeros_like(acc_s)

    x = xn_ref[...].reshape(rows, D_MODEL)
    hg = jnp.dot(x, wg_ref[0], preferred_element_type=F32)
    hu = jnp.dot(x, wu_ref[0], preferred_element_type=F32)
    hdn = (hg * jax.nn.sigmoid(hg)) * hu
    yd = jnp.dot(hdn.astype(BF16), wd_ref[0], preferred_element_type=F32)
    gates = gates_ref[...].reshape(rows, LANES)
    lane = lax.broadcasted_iota(jnp.int32, gates.shape, 1)
    gcol = jnp.sum(jnp.where(lane == e + N_GROUPS, gates, 0.0), axis=1, keepdims=True)
    acc_s[...] += gcol * yd

    @pl.when(e == pl.num_programs(1) - 1)
    def _():
        x2 = x1_ref[...].reshape(rows, D_MODEL) + acc_s[...]
        y_ref[...] = _rms(x2, nf_ref[...]).reshape(nb, tt, D_MODEL)


def _moe(xn2, gates, x1, wg, wu, wd, nf, *, tt, has_head):
    nb, tl, _ = xn2.shape
    blk = lambda i, e: (0, i, 0)
    wblk = lambda i, e: (e, 0, 0)
    if has_head:
        out_map = lambda i, e: (0, jnp.maximum(i - 1, 0), 0)
        seq = tl - tt
    else:
        out_map = blk
        seq = tl
    return pl.pallas_call(
        functools.partial(_moe_kernel, nb=nb, tt=tt),
        grid=(tl // tt, N_EXPERTS),
        in_specs=[pl.BlockSpec((nb, tt, D_MODEL), blk),
                  pl.BlockSpec((nb, tt, LANES), blk),
                  pl.BlockSpec((nb, tt, D_MODEL), blk),
                  pl.BlockSpec((1, D_MODEL, D_EXPERT), wblk),
                  pl.BlockSpec((1, D_MODEL, D_EXPERT), wblk),
                  pl.BlockSpec((1, D_EXPERT, D_MODEL), wblk),
                  pl.BlockSpec((1, D_MODEL), lambda i, e: (0, 0))],
        out_specs=pl.BlockSpec((nb, tt, D_MODEL), out_map),
        out_shape=jax.ShapeDtypeStruct((nb, seq, D_MODEL), F32),
        scratch_shapes=[pltpu.VMEM((nb * tt, D_MODEL), F32)],
        compiler_params=_cparams(("arbitrary", "arbitrary")),
        name="moe",
    )(xn2, gates, x1, wg, wu, wd, nf)


_NT = (((1,), (1,)), ((), ()))


def _plan_kernel(ri_ref, dest_ref, tinfo_ref, cnt_s, carry_s, offs_s, *, rows):
    ph = pl.program_id(0)
    i = pl.program_id(1)
    ri = ri_ref[...].reshape(rows, LANES)
    lanef = lax.broadcasted_iota(jnp.int32, (rows, LANES), 1).astype(F32)
    oh = jnp.where(lanef == ri[:, 0:1], 1.0, 0.0)
    colsum = jnp.sum(oh, axis=0, keepdims=True)
    r128 = lax.broadcasted_iota(jnp.int32, (LANES, LANES), 0)
    c128 = lax.broadcasted_iota(jnp.int32, (LANES, LANES), 1)
    eye = jnp.where(r128 == c128, 1.0, 0.0).astype(BF16)

    @pl.when((ph == 0) & (i == 0))
    def _():
        cnt_s[...] = jnp.zeros_like(cnt_s)

    @pl.when(ph == 0)
    def _():
        cnt_s[...] += colsum

    @pl.when((ph == 1) & (i == 0))
    def _():
        tiles = jnp.floor((cnt_s[...] + float(TM_E - 1)) * (1.0 / TM_E))
        upper = jnp.where(r128 <= c128, 1.0, 0.0).astype(BF16)
        cum_incl = jnp.dot(jnp.broadcast_to(tiles, (8, LANES)).astype(BF16), upper, preferred_element_type=F32)
        offs_s[...] = (cum_incl[0:1, :] - tiles) * float(TM_E)
        carry_s[...] = jnp.zeros_like(carry_s)
        cum_col = lax.dot_general(eye, cum_incl.astype(BF16), _NT, preferred_element_type=F32)
        ended = (cum_col[:, 0:1] <= c128.astype(F32)) & (r128 < N_CLASSES)
        tcls = jnp.sum(jnp.where(ended, 1.0, 0.0), axis=0, keepdims=True)
        total = cum_incl[0:1, LANES - 1:LANES]
        row8 = lax.broadcasted_iota(jnp.int32, (8, LANES), 0)
        tinfo_ref[...] = jnp.where(row8 == 0, tcls, total).astype(jnp.int32)

    @pl.when(ph == 1)
    def _():
        tri = jnp.where(lax.broadcasted_iota(jnp.int32, (rows, rows), 0)
                        > lax.broadcasted_iota(jnp.int32, (rows, rows), 1), 1.0, 0.0).astype(BF16)
        rank = jnp.dot(tri, oh.astype(BF16), preferred_element_type=F32)
        dest = jnp.sum(oh * (rank + (offs_s[...] + carry_s[...])), axis=1, keepdims=True)
        carry_s[...] += colsum
        dhi = jnp.floor(dest * (1.0 / 256.0))
        dlo = dest - 256.0 * dhi
        digits = jnp.where(lanef == 0.0, dlo, jnp.where(lanef == 1.0, dhi, 0.0)).astype(BF16)
        dt = lax.dot_general(eye, digits, _NT, preferred_element_type=F32)
        dest_ref[...] = (dt[0:1, :] + 256.0 * dt[1:2, :]).astype(jnp.int32).reshape(1, 1, rows)


def _plan(rinfo, *, tt, skip):
    nb, tl, _ = rinfo.shape
    rows = nb * tt
    nt = tl // tt - skip
    assert (nt * rows) // TM_E + N_CLASSES <= LANES and rows % LANES == 0
    return pl.pallas_call(
        functools.partial(_plan_kernel, rows=rows),
        grid=(2, nt),
        in_specs=[pl.BlockSpec((nb, tt, LANES), lambda ph, i: (0, i + skip, 0))],
        out_specs=[pl.BlockSpec((1, 1, rows), lambda ph, i: (i * ph, 0, 0)),
                   pl.BlockSpec((8, LANES), lambda ph, i: (0, 0))],
        out_shape=[jax.ShapeDtypeStruct((nt, 1, rows), jnp.int32),
                   jax.ShapeDtypeStruct((8, LANES), jnp.int32)],
        scratch_shapes=[pltpu.VMEM((1, LANES), F32)] * 3,
        compiler_params=_cparams(("arbitrary", "arbitrary")),
        name="moe_plan",
    )(rinfo)


def _dispatch_kernel(tinfo_ref, dest_ref, x_ref, xs_ref, zbuf, zsem, sem, *, nb, tt):
    i = pl.program_id(0)

    @pl.when(i == 0)
    def _():
        zbuf[...] = jnp.zeros_like(zbuf)
        n_used = tinfo_ref[1, 0]

        def fill(start):
            def body(j, carry):
                last = (j >= n_used - 1) | (tinfo_ref[0, jnp.minimum(j + 1, LANES - 1)] != tinfo_ref[0, j])

                @pl.when(last)
                def _():
                    cp = pltpu.make_async_copy(
                        zbuf, xs_ref.at[pl.ds(pl.multiple_of(j * TM_E, TM_E), TM_E), :], zsem)
                    if start:
                        cp.start()
                    else:
                        cp.wait()
                return carry
            lax.fori_loop(0, xs_ref.shape[0] // TM_E, body, 0)

        fill(True)
        fill(False)

    for b in range(nb):
        for t in range(tt):
            d = dest_ref[0, 0, b * tt + t]
            pltpu.make_async_copy(x_ref.at[b, pl.ds(t, 1), :], xs_ref.at[pl.ds(d, 1), :], sem).start()
    whole = xs_ref.at[pl.ds(0, nb * tt), :]
    pltpu.make_async_copy(whole, whole, sem).wait()


def _dispatch(tinfo, dest, xrow, *, tt, skip):
    nb, tl, _ = xrow.shape
    nt = tl // tt - skip
    n_sorted = ((nt * nb * tt) // TM_E + N_CLASSES) * TM_E
    return pl.pallas_call(
        functools.partial(_dispatch_kernel, nb=nb, tt=tt),
        grid_spec=pltpu.PrefetchScalarGridSpec(
            num_scalar_prefetch=1,
            grid=(nt,),
            in_specs=[pl.BlockSpec((1, 1, nb * tt), lambda i, ti: (i, 0, 0), memory_space=pltpu.SMEM),
                      pl.BlockSpec((nb, tt, ROW_W), lambda i, ti: (0, i + skip, 0))],
            out_specs=pl.BlockSpec(memory_space=pl.ANY),
            scratch_shapes=[pltpu.VMEM((TM_E, ROW_W), F32),
                            pltpu.SemaphoreType.DMA(()),
                            pltpu.SemaphoreType.DMA(())]),
        out_shape=jax.ShapeDtypeStruct((n_sorted, ROW_W), F32),
        compiler_params=_cparams(("arbitrary",)),
        name="moe_dispatch",
    )(tinfo, dest, xrow)


def _class_expert(c, hi):
    g = c // PAIRS_PER_GROUP
    p = c - PAIRS_PER_GROUP * g
    ge3 = (p >= 3).astype(jnp.int32)
    ge5 = (p >= 5).astype(jnp.int32)
    local = (p + 1 - 2 * ge3 - ge5) if hi else (ge3 + ge5)
    return EXPERTS_PER_GROUP * g + local


def _experts_kernel(tinfo_ref, xs_ref, wga_ref, wua_ref, wda_ref, wgb_ref, wub_ref, wdb_ref, ys_ref):
    @pl.when(pl.program_id(0) < tinfo_ref[1, 0])
    def _():
        x = xs_ref[:, :D_MODEL].astype(BF16)

        def ffn(wg_ref, wu_ref, wd_ref):
            hg = jnp.dot(x, wg_ref[0], preferred_element_type=F32)
            hu = jnp.dot(x, wu_ref[0], preferred_element_type=F32)
            hdn = (hg * jax.nn.sigmoid(hg)) * hu
            return jnp.dot(hdn.astype(BF16), wd_ref[0], preferred_element_type=F32)

        ys_ref[...] = (xs_ref[:, D_MODEL + 1:D_MODEL + 2] * ffn(wga_ref, wua_ref, wda_ref)
                       + xs_ref[:, D_MODEL + 2:D_MODEL + 3] * ffn(wgb_ref, wub_ref, wdb_ref))

    @pl.when(pl.program_id(0) >= tinfo_ref[1, 0])
    def _():
        ys_ref[...] = jnp.zeros_like(ys_ref)


def _experts(tinfo, xs, wg, wu, wd):
    n_sorted = xs.shape[0]
    nt = n_sorted // TM_E

    def tile(j, ti):
        return jnp.minimum(j, ti[1, 0] - 1)

    def wmap(hi):
        return lambda j, ti: (_class_expert(ti[0, tile(j, ti)], hi), 0, 0)

    up = lambda hi: pl.BlockSpec((1, D_MODEL, D_EXPERT), wmap(hi))
    down = lambda hi: pl.BlockSpec((1, D_EXPERT, D_MODEL), wmap(hi))
    return pl.pallas_call(
        _experts_kernel,
        grid_spec=pltpu.PrefetchScalarGridSpec(
            num_scalar_prefetch=1,
            grid=(nt,),
            in_specs=[pl.BlockSpec((TM_E, ROW_W), lambda j, ti: (tile(j, ti), 0)),
                      up(0), up(0), down(0), up(1), up(1), down(1)],
            out_specs=pl.BlockSpec((TM_E, D_MODEL), lambda j, ti: (j, 0))),
        out_shape=jax.ShapeDtypeStruct((n_sorted, D_MODEL), F32),
        compiler_params=_cparams(("arbitrary",)),
        name="moe_experts",
    )(tinfo, xs, wg, wu, wd, wg, wu, wd)


def _combine_kernel(dcur_ref, dnxt_ref, x1_ref, nf_ref, ys_ref, y_ref, buf, sem, *, nb, tt):
    i = pl.program_id(0)
    slot = i % 2

    def gather(dref, s):
        for b in range(nb):
            for t in range(tt):
                d = dref[0, 0, b * tt + t]
                pltpu.make_async_copy(ys_ref.at[pl.ds(d, 1), :], buf.at[s, b, pl.ds(t, 1), :],
                                      sem.at[s]).start()

    @pl.when(i == 0)
    def _():
        gather(dcur_ref, 0)

    @pl.when(i + 1 < pl.num_programs(0))
    def _():
        gather(dnxt_ref, 1 - slot)

    whole = ys_ref.at[pl.ds(0, nb * tt), :]
    pltpu.make_async_copy(whole, whole, sem.at[slot]).wait()
    x2 = x1_ref[...].reshape(nb * tt, D_MODEL) + buf[slot].reshape(nb * tt, D_MODEL)
    y_ref[...] = _rms(x2, nf_ref[...]).reshape(nb, tt, D_MODEL)


def _combine(dest, x1, nf, ys, *, tt, skip):
    nb, tl, _ = x1.shape
    nt = tl // tt - skip
    dspec = lambda f: pl.BlockSpec((1, 1, nb * tt), f, memory_space=pltpu.SMEM)
    return pl.pallas_call(
        functools.partial(_combine_kernel, nb=nb, tt=tt),
        grid=(nt,),
        in_specs=[dspec(lambda i: (i, 0, 0)),
                  dspec(lambda i: (jnp.minimum(i + 1, nt - 1), 0, 0)),
                  pl.BlockSpec((nb, tt, D_MODEL), lambda i: (0, i + skip, 0)),
                  pl.BlockSpec((1, D_MODEL), lambda i: (0, 0)),
                  pl.BlockSpec(memory_space=pl.ANY)],
        out_specs=pl.BlockSpec((nb, tt, D_MODEL), lambda i: (0, i, 0)),
        out_shape=jax.ShapeDtypeStruct((nb, nt * tt, D_MODEL), F32),
        scratch_shapes=[pltpu.VMEM((2, nb, tt, D_MODEL), F32),
                        pltpu.SemaphoreType.DMA((2,))],
        compiler_params=_cparams(("arbitrary",)),
        name="moe_combine",
    )(dest, dest, x1, nf, ys)


def _block_diag(w):
    per = RNN_SUPER // RNN_BLOCK
    w4 = w.reshape(N_SUPER, per, RNN_BLOCK, RNN_BLOCK)
    eye = jnp.eye(per, dtype=w.dtype)
    bd = jnp.einsum('spij,pq->spiqj', w4, eye)
    return bd.reshape(N_SUPER, RNN_SUPER, RNN_SUPER).astype(BF16)


def _stream(x, meta, conv0, h0, c0, n0, m0, p, *, has_head):
    nb, seq, _ = x.shape
    n_null = N_NULL if has_head else 0
    tt = CHUNK if has_head else seq
    xn, gif = _prep(x, meta, p['norm1_w'], p['w_gif'], p['b_gif'], has_head=has_head)
    w_in = p['w_in']
    tt_mm = MM_TILES * CHUNK if has_head else seq
    xr = _mm(xn, w_in, col0=0, ncols=D_RNN, tn=512, tt=tt_mm, out_dtype=F32, w_t=True, name="in_xr")
    gr = _mm(xn, w_in, col0=D_RNN, ncols=D_RNN, tn=512, tt=tt_mm, out_dtype=BF16, w_t=True, gelu=True,
             name="in_gr")
    qkvo = _mm(xn, w_in, col0=2 * D_RNN, ncols=QKVO, tn=1024, tt=tt_mm, out_dtype=BF16, w_t=True,
               name="in_qkvo")
    mg = _mm(xn, p['w_mg'], col0=0, ncols=2 * D_MODEL, tn=1024, tt=tt_mm, out_dtype=BF16, w_t=True,
             name="in_mg")

    tail0 = jnp.transpose(conv0, (1, 0, 2)).reshape((CONV_W - 1) * nb, D_RNN)
    y_r, h_new, tail = _rglru(xr, gr, tail0, h0, p['conv_w'], p['conv_b'], p['rg_a_b'], p['rg_x_b'],
                              p['rg_lambda'], p['wa_bd'], p['wx_bd'], tt=tt, n_null=n_null)
    conv_new = jnp.transpose(tail.reshape(CONV_W - 1, nb, D_RNN), (1, 0, 2))

    y_m, c_new, n_new, m_new = _mlstm_fused(qkvo, gif, c0, n0, m0, p['mlstm_norm_w'], L=tt, n_null=n_null,
                                            nbk=MLSTM_NBK)

    mix = _merge(y_r, y_m, p['w_proj_rnn'], p['w_proj_mlstm'], mg, tn=512, tt=tt)
    routed = has_head
    x1, xn2, gates = _outproj(mix, p['w_out'], x, meta, p['norm2_w'], p['w_router'], p['b_router'],
                              tt=tt, has_head=has_head, routed=routed)
    if routed:
        skip = 1
        dest, tinfo = _plan(gates, tt=tt, skip=skip)
        xs = _dispatch(tinfo, dest, xn2, tt=tt, skip=skip)
        ys = _experts(tinfo, xs, p['w_exp_gate'], p['w_exp_up'], p['w_exp_down'])
        y = _combine(dest, x1, p['norm_f_w'], ys, tt=tt, skip=skip)
    else:
        y = _moe(xn2, gates, x1, p['w_exp_gate'], p['w_exp_up'], p['w_exp_down'], p['norm_f_w'],
                 tt=tt, has_head=has_head)
    return y, (conv_new[None], h_new[None], c_new[None], n_new[None], m_new[None])


def kernel(x_prompt, x_sample, state_rglru_conv, state_rglru_h, state_mlstm_C, state_mlstm_n, state_mlstm_m, meta_tokens, norm1_w, w_in, b_gates, conv_w, conv_b, rg_a_w, rg_a_b, rg_x_w, rg_x_b, rg_lambda, mlstm_norm_w, w_proj_rnn, w_proj_mlstm, w_out, norm2_w, w_router_group, b_router_group, w_router_expert, b_router_expert, w_exp_gate, w_exp_up, w_exp_down, norm_f_w):
    l = 0
    w_in_t = jnp.swapaxes(w_in[l], 0, 1)
    pad_r = LANES - N_GROUPS - N_EXPERTS
    p = dict(
        norm1_w=norm1_w[l][None], w_in=w_in_t,
        w_gif=jnp.pad(w_in_t[GIF_OFF:MG_OFF].T, ((0, 0), (0, LANES - 2 * N_HEADS_M))).astype(BF16),
        b_gif=jnp.pad(b_gates[l], (0, LANES - 2 * N_HEADS_M))[None],
        w_mg=w_in_t[MG_OFF:],
        conv_w=conv_w[l], conv_b=conv_b[l][None], rg_a_b=rg_a_b[l][None], rg_x_b=rg_x_b[l][None],
        rg_lambda=rg_lambda[l][None], wa_bd=_block_diag(rg_a_w[l]), wx_bd=_block_diag(rg_x_w[l]),
        mlstm_norm_w=mlstm_norm_w[l][None],
        w_proj_rnn=w_proj_rnn[l], w_proj_mlstm=w_proj_mlstm[l], w_out=w_out[l].astype(BF16),
        norm2_w=norm2_w[l][None],
        w_router=jnp.pad(jnp.concatenate([w_router_group[l], w_router_expert[l]], axis=1),
                         ((0, 0), (0, pad_r))).astype(BF16),
        b_router=jnp.pad(jnp.concatenate([b_router_group[l], b_router_expert[l]]), (0, pad_r))[None],
        w_exp_gate=w_exp_gate[l].astype(BF16), w_exp_up=w_exp_up[l].astype(BF16),
        w_exp_down=w_exp_down[l].astype(BF16), norm_f_w=norm_f_w[None],
    )
    nbp = x_prompt.shape[0]
    dt = x_prompt.dtype
    y_p, st_p = _stream(
        x_prompt, meta_tokens,
        jnp.zeros((nbp, CONV_W - 1, D_RNN), dt), jnp.zeros((nbp, D_RNN), F32),
        jnp.zeros((nbp, N_HEADS_M, DQK_M, DV_M), F32), jnp.zeros((nbp, N_HEADS_M, DQK_M), F32),
        jnp.zeros((nbp, N_HEADS_M), F32), p, has_head=True)
    y_s, st_s = _stream(
        x_sample, meta_tokens, state_rglru_conv[l], state_rglru_h[l], state_mlstm_C[l],
        state_mlstm_n[l], state_mlstm_m[l], p, has_head=False)
    return (y_p, y_s) + st_p + st_s
```

```python
import functools

import jax
import jax.numpy as jnp
from jax import lax
from jax.experimental import pallas as pl
from jax.experimental.pallas import tpu as pltpu

F32 = jnp.float32
BF16 = jnp.bfloat16

D_MODEL = 2048
CHUNK = 64
N_META = 16
N_NULL = CHUNK - N_META
D_RNN = 2560
N_RNN_BLOCKS = 16
RNN_BLOCK = D_RNN // N_RNN_BLOCKS
RNN_SUPER = 640
N_SUPER = D_RNN // RNN_SUPER
CONV_W = 4
LRU_C = 8.0
N_HEADS_M = 8
DV_M = D_MODEL // N_HEADS_M
DQK_M = DV_M // 2
HQK = N_HEADS_M * DQK_M
HV = N_HEADS_M * DV_M
QKVO = 2 * HQK + 2 * HV
N_GROUPS = 4
EXPERTS_PER_GROUP = 4
N_EXPERTS = 16
D_EXPERT = 512
PAIRS_PER_GROUP = 6
N_CLASSES = N_GROUPS * PAIRS_PER_GROUP
MM_TILES = 3
MLSTM_NBK = 2
TM_E = 256
ROW_W = D_MODEL + 128
EPS = 1e-6
LANES = 128
GIF_OFF = 2 * D_RNN + QKVO
MG_OFF = GIF_OFF + 2 * N_HEADS_M
VMEM_LIMIT = 56 * 1024 * 1024
NEG_BIG = -1e30


def _cparams(sem):
    return pltpu.CompilerParams(dimension_semantics=sem, vmem_limit_bytes=VMEM_LIMIT)


def _rms(x, w):
    return x * lax.rsqrt(jnp.mean(x * x, axis=-1, keepdims=True) + EPS) * w


def _softplus(x):
    return jnp.maximum(x, 0.0) + jnp.log1p(jnp.exp(-jnp.abs(x)))


def _head_rows(meta_ref, nb):
    head = jnp.concatenate([jnp.zeros((N_NULL, D_MODEL), F32), meta_ref[...]], axis=0)
    return jnp.broadcast_to(head[None], (nb, CHUNK, D_MODEL)).reshape(nb * CHUNK, D_MODEL)


def _x_map(has_head):
    if has_head:
        return lambda ti: (0, jnp.maximum(ti - 1, 0), 0)
    return lambda ti: (0, ti, 0)


def _prep_kernel(x_ref, meta_ref, nw_ref, wg_ref, bg_ref, xn_ref, gif_ref, *, nb, tt, has_head):
    def emit(x):
        xnb = _rms(x, nw_ref[...]).astype(BF16)
        xn_ref[...] = xnb.reshape(nb, tt, D_MODEL)
        gif = jnp.dot(xnb, wg_ref[...], preferred_element_type=F32) + bg_ref[...]
        gif_ref[...] = gif.reshape(nb, tt, LANES)

    if has_head:
        @pl.when(pl.program_id(0) == 0)
        def _():
            emit(_head_rows(meta_ref, nb))

        @pl.when(pl.program_id(0) > 0)
        def _():
            emit(x_ref[...].reshape(nb * tt, D_MODEL))
    else:
        emit(x_ref[...].reshape(nb * tt, D_MODEL))


def _prep(x, meta, norm_w, w_gif, b_gif, *, has_head):
    nb, seq, _ = x.shape
    tt = CHUNK if has_head else seq
    n_tiles = seq // tt + (1 if has_head else 0)
    tl = n_tiles * tt
    const = lambda ti: (0, 0)
    return pl.pallas_call(
        functools.partial(_prep_kernel, nb=nb, tt=tt, has_head=has_head),
        grid=(n_tiles,),
        in_specs=[pl.BlockSpec((nb, tt, D_MODEL), _x_map(has_head)),
                  pl.BlockSpec((N_META, D_MODEL), const),
                  pl.BlockSpec((1, D_MODEL), const),
                  pl.BlockSpec((D_MODEL, LANES), const),
                  pl.BlockSpec((1, LANES), const)],
        out_specs=[pl.BlockSpec((nb, tt, D_MODEL), lambda ti: (0, ti, 0)),
                   pl.BlockSpec((nb, tt, LANES), lambda ti: (0, ti, 0))],
        out_shape=[jax.ShapeDtypeStruct((nb, tl, D_MODEL), BF16),
                   jax.ShapeDtypeStruct((nb, tl, LANES), F32)],
        compiler_params=_cparams(("arbitrary",)),
        name="prep",
    )(x, meta, norm_w, w_gif, b_gif)


def _gelu_tanh(g):
    return g * (0.5 * (1.0 + jnp.tanh(0.7978845608028654 * (g + 0.044715 * (g * g * g)))))


def _mm_kernel(*refs, has_gate, has_add, w_t, gelu):
    lhs_ref, w_ref = refs[0], refs[1]
    pos = 2
    gate_ref = add_ref = None
    if has_gate:
        gate_ref = refs[pos]
        pos += 1
    if has_add:
        add_ref = refs[pos]
        pos += 1
    out_ref, wbf_ref = refs[pos], refs[pos + 1]
    nb, tt, k = lhs_ref.shape
    tn = out_ref.shape[-1]

    @pl.when(pl.program_id(1) == 0)
    def _():
        w = w_ref[...]
        wbf_ref[...] = (w.T if w_t else w).astype(BF16)

    acc = jnp.dot(lhs_ref[...].reshape(nb * tt, k), wbf_ref[...], preferred_element_type=F32)
    if gelu:
        acc = _gelu_tanh(acc)
    if has_gate:
        acc = jax.nn.sigmoid(gate_ref[...].reshape(nb * tt, tn).astype(F32)) * acc
    if has_add:
        acc = add_ref[...].reshape(nb * tt, tn).astype(F32) + acc
    out_ref[...] = acc.astype(out_ref.dtype).reshape(nb, tt, tn)


def _mm(lhs, w, *, col0, ncols, tn, tt, out_dtype, gate=None, gate_col0=0, add=None, w_t=False, gelu=False,
        name):
    nb, tl, k = lhs.shape
    assert tl % tt == 0 and ncols % tn == 0 and col0 % tn == 0 and gate_col0 % tn == 0
    if w_t:
        w_spec = pl.BlockSpec((tn, k), lambda j, i: (col0 // tn + j, 0))
    else:
        w_spec = pl.BlockSpec((k, tn), lambda j, i: (0, col0 // tn + j))
    in_specs = [pl.BlockSpec((nb, tt, k), lambda j, i: (0, i, 0)), w_spec]
    args = [lhs, w]
    if gate is not None:
        in_specs.append(pl.BlockSpec((nb, tt, tn), lambda j, i: (0, i, gate_col0 // tn + j)))
        args.append(gate)
    if add is not None:
        in_specs.append(pl.BlockSpec((nb, tt, tn), lambda j, i: (0, i, j)))
        args.append(add)
    return pl.pallas_call(
        functools.partial(_mm_kernel, has_gate=gate is not None, has_add=add is not None, w_t=w_t, gelu=gelu),
        grid=(ncols // tn, tl // tt),
        in_specs=in_specs,
        out_specs=pl.BlockSpec((nb, tt, tn), lambda j, i: (0, i, j)),
        out_shape=jax.ShapeDtypeStruct((nb, tl, ncols), out_dtype),
        scratch_shapes=[pltpu.VMEM((k, tn), BF16)],
        compiler_params=_cparams(("arbitrary", "arbitrary")),
        name=name,
    )(*args)


def _merge_kernel(yr_ref, ym_ref, wr_ref, wm_ref, gr_ref, gm_ref, out_ref, wr_bf, wm_bf):
    nb, tt, tn = out_ref.shape
    rows = nb * tt

    @pl.when(pl.program_id(1) == 0)
    def _():
        wr_bf[...] = wr_ref[...].astype(BF16)
        wm_bf[...] = wm_ref[...].astype(BF16)

    pr = jnp.dot(yr_ref[...].reshape(rows, D_RNN), wr_bf[...], preferred_element_type=F32)
    pm = jnp.dot(ym_ref[...].reshape(rows, HV), wm_bf[...], preferred_element_type=F32)
    mix = (jax.nn.sigmoid(gr_ref[...].reshape(rows, tn).astype(F32)) * pr
           + jax.nn.sigmoid(gm_ref[...].reshape(rows, tn).astype(F32)) * pm)
    out_ref[...] = mix.astype(BF16).reshape(nb, tt, tn)


def _merge(y_r, y_m, w_r, w_m, mg, *, tn, tt):
    nb, tl, _ = y_r.shape
    nj = D_MODEL // tn
    return pl.pallas_call(
        _merge_kernel,
        grid=(nj, tl // tt),
        in_specs=[pl.BlockSpec((nb, tt, D_RNN), lambda j, i: (0, i, 0)),
                  pl.BlockSpec((nb, tt, HV), lambda j, i: (0, i, 0)),
                  pl.BlockSpec((D_RNN, tn), lambda j, i: (0, j)),
                  pl.BlockSpec((HV, tn), lambda j, i: (0, j)),
                  pl.BlockSpec((nb, tt, tn), lambda j, i: (0, i, j)),
                  pl.BlockSpec((nb, tt, tn), lambda j, i: (0, i, nj + j))],
        out_specs=pl.BlockSpec((nb, tt, tn), lambda j, i: (0, i, j)),
        out_shape=jax.ShapeDtypeStruct((nb, tl, D_MODEL), BF16),
        scratch_shapes=[pltpu.VMEM((D_RNN, tn), BF16), pltpu.VMEM((HV, tn), BF16)],
        compiler_params=_cparams(("arbitrary", "arbitrary")),
        name="merge",
    )(y_r, y_m, w_r, w_m, mg, mg)


def _rglru_kernel(xr_ref, gr_ref, tail0_ref, h0_ref, cw_ref, cb_ref, ab_ref, xb_ref, lam_ref,
                  wa_ref, wx_ref, y_ref, hl_ref, tail_ref, xp_s, h_s, a_s, u_s, hs_s, *, nb, tt, n_null):
    ti = pl.program_id(1)
    rt = tt * nb
    tb = (CONV_W - 1) * nb

    def time_major(x):
        return jnp.swapaxes(x, 0, 1).reshape(rt, x.shape[-1])

    @pl.when(ti == 0)
    def _():
        xp_s[0:tb, :] = tail0_ref[...]
        h_s[...] = h0_ref[...]

    xp_s[tb:tb + rt, :] = time_major(xr_ref[...])
    cw = cw_ref[...]
    xc = cb_ref[...] + cw[0:1, :] * xp_s[0:rt, :]
    for j in range(1, CONV_W):
        xc = xc + cw[j:j + 1, :] * xp_s[j * nb:j * nb + rt, :]
    xcb = xc.astype(BF16)
    r = jax.nn.sigmoid(jnp.dot(xcb, wa_ref[0], preferred_element_type=F32) + ab_ref[...])
    i = jax.nn.sigmoid(jnp.dot(xcb, wx_ref[0], preferred_element_type=F32) + xb_ref[...])
    log_a = r * ((-LRU_C) * _softplus(-lam_ref[...]))
    a = jnp.exp(log_a)
    a_s[...] = a
    u = jnp.sqrt(-jnp.tanh(log_a) * (a * a + 1.0)) * (i * xc)
    if n_null:
        row = lax.broadcasted_iota(jnp.int32, (rt, 1), 0)
        u = jnp.where((row >= n_null * nb) | (ti > 0), u, 0.0)
    u_s[...] = u

    def step(t, h):
        o = pl.multiple_of(t * nb, nb)
        h = a_s[pl.ds(o, nb), :] * h + u_s[pl.ds(o, nb), :]
        hs_s[pl.ds(o, nb), :] = h
        return h

    h = lax.fori_loop(0, tt, step, h_s[...], unroll=8)
    h_s[...] = h
    h_bm = jnp.swapaxes(hs_s[...].reshape(tt, nb, hs_s.shape[-1]), 0, 1)
    y_ref[...] = (gr_ref[...].astype(F32) * h_bm).astype(BF16)
    xp_s[0:tb, :] = xp_s[rt:rt + tb, :]

    @pl.when(ti == pl.num_programs(1) - 1)
    def _():
        hl_ref[...] = h
        tail_ref[...] = xp_s[rt:rt + tb, :]


def _rglru(xr, gr, tail0, h0, cw, cb, ab, xb, lam, wa, wx, *, tt, n_null):
    nb, tl, _ = xr.shape
    rt = tt * nb
    tb = (CONV_W - 1) * nb
    blk = lambda s, ti: (0, ti, s)
    col = lambda s, ti: (0, s)
    return pl.pallas_call(
        functools.partial(_rglru_kernel, nb=nb, tt=tt, n_null=n_null),
        grid=(N_SUPER, tl // tt),
        in_specs=[pl.BlockSpec((nb, tt, RNN_SUPER), blk),
                  pl.BlockSpec((nb, tt, RNN_SUPER), blk),
                  pl.BlockSpec((tb, RNN_SUPER), col),
                  pl.BlockSpec((nb, RNN_SUPER), col),
                  pl.BlockSpec((CONV_W, RNN_SUPER), col),
                  pl.BlockSpec((1, RNN_SUPER), col),
                  pl.BlockSpec((1, RNN_SUPER), col),
                  pl.BlockSpec((1, RNN_SUPER), col),
                  pl.BlockSpec((1, RNN_SUPER), col),
                  pl.BlockSpec((1, RNN_SUPER, RNN_SUPER), lambda s, ti: (s, 0, 0)),
                  pl.BlockSpec((1, RNN_SUPER, RNN_SUPER), lambda s, ti: (s, 0, 0))],
        out_specs=[pl.BlockSpec((nb, tt, RNN_SUPER), blk),
                   pl.BlockSpec((nb, RNN_SUPER), col),
                   pl.BlockSpec((tb, RNN_SUPER), col)],
        out_shape=[jax.ShapeDtypeStruct((nb, tl, D_RNN), BF16),
                   jax.ShapeDtypeStruct((nb, D_RNN), F32),
                   jax.ShapeDtypeStruct((tb, D_RNN), F32)],
        scratch_shapes=[pltpu.VMEM((rt + tb, RNN_SUPER), F32),
                        pltpu.VMEM((nb, RNN_SUPER), F32),
                        pltpu.VMEM((rt, RNN_SUPER), F32),
                        pltpu.VMEM((rt, RNN_SUPER), F32),
                        pltpu.VMEM((rt, RNN_SUPER), F32)],
        compiler_params=_cparams(("arbitrary", "arbitrary")),
        name="rglru",
    )(xr, gr, tail0, h0, cw, cb, ab, xb, lam, wa, wx)


DVA = DV_M + 128


def _mlstm_gates(c, gif_ref, m_s, *, L, n_null):
    hm = N_HEADS_M
    g = gif_ref[...]
    if L < LANES:
        g = jnp.concatenate([g, jnp.zeros((LANES - L, LANES), F32)], axis=0)
    g_t = g.T
    lane = lax.broadcasted_iota(jnp.int32, (hm, LANES), 1)
    valid = lane < L
    li = g_t[0:hm, :]
    lfr = g_t[hm:2 * hm, :]
    lf = jnp.where(valid, jnp.minimum(lfr, 0.0) - jnp.log1p(jnp.exp(-jnp.abs(lfr))), 0.0)
    if n_null:
        null = (lane < n_null) & (c == 0)
        lf = jnp.where(null, 0.0, lf)
        li = jnp.where(null, NEG_BIG, li)
    r128 = lax.broadcasted_iota(jnp.int32, (LANES, LANES), 0)
    c128 = lax.broadcasted_iota(jnp.int32, (LANES, LANES), 1)
    upper = jnp.where(r128 <= c128, 1.0, 0.0)
    b = jnp.dot(lf, upper, preferred_element_type=F32, precision=lax.Precision.HIGHEST)
    a = li - b
    cm = jnp.where(valid, a, -jnp.inf)
    sh = 1
    while sh < L:
        cm = jnp.maximum(cm, jnp.where(lane >= sh, pltpu.roll(cm, sh, axis=1), -jnp.inf))
        sh *= 2
    m0 = m_s[...]
    big_m = jnp.maximum(cm, m0)
    m_last = jnp.max(jnp.where(valid, big_m, -jnp.inf), axis=1, keepdims=True)
    b_last = jnp.sum(lf, axis=1, keepdims=True)
    scale = DQK_M ** -0.5
    e_mv = jnp.exp(-(b + big_m))
    w_int = jnp.exp(m0 - big_m)
    w_end = jnp.exp(a - m_last) * scale
    decay = jnp.exp(m0 - m_last)
    m_s[...] = jnp.broadcast_to(b_last + m_last, (hm, LANES))
    cols = jnp.concatenate([big_m, e_mv, w_int, w_end, jnp.zeros((LANES - 4 * hm, LANES), F32)], axis=0).T
    return a, cols, decay


def _mlstm_heads(gates, q_ref, k_ref, v_ref, og_ref, nw_ref, y_ref, c_s, *, L):
    hm = N_HEADS_M
    scale = DQK_M ** -0.5
    rr = lax.broadcasted_iota(jnp.int32, (L, L), 0)
    cc = lax.broadcasted_iota(jnp.int32, (L, L), 1)
    causal = rr >= cc
    ones_col = jnp.where(lax.broadcasted_iota(jnp.int32, (L, 128), 1) == 0, 1.0, 0.0).astype(BF16)
    nw = nw_ref[...]
    dn_nt = (((1,), (1,)), ((), ()))
    dn_tn = (((0,), (0,)), ((), ()))
    bh = [(bi, h) for bi in range(len(gates)) for h in range(hm)]
    col = lambda bi, k, h: gates[bi][1][0:L, k * hm + h:k * hm + h + 1]
    qs = [q_ref[bi, :, h * DQK_M:(h + 1) * DQK_M] for bi, h in bh]
    ks = [k_ref[bi, :, h * DQK_M:(h + 1) * DQK_M] for bi, h in bh]
    v_augs = [jnp.concatenate([v_ref[bi, :, h * DV_M:(h + 1) * DV_M], ones_col], axis=1) for bi, h in bh]
    c_augs = [c_s[bi, h] for bi, h in bh]
    s_raw = [lax.dot_general(q, k, dn_nt, preferred_element_type=F32) for q, k in zip(qs, ks)]
    inter = [jnp.dot((qs[i].astype(F32) * col(bi, 2, h)).astype(BF16), c_augs[i].astype(BF16),
                     preferred_element_type=F32) for i, (bi, h) in enumerate(bh)]
    upd = [lax.dot_general((ks[i].astype(F32) * col(bi, 3, h)).astype(BF16), v_augs[i], dn_tn,
                           preferred_element_type=F32) for i, (bi, h) in enumerate(bh)]
    s_w = [(s_raw[i] * (scale * jnp.where(causal, jnp.exp(gates[bi][0][h:h + 1, 0:L] - col(bi, 0, h)), 0.0))
            ).astype(BF16) for i, (bi, h) in enumerate(bh)]
    res = [jnp.dot(s_w[i], v_augs[i], preferred_element_type=F32) + inter[i] for i in range(len(bh))]
    for i, (bi, h) in enumerate(bh):
        c_s[bi, h] = gates[bi][2][h:h + 1, 0:1] * c_augs[i] + upd[i]
    for i, (bi, h) in enumerate(bh):
        num = res[i][:, :DV_M]
        denom = jnp.maximum(jnp.abs(res[i][:, DV_M:DV_M + 1]), col(bi, 1, h))
        hh = num / denom
        hh = hh * lax.rsqrt(jnp.mean(hh * hh, axis=-1, keepdims=True) + EPS)
        hh = hh * nw[:, h * DV_M:(h + 1) * DV_M]
        og = og_ref[bi, :, h * DV_M:(h + 1) * DV_M].astype(F32)
        y_ref[bi, :, h * DV_M:(h + 1) * DV_M] = (hh * jax.nn.sigmoid(og)).astype(BF16)


def _mlstm_fused_kernel(q_ref, k_ref, v_ref, og_ref, gif_ref, c0_ref, n0_ref, m0_ref, nw_ref,
                        y_ref, co_ref, no_ref, mo_ref, c_s, m_s, *, L, n_null, nbk):
    c = pl.program_id(1)
    bh = [(bi, h) for bi in range(nbk) for h in range(N_HEADS_M)]

    @pl.when(c == 0)
    def _():
        m_s[...] = m0_ref[...]
        for bi, h in bh:
            c_s[bi, h, :, :DV_M] = c0_ref[bi, h]
            n_tile = jnp.concatenate([n0_ref[bi, h:h + 1, :], jnp.zeros((DQK_M - 1, DQK_M), F32)], axis=0)
            c_s[bi, h, :, DV_M:] = n_tile.T

    gates = [_mlstm_gates(c, gif_ref.at[bi], m_s.at[bi], L=L, n_null=n_null) for bi in range(nbk)]
    _mlstm_heads(gates, q_ref, k_ref, v_ref, og_ref, nw_ref, y_ref, c_s, L=L)

    @pl.when(c == pl.num_programs(1) - 1)
    def _():
        mo_ref[...] = m_s[...]
        for bi, h in bh:
            co_ref[bi, h] = c_s[bi, h, :, :DV_M]
            no_ref[bi, h:h + 1, :] = c_s[bi, h, :, DV_M:].T[0:1, :]


def _mlstm_fused(qkvo, gif, c0, n0, m0, nw, *, L, n_null, nbk):
    nb, tl, _ = qkvo.shape
    assert nb % nbk == 0
    hm = N_HEADS_M
    assert DQK_M == LANES and DVA - DV_M == DQK_M
    m_rep0 = jnp.broadcast_to(m0[..., None], (nb, hm, LANES))
    st4 = lambda b, c: (b, 0, 0, 0)
    st3 = lambda b, c: (b, 0, 0)
    y, c_new, n_new, m_rep = pl.pallas_call(
        functools.partial(_mlstm_fused_kernel, L=L, n_null=n_null, nbk=nbk),
        grid=(nb // nbk, tl // L),
        in_specs=[pl.BlockSpec((nbk, L, HQK), lambda b, c: (b, c, 0)),
                  pl.BlockSpec((nbk, L, HQK), lambda b, c: (b, c, 1)),
                  pl.BlockSpec((nbk, L, HV), lambda b, c: (b, c, 1)),
                  pl.BlockSpec((nbk, L, HV), lambda b, c: (b, c, 2)),
                  pl.BlockSpec((nbk, L, LANES), lambda b, c: (b, c, 0)),
                  pl.BlockSpec((nbk, hm, DQK_M, DV_M), st4),
                  pl.BlockSpec((nbk, hm, DQK_M), st3),
                  pl.BlockSpec((nbk, hm, LANES), st3),
                  pl.BlockSpec((1, HV), lambda b, c: (0, 0))],
        out_specs=[pl.BlockSpec((nbk, L, HV), lambda b, c: (b, c, 0)),
                   pl.BlockSpec((nbk, hm, DQK_M, DV_M), st4),
                   pl.BlockSpec((nbk, hm, DQK_M), st3),
                   pl.BlockSpec((nbk, hm, LANES), st3)],
        out_shape=[jax.ShapeDtypeStruct((nb, tl, HV), BF16),
                   jax.ShapeDtypeStruct((nb, hm, DQK_M, DV_M), F32),
                   jax.ShapeDtypeStruct((nb, hm, DQK_M), F32),
                   jax.ShapeDtypeStruct((nb, hm, LANES), F32)],
        scratch_shapes=[pltpu.VMEM((nbk, hm, DQK_M, DVA), F32),
                        pltpu.VMEM((nbk, hm, LANES), F32)],
        compiler_params=_cparams(("arbitrary", "arbitrary")),
        name="mlstm",
    )(qkvo, qkvo, qkvo, qkvo, gif, c0, n0, m_rep0, nw)
    return y, c_new, n_new, m_rep[..., 0]


def _route(logits, dense):
    tm = logits.shape[0]
    lane = lax.broadcasted_iota(jnp.int32, (tm, LANES), 1)
    lanef = lane.astype(F32)
    gl = jnp.where(lane < N_GROUPS, logits, -jnp.inf)
    gmax = jnp.max(gl, axis=1, keepdims=True)
    gidx = jnp.min(jnp.where(gl == gmax, lanef, float(LANES)), axis=1, keepdims=True)
    gw = 1.0 / jnp.sum(jnp.exp(gl - gmax), axis=1, keepdims=True)
    lo = float(N_GROUPS) + float(EXPERTS_PER_GROUP) * gidx
    sel = (lanef >= lo) & (lanef < lo + float(EXPERTS_PER_GROUP))
    ev = jnp.where(sel, logits, -jnp.inf)
    v1 = jnp.max(ev, axis=1, keepdims=True)
    i1 = jnp.min(jnp.where(ev == v1, lanef, float(LANES)), axis=1, keepdims=True)
    ev2 = jnp.where(lanef == i1, -jnp.inf, ev)
    v2 = jnp.max(ev2, axis=1, keepdims=True)
    i2 = jnp.min(jnp.where(ev2 == v2, lanef, float(LANES)), axis=1, keepdims=True)
    t = jnp.exp(v2 - v1)
    w1 = gw / (1.0 + t)
    w2 = gw * (t / (1.0 + t))
    if dense:
        return jnp.where(lanef == i1, w1, jnp.where(lanef == i2, w2, 0.0))
    l1 = i1 - lo
    l2 = i2 - lo
    first = l1 < l2
    a = jnp.minimum(l1, l2)
    b = jnp.maximum(l1, l2)
    pair = jnp.where(a == 0.0, b - 1.0, jnp.where(a == 1.0, b + 1.0, 5.0))
    cls = float(PAIRS_PER_GROUP) * gidx + pair
    return jnp.where(lane == 0, cls,
                     jnp.where(lane == 1, jnp.where(first, w1, w2),
                               jnp.where(lane == 2, jnp.where(first, w2, w1), 0.0)))


def _outproj_kernel(mix_ref, w_ref, x_ref, meta_ref, n2_ref, wr_ref, br_ref, x1_ref, xn2_ref, gates_ref,
                    *, nb, tt, has_head, routed):
    def emit(x):
        mix = mix_ref[...].reshape(nb * tt, D_MODEL)
        x1 = x + jnp.dot(mix, w_ref[...], preferred_element_type=F32)
        x1_ref[...] = x1.reshape(nb, tt, D_MODEL)
        xn2 = _rms(x1, n2_ref[...]).astype(BF16)
        logits = jnp.dot(xn2, wr_ref[...], preferred_element_type=F32) + br_ref[...]
        info = _route(logits, dense=not routed)
        gates_ref[...] = info.reshape(nb, tt, LANES)
        if routed:
            xn2_ref[:, :, :D_MODEL] = xn2.astype(F32).reshape(nb, tt, D_MODEL)
            xn2_ref[:, :, D_MODEL:] = info.reshape(nb, tt, LANES)
        else:
            xn2_ref[...] = xn2.reshape(nb, tt, D_MODEL)

    if has_head:
        @pl.when(pl.program_id(0) == 0)
        def _():
            emit(_head_rows(meta_ref, nb))

        @pl.when(pl.program_id(0) > 0)
        def _():
            emit(x_ref[...].reshape(nb * tt, D_MODEL))
    else:
        emit(x_ref[...].reshape(nb * tt, D_MODEL))


def _outproj(mix, w_out, x, meta, n2, w_router, b_router, *, tt, has_head, routed):
    nb, tl, _ = mix.shape
    const = lambda i: (0, 0)
    blk = lambda i: (0, i, 0)
    xn2_w, xn2_dt = (ROW_W, F32) if routed else (D_MODEL, BF16)
    return pl.pallas_call(
        functools.partial(_outproj_kernel, nb=nb, tt=tt, has_head=has_head, routed=routed),
        grid=(tl // tt,),
        in_specs=[pl.BlockSpec((nb, tt, D_MODEL), blk),
                  pl.BlockSpec((D_MODEL, D_MODEL), const, pipeline_mode=pl.Buffered(1)),
                  pl.BlockSpec((nb, tt, D_MODEL), _x_map(has_head)),
                  pl.BlockSpec((N_META, D_MODEL), const),
                  pl.BlockSpec((1, D_MODEL), const),
                  pl.BlockSpec((D_MODEL, LANES), const),
                  pl.BlockSpec((1, LANES), const)],
        out_specs=[pl.BlockSpec((nb, tt, D_MODEL), blk),
                   pl.BlockSpec((nb, tt, xn2_w), blk),
                   pl.BlockSpec((nb, tt, LANES), blk)],
        out_shape=[jax.ShapeDtypeStruct((nb, tl, D_MODEL), F32),
                   jax.ShapeDtypeStruct((nb, tl, xn2_w), xn2_dt),
                   jax.ShapeDtypeStruct((nb, tl, LANES), F32)],
        compiler_params=_cparams(("arbitrary",)),
        name="outproj",
    )(mix, w_out, x, meta, n2, w_router, b_router)


def _moe_kernel(xn_ref, gates_ref, x1_ref, wg_ref, wu_ref, wd_ref, nf_ref, y_ref, wgb_ref, wub_ref, wdb_ref,
                acc_s, *, nb, tt):
    e = pl.program_id(0)
    hf = pl.program_id(1)
    rows = nb * tt

    @pl.when((e == 0) & (hf == 0))
    def _():
        acc_s[...] = jnp.zeros_like(acc_s)

    wg = wg_ref[0].astype(BF16)
    wu = wu_ref[0].astype(BF16)
    wd = wd_ref[0].astype(BF16)
    wgb_ref[0] = wg
    wub_ref[0] = wu
    wdb_ref[0] = wd
    x = xn_ref[...].reshape(rows, D_MODEL)
    hg = jnp.dot(x, wg, preferred_element_type=F32)
    hu = jnp.dot(x, wu, preferred_element_type=F32)
    hdn = (hg * jax.nn.sigmoid(hg)) * hu
    yd = jnp.dot(hdn.astype(BF16), wd, preferred_element_type=F32)
    gates = gates_ref[...].reshape(rows, LANES)
    lane = lax.broadcasted_iota(jnp.int32, gates.shape, 1)
    gcol = jnp.sum(jnp.where(lane == e + N_GROUPS, gates, 0.0), axis=1, keepdims=True)
    acc_s[...] += gcol * yd

    @pl.when((e == pl.num_programs(0) - 1) & (hf == pl.num_programs(1) - 1))
    def _():
        x2 = x1_ref[...].reshape(rows, D_MODEL) + acc_s[...]
        y_ref[...] = _rms(x2, nf_ref[...]).reshape(nb, tt, D_MODEL)


def _moe(xn2, gates, x1, wg, wu, wd, nf):
    nb, tt, _ = xn2.shape
    halves = 2
    dh = D_EXPERT // halves
    blk = lambda e, hf: (0, 0, 0)
    up = pl.BlockSpec((1, D_MODEL, dh), lambda e, hf: (e, 0, hf))
    down = pl.BlockSpec((1, dh, D_MODEL), lambda e, hf: (e, hf, 0))
    return pl.pallas_call(
        functools.partial(_moe_kernel, nb=nb, tt=tt),
        grid=(N_EXPERTS, halves),
        in_specs=[pl.BlockSpec((nb, tt, D_MODEL), blk),
                  pl.BlockSpec((nb, tt, LANES), blk),
                  pl.BlockSpec((nb, tt, D_MODEL), blk),
                  up, up, down,
                  pl.BlockSpec((1, D_MODEL), lambda e, hf: (0, 0))],
        out_specs=[pl.BlockSpec((nb, tt, D_MODEL), blk), up, up, down],
        out_shape=[jax.ShapeDtypeStruct((nb, tt, D_MODEL), F32),
                   jax.ShapeDtypeStruct(wg.shape, BF16),
                   jax.ShapeDtypeStruct(wu.shape, BF16),
                   jax.ShapeDtypeStruct(wd.shape, BF16)],
        scratch_shapes=[pltpu.VMEM((nb * tt, D_MODEL), F32)],
        compiler_params=_cparams(("arbitrary", "arbitrary")),
        name="moe",
    )(xn2, gates, x1, wg, wu, wd, nf)


_NT = (((1,), (1,)), ((), ()))


def _plan_kernel(ri_ref, dest_ref, tinfo_ref, cnt_s, carry_s, offs_s, *, rows):
    ph = pl.program_id(0)
    i = pl.program_id(1)
    ri = ri_ref[...].reshape(rows, LANES)
    lanef = lax.broadcasted_iota(jnp.int32, (rows, LANES), 1).astype(F32)
    oh = jnp.where(lanef == ri[:, 0:1], 1.0, 0.0)
    colsum = jnp.sum(oh, axis=0, keepdims=True)
    r128 = lax.broadcasted_iota(jnp.int32, (LANES, LANES), 0)
    c128 = lax.broadcasted_iota(jnp.int32, (LANES, LANES), 1)
    eye = jnp.where(r128 == c128, 1.0, 0.0).astype(BF16)

    @pl.when((ph == 0) & (i == 0))
    def _():
        cnt_s[...] = jnp.zeros_like(cnt_s)

    @pl.when(ph == 0)
    def _():
        cnt_s[...] += colsum

    @pl.when((ph == 1) & (i == 0))
    def _():
        tiles = jnp.floor((cnt_s[...] + float(TM_E - 1)) * (1.0 / TM_E))
        upper = jnp.where(r128 <= c128, 1.0, 0.0).astype(BF16)
        cum_incl = jnp.dot(jnp.broadcast_to(tiles, (8, LANES)).astype(BF16), upper, preferred_element_type=F32)
        offs_s[...] = (cum_incl[0:1, :] - tiles) * float(TM_E)
        carry_s[...] = jnp.zeros_like(carry_s)
        cum_col = lax.dot_general(eye, cum_incl.astype(BF16), _NT, preferred_element_type=F32)
        ended = (cum_col[:, 0:1] <= c128.astype(F32)) & (r128 < N_CLASSES)
        tcls = jnp.sum(jnp.where(ended, 1.0, 0.0), axis=0, keepdims=True)
        total = cum_incl[0:1, LANES - 1:LANES]
        row8 = lax.broadcasted_iota(jnp.int32, (8, LANES), 0)
        tinfo_ref[...] = jnp.where(row8 == 0, tcls, total).astype(jnp.int32)

    @pl.when(ph == 1)
    def _():
        tri = jnp.where(lax.broadcasted_iota(jnp.int32, (rows, rows), 0)
                        > lax.broadcasted_iota(jnp.int32, (rows, rows), 1), 1.0, 0.0).astype(BF16)
        rank = jnp.dot(tri, oh.astype(BF16), preferred_element_type=F32)
        dest = jnp.sum(oh * (rank + (offs_s[...] + carry_s[...])), axis=1, keepdims=True)
        carry_s[...] += colsum
        dhi = jnp.floor(dest * (1.0 / 256.0))
        dlo = dest - 256.0 * dhi
        digits = jnp.where(lanef == 0.0, dlo, jnp.where(lanef == 1.0, dhi, 0.0)).astype(BF16)
        dt = lax.dot_general(eye, digits, _NT, preferred_element_type=F32)
        dest_ref[...] = (dt[0:1, :] + 256.0 * dt[1:2, :]).astype(jnp.int32).reshape(1, 1, rows)


def _plan(rinfo, *, tt, skip):
    nb, tl, _ = rinfo.shape
    rows = nb * tt
    nt = tl // tt - skip
    assert (nt * rows) // TM_E + N_CLASSES <= LANES and rows % LANES == 0
    return pl.pallas_call(
        functools.partial(_plan_kernel, rows=rows),
        grid=(2, nt),
        in_specs=[pl.BlockSpec((nb, tt, LANES), lambda ph, i: (0, i + skip, 0))],
        out_specs=[pl.BlockSpec((1, 1, rows), lambda ph, i: (i * ph, 0, 0)),
                   pl.BlockSpec((8, LANES), lambda ph, i: (0, 0))],
        out_shape=[jax.ShapeDtypeStruct((nt, 1, rows), jnp.int32),
                   jax.ShapeDtypeStruct((8, LANES), jnp.int32)],
        scratch_shapes=[pltpu.VMEM((1, LANES), F32)] * 3,
        compiler_params=_cparams(("arbitrary", "arbitrary")),
        name="moe_plan",
    )(rinfo)


def _dispatch_kernel(tinfo_ref, dest_ref, x_ref, xs_ref, zbuf, zsem, sem, *, nb, tt):
    i = pl.program_id(0)

    @pl.when(i == 0)
    def _():
        zbuf[...] = jnp.zeros_like(zbuf)
        n_used = tinfo_ref[1, 0]

        def fill(start):
            def body(j, carry):
                last = (j >= n_used - 1) | (tinfo_ref[0, jnp.minimum(j + 1, LANES - 1)] != tinfo_ref[0, j])

                @pl.when(last)
                def _():
                    cp = pltpu.make_async_copy(
                        zbuf, xs_ref.at[pl.ds(pl.multiple_of(j * TM_E, TM_E), TM_E), :], zsem)
                    if start:
                        cp.start()
                    else:
                        cp.wait()
                return carry
            lax.fori_loop(0, xs_ref.shape[0] // TM_E, body, 0)

        fill(True)
        fill(False)

    for b in range(nb):
        for t in range(tt):
            d = dest_ref[0, 0, b * tt + t]
            pltpu.make_async_copy(x_ref.at[b, pl.ds(t, 1), :], xs_ref.at[pl.ds(d, 1), :], sem).start()
    whole = xs_ref.at[pl.ds(0, nb * tt), :]
    pltpu.make_async_copy(whole, whole, sem).wait()


def _dispatch(tinfo, dest, xrow, *, tt, skip):
    nb, tl, _ = xrow.shape
    nt = tl // tt - skip
    n_sorted = ((nt * nb * tt) // TM_E + N_CLASSES) * TM_E
    return pl.pallas_call(
        functools.partial(_dispatch_kernel, nb=nb, tt=tt),
        grid_spec=pltpu.PrefetchScalarGridSpec(
            num_scalar_prefetch=1,
            grid=(nt,),
            in_specs=[pl.BlockSpec((1, 1, nb * tt), lambda i, ti: (i, 0, 0), memory_space=pltpu.SMEM),
                      pl.BlockSpec((nb, tt, ROW_W), lambda i, ti: (0, i + skip, 0))],
            out_specs=pl.BlockSpec(memory_space=pl.ANY),
            scratch_shapes=[pltpu.VMEM((TM_E, ROW_W), F32),
                            pltpu.SemaphoreType.DMA(()),
                            pltpu.SemaphoreType.DMA(())]),
        out_shape=jax.ShapeDtypeStruct((n_sorted, ROW_W), F32),
        compiler_params=_cparams(("arbitrary",)),
        name="moe_dispatch",
    )(tinfo, dest, xrow)


def _class_expert(c, hi):
    g = c // PAIRS_PER_GROUP
    p = c - PAIRS_PER_GROUP * g
    ge3 = (p >= 3).astype(jnp.int32)
    ge5 = (p >= 5).astype(jnp.int32)
    local = (p + 1 - 2 * ge3 - ge5) if hi else (ge3 + ge5)
    return EXPERTS_PER_GROUP * g + local


def _experts_kernel(tinfo_ref, xs_ref, wga_ref, wua_ref, wda_ref, wgb_ref, wub_ref, wdb_ref, ys_ref):
    @pl.when(pl.program_id(0) < tinfo_ref[1, 0])
    def _():
        x = xs_ref[:, :D_MODEL].astype(BF16)

        def ffn(wg_ref, wu_ref, wd_ref):
            hg = jnp.dot(x, wg_ref[0], preferred_element_type=F32)
            hu = jnp.dot(x, wu_ref[0], preferred_element_type=F32)
            hdn = (hg * jax.nn.sigmoid(hg)) * hu
            return jnp.dot(hdn.astype(BF16), wd_ref[0], preferred_element_type=F32)

        ys_ref[...] = (xs_ref[:, D_MODEL + 1:D_MODEL + 2] * ffn(wga_ref, wua_ref, wda_ref)
                       + xs_ref[:, D_MODEL + 2:D_MODEL + 3] * ffn(wgb_ref, wub_ref, wdb_ref))

    @pl.when(pl.program_id(0) >= tinfo_ref[1, 0])
    def _():
        ys_ref[...] = jnp.zeros_like(ys_ref)


def _experts(tinfo, xs, wg, wu, wd):
    n_sorted = xs.shape[0]
    nt = n_sorted // TM_E

    def tile(j, ti):
        return jnp.minimum(j, ti[1, 0] - 1)

    def wmap(hi):
        return lambda j, ti: (_class_expert(ti[0, tile(j, ti)], hi), 0, 0)

    up = lambda hi: pl.BlockSpec((1, D_MODEL, D_EXPERT), wmap(hi))
    down = lambda hi: pl.BlockSpec((1, D_EXPERT, D_MODEL), wmap(hi))
    return pl.pallas_call(
        _experts_kernel,
        grid_spec=pltpu.PrefetchScalarGridSpec(
            num_scalar_prefetch=1,
            grid=(nt,),
            in_specs=[pl.BlockSpec((TM_E, ROW_W), lambda j, ti: (tile(j, ti), 0)),
                      up(0), up(0), down(0), up(1), up(1), down(1)],
            out_specs=pl.BlockSpec((TM_E, D_MODEL), lambda j, ti: (j, 0))),
        out_shape=jax.ShapeDtypeStruct((n_sorted, D_MODEL), F32),
        compiler_params=_cparams(("arbitrary",)),
        name="moe_experts",
    )(tinfo, xs, wg, wu, wd, wg, wu, wd)


def _combine_kernel(dcur_ref, dnxt_ref, x1_ref, nf_ref, ys_ref, y_ref, buf, sem, *, nb, tt):
    i = pl.program_id(0)
    slot = i % 2

    def gather(dref, s):
        for b in range(nb):
            for t in range(tt):
                d = dref[0, 0, b * tt + t]
                pltpu.make_async_copy(ys_ref.at[pl.ds(d, 1), :], buf.at[s, b, pl.ds(t, 1), :],
                                      sem.at[s]).start()

    @pl.when(i == 0)
    def _():
        gather(dcur_ref, 0)

    @pl.when(i + 1 < pl.num_programs(0))
    def _():
        gather(dnxt_ref, 1 - slot)

    whole = ys_ref.at[pl.ds(0, nb * tt), :]
    pltpu.make_async_copy(whole, whole, sem.at[slot]).wait()
    x2 = x1_ref[...].reshape(nb * tt, D_MODEL) + buf[slot].reshape(nb * tt, D_MODEL)
    y_ref[...] = _rms(x2, nf_ref[...]).reshape(nb, tt, D_MODEL)


def _combine(dest, x1, nf, ys, *, tt, skip):
    nb, tl, _ = x1.shape
    nt = tl // tt - skip
    dspec = lambda f: pl.BlockSpec((1, 1, nb * tt), f, memory_space=pltpu.SMEM)
    return pl.pallas_call(
        functools.partial(_combine_kernel, nb=nb, tt=tt),
        grid=(nt,),
        in_specs=[dspec(lambda i: (i, 0, 0)),
                  dspec(lambda i: (jnp.minimum(i + 1, nt - 1), 0, 0)),
                  pl.BlockSpec((nb, tt, D_MODEL), lambda i: (0, i + skip, 0)),
                  pl.BlockSpec((1, D_MODEL), lambda i: (0, 0)),
                  pl.BlockSpec(memory_space=pl.ANY)],
        out_specs=pl.BlockSpec((nb, tt, D_MODEL), lambda i: (0, i, 0)),
        out_shape=jax.ShapeDtypeStruct((nb, nt * tt, D_MODEL), F32),
        scratch_shapes=[pltpu.VMEM((2, nb, tt, D_MODEL), F32),
                        pltpu.SemaphoreType.DMA((2,))],
        compiler_params=_cparams(("arbitrary",)),
        name="moe_combine",
    )(dest, dest, x1, nf, ys)


def _block_diag(w):
    per = RNN_SUPER // RNN_BLOCK
    w4 = w.reshape(N_SUPER, per, RNN_BLOCK, RNN_BLOCK).astype(BF16)
    rows = [jnp.pad(w4[:, q], ((0, 0), (0, 0), (q * RNN_BLOCK, (per - 1 - q) * RNN_BLOCK))) for q in range(per)]
    return jnp.concatenate(rows, axis=1)


def _stream(x, meta, conv0, h0, c0, n0, m0, p, expert_w, *, has_head):
    nb, seq, _ = x.shape
    n_null = N_NULL if has_head else 0
    tt = CHUNK if has_head else seq
    xn, gif = _prep(x, meta, p['norm1_w'], p['w_gif'], p['b_gif'], has_head=has_head)
    w_in = p['w_in']
    tt_mm = MM_TILES * CHUNK if has_head else seq
    xr = _mm(xn, w_in, col0=0, ncols=D_RNN, tn=512, tt=tt_mm, out_dtype=F32, w_t=True, name="in_xr")
    gr = _mm(xn, w_in, col0=D_RNN, ncols=D_RNN, tn=512, tt=tt_mm, out_dtype=BF16, w_t=True, gelu=True,
             name="in_gr")
    qkvo = _mm(xn, w_in, col0=2 * D_RNN, ncols=QKVO, tn=1024, tt=tt_mm, out_dtype=BF16, w_t=True,
               name="in_qkvo")
    mg = _mm(xn, p['w_mg'], col0=0, ncols=2 * D_MODEL, tn=1024, tt=tt_mm, out_dtype=BF16, w_t=True,
             name="in_mg")

    tail0 = jnp.transpose(conv0, (1, 0, 2)).reshape((CONV_W - 1) * nb, D_RNN)
    y_r, h_new, tail = _rglru(xr, gr, tail0, h0, p['conv_w'], p['conv_b'], p['rg_a_b'], p['rg_x_b'],
                              p['rg_lambda'], p['wa_bd'], p['wx_bd'], tt=tt, n_null=n_null)
    conv_new = jnp.transpose(tail.reshape(CONV_W - 1, nb, D_RNN), (1, 0, 2))

    y_m, c_new, n_new, m_new = _mlstm_fused(qkvo, gif, c0, n0, m0, p['mlstm_norm_w'], L=tt, n_null=n_null,
                                            nbk=MLSTM_NBK)

    mix = _merge(y_r, y_m, p['w_proj_rnn'], p['w_proj_mlstm'], mg, tn=512, tt=tt)
    routed = has_head
    x1, xn2, gates = _outproj(mix, p['w_out'], x, meta, p['norm2_w'], p['w_router'], p['b_router'],
                              tt=tt, has_head=has_head, routed=routed)
    if routed:
        skip = 1
        dest, tinfo = _plan(gates, tt=tt, skip=skip)
        xs = _dispatch(tinfo, dest, xn2, tt=tt, skip=skip)
        ys = _experts(tinfo, xs, *expert_w)
        y = _combine(dest, x1, p['norm_f_w'], ys, tt=tt, skip=skip)
        expert_w_bf16 = expert_w
    else:
        y, *expert_w_bf16 = _moe(xn2, gates, x1, *expert_w, p['norm_f_w'])
    return y, (conv_new[None], h_new[None], c_new[None], n_new[None], m_new[None]), tuple(expert_w_bf16)


def kernel(x_prompt, x_sample, state_rglru_conv, state_rglru_h, state_mlstm_C, state_mlstm_n, state_mlstm_m, meta_tokens, norm1_w, w_in, b_gates, conv_w, conv_b, rg_a_w, rg_a_b, rg_x_w, rg_x_b, rg_lambda, mlstm_norm_w, w_proj_rnn, w_proj_mlstm, w_out, norm2_w, w_router_group, b_router_group, w_router_expert, b_router_expert, w_exp_gate, w_exp_up, w_exp_down, norm_f_w):
    l = 0
    w_in_t = jnp.swapaxes(w_in[l], 0, 1)
    pad_r = LANES - N_GROUPS - N_EXPERTS
    p = dict(
        norm1_w=norm1_w[l][None], w_in=w_in_t,
        w_gif=jnp.pad(w_in_t[GIF_OFF:MG_OFF].T, ((0, 0), (0, LANES - 2 * N_HEADS_M))).astype(BF16),
        b_gif=jnp.pad(b_gates[l], (0, LANES - 2 * N_HEADS_M))[None],
        w_mg=w_in_t[MG_OFF:],
        conv_w=conv_w[l], conv_b=conv_b[l][None], rg_a_b=rg_a_b[l][None], rg_x_b=rg_x_b[l][None],
        rg_lambda=rg_lambda[l][None], wa_bd=_block_diag(rg_a_w[l]), wx_bd=_block_diag(rg_x_w[l]),
        mlstm_norm_w=mlstm_norm_w[l][None],
        w_proj_rnn=w_proj_rnn[l], w_proj_mlstm=w_proj_mlstm[l], w_out=w_out[l].astype(BF16),
        norm2_w=norm2_w[l][None],
        w_router=jnp.pad(jnp.concatenate([w_router_group[l], w_router_expert[l]], axis=1),
                         ((0, 0), (0, pad_r))).astype(BF16),
        b_router=jnp.pad(jnp.concatenate([b_router_group[l], b_router_expert[l]]), (0, pad_r))[None],
        norm_f_w=norm_f_w[None],
    )
    nbp = x_prompt.shape[0]
    dt = x_prompt.dtype
    y_s, st_s, expert_w_bf16 = _stream(
        x_sample, meta_tokens, state_rglru_conv[l], state_rglru_h[l], state_mlstm_C[l],
        state_mlstm_n[l], state_mlstm_m[l], p, (w_exp_gate[l], w_exp_up[l], w_exp_down[l]), has_head=False)
    y_p, st_p, _ = _stream(
        x_prompt, meta_tokens,
        jnp.zeros((nbp, CONV_W - 1, D_RNN), dt), jnp.zeros((nbp, D_RNN), F32),
        jnp.zeros((nbp, N_HEADS_M, DQK_M, DV_M), F32), jnp.zeros((nbp, N_HEADS_M, DQK_M), F32),
        jnp.zeros((nbp, N_HEADS_M), F32), p, expert_w_bf16, has_head=True)
    return (y_p, y_s) + st_p + st_s
```

```python
import functools

import jax
import jax.numpy as jnp
from jax import lax
from jax.experimental import pallas as pl
from jax.experimental.pallas import tpu as pltpu

F32 = jnp.float32
BF16 = jnp.bfloat16

D_MODEL = 2048
CHUNK = 64
N_META = 16
N_NULL = CHUNK - N_META
D_RNN = 2560
N_RNN_BLOCKS = 16
RNN_BLOCK = D_RNN // N_RNN_BLOCKS
RNN_SUPER = 640
N_SUPER = D_RNN // RNN_SUPER
CONV_W = 4
LRU_C = 8.0
N_HEADS_M = 8
DV_M = D_MODEL // N_HEADS_M
DQK_M = DV_M // 2
HQK = N_HEADS_M * DQK_M
HV = N_HEADS_M * DV_M
QKVO = 2 * HQK + 2 * HV
N_GROUPS = 4
EXPERTS_PER_GROUP = 4
N_EXPERTS = 16
D_EXPERT = 512
PAIRS_PER_GROUP = 6
N_CLASSES = N_GROUPS * PAIRS_PER_GROUP
MM_TILES = 3
MLSTM_NBK = 2
TM_E = 256
ROW_W = D_MODEL + 128
EPS = 1e-6
LANES = 128
GIF_OFF = 2 * D_RNN + QKVO
MG_OFF = GIF_OFF + 2 * N_HEADS_M
VMEM_LIMIT = 56 * 1024 * 1024
NEG_BIG = -1e30


def _cparams(sem):
    return pltpu.CompilerParams(dimension_semantics=sem, vmem_limit_bytes=VMEM_LIMIT)


def _rms(x, w):
    return x * lax.rsqrt(jnp.mean(x * x, axis=-1, keepdims=True) + EPS) * w


def _softplus(x):
    return jnp.maximum(x, 0.0) + jnp.log1p(jnp.exp(-jnp.abs(x)))


def _head_rows(meta_ref, nb):
    head = jnp.concatenate([jnp.zeros((N_NULL, D_MODEL), F32), meta_ref[...]], axis=0)
    return jnp.broadcast_to(head[None], (nb, CHUNK, D_MODEL)).reshape(nb * CHUNK, D_MODEL)


def _x_map(has_head):
    if has_head:
        return lambda ti: (0, jnp.maximum(ti - 1, 0), 0)
    return lambda ti: (0, ti, 0)


def _prep_kernel(x_ref, meta_ref, nw_ref, wg_ref, bg_ref, xn_ref, gif_ref, *, nb, tt, has_head):
    def emit(x):
        xnb = _rms(x, nw_ref[...]).astype(BF16)
        xn_ref[...] = xnb.reshape(nb, tt, D_MODEL)
        gif = jnp.dot(xnb, wg_ref[...], preferred_element_type=F32) + bg_ref[...]
        gif_ref[...] = gif.reshape(nb, tt, LANES)

    if has_head:
        @pl.when(pl.program_id(0) == 0)
        def _():
            emit(_head_rows(meta_ref, nb))

        @pl.when(pl.program_id(0) > 0)
        def _():
            emit(x_ref[...].reshape(nb * tt, D_MODEL))
    else:
        emit(x_ref[...].reshape(nb * tt, D_MODEL))


def _prep(x, meta, norm_w, w_gif, b_gif, *, has_head):
    nb, seq, _ = x.shape
    tt = CHUNK if has_head else seq
    n_tiles = seq // tt + (1 if has_head else 0)
    tl = n_tiles * tt
    const = lambda ti: (0, 0)
    return pl.pallas_call(
        functools.partial(_prep_kernel, nb=nb, tt=tt, has_head=has_head),
        grid=(n_tiles,),
        in_specs=[pl.BlockSpec((nb, tt, D_MODEL), _x_map(has_head)),
                  pl.BlockSpec((N_META, D_MODEL), const),
                  pl.BlockSpec((1, D_MODEL), const),
                  pl.BlockSpec((D_MODEL, LANES), const),
                  pl.BlockSpec((1, LANES), const)],
        out_specs=[pl.BlockSpec((nb, tt, D_MODEL), lambda ti: (0, ti, 0)),
                   pl.BlockSpec((nb, tt, LANES), lambda ti: (0, ti, 0))],
        out_shape=[jax.ShapeDtypeStruct((nb, tl, D_MODEL), BF16),
                   jax.ShapeDtypeStruct((nb, tl, LANES), F32)],
        compiler_params=_cparams(("arbitrary",)),
        name="prep",
    )(x, meta, norm_w, w_gif, b_gif)


def _gelu_tanh(g):
    return g * (0.5 * (1.0 + jnp.tanh(0.7978845608028654 * (g + 0.044715 * (g * g * g)))))


def _mm_kernel(lhs_ref, w_ref, out_ref, *wbf_ref, gelu):
    nb, tt, k = lhs_ref.shape
    tn = out_ref.shape[-1]
    if wbf_ref:
        w_bf = wbf_ref[0]

        @pl.when(pl.program_id(1) == 0)
        def _():
            w_bf[...] = w_ref[...].T.astype(BF16)
    else:
        w_bf = w_ref

    acc = jnp.dot(lhs_ref[...].reshape(nb * tt, k), w_bf[...], preferred_element_type=F32)
    if gelu:
        acc = _gelu_tanh(acc)
    out_ref[...] = acc.astype(out_ref.dtype).reshape(nb, tt, tn)


def _mm(lhs, w, *, col0, ncols, tn, tt, out_dtype, gelu=False, ready=False, name):
    nb, tl, k = lhs.shape
    assert tl % tt == 0 and ncols % tn == 0 and col0 % tn == 0
    lhs_spec = pl.BlockSpec((nb, tt, k), lambda j, i: (0, i, 0))
    out_spec = pl.BlockSpec((nb, tt, tn), lambda j, i: (0, i, j))
    out_shape = jax.ShapeDtypeStruct((nb, tl, ncols), out_dtype)
    wbf_spec = pl.BlockSpec((k, tn), lambda j, i: (0, j))
    if ready:
        in_specs, out_specs, out_shapes = [lhs_spec, wbf_spec], out_spec, out_shape
    else:
        in_specs = [lhs_spec, pl.BlockSpec((tn, k), lambda j, i: (col0 // tn + j, 0))]
        out_specs = [out_spec, wbf_spec]
        out_shapes = [out_shape, jax.ShapeDtypeStruct((k, ncols), BF16)]
    return pl.pallas_call(
        functools.partial(_mm_kernel, gelu=gelu),
        grid=(ncols // tn, tl // tt),
        in_specs=in_specs,
        out_specs=out_specs,
        out_shape=out_shapes,
        compiler_params=_cparams(("arbitrary", "arbitrary")),
        name=name,
    )(lhs, w)


def _merge_kernel(yr_ref, ym_ref, wr_ref, wm_ref, gr_ref, gm_ref, out_ref, *wbf_refs):
    nb, tt, tn = out_ref.shape
    rows = nb * tt
    if wbf_refs:
        wr_bf, wm_bf = wbf_refs

        @pl.when(pl.program_id(1) == 0)
        def _():
            wr_bf[...] = wr_ref[...].astype(BF16)
            wm_bf[...] = wm_ref[...].astype(BF16)
    else:
        wr_bf, wm_bf = wr_ref, wm_ref

    pr = jnp.dot(yr_ref[...].reshape(rows, D_RNN), wr_bf[...], preferred_element_type=F32)
    pm = jnp.dot(ym_ref[...].reshape(rows, HV), wm_bf[...], preferred_element_type=F32)
    mix = (jax.nn.sigmoid(gr_ref[...].reshape(rows, tn).astype(F32)) * pr
           + jax.nn.sigmoid(gm_ref[...].reshape(rows, tn).astype(F32)) * pm)
    out_ref[...] = mix.astype(BF16).reshape(nb, tt, tn)


def _merge(y_r, y_m, w_r, w_m, mg, *, tn, tt, ready=False):
    nb, tl, _ = y_r.shape
    nj = D_MODEL // tn
    wr_spec = pl.BlockSpec((D_RNN, tn), lambda j, i: (0, j))
    wm_spec = pl.BlockSpec((HV, tn), lambda j, i: (0, j))
    out_specs = pl.BlockSpec((nb, tt, tn), lambda j, i: (0, i, j))
    out_shape = jax.ShapeDtypeStruct((nb, tl, D_MODEL), BF16)
    if not ready:
        out_specs = [out_specs, wr_spec, wm_spec]
        out_shape = [out_shape, jax.ShapeDtypeStruct(w_r.shape, BF16), jax.ShapeDtypeStruct(w_m.shape, BF16)]
    return pl.pallas_call(
        _merge_kernel,
        grid=(nj, tl // tt),
        in_specs=[pl.BlockSpec((nb, tt, D_RNN), lambda j, i: (0, i, 0)),
                  pl.BlockSpec((nb, tt, HV), lambda j, i: (0, i, 0)),
                  wr_spec, wm_spec,
                  pl.BlockSpec((nb, tt, tn), lambda j, i: (0, i, j)),
                  pl.BlockSpec((nb, tt, tn), lambda j, i: (0, i, nj + j))],
        out_specs=out_specs,
        out_shape=out_shape,
        compiler_params=_cparams(("arbitrary", "arbitrary")),
        name="merge",
    )(y_r, y_m, w_r, w_m, mg, mg)


def _rglru_kernel(xr_ref, gr_ref, tail0_ref, h0_ref, cw_ref, cb_ref, ab_ref, xb_ref, lam_ref,
                  wa_ref, wx_ref, y_ref, hl_ref, tail_ref, xp_s, h_s, a_s, u_s, hs_s, *, nb, tt, n_null):
    ti = pl.program_id(1)
    rt = tt * nb
    tb = (CONV_W - 1) * nb

    def time_major(x):
        return jnp.swapaxes(x, 0, 1).reshape(rt, x.shape[-1])

    @pl.when(ti == 0)
    def _():
        xp_s[0:tb, :] = tail0_ref[...]
        h_s[...] = h0_ref[...]

    xp_s[tb:tb + rt, :] = time_major(xr_ref[...])
    cw = cw_ref[...]
    xc = cb_ref[...] + cw[0:1, :] * xp_s[0:rt, :]
    for j in range(1, CONV_W):
        xc = xc + cw[j:j + 1, :] * xp_s[j * nb:j * nb + rt, :]
    xcb = xc.astype(BF16)
    r = jax.nn.sigmoid(jnp.dot(xcb, wa_ref[0], preferred_element_type=F32) + ab_ref[...])
    i = jax.nn.sigmoid(jnp.dot(xcb, wx_ref[0], preferred_element_type=F32) + xb_ref[...])
    log_a = r * ((-LRU_C) * _softplus(-lam_ref[...]))
    a = jnp.exp(log_a)
    a_s[...] = a
    u = jnp.sqrt(-jnp.tanh(log_a) * (a * a + 1.0)) * (i * xc)
    u_s[...] = u
    if n_null:
        @pl.when(ti == 0)
        def _():
            a_s[n_null * nb:(n_null + 1) * nb, :] = jnp.zeros((nb, a_s.shape[-1]), F32)

    def step(t, h):
        o = pl.multiple_of(t * nb, nb)
        h = a_s[pl.ds(o, nb), :] * h + u_s[pl.ds(o, nb), :]
        hs_s[pl.ds(o, nb), :] = h
        return h

    h = lax.fori_loop(0, tt, step, h_s[...], unroll=8)
    h_s[...] = h
    h_bm = jnp.swapaxes(hs_s[...].reshape(tt, nb, hs_s.shape[-1]), 0, 1)
    y_ref[...] = (gr_ref[...].astype(F32) * h_bm).astype(BF16)
    xp_s[0:tb, :] = xp_s[rt:rt + tb, :]

    @pl.when(ti == pl.num_programs(1) - 1)
    def _():
        hl_ref[...] = h
        tail_ref[...] = xp_s[rt:rt + tb, :]


def _rglru(xr, gr, tail0, h0, cw, cb, ab, xb, lam, wa, wx, *, tt, n_null):
    nb, tl, _ = xr.shape
    rt = tt * nb
    tb = (CONV_W - 1) * nb
    blk = lambda s, ti: (0, ti, s)
    col = lambda s, ti: (0, s)
    return pl.pallas_call(
        functools.partial(_rglru_kernel, nb=nb, tt=tt, n_null=n_null),
        grid=(N_SUPER, tl // tt),
        in_specs=[pl.BlockSpec((nb, tt, RNN_SUPER), blk),
                  pl.BlockSpec((nb, tt, RNN_SUPER), blk),
                  pl.BlockSpec((tb, RNN_SUPER), col),
                  pl.BlockSpec((nb, RNN_SUPER), col),
                  pl.BlockSpec((CONV_W, RNN_SUPER), col),
                  pl.BlockSpec((1, RNN_SUPER), col),
                  pl.BlockSpec((1, RNN_SUPER), col),
                  pl.BlockSpec((1, RNN_SUPER), col),
                  pl.BlockSpec((1, RNN_SUPER), col),
                  pl.BlockSpec((1, RNN_SUPER, RNN_SUPER), lambda s, ti: (s, 0, 0)),
                  pl.BlockSpec((1, RNN_SUPER, RNN_SUPER), lambda s, ti: (s, 0, 0))],
        out_specs=[pl.BlockSpec((nb, tt, RNN_SUPER), blk),
                   pl.BlockSpec((nb, RNN_SUPER), col),
                   pl.BlockSpec((tb, RNN_SUPER), col)],
        out_shape=[jax.ShapeDtypeStruct((nb, tl, D_RNN), BF16),
                   jax.ShapeDtypeStruct((nb, D_RNN), F32),
                   jax.ShapeDtypeStruct((tb, D_RNN), F32)],
        scratch_shapes=[pltpu.VMEM((rt + tb, RNN_SUPER), F32),
                        pltpu.VMEM((nb, RNN_SUPER), F32),
                        pltpu.VMEM((rt, RNN_SUPER), F32),
                        pltpu.VMEM((rt, RNN_SUPER), F32),
                        pltpu.VMEM((rt, RNN_SUPER), F32)],
        compiler_params=_cparams(("arbitrary", "arbitrary")),
        name="rglru",
    )(xr, gr, tail0, h0, cw, cb, ab, xb, lam, wa, wx)


DVA = DV_M + 128


def _mlstm_gates(c, gif_ref, m_s, *, L, n_null):
    hm = N_HEADS_M
    g = gif_ref[...]
    if L < LANES:
        g = jnp.concatenate([g, jnp.zeros((LANES - L, LANES), F32)], axis=0)
    g_t = g.T
    lane = lax.broadcasted_iota(jnp.int32, (hm, LANES), 1)
    valid = lane < L
    li = g_t[0:hm, :]
    lfr = g_t[hm:2 * hm, :]
    lf = jnp.where(valid, jnp.minimum(lfr, 0.0) - jnp.log1p(jnp.exp(-jnp.abs(lfr))), 0.0)
    if n_null:
        null = (lane < n_null) & (c == 0)
        lf = jnp.where(null, 0.0, lf)
        li = jnp.where(null, NEG_BIG, li)
    r128 = lax.broadcasted_iota(jnp.int32, (LANES, LANES), 0)
    c128 = lax.broadcasted_iota(jnp.int32, (LANES, LANES), 1)
    upper = jnp.where(r128 <= c128, 1.0, 0.0)
    b = jnp.dot(lf, upper, preferred_element_type=F32, precision=lax.Precision.HIGHEST)
    a = li - b
    cm = jnp.where(valid, a, -jnp.inf)
    sh = 1
    while sh < L:
        cm = jnp.maximum(cm, jnp.where(lane >= sh, pltpu.roll(cm, sh, axis=1), -jnp.inf))
        sh *= 2
    m0 = m_s[...]
    big_m = jnp.maximum(cm, m0)
    m_last = jnp.max(jnp.where(valid, big_m, -jnp.inf), axis=1, keepdims=True)
    b_last = jnp.sum(lf, axis=1, keepdims=True)
    scale = DQK_M ** -0.5
    e_mv = jnp.exp(-(b + big_m))
    w_int = jnp.exp(m0 - big_m)
    w_end = jnp.exp(a - m_last) * scale
    decay = jnp.exp(m0 - m_last)
    m_s[...] = jnp.broadcast_to(b_last + m_last, (hm, LANES))
    cols = jnp.concatenate([big_m, e_mv, w_int, w_end, jnp.zeros((LANES - 4 * hm, LANES), F32)], axis=0).T
    return a, cols, decay


def _mlstm_heads(gates, q_ref, k_ref, v_ref, og_ref, nw_ref, y_ref, c_s, *, L):
    hm = N_HEADS_M
    scale = DQK_M ** -0.5
    rr = lax.broadcasted_iota(jnp.int32, (L, L), 0)
    cc = lax.broadcasted_iota(jnp.int32, (L, L), 1)
    causal = rr >= cc
    ones_col = jnp.where(lax.broadcasted_iota(jnp.int32, (L, 128), 1) == 0, 1.0, 0.0).astype(BF16)
    nw = nw_ref[...]
    dn_nt = (((1,), (1,)), ((), ()))
    dn_tn = (((0,), (0,)), ((), ()))
    bh = [(bi, h) for bi in range(len(gates)) for h in range(hm)]
    col = lambda bi, k, h: gates[bi][1][0:L, k * hm + h:k * hm + h + 1]
    qs = [q_ref[bi, :, h * DQK_M:(h + 1) * DQK_M] for bi, h in bh]
    ks = [k_ref[bi, :, h * DQK_M:(h + 1) * DQK_M] for bi, h in bh]
    v_augs = [jnp.concatenate([v_ref[bi, :, h * DV_M:(h + 1) * DV_M], ones_col], axis=1) for bi, h in bh]
    c_augs = [c_s[bi, h] for bi, h in bh]
    s_raw = [lax.dot_general(q, k, dn_nt, preferred_element_type=F32) for q, k in zip(qs, ks)]
    inter = [jnp.dot((qs[i].astype(F32) * col(bi, 2, h)).astype(BF16), c_augs[i].astype(BF16),
                     preferred_element_type=F32) for i, (bi, h) in enumerate(bh)]
    upd = [lax.dot_general((ks[i].astype(F32) * col(bi, 3, h)).astype(BF16), v_augs[i], dn_tn,
                           preferred_element_type=F32) for i, (bi, h) in enumerate(bh)]
    s_w = [(s_raw[i] * (scale * jnp.where(causal, jnp.exp(gates[bi][0][h:h + 1, 0:L] - col(bi, 0, h)), 0.0))
            ).astype(BF16) for i, (bi, h) in enumerate(bh)]
    res = [jnp.dot(s_w[i], v_augs[i], preferred_element_type=F32) + inter[i] for i in range(len(bh))]
    for i, (bi, h) in enumerate(bh):
        c_s[bi, h] = gates[bi][2][h:h + 1, 0:1] * c_augs[i] + upd[i]
    for i, (bi, h) in enumerate(bh):
        num = res[i][:, :DV_M]
        denom = jnp.maximum(jnp.abs(res[i][:, DV_M:DV_M + 1]), col(bi, 1, h))
        hh = num / denom
        hh = hh * lax.rsqrt(jnp.mean(hh * hh, axis=-1, keepdims=True) + EPS)
        hh = hh * nw[:, h * DV_M:(h + 1) * DV_M]
        og = og_ref[bi, :, h * DV_M:(h + 1) * DV_M].astype(F32)
        y_ref[bi, :, h * DV_M:(h + 1) * DV_M] = (hh * jax.nn.sigmoid(og)).astype(BF16)


def _mlstm_fused_kernel(q_ref, k_ref, v_ref, og_ref, gif_ref, c0_ref, n0_ref, m0_ref, nw_ref,
                        y_ref, co_ref, no_ref, mo_ref, c_s, m_s, *, L, n_null, nbk):
    c = pl.program_id(1)
    bh = [(bi, h) for bi in range(nbk) for h in range(N_HEADS_M)]

    @pl.when(c == 0)
    def _():
        m_s[...] = m0_ref[...]
        for bi, h in bh:
            c_s[bi, h, :, :DV_M] = c0_ref[bi, h]
            n_tile = jnp.concatenate([n0_ref[bi, h:h + 1, :], jnp.zeros((DQK_M - 1, DQK_M), F32)], axis=0)
            c_s[bi, h, :, DV_M:] = n_tile.T

    gates = [_mlstm_gates(c, gif_ref.at[bi], m_s.at[bi], L=L, n_null=n_null) for bi in range(nbk)]
    _mlstm_heads(gates, q_ref, k_ref, v_ref, og_ref, nw_ref, y_ref, c_s, L=L)

    @pl.when(c == pl.num_programs(1) - 1)
    def _():
        mo_ref[...] = m_s[...]
        for bi, h in bh:
            co_ref[bi, h] = c_s[bi, h, :, :DV_M]
            no_ref[bi, h:h + 1, :] = c_s[bi, h, :, DV_M:].T[0:1, :]


def _mlstm_fused(qkvo, gif, c0, n0, m0, nw, *, L, n_null, nbk):
    nb, tl, _ = qkvo.shape
    assert nb % nbk == 0
    hm = N_HEADS_M
    assert DQK_M == LANES and DVA - DV_M == DQK_M
    m_rep0 = jnp.broadcast_to(m0[..., None], (nb, hm, LANES))
    st4 = lambda b, c: (b, 0, 0, 0)
    st3 = lambda b, c: (b, 0, 0)
    y, c_new, n_new, m_rep = pl.pallas_call(
        functools.partial(_mlstm_fused_kernel, L=L, n_null=n_null, nbk=nbk),
        grid=(nb // nbk, tl // L),
        in_specs=[pl.BlockSpec((nbk, L, HQK), lambda b, c: (b, c, 0)),
                  pl.BlockSpec((nbk, L, HQK), lambda b, c: (b, c, 1)),
                  pl.BlockSpec((nbk, L, HV), lambda b, c: (b, c, 1)),
                  pl.BlockSpec((nbk, L, HV), lambda b, c: (b, c, 2)),
                  pl.BlockSpec((nbk, L, LANES), lambda b, c: (b, c, 0)),
                  pl.BlockSpec((nbk, hm, DQK_M, DV_M), st4),
                  pl.BlockSpec((nbk, hm, DQK_M), st3),
                  pl.BlockSpec((nbk, hm, LANES), st3),
                  pl.BlockSpec((1, HV), lambda b, c: (0, 0))],
        out_specs=[pl.BlockSpec((nbk, L, HV), lambda b, c: (b, c, 0)),
                   pl.BlockSpec((nbk, hm, DQK_M, DV_M), st4),
                   pl.BlockSpec((nbk, hm, DQK_M), st3),
                   pl.BlockSpec((nbk, hm, LANES), st3)],
        out_shape=[jax.ShapeDtypeStruct((nb, tl, HV), BF16),
                   jax.ShapeDtypeStruct((nb, hm, DQK_M, DV_M), F32),
                   jax.ShapeDtypeStruct((nb, hm, DQK_M), F32),
                   jax.ShapeDtypeStruct((nb, hm, LANES), F32)],
        scratch_shapes=[pltpu.VMEM((nbk, hm, DQK_M, DVA), F32),
                        pltpu.VMEM((nbk, hm, LANES), F32)],
        compiler_params=_cparams(("arbitrary", "arbitrary")),
        name="mlstm",
    )(qkvo, qkvo, qkvo, qkvo, gif, c0, n0, m_rep0, nw)
    return y, c_new, n_new, m_rep[..., 0]


def _route(logits, dense):
    tm = logits.shape[0]
    lane = lax.broadcasted_iota(jnp.int32, (tm, LANES), 1)
    lanef = lane.astype(F32)
    gl = jnp.where(lane < N_GROUPS, logits, -jnp.inf)
    gmax = jnp.max(gl, axis=1, keepdims=True)
    gidx = jnp.min(jnp.where(gl == gmax, lanef, float(LANES)), axis=1, keepdims=True)
    gw = 1.0 / jnp.sum(jnp.exp(gl - gmax), axis=1, keepdims=True)
    lo = float(N_GROUPS) + float(EXPERTS_PER_GROUP) * gidx
    sel = (lanef >= lo) & (lanef < lo + float(EXPERTS_PER_GROUP))
    ev = jnp.where(sel, logits, -jnp.inf)
    v1 = jnp.max(ev, axis=1, keepdims=True)
    i1 = jnp.min(jnp.where(ev == v1, lanef, float(LANES)), axis=1, keepdims=True)
    ev2 = jnp.where(lanef == i1, -jnp.inf, ev)
    v2 = jnp.max(ev2, axis=1, keepdims=True)
    i2 = jnp.min(jnp.where(ev2 == v2, lanef, float(LANES)), axis=1, keepdims=True)
    t = jnp.exp(v2 - v1)
    w1 = gw / (1.0 + t)
    w2 = gw * (t / (1.0 + t))
    if dense:
        return jnp.where(lanef == i1, w1, jnp.where(lanef == i2, w2, 0.0))
    l1 = i1 - lo
    l2 = i2 - lo
    first = l1 < l2
    a = jnp.minimum(l1, l2)
    b = jnp.maximum(l1, l2)
    pair = jnp.where(a == 0.0, b - 1.0, jnp.where(a == 1.0, b + 1.0, 5.0))
    cls = float(PAIRS_PER_GROUP) * gidx + pair
    return jnp.where(lane == 0, cls,
                     jnp.where(lane == 1, jnp.where(first, w1, w2),
                               jnp.where(lane == 2, jnp.where(first, w2, w1), 0.0)))


def _outproj_kernel(mix_ref, w_ref, x_ref, meta_ref, n2_ref, wr_ref, br_ref, x1_ref, xn2_ref, gates_ref,
                    *, nb, tt, has_head, routed):
    def emit(x):
        mix = mix_ref[...].reshape(nb * tt, D_MODEL)
        x1 = x + jnp.dot(mix, w_ref[...], preferred_element_type=F32)
        x1_ref[...] = x1.reshape(nb, tt, D_MODEL)
        xn2 = _rms(x1, n2_ref[...]).astype(BF16)
        logits = jnp.dot(xn2, wr_ref[...], preferred_element_type=F32) + br_ref[...]
        info = _route(logits, dense=not routed)
        gates_ref[...] = info.reshape(nb, tt, LANES)
        if routed:
            xn2_ref[:, :, :D_MODEL] = xn2.astype(F32).reshape(nb, tt, D_MODEL)
            xn2_ref[:, :, D_MODEL:] = info.reshape(nb, tt, LANES)
        else:
            xn2_ref[...] = xn2.reshape(nb, tt, D_MODEL)

    if has_head:
        @pl.when(pl.program_id(0) == 0)
        def _():
            emit(_head_rows(meta_ref, nb))

        @pl.when(pl.program_id(0) > 0)
        def _():
            emit(x_ref[...].reshape(nb * tt, D_MODEL))
    else:
        emit(x_ref[...].reshape(nb * tt, D_MODEL))


def _outproj(mix, w_out, x, meta, n2, w_router, b_router, *, tt, has_head, routed):
    nb, tl, _ = mix.shape
    const = lambda i: (0, 0)
    blk = lambda i: (0, i, 0)
    xn2_w, xn2_dt = (ROW_W, F32) if routed else (D_MODEL, BF16)
    return pl.pallas_call(
        functools.partial(_outproj_kernel, nb=nb, tt=tt, has_head=has_head, routed=routed),
        grid=(tl // tt,),
        in_specs=[pl.BlockSpec((nb, tt, D_MODEL), blk),
                  pl.BlockSpec((D_MODEL, D_MODEL), const, pipeline_mode=pl.Buffered(1)),
                  pl.BlockSpec((nb, tt, D_MODEL), _x_map(has_head)),
                  pl.BlockSpec((N_META, D_MODEL), const),
                  pl.BlockSpec((1, D_MODEL), const),
                  pl.BlockSpec((D_MODEL, LANES), const),
                  pl.BlockSpec((1, LANES), const)],
        out_specs=[pl.BlockSpec((nb, tt, D_MODEL), blk),
                   pl.BlockSpec((nb, tt, xn2_w), blk),
                   pl.BlockSpec((nb, tt, LANES), blk)],
        out_shape=[jax.ShapeDtypeStruct((nb, tl, D_MODEL), F32),
                   jax.ShapeDtypeStruct((nb, tl, xn2_w), xn2_dt),
                   jax.ShapeDtypeStruct((nb, tl, LANES), F32)],
        compiler_params=_cparams(("arbitrary",)),
        name="outproj",
    )(mix, w_out, x, meta, n2, w_router, b_router)


def _moe_kernel(xn_ref, gates_ref, x1_ref, wg_ref, wu_ref, wd_ref, nf_ref, y_ref, wgb_ref, wub_ref, wdb_ref,
                acc_s, *, nb, tt):
    e = pl.program_id(0)
    hf = pl.program_id(1)
    rows = nb * tt

    @pl.when((e == 0) & (hf == 0))
    def _():
        acc_s[...] = jnp.zeros_like(acc_s)

    wg = wg_ref[0].astype(BF16)
    wu = wu_ref[0].astype(BF16)
    wd = wd_ref[0].astype(BF16)
    wgb_ref[0] = wg
    wub_ref[0] = wu
    wdb_ref[0] = wd
    x = xn_ref[...].reshape(rows, D_MODEL)
    hg = jnp.dot(x, wg, preferred_element_type=F32)
    hu = jnp.dot(x, wu, preferred_element_type=F32)
    hdn = (hg * jax.nn.sigmoid(hg)) * hu
    yd = jnp.dot(hdn.astype(BF16), wd, preferred_element_type=F32)
    gates = gates_ref[...].reshape(rows, LANES)
    lane = lax.broadcasted_iota(jnp.int32, gates.shape, 1)
    gcol = jnp.sum(jnp.where(lane == e + N_GROUPS, gates, 0.0), axis=1, keepdims=True)
    acc_s[...] += gcol * yd

    @pl.when((e == pl.num_programs(0) - 1) & (hf == pl.num_programs(1) - 1))
    def _():
        x2 = x1_ref[...].reshape(rows, D_MODEL) + acc_s[...]
        y_ref[...] = _rms(x2, nf_ref[...]).reshape(nb, tt, D_MODEL)


def _moe(xn2, gates, x1, wg, wu, wd, nf):
    nb, tt, _ = xn2.shape
    halves = 2
    dh = D_EXPERT // halves
    blk = lambda e, hf: (0, 0, 0)
    up = pl.BlockSpec((1, D_MODEL, dh), lambda e, hf: (e, 0, hf))
    down = pl.BlockSpec((1, dh, D_MODEL), lambda e, hf: (e, hf, 0))
    return pl.pallas_call(
        functools.partial(_moe_kernel, nb=nb, tt=tt),
        grid=(N_EXPERTS, halves),
        in_specs=[pl.BlockSpec((nb, tt, D_MODEL), blk),
                  pl.BlockSpec((nb, tt, LANES), blk),
                  pl.BlockSpec((nb, tt, D_MODEL), blk),
                  up, up, down,
                  pl.BlockSpec((1, D_MODEL), lambda e, hf: (0, 0))],
        out_specs=[pl.BlockSpec((nb, tt, D_MODEL), blk), up, up, down],
        out_shape=[jax.ShapeDtypeStruct((nb, tt, D_MODEL), F32),
                   jax.ShapeDtypeStruct(wg.shape, BF16),
                   jax.ShapeDtypeStruct(wu.shape, BF16),
                   jax.ShapeDtypeStruct(wd.shape, BF16)],
        scratch_shapes=[pltpu.VMEM((nb * tt, D_MODEL), F32)],
        compiler_params=_cparams(("arbitrary", "arbitrary")),
        name="moe",
    )(xn2, gates, x1, wg, wu, wd, nf)


_NT = (((1,), (1,)), ((), ()))


def _plan_kernel(ri_ref, dest_ref, tinfo_ref, cnt_s, carry_s, offs_s, *, rows):
    ph = pl.program_id(0)
    i = pl.program_id(1)
    ri = ri_ref[...].reshape(rows, LANES)
    lanef = lax.broadcasted_iota(jnp.int32, (rows, LANES), 1).astype(F32)
    oh = jnp.where(lanef == ri[:, 0:1], 1.0, 0.0)
    colsum = jnp.sum(oh, axis=0, keepdims=True)
    r128 = lax.broadcasted_iota(jnp.int32, (LANES, LANES), 0)
    c128 = lax.broadcasted_iota(jnp.int32, (LANES, LANES), 1)
    eye = jnp.where(r128 == c128, 1.0, 0.0).astype(BF16)

    @pl.when((ph == 0) & (i == 0))
    def _():
        cnt_s[...] = jnp.zeros_like(cnt_s)

    @pl.when(ph == 0)
    def _():
        cnt_s[...] += colsum

    @pl.when((ph == 1) & (i == 0))
    def _():
        tiles = jnp.floor((cnt_s[...] + float(TM_E - 1)) * (1.0 / TM_E))
        upper = jnp.where(r128 <= c128, 1.0, 0.0).astype(BF16)
        cum_incl = jnp.dot(jnp.broadcast_to(tiles, (8, LANES)).astype(BF16), upper, preferred_element_type=F32)
        offs_s[...] = (cum_incl[0:1, :] - tiles) * float(TM_E)
        carry_s[...] = jnp.zeros_like(carry_s)
        cum_col = lax.dot_general(eye, cum_incl.astype(BF16), _NT, preferred_element_type=F32)
        ended = (cum_col[:, 0:1] <= c128.astype(F32)) & (r128 < N_CLASSES)
        tcls = jnp.sum(jnp.where(ended, 1.0, 0.0), axis=0, keepdims=True)
        total = cum_incl[0:1, LANES - 1:LANES]
        row8 = lax.broadcasted_iota(jnp.int32, (8, LANES), 0)
        tinfo_ref[...] = jnp.where(row8 == 0, tcls, total).astype(jnp.int32)

    @pl.when(ph == 1)
    def _():
        tri = jnp.where(lax.broadcasted_iota(jnp.int32, (rows, rows), 0)
                        > lax.broadcasted_iota(jnp.int32, (rows, rows), 1), 1.0, 0.0).astype(BF16)
        rank = jnp.dot(tri, oh.astype(BF16), preferred_element_type=F32)
        dest = jnp.sum(oh * (rank + (offs_s[...] + carry_s[...])), axis=1, keepdims=True)
        carry_s[...] += colsum
        dhi = jnp.floor(dest * (1.0 / 256.0))
        dlo = dest - 256.0 * dhi
        digits = jnp.where(lanef == 0.0, dlo, jnp.where(lanef == 1.0, dhi, 0.0)).astype(BF16)
        dt = lax.dot_general(eye, digits, _NT, preferred_element_type=F32)
        dest_ref[...] = (dt[0:1, :] + 256.0 * dt[1:2, :]).astype(jnp.int32).reshape(1, 1, rows)


def _plan(rinfo, *, tt, skip):
    nb, tl, _ = rinfo.shape
    rows = nb * tt
    nt = tl // tt - skip
    assert (nt * rows) // TM_E + N_CLASSES <= LANES and rows % LANES == 0
    return pl.pallas_call(
        functools.partial(_plan_kernel, rows=rows),
        grid=(2, nt),
        in_specs=[pl.BlockSpec((nb, tt, LANES), lambda ph, i: (0, i + skip, 0))],
        out_specs=[pl.BlockSpec((1, 1, rows), lambda ph, i: (i * ph, 0, 0)),
                   pl.BlockSpec((8, LANES), lambda ph, i: (0, 0))],
        out_shape=[jax.ShapeDtypeStruct((nt, 1, rows), jnp.int32),
                   jax.ShapeDtypeStruct((8, LANES), jnp.int32)],
        scratch_shapes=[pltpu.VMEM((1, LANES), F32)] * 3,
        compiler_params=_cparams(("arbitrary", "arbitrary")),
        name="moe_plan",
    )(rinfo)


def _dispatch_kernel(tinfo_ref, dest_ref, x_ref, xs_ref, zbuf, zsem, sem, *, nb, tt):
    i = pl.program_id(0)

    @pl.when(i == 0)
    def _():
        zbuf[...] = jnp.zeros_like(zbuf)
        n_used = tinfo_ref[1, 0]

        def fill(start):
            def body(j, carry):
                last = (j >= n_used - 1) | (tinfo_ref[0, jnp.minimum(j + 1, LANES - 1)] != tinfo_ref[0, j])

                @pl.when(last)
                def _():
                    cp = pltpu.make_async_copy(
                        zbuf, xs_ref.at[pl.ds(pl.multiple_of(j * TM_E, TM_E), TM_E), :], zsem)
                    if start:
                        cp.start()
                    else:
                        cp.wait()
                return carry
            lax.fori_loop(0, xs_ref.shape[0] // TM_E, body, 0)

        fill(True)
        fill(False)

    for b in range(nb):
        for t in range(tt):
            d = dest_ref[0, 0, b * tt + t]
            pltpu.make_async_copy(x_ref.at[b, pl.ds(t, 1), :], xs_ref.at[pl.ds(d, 1), :], sem).start()
    whole = xs_ref.at[pl.ds(0, nb * tt), :]
    pltpu.make_async_copy(whole, whole, sem).wait()


def _dispatch(tinfo, dest, xrow, *, tt, skip):
    nb, tl, _ = xrow.shape
    nt = tl // tt - skip
    n_sorted = ((nt * nb * tt) // TM_E + N_CLASSES) * TM_E
    return pl.pallas_call(
        functools.partial(_dispatch_kernel, nb=nb, tt=tt),
        grid_spec=pltpu.PrefetchScalarGridSpec(
            num_scalar_prefetch=1,
            grid=(nt,),
            in_specs=[pl.BlockSpec((1, 1, nb * tt), lambda i, ti: (i, 0, 0), memory_space=pltpu.SMEM),
                      pl.BlockSpec((nb, tt, ROW_W), lambda i, ti: (0, i + skip, 0))],
            out_specs=pl.BlockSpec(memory_space=pl.ANY),
            scratch_shapes=[pltpu.VMEM((TM_E, ROW_W), F32),
                            pltpu.SemaphoreType.DMA(()),
                            pltpu.SemaphoreType.DMA(())]),
        out_shape=jax.ShapeDtypeStruct((n_sorted, ROW_W), F32),
        compiler_params=_cparams(("arbitrary",)),
        name="moe_dispatch",
    )(tinfo, dest, xrow)


def _class_expert(c, hi):
    g = c // PAIRS_PER_GROUP
    p = c - PAIRS_PER_GROUP * g
    ge3 = (p >= 3).astype(jnp.int32)
    ge5 = (p >= 5).astype(jnp.int32)
    local = (p + 1 - 2 * ge3 - ge5) if hi else (ge3 + ge5)
    return EXPERTS_PER_GROUP * g + local


def _experts_kernel(tinfo_ref, xs_ref, wga_ref, wua_ref, wda_ref, wgb_ref, wub_ref, wdb_ref, ys_ref):
    @pl.when(pl.program_id(0) < tinfo_ref[1, 0])
    def _():
        x = xs_ref[:, :D_MODEL].astype(BF16)

        def ffn(wg_ref, wu_ref, wd_ref):
            hg = jnp.dot(x, wg_ref[0], preferred_element_type=F32)
            hu = jnp.dot(x, wu_ref[0], preferred_element_type=F32)
            hdn = (hg * jax.nn.sigmoid(hg)) * hu
            return jnp.dot(hdn.astype(BF16), wd_ref[0], preferred_element_type=F32)

        ys_ref[...] = (xs_ref[:, D_MODEL + 1:D_MODEL + 2] * ffn(wga_ref, wua_ref, wda_ref)
                       + xs_ref[:, D_MODEL + 2:D_MODEL + 3] * ffn(wgb_ref, wub_ref, wdb_ref))

    @pl.when(pl.program_id(0) >= tinfo_ref[1, 0])
    def _():
        ys_ref[...] = jnp.zeros_like(ys_ref)


def _experts(tinfo, xs, wg, wu, wd):
    n_sorted = xs.shape[0]
    nt = n_sorted // TM_E

    def tile(j, ti):
        return jnp.minimum(j, ti[1, 0] - 1)

    def wmap(hi):
        return lambda j, ti: (_class_expert(ti[0, tile(j, ti)], hi), 0, 0)

    up = lambda hi: pl.BlockSpec((1, D_MODEL, D_EXPERT), wmap(hi))
    down = lambda hi: pl.BlockSpec((1, D_EXPERT, D_MODEL), wmap(hi))
    return pl.pallas_call(
        _experts_kernel,
        grid_spec=pltpu.PrefetchScalarGridSpec(
            num_scalar_prefetch=1,
            grid=(nt,),
            in_specs=[pl.BlockSpec((TM_E, ROW_W), lambda j, ti: (tile(j, ti), 0)),
                      up(0), up(0), down(0), up(1), up(1), down(1)],
            out_specs=pl.BlockSpec((TM_E, D_MODEL), lambda j, ti: (j, 0))),
        out_shape=jax.ShapeDtypeStruct((n_sorted, D_MODEL), F32),
        compiler_params=_cparams(("arbitrary",)),
        name="moe_experts",
    )(tinfo, xs, wg, wu, wd, wg, wu, wd)


def _combine_kernel(dcur_ref, dnxt_ref, x1_ref, nf_ref, ys_ref, y_ref, buf, sem, *, nb, tt):
    i = pl.program_id(0)
    slot = i % 2

    def gather(dref, s):
        for b in range(nb):
            for t in range(tt):
                d = dref[0, 0, b * tt + t]
                pltpu.make_async_copy(ys_ref.at[pl.ds(d, 1), :], buf.at[s, b, pl.ds(t, 1), :],
                                      sem.at[s]).start()

    @pl.when(i == 0)
    def _():
        gather(dcur_ref, 0)

    @pl.when(i + 1 < pl.num_programs(0))
    def _():
        gather(dnxt_ref, 1 - slot)

    whole = ys_ref.at[pl.ds(0, nb * tt), :]
    pltpu.make_async_copy(whole, whole, sem.at[slot]).wait()
    x2 = x1_ref[...].reshape(nb * tt, D_MODEL) + buf[slot].reshape(nb * tt, D_MODEL)
    y_ref[...] = _rms(x2, nf_ref[...]).reshape(nb, tt, D_MODEL)


def _combine(dest, x1, nf, ys, *, tt, skip):
    nb, tl, _ = x1.shape
    nt = tl // tt - skip
    dspec = lambda f: pl.BlockSpec((1, 1, nb * tt), f, memory_space=pltpu.SMEM)
    return pl.pallas_call(
        functools.partial(_combine_kernel, nb=nb, tt=tt),
        grid=(nt,),
        in_specs=[dspec(lambda i: (i, 0, 0)),
                  dspec(lambda i: (jnp.minimum(i + 1, nt - 1), 0, 0)),
                  pl.BlockSpec((nb, tt, D_MODEL), lambda i: (0, i + skip, 0)),
                  pl.BlockSpec((1, D_MODEL), lambda i: (0, 0)),
                  pl.BlockSpec(memory_space=pl.ANY)],
        out_specs=pl.BlockSpec((nb, tt, D_MODEL), lambda i: (0, i, 0)),
        out_shape=jax.ShapeDtypeStruct((nb, nt * tt, D_MODEL), F32),
        scratch_shapes=[pltpu.VMEM((2, nb, tt, D_MODEL), F32),
                        pltpu.SemaphoreType.DMA((2,))],
        compiler_params=_cparams(("arbitrary",)),
        name="moe_combine",
    )(dest, dest, x1, nf, ys)


def _block_diag(w):
    per = RNN_SUPER // RNN_BLOCK
    w4 = w.reshape(N_SUPER, per, RNN_BLOCK, RNN_BLOCK).astype(BF16)
    rows = [jnp.pad(w4[:, q], ((0, 0), (0, 0), (q * RNN_BLOCK, (per - 1 - q) * RNN_BLOCK))) for q in range(per)]
    return jnp.concatenate(rows, axis=1)


def _mixers(x, meta, conv0, h0, c0, n0, m0, p, w_ready, *, has_head):
    nb, seq, _ = x.shape
    ready = w_ready is not None
    n_null = N_NULL if has_head else 0
    tt = CHUNK if has_head else seq
    tt_mm = MM_TILES * CHUNK if has_head else seq
    xn, gif = _prep(x, meta, p['norm1_w'], p['w_gif'], p['b_gif'], has_head=has_head)
    w_bf16 = {}

    def in_proj(key, w, col0, ncols, tn, out_dtype, gelu=False):
        if ready:
            return _mm(xn, w_ready[key], col0=0, ncols=ncols, tn=tn, tt=tt_mm, out_dtype=out_dtype, gelu=gelu,
                       ready=True, name="in_" + key)
        out, w_bf16[key] = _mm(xn, w, col0=col0, ncols=ncols, tn=tn, tt=tt_mm, out_dtype=out_dtype, gelu=gelu,
                               name="in_" + key)
        return out

    xr = in_proj('xr', p['w_in'], 0, D_RNN, 512, F32)
    gr = in_proj('gr', p['w_in'], D_RNN, D_RNN, 512, BF16, gelu=True)
    qkvo = in_proj('qkvo', p['w_in'], 2 * D_RNN, QKVO, 1024, BF16)
    mg = in_proj('mg', p['w_mg'], 0, 2 * D_MODEL, 1024, BF16)

    tail0 = jnp.transpose(conv0, (1, 0, 2)).reshape((CONV_W - 1) * nb, D_RNN)
    y_r, h_new, tail = _rglru(xr, gr, tail0, h0, p['conv_w'], p['conv_b'], p['rg_a_b'], p['rg_x_b'],
                              p['rg_lambda'], p['wa_bd'], p['wx_bd'], tt=tt, n_null=n_null)
    conv_new = jnp.transpose(tail.reshape(CONV_W - 1, nb, D_RNN), (1, 0, 2))

    y_m, c_new, n_new, m_new = _mlstm_fused(qkvo, gif, c0, n0, m0, p['mlstm_norm_w'], L=tt, n_null=n_null,
                                            nbk=MLSTM_NBK)

    if ready:
        mix = _merge(y_r, y_m, w_ready['proj_rnn'], w_ready['proj_mlstm'], mg, tn=512, tt=tt, ready=True)
    else:
        mix, w_bf16['proj_rnn'], w_bf16['proj_mlstm'] = _merge(y_r, y_m, p['w_proj_rnn'], p['w_proj_mlstm'], mg,
                                                               tn=512, tt=tt)
    states = (conv_new[None], h_new[None], c_new[None], n_new[None], m_new[None])
    return mix, states, (w_ready if ready else w_bf16)


def _ffn(mix, x, meta, p, expert_w, *, has_head):
    tt = CHUNK if has_head else x.shape[1]
    routed = has_head
    x1, xn2, gates = _outproj(mix, p['w_out'], x, meta, p['norm2_w'], p['w_router'], p['b_router'],
                              tt=tt, has_head=has_head, routed=routed)
    if routed:
        skip = 1
        dest, tinfo = _plan(gates, tt=tt, skip=skip)
        xs = _dispatch(tinfo, dest, xn2, tt=tt, skip=skip)
        ys = _experts(tinfo, xs, *expert_w)
        return _combine(dest, x1, p['norm_f_w'], ys, tt=tt, skip=skip), expert_w
    y, *expert_w_bf16 = _moe(xn2, gates, x1, *expert_w, p['norm_f_w'])
    return y, tuple(expert_w_bf16)


def kernel(x_prompt, x_sample, state_rglru_conv, state_rglru_h, state_mlstm_C, state_mlstm_n, state_mlstm_m, meta_tokens, norm1_w, w_in, b_gates, conv_w, conv_b, rg_a_w, rg_a_b, rg_x_w, rg_x_b, rg_lambda, mlstm_norm_w, w_proj_rnn, w_proj_mlstm, w_out, norm2_w, w_router_group, b_router_group, w_router_expert, b_router_expert, w_exp_gate, w_exp_up, w_exp_down, norm_f_w):
    l = 0
    w_in_t = jnp.swapaxes(w_in[l], 0, 1)
    pad_r = LANES - N_GROUPS - N_EXPERTS
    p = dict(
        norm1_w=norm1_w[l][None], w_in=w_in_t,
        w_gif=jnp.pad(w_in_t[GIF_OFF:MG_OFF].T, ((0, 0), (0, LANES - 2 * N_HEADS_M))).astype(BF16),
        b_gif=jnp.pad(b_gates[l], (0, LANES - 2 * N_HEADS_M))[None],
        w_mg=w_in_t[MG_OFF:],
        conv_w=conv_w[l], conv_b=conv_b[l][None], rg_a_b=rg_a_b[l][None], rg_x_b=rg_x_b[l][None],
        rg_lambda=rg_lambda[l][None], wa_bd=_block_diag(rg_a_w[l]), wx_bd=_block_diag(rg_x_w[l]),
        mlstm_norm_w=mlstm_norm_w[l][None],
        w_proj_rnn=w_proj_rnn[l], w_proj_mlstm=w_proj_mlstm[l], w_out=w_out[l].astype(BF16),
        norm2_w=norm2_w[l][None],
        w_router=jnp.pad(jnp.concatenate([w_router_group[l], w_router_expert[l]], axis=1),
                         ((0, 0), (0, pad_r))).astype(BF16),
        b_router=jnp.pad(jnp.concatenate([b_router_group[l], b_router_expert[l]]), (0, pad_r))[None],
        norm_f_w=norm_f_w[None],
    )
    nbp = x_prompt.shape[0]
    dt = x_prompt.dtype
    mix_p, st_p, w_bf16 = _mixers(
        x_prompt, meta_tokens,
        jnp.zeros((nbp, CONV_W - 1, D_RNN), dt), jnp.zeros((nbp, D_RNN), F32),
        jnp.zeros((nbp, N_HEADS_M, DQK_M, DV_M), F32), jnp.zeros((nbp, N_HEADS_M, DQK_M), F32),
        jnp.zeros((nbp, N_HEADS_M), F32), p, None, has_head=True)
    mix_s, st_s, _ = _mixers(
        x_sample, meta_tokens, state_rglru_conv[l], state_rglru_h[l], state_mlstm_C[l],
        state_mlstm_n[l], state_mlstm_m[l], p, w_bf16, has_head=False)
    y_s, expert_w_bf16 = _ffn(mix_s, x_sample, meta_tokens, p, (w_exp_gate[l], w_exp_up[l], w_exp_down[l]),
                              has_head=False)
    y_p, _ = _ffn(mix_p, x_prompt, meta_tokens, p, expert_w_bf16, has_head=True)
    return (y_p, y_s) + st_p + st_s
```

```python
import functools

import jax
import jax.numpy as jnp
from jax import lax
from jax.experimental import pallas as pl
from jax.experimental.pallas import tpu as pltpu

F32 = jnp.float32
BF16 = jnp.bfloat16

D_MODEL = 2048
CHUNK = 64
N_META = 16
N_NULL = CHUNK - N_META
D_RNN = 2560
N_RNN_BLOCKS = 16
RNN_BLOCK = D_RNN // N_RNN_BLOCKS
RNN_SUPER = 640
N_SUPER = D_RNN // RNN_SUPER
CONV_W = 4
LRU_C = 8.0
N_HEADS_M = 8
DV_M = D_MODEL // N_HEADS_M
DQK_M = DV_M // 2
HQK = N_HEADS_M * DQK_M
HV = N_HEADS_M * DV_M
QKVO = 2 * HQK + 2 * HV
N_GROUPS = 4
EXPERTS_PER_GROUP = 4
N_EXPERTS = 16
D_EXPERT = 512
PAIRS_PER_GROUP = 6
N_CLASSES = N_GROUPS * PAIRS_PER_GROUP
MM_TILES = 3
MLSTM_NBK = 2
TM_E = 256
ROW_W = D_MODEL + 128
EPS = 1e-6
LANES = 128
GIF_OFF = 2 * D_RNN + QKVO
MG_OFF = GIF_OFF + 2 * N_HEADS_M
VMEM_LIMIT = 56 * 1024 * 1024
NEG_BIG = -1e30


def _cparams(sem):
    return pltpu.CompilerParams(dimension_semantics=sem, vmem_limit_bytes=VMEM_LIMIT)


def _rms(x, w):
    return x * lax.rsqrt(jnp.mean(x * x, axis=-1, keepdims=True) + EPS) * w


def _softplus(x):
    return jnp.maximum(x, 0.0) + jnp.log1p(jnp.exp(-jnp.abs(x)))


def _head_rows(meta_ref, nb):
    head = jnp.concatenate([jnp.zeros((N_NULL, D_MODEL), F32), meta_ref[...]], axis=0)
    return jnp.broadcast_to(head[None], (nb, CHUNK, D_MODEL)).reshape(nb * CHUNK, D_MODEL)


def _x_map(has_head):
    if has_head:
        return lambda ti: (0, jnp.maximum(ti - 1, 0), 0)
    return lambda ti: (0, ti, 0)


def _prep_kernel(x_ref, meta_ref, nw_ref, wg_ref, bg_ref, xn_ref, gif_ref, *, nb, tt, has_head):
    def emit(x):
        xnb = _rms(x, nw_ref[...]).astype(BF16)
        xn_ref[...] = xnb.reshape(nb, tt, D_MODEL)
        gif = lax.dot_general(xnb, wg_ref[...].astype(BF16), (((1,), (1,)), ((), ())),
                              preferred_element_type=F32) + bg_ref[...]
        gif_ref[...] = gif.reshape(nb, tt, LANES)

    if has_head:
        @pl.when(pl.program_id(0) == 0)
        def _():
            emit(_head_rows(meta_ref, nb))

        @pl.when(pl.program_id(0) > 0)
        def _():
            emit(x_ref[...].reshape(nb * tt, D_MODEL))
    else:
        emit(x_ref[...].reshape(nb * tt, D_MODEL))


def _prep(x, meta, norm_w, w_in_t, b_gif, *, has_head):
    nb, seq, _ = x.shape
    assert GIF_OFF % LANES == 0
    tt = CHUNK if has_head else seq
    n_tiles = seq // tt + (1 if has_head else 0)
    tl = n_tiles * tt
    const = lambda ti: (0, 0)
    return pl.pallas_call(
        functools.partial(_prep_kernel, nb=nb, tt=tt, has_head=has_head),
        grid=(n_tiles,),
        in_specs=[pl.BlockSpec((nb, tt, D_MODEL), _x_map(has_head)),
                  pl.BlockSpec((N_META, D_MODEL), const),
                  pl.BlockSpec((1, D_MODEL), const),
                  pl.BlockSpec((LANES, D_MODEL), lambda ti: (GIF_OFF // LANES, 0)),
                  pl.BlockSpec((1, LANES), const)],
        out_specs=[pl.BlockSpec((nb, tt, D_MODEL), lambda ti: (0, ti, 0)),
                   pl.BlockSpec((nb, tt, LANES), lambda ti: (0, ti, 0))],
        out_shape=[jax.ShapeDtypeStruct((nb, tl, D_MODEL), BF16),
                   jax.ShapeDtypeStruct((nb, tl, LANES), F32)],
        compiler_params=_cparams(("arbitrary",)),
        name="prep",
    )(x, meta, norm_w, w_in_t, b_gif)


def _gelu_tanh(g):
    return g * (0.5 * (1.0 + jnp.tanh(0.7978845608028654 * (g + 0.044715 * (g * g * g)))))


def _mm_kernel(lhs_ref, w_ref, out_ref, *wbf_ref, gelu):
    nb, tt, k = lhs_ref.shape
    tn = out_ref.shape[-1]
    if wbf_ref:
        w_bf = wbf_ref[0]

        @pl.when(pl.program_id(1) == 0)
        def _():
            w_bf[...] = w_ref[...].T.astype(BF16)
    else:
        w_bf = w_ref

    acc = jnp.dot(lhs_ref[...].reshape(nb * tt, k), w_bf[...], preferred_element_type=F32)
    if gelu:
        acc = _gelu_tanh(acc)
    out_ref[...] = acc.astype(out_ref.dtype).reshape(nb, tt, tn)


def _mm(lhs, w, *, col0, ncols, tn, tt, out_dtype, gelu=False, ready=False, name):
    nb, tl, k = lhs.shape
    assert tl % tt == 0 and ncols % tn == 0 and col0 % 8 == 0
    lhs_spec = pl.BlockSpec((nb, tt, k), lambda j, i: (0, i, 0))
    out_spec = pl.BlockSpec((nb, tt, tn), lambda j, i: (0, i, j))
    out_shape = jax.ShapeDtypeStruct((nb, tl, ncols), out_dtype)
    wbf_spec = pl.BlockSpec((k, tn), lambda j, i: (0, j))
    if ready:
        in_specs, out_specs, out_shapes = [lhs_spec, wbf_spec], out_spec, out_shape
    else:
        in_specs = [lhs_spec, pl.BlockSpec((pl.Element(tn), pl.Element(k)),
                                           lambda j, i: (pl.multiple_of(col0 + j * tn, 8), 0))]
        out_specs = [out_spec, wbf_spec]
        out_shapes = [out_shape, jax.ShapeDtypeStruct((k, ncols), BF16)]
    return pl.pallas_call(
        functools.partial(_mm_kernel, gelu=gelu),
        grid=(ncols // tn, tl // tt),
        in_specs=in_specs,
        out_specs=out_specs,
        out_shape=out_shapes,
        compiler_params=_cparams(("arbitrary", "arbitrary")),
        name=name,
    )(lhs, w)


def _merge_kernel(yr_ref, ym_ref, wr_ref, wm_ref, gr_ref, gm_ref, out_ref, *wbf_refs):
    nb, tt, tn = out_ref.shape
    rows = nb * tt
    if wbf_refs:
        wr_bf, wm_bf = wbf_refs

        @pl.when(pl.program_id(1) == 0)
        def _():
            wr_bf[...] = wr_ref[...].astype(BF16)
            wm_bf[...] = wm_ref[...].astype(BF16)
    else:
        wr_bf, wm_bf = wr_ref, wm_ref

    pr = jnp.dot(yr_ref[...].reshape(rows, D_RNN), wr_bf[...], preferred_element_type=F32)
    pm = jnp.dot(ym_ref[...].reshape(rows, HV), wm_bf[...], preferred_element_type=F32)
    mix = (jax.nn.sigmoid(gr_ref[...].reshape(rows, tn).astype(F32)) * pr
           + jax.nn.sigmoid(gm_ref[...].reshape(rows, tn).astype(F32)) * pm)
    out_ref[...] = mix.astype(BF16).reshape(nb, tt, tn)


def _merge(y_r, y_m, w_r, w_m, mg, *, tn, tt, ready=False):
    nb, tl, _ = y_r.shape
    nj = D_MODEL // tn
    wr_spec = pl.BlockSpec((D_RNN, tn), lambda j, i: (0, j))
    wm_spec = pl.BlockSpec((HV, tn), lambda j, i: (0, j))
    out_specs = pl.BlockSpec((nb, tt, tn), lambda j, i: (0, i, j))
    out_shape = jax.ShapeDtypeStruct((nb, tl, D_MODEL), BF16)
    if not ready:
        out_specs = [out_specs, wr_spec, wm_spec]
        out_shape = [out_shape, jax.ShapeDtypeStruct(w_r.shape, BF16), jax.ShapeDtypeStruct(w_m.shape, BF16)]
    return pl.pallas_call(
        _merge_kernel,
        grid=(nj, tl // tt),
        in_specs=[pl.BlockSpec((nb, tt, D_RNN), lambda j, i: (0, i, 0)),
                  pl.BlockSpec((nb, tt, HV), lambda j, i: (0, i, 0)),
                  wr_spec, wm_spec,
                  pl.BlockSpec((nb, tt, tn), lambda j, i: (0, i, j)),
                  pl.BlockSpec((nb, tt, tn), lambda j, i: (0, i, nj + j))],
        out_specs=out_specs,
        out_shape=out_shape,
        compiler_params=_cparams(("arbitrary", "arbitrary")),
        name="merge",
    )(y_r, y_m, w_r, w_m, mg, mg)


def _rglru_kernel(xr_ref, gr_ref, tail0_ref, h0_ref, cw_ref, cb_ref, ab_ref, xb_ref, lam_ref,
                  wa_ref, wx_ref, y_ref, hl_ref, tail_ref, xp_s, h_s, a_s, u_s, hs_s, *, nb, tt, n_null):
    ti = pl.program_id(1)
    rt = tt * nb
    tb = (CONV_W - 1) * nb

    def time_major(x):
        return jnp.swapaxes(x, 0, 1).reshape(rt, x.shape[-1])

    @pl.when(ti == 0)
    def _():
        xp_s[0:tb, :] = tail0_ref[...]
        h_s[...] = h0_ref[...]

    xp_s[tb:tb + rt, :] = time_major(xr_ref[...])
    cw = cw_ref[...]
    xc = cb_ref[...] + cw[0:1, :] * xp_s[0:rt, :]
    for j in range(1, CONV_W):
        xc = xc + cw[j:j + 1, :] * xp_s[j * nb:j * nb + rt, :]
    xcb = xc.astype(BF16)
    r = jax.nn.sigmoid(jnp.dot(xcb, wa_ref[0], preferred_element_type=F32) + ab_ref[...])
    i = jax.nn.sigmoid(jnp.dot(xcb, wx_ref[0], preferred_element_type=F32) + xb_ref[...])
    log_a = r * ((-LRU_C) * _softplus(-lam_ref[...]))
    a = jnp.exp(log_a)
    a_s[...] = a
    u = jnp.sqrt(-jnp.tanh(log_a) * (a * a + 1.0)) * (i * xc)
    u_s[...] = u
    if n_null:
        @pl.when(ti == 0)
        def _():
            a_s[n_null * nb:(n_null + 1) * nb, :] = jnp.zeros((nb, a_s.shape[-1]), F32)

    def step(t, h):
        o = pl.multiple_of(t * nb, nb)
        h = a_s[pl.ds(o, nb), :] * h + u_s[pl.ds(o, nb), :]
        hs_s[pl.ds(o, nb), :] = h
        return h

    h = lax.fori_loop(0, tt, step, h_s[...], unroll=8)
    h_s[...] = h
    h_bm = jnp.swapaxes(hs_s[...].reshape(tt, nb, hs_s.shape[-1]), 0, 1)
    y_ref[...] = (gr_ref[...].astype(F32) * h_bm).astype(BF16)
    xp_s[0:tb, :] = xp_s[rt:rt + tb, :]

    @pl.when(ti == pl.num_programs(1) - 1)
    def _():
        hl_ref[...] = h
        tail_ref[...] = xp_s[rt:rt + tb, :]


def _rglru(xr, gr, tail0, h0, cw, cb, ab, xb, lam, wa, wx, *, tt, n_null):
    nb, tl, _ = xr.shape
    rt = tt * nb
    tb = (CONV_W - 1) * nb
    blk = lambda s, ti: (0, ti, s)
    col = lambda s, ti: (0, s)
    return pl.pallas_call(
        functools.partial(_rglru_kernel, nb=nb, tt=tt, n_null=n_null),
        grid=(N_SUPER, tl // tt),
        in_specs=[pl.BlockSpec((nb, tt, RNN_SUPER), blk),
                  pl.BlockSpec((nb, tt, RNN_SUPER), blk),
                  pl.BlockSpec((tb, RNN_SUPER), col),
                  pl.BlockSpec((nb, RNN_SUPER), col),
                  pl.BlockSpec((CONV_W, RNN_SUPER), col),
                  pl.BlockSpec((1, RNN_SUPER), col),
                  pl.BlockSpec((1, RNN_SUPER), col),
                  pl.BlockSpec((1, RNN_SUPER), col),
                  pl.BlockSpec((1, RNN_SUPER), col),
                  pl.BlockSpec((1, RNN_SUPER, RNN_SUPER), lambda s, ti: (s, 0, 0)),
                  pl.BlockSpec((1, RNN_SUPER, RNN_SUPER), lambda s, ti: (s, 0, 0))],
        out_specs=[pl.BlockSpec((nb, tt, RNN_SUPER), blk),
                   pl.BlockSpec((nb, RNN_SUPER), col),
                   pl.BlockSpec((tb, RNN_SUPER), col)],
        out_shape=[jax.ShapeDtypeStruct((nb, tl, D_RNN), BF16),
                   jax.ShapeDtypeStruct((nb, D_RNN), F32),
                   jax.ShapeDtypeStruct((tb, D_RNN), F32)],
        scratch_shapes=[pltpu.VMEM((rt + tb, RNN_SUPER), F32),
                        pltpu.VMEM((nb, RNN_SUPER), F32),
                        pltpu.VMEM((rt, RNN_SUPER), F32),
                        pltpu.VMEM((rt, RNN_SUPER), F32),
                        pltpu.VMEM((rt, RNN_SUPER), F32)],
        compiler_params=_cparams(("arbitrary", "arbitrary")),
        name="rglru",
    )(xr, gr, tail0, h0, cw, cb, ab, xb, lam, wa, wx)


DVA = DV_M + 128


def _mlstm_gates(c, gif_ref, m_s, *, L, n_null):
    hm = N_HEADS_M
    g = gif_ref[...]
    if L < LANES:
        g = jnp.concatenate([g, jnp.zeros((LANES - L, LANES), F32)], axis=0)
    g_t = g.T
    lane = lax.broadcasted_iota(jnp.int32, (hm, LANES), 1)
    valid = lane < L
    li = g_t[0:hm, :]
    lfr = g_t[hm:2 * hm, :]
    lf = jnp.where(valid, jnp.minimum(lfr, 0.0) - jnp.log1p(jnp.exp(-jnp.abs(lfr))), 0.0)
    if n_null:
        null = (lane < n_null) & (c == 0)
        lf = jnp.where(null, 0.0, lf)
        li = jnp.where(null, NEG_BIG, li)
    r128 = lax.broadcasted_iota(jnp.int32, (LANES, LANES), 0)
    c128 = lax.broadcasted_iota(jnp.int32, (LANES, LANES), 1)
    upper = jnp.where(r128 <= c128, 1.0, 0.0)
    b = jnp.dot(lf, upper, preferred_element_type=F32, precision=lax.Precision.HIGHEST)
    a = li - b
    cm = jnp.where(valid, a, -jnp.inf)
    sh = 1
    while sh < L:
        cm = jnp.maximum(cm, jnp.where(lane >= sh, pltpu.roll(cm, sh, axis=1), -jnp.inf))
        sh *= 2
    m0 = m_s[...]
    big_m = jnp.maximum(cm, m0)
    m_last = jnp.max(jnp.where(valid, big_m, -jnp.inf), axis=1, keepdims=True)
    b_last = jnp.sum(lf, axis=1, keepdims=True)
    scale = DQK_M ** -0.5
    e_mv = jnp.exp(-(b + big_m))
    w_int = jnp.exp(m0 - big_m)
    w_end = jnp.exp(a - m_last) * scale
    decay = jnp.exp(m0 - m_last)
    m_s[...] = jnp.broadcast_to(b_last + m_last, (hm, LANES))
    cols = jnp.concatenate([big_m, e_mv, w_int, w_end, jnp.zeros((LANES - 4 * hm, LANES), F32)], axis=0).T
    return a, cols, decay


def _mlstm_heads(gates, q_ref, k_ref, v_ref, og_ref, nw_ref, y_ref, c_s, *, L):
    hm = N_HEADS_M
    scale = DQK_M ** -0.5
    rr = lax.broadcasted_iota(jnp.int32, (L, L), 0)
    cc = lax.broadcasted_iota(jnp.int32, (L, L), 1)
    causal = rr >= cc
    ones_col = jnp.where(lax.broadcasted_iota(jnp.int32, (L, 128), 1) == 0, 1.0, 0.0).astype(BF16)
    nw = nw_ref[...]
    dn_nt = (((1,), (1,)), ((), ()))
    dn_tn = (((0,), (0,)), ((), ()))
    bh = [(bi, h) for bi in range(len(gates)) for h in range(hm)]
    col = lambda bi, k, h: gates[bi][1][0:L, k * hm + h:k * hm + h + 1]
    qs = [q_ref[bi, :, h * DQK_M:(h + 1) * DQK_M] for bi, h in bh]
    ks = [k_ref[bi, :, h * DQK_M:(h + 1) * DQK_M] for bi, h in bh]
    v_augs = [jnp.concatenate([v_ref[bi, :, h * DV_M:(h + 1) * DV_M], ones_col], axis=1) for bi, h in bh]
    c_augs = [c_s[bi, h] for bi, h in bh]
    s_raw = [lax.dot_general(q, k, dn_nt, preferred_element_type=F32) for q, k in zip(qs, ks)]
    inter = [jnp.dot((qs[i].astype(F32) * col(bi, 2, h)).astype(BF16), c_augs[i].astype(BF16),
                     preferred_element_type=F32) for i, (bi, h) in enumerate(bh)]
    upd = [lax.dot_general((ks[i].astype(F32) * col(bi, 3, h)).astype(BF16), v_augs[i], dn_tn,
                           preferred_element_type=F32) for i, (bi, h) in enumerate(bh)]
    s_w = [(s_raw[i] * (scale * jnp.where(causal, jnp.exp(gates[bi][0][h:h + 1, 0:L] - col(bi, 0, h)), 0.0))
            ).astype(BF16) for i, (bi, h) in enumerate(bh)]
    res = [jnp.dot(s_w[i], v_augs[i], preferred_element_type=F32) + inter[i] for i in range(len(bh))]
    for i, (bi, h) in enumerate(bh):
        c_s[bi, h] = gates[bi][2][h:h + 1, 0:1] * c_augs[i] + upd[i]
    for i, (bi, h) in enumerate(bh):
        num = res[i][:, :DV_M]
        denom = jnp.maximum(jnp.abs(res[i][:, DV_M:DV_M + 1]), col(bi, 1, h))
        hh = num / denom
        hh = hh * lax.rsqrt(jnp.mean(hh * hh, axis=-1, keepdims=True) + EPS)
        hh = hh * nw[:, h * DV_M:(h + 1) * DV_M]
        og = og_ref[bi, :, h * DV_M:(h + 1) * DV_M].astype(F32)
        y_ref[bi, :, h * DV_M:(h + 1) * DV_M] = (hh * jax.nn.sigmoid(og)).astype(BF16)


def _mlstm_fused_kernel(q_ref, k_ref, v_ref, og_ref, gif_ref, c0_ref, n0_ref, m0_ref, nw_ref,
                        y_ref, co_ref, no_ref, mo_ref, c_s, m_s, *, L, n_null, nbk):
    c = pl.program_id(1)
    bh = [(bi, h) for bi in range(nbk) for h in range(N_HEADS_M)]

    @pl.when(c == 0)
    def _():
        m_s[...] = m0_ref[...]
        for bi, h in bh:
            c_s[bi, h, :, :DV_M] = c0_ref[bi, h]
            n_tile = jnp.concatenate([n0_ref[bi, h:h + 1, :], jnp.zeros((DQK_M - 1, DQK_M), F32)], axis=0)
            c_s[bi, h, :, DV_M:] = n_tile.T

    gates = [_mlstm_gates(c, gif_ref.at[bi], m_s.at[bi], L=L, n_null=n_null) for bi in range(nbk)]
    _mlstm_heads(gates, q_ref, k_ref, v_ref, og_ref, nw_ref, y_ref, c_s, L=L)

    @pl.when(c == pl.num_programs(1) - 1)
    def _():
        mo_ref[...] = m_s[...]
        for bi, h in bh:
            co_ref[bi, h] = c_s[bi, h, :, :DV_M]
            no_ref[bi, h:h + 1, :] = c_s[bi, h, :, DV_M:].T[0:1, :]


def _mlstm_fused(qkvo, gif, c0, n0, m0, nw, *, L, n_null, nbk):
    nb, tl, _ = qkvo.shape
    assert nb % nbk == 0
    hm = N_HEADS_M
    assert DQK_M == LANES and DVA - DV_M == DQK_M
    m_rep0 = jnp.broadcast_to(m0[..., None], (nb, hm, LANES))
    st4 = lambda b, c: (b, 0, 0, 0)
    st3 = lambda b, c: (b, 0, 0)
    y, c_new, n_new, m_rep = pl.pallas_call(
        functools.partial(_mlstm_fused_kernel, L=L, n_null=n_null, nbk=nbk),
        grid=(nb // nbk, tl // L),
        in_specs=[pl.BlockSpec((nbk, L, HQK), lambda b, c: (b, c, 0)),
                  pl.BlockSpec((nbk, L, HQK), lambda b, c: (b, c, 1)),
                  pl.BlockSpec((nbk, L, HV), lambda b, c: (b, c, 1)),
                  pl.BlockSpec((nbk, L, HV), lambda b, c: (b, c, 2)),
                  pl.BlockSpec((nbk, L, LANES), lambda b, c: (b, c, 0)),
                  pl.BlockSpec((nbk, hm, DQK_M, DV_M), st4),
                  pl.BlockSpec((nbk, hm, DQK_M), st3),
                  pl.BlockSpec((nbk, hm, LANES), st3),
                  pl.BlockSpec((1, HV), lambda b, c: (0, 0))],
        out_specs=[pl.BlockSpec((nbk, L, HV), lambda b, c: (b, c, 0)),
                   pl.BlockSpec((nbk, hm, DQK_M, DV_M), st4),
                   pl.BlockSpec((nbk, hm, DQK_M), st3),
                   pl.BlockSpec((nbk, hm, LANES), st3)],
        out_shape=[jax.ShapeDtypeStruct((nb, tl, HV), BF16),
                   jax.ShapeDtypeStruct((nb, hm, DQK_M, DV_M), F32),
                   jax.ShapeDtypeStruct((nb, hm, DQK_M), F32),
                   jax.ShapeDtypeStruct((nb, hm, LANES), F32)],
        scratch_shapes=[pltpu.VMEM((nbk, hm, DQK_M, DVA), F32),
                        pltpu.VMEM((nbk, hm, LANES), F32)],
        compiler_params=_cparams(("arbitrary", "arbitrary")),
        name="mlstm",
    )(qkvo, qkvo, qkvo, qkvo, gif, c0, n0, m_rep0, nw)
    return y, c_new, n_new, m_rep[..., 0]


def _route(logits, dense):
    tm = logits.shape[0]
    lane = lax.broadcasted_iota(jnp.int32, (tm, LANES), 1)
    lanef = lane.astype(F32)
    gl = jnp.where(lane < N_GROUPS, logits, -jnp.inf)
    gmax = jnp.max(gl, axis=1, keepdims=True)
    gidx = jnp.min(jnp.where(gl == gmax, lanef, float(LANES)), axis=1, keepdims=True)
    gw = 1.0 / jnp.sum(jnp.exp(gl - gmax), axis=1, keepdims=True)
    lo = float(N_GROUPS) + float(EXPERTS_PER_GROUP) * gidx
    sel = (lanef >= lo) & (lanef < lo + float(EXPERTS_PER_GROUP))
    ev = jnp.where(sel, logits, -jnp.inf)
    v1 = jnp.max(ev, axis=1, keepdims=True)
    i1 = jnp.min(jnp.where(ev == v1, lanef, float(LANES)), axis=1, keepdims=True)
    ev2 = jnp.where(lanef == i1, -jnp.inf, ev)
    v2 = jnp.max(ev2, axis=1, keepdims=True)
    i2 = jnp.min(jnp.where(ev2 == v2, lanef, float(LANES)), axis=1, keepdims=True)
    t = jnp.exp(v2 - v1)
    w1 = gw / (1.0 + t)
    w2 = gw * (t / (1.0 + t))
    if dense:
        return jnp.where(lanef == i1, w1, jnp.where(lanef == i2, w2, 0.0))
    l1 = i1 - lo
    l2 = i2 - lo
    first = l1 < l2
    a = jnp.minimum(l1, l2)
    b = jnp.maximum(l1, l2)
    pair = jnp.where(a == 0.0, b - 1.0, jnp.where(a == 1.0, b + 1.0, 5.0))
    cls = float(PAIRS_PER_GROUP) * gidx + pair
    return jnp.where(lane == 0, cls,
                     jnp.where(lane == 1, jnp.where(first, w1, w2),
                               jnp.where(lane == 2, jnp.where(first, w2, w1), 0.0)))


def _outproj_kernel(mix_ref, w_ref, x_ref, meta_ref, n2_ref, wr_ref, br_ref, x1_ref, xn2_ref, gates_ref,
                    *, nb, tt, has_head, routed):
    def emit(x):
        mix = mix_ref[...].reshape(nb * tt, D_MODEL)
        x1 = x + jnp.dot(mix, w_ref[...], preferred_element_type=F32)
        x1_ref[...] = x1.reshape(nb, tt, D_MODEL)
        xn2 = _rms(x1, n2_ref[...]).astype(BF16)
        logits = jnp.dot(xn2, wr_ref[...], preferred_element_type=F32) + br_ref[...]
        info = _route(logits, dense=not routed)
        gates_ref[...] = info.reshape(nb, tt, LANES)
        if routed:
            xn2_ref[:, :, :D_MODEL] = xn2.astype(F32).reshape(nb, tt, D_MODEL)
            xn2_ref[:, :, D_MODEL:] = info.reshape(nb, tt, LANES)
        else:
            xn2_ref[...] = xn2.reshape(nb, tt, D_MODEL)

    if has_head:
        @pl.when(pl.program_id(0) == 0)
        def _():
            emit(_head_rows(meta_ref, nb))

        @pl.when(pl.program_id(0) > 0)
        def _():
            emit(x_ref[...].reshape(nb * tt, D_MODEL))
    else:
        emit(x_ref[...].reshape(nb * tt, D_MODEL))


def _outproj(mix, w_out, x, meta, n2, w_router, b_router, *, tt, has_head, routed):
    nb, tl, _ = mix.shape
    const = lambda i: (0, 0)
    blk = lambda i: (0, i, 0)
    xn2_w, xn2_dt = (ROW_W, F32) if routed else (D_MODEL, BF16)
    return pl.pallas_call(
        functools.partial(_outproj_kernel, nb=nb, tt=tt, has_head=has_head, routed=routed),
        grid=(tl // tt,),
        in_specs=[pl.BlockSpec((nb, tt, D_MODEL), blk),
                  pl.BlockSpec((D_MODEL, D_MODEL), const, pipeline_mode=pl.Buffered(1)),
                  pl.BlockSpec((nb, tt, D_MODEL), _x_map(has_head)),
                  pl.BlockSpec((N_META, D_MODEL), const),
                  pl.BlockSpec((1, D_MODEL), const),
                  pl.BlockSpec((D_MODEL, LANES), const),
                  pl.BlockSpec((1, LANES), const)],
        out_specs=[pl.BlockSpec((nb, tt, D_MODEL), blk),
                   pl.BlockSpec((nb, tt, xn2_w), blk),
                   pl.BlockSpec((nb, tt, LANES), blk)],
        out_shape=[jax.ShapeDtypeStruct((nb, tl, D_MODEL), F32),
                   jax.ShapeDtypeStruct((nb, tl, xn2_w), xn2_dt),
                   jax.ShapeDtypeStruct((nb, tl, LANES), F32)],
        compiler_params=_cparams(("arbitrary",)),
        name="outproj",
    )(mix, w_out, x, meta, n2, w_router, b_router)


def _moe_kernel(xn_ref, gates_ref, x1_ref, wg_ref, wu_ref, wd_ref, nf_ref, y_ref, wgb_ref, wub_ref, wdb_ref,
                acc_s, *, nb, tt):
    e = pl.program_id(0)
    hf = pl.program_id(1)
    rows = nb * tt

    @pl.when((e == 0) & (hf == 0))
    def _():
        acc_s[...] = jnp.zeros_like(acc_s)

    wg = wg_ref[0].astype(BF16)
    wu = wu_ref[0].astype(BF16)
    wd = wd_ref[0].astype(BF16)
    wgb_ref[0] = wg
    wub_ref[0] = wu
    wdb_ref[0] = wd
    x = xn_ref[...].reshape(rows, D_MODEL)
    hg = jnp.dot(x, wg, preferred_element_type=F32)
    hu = jnp.dot(x, wu, preferred_element_type=F32)
    hdn = (hg * jax.nn.sigmoid(hg)) * hu
    yd = jnp.dot(hdn.astype(BF16), wd, preferred_element_type=F32)
    gates = gates_ref[...].reshape(rows, LANES)
    lane = lax.broadcasted_iota(jnp.int32, gates.shape, 1)
    gcol = jnp.sum(jnp.where(lane == e + N_GROUPS, gates, 0.0), axis=1, keepdims=True)
    acc_s[...] += gcol * yd

    @pl.when((e == pl.num_programs(0) - 1) & (hf == pl.num_programs(1) - 1))
    def _():
        x2 = x1_ref[...].reshape(rows, D_MODEL) + acc_s[...]
        y_ref[...] = _rms(x2, nf_ref[...]).reshape(nb, tt, D_MODEL)


def _moe(xn2, gates, x1, wg, wu, wd, nf):
    nb, tt, _ = xn2.shape
    halves = 2
    dh = D_EXPERT // halves
    blk = lambda e, hf: (0, 0, 0)
    up = pl.BlockSpec((1, D_MODEL, dh), lambda e, hf: (e, 0, hf))
    down = pl.BlockSpec((1, dh, D_MODEL), lambda e, hf: (e, hf, 0))
    return pl.pallas_call(
        functools.partial(_moe_kernel, nb=nb, tt=tt),
        grid=(N_EXPERTS, halves),
        in_specs=[pl.BlockSpec((nb, tt, D_MODEL), blk),
                  pl.BlockSpec((nb, tt, LANES), blk),
                  pl.BlockSpec((nb, tt, D_MODEL), blk),
                  up, up, down,
                  pl.BlockSpec((1, D_MODEL), lambda e, hf: (0, 0))],
        out_specs=[pl.BlockSpec((nb, tt, D_MODEL), blk), up, up, down],
        out_shape=[jax.ShapeDtypeStruct((nb, tt, D_MODEL), F32),
                   jax.ShapeDtypeStruct(wg.shape, BF16),
                   jax.ShapeDtypeStruct(wu.shape, BF16),
                   jax.ShapeDtypeStruct(wd.shape, BF16)],
        scratch_shapes=[pltpu.VMEM((nb * tt, D_MODEL), F32)],
        compiler_params=_cparams(("arbitrary", "arbitrary")),
        name="moe",
    )(xn2, gates, x1, wg, wu, wd, nf)


_NT = (((1,), (1,)), ((), ()))


def _plan_kernel(ri_ref, dest_ref, tinfo_ref, cnt_s, carry_s, offs_s, *, rows):
    ph = pl.program_id(0)
    i = pl.program_id(1)
    ri = ri_ref[...].reshape(rows, LANES)
    lanef = lax.broadcasted_iota(jnp.int32, (rows, LANES), 1).astype(F32)
    oh = jnp.where(lanef == ri[:, 0:1], 1.0, 0.0)
    colsum = jnp.sum(oh, axis=0, keepdims=True)
    r128 = lax.broadcasted_iota(jnp.int32, (LANES, LANES), 0)
    c128 = lax.broadcasted_iota(jnp.int32, (LANES, LANES), 1)
    eye = jnp.where(r128 == c128, 1.0, 0.0).astype(BF16)

    @pl.when((ph == 0) & (i == 0))
    def _():
        cnt_s[...] = jnp.zeros_like(cnt_s)

    @pl.when(ph == 0)
    def _():
        cnt_s[...] += colsum

    @pl.when((ph == 1) & (i == 0))
    def _():
        tiles = jnp.floor((cnt_s[...] + float(TM_E - 1)) * (1.0 / TM_E))
        upper = jnp.where(r128 <= c128, 1.0, 0.0).astype(BF16)
        cum_incl = jnp.dot(jnp.broadcast_to(tiles, (8, LANES)).astype(BF16), upper, preferred_element_type=F32)
        offs_s[...] = (cum_incl[0:1, :] - tiles) * float(TM_E)
        carry_s[...] = jnp.zeros_like(carry_s)
        cum_col = lax.dot_general(eye, cum_incl.astype(BF16), _NT, preferred_element_type=F32)
        ended = (cum_col[:, 0:1] <= c128.astype(F32)) & (r128 < N_CLASSES)
        tcls = jnp.sum(jnp.where(ended, 1.0, 0.0), axis=0, keepdims=True)
        total = cum_incl[0:1, LANES - 1:LANES]
        row8 = lax.broadcasted_iota(jnp.int32, (8, LANES), 0)
        tinfo_ref[...] = jnp.where(row8 == 0, tcls, total).astype(jnp.int32)

    @pl.when(ph == 1)
    def _():
        tri = jnp.where(lax.broadcasted_iota(jnp.int32, (rows, rows), 0)
                        > lax.broadcasted_iota(jnp.int32, (rows, rows), 1), 1.0, 0.0).astype(BF16)
        rank = jnp.dot(tri, oh.astype(BF16), preferred_element_type=F32)
        dest = jnp.sum(oh * (rank + (offs_s[...] + carry_s[...])), axis=1, keepdims=True)
        carry_s[...] += colsum
        dhi = jnp.floor(dest * (1.0 / 256.0))
        dlo = dest - 256.0 * dhi
        digits = jnp.where(lanef == 0.0, dlo, jnp.where(lanef == 1.0, dhi, 0.0)).astype(BF16)
        dt = lax.dot_general(eye, digits, _NT, preferred_element_type=F32)
        dest_ref[...] = (dt[0:1, :] + 256.0 * dt[1:2, :]).astype(jnp.int32).reshape(1, 1, rows)


def _plan(rinfo, *, tt, skip):
    nb, tl, _ = rinfo.shape
    rows = nb * tt
    nt = tl // tt - skip
    assert (nt * rows) // TM_E + N_CLASSES <= LANES and rows % LANES == 0
    return pl.pallas_call(
        functools.partial(_plan_kernel, rows=rows),
        grid=(2, nt),
        in_specs=[pl.BlockSpec((nb, tt, LANES), lambda ph, i: (0, i + skip, 0))],
        out_specs=[pl.BlockSpec((1, 1, rows), lambda ph, i: (i * ph, 0, 0)),
                   pl.BlockSpec((8, LANES), lambda ph, i: (0, 0))],
        out_shape=[jax.ShapeDtypeStruct((nt, 1, rows), jnp.int32),
                   jax.ShapeDtypeStruct((8, LANES), jnp.int32)],
        scratch_shapes=[pltpu.VMEM((1, LANES), F32)] * 3,
        compiler_params=_cparams(("arbitrary", "arbitrary")),
        name="moe_plan",
    )(rinfo)


def _dispatch_kernel(tinfo_ref, dest_ref, x_ref, xs_ref, zbuf, zsem, sem, *, nb, tt):
    i = pl.program_id(0)

    @pl.when(i == 0)
    def _():
        zbuf[...] = jnp.zeros_like(zbuf)
        n_used = tinfo_ref[1, 0]

        def fill(start):
            def body(j, carry):
                last = (j >= n_used - 1) | (tinfo_ref[0, jnp.minimum(j + 1, LANES - 1)] != tinfo_ref[0, j])

                @pl.when(last)
                def _():
                    cp = pltpu.make_async_copy(
                        zbuf, xs_ref.at[pl.ds(pl.multiple_of(j * TM_E, TM_E), TM_E), :], zsem)
                    if start:
                        cp.start()
                    else:
                        cp.wait()
                return carry
            lax.fori_loop(0, xs_ref.shape[0] // TM_E, body, 0)

        fill(True)
        fill(False)

    for b in range(nb):
        for t in range(tt):
            d = dest_ref[0, 0, b * tt + t]
            pltpu.make_async_copy(x_ref.at[b, pl.ds(t, 1), :], xs_ref.at[pl.ds(d, 1), :], sem).start()
    whole = xs_ref.at[pl.ds(0, nb * tt), :]
    pltpu.make_async_copy(whole, whole, sem).wait()


def _dispatch(tinfo, dest, xrow, *, tt, skip):
    nb, tl, _ = xrow.shape
    nt = tl // tt - skip
    n_sorted = ((nt * nb * tt) // TM_E + N_CLASSES) * TM_E
    return pl.pallas_call(
        functools.partial(_dispatch_kernel, nb=nb, tt=tt),
        grid_spec=pltpu.PrefetchScalarGridSpec(
            num_scalar_prefetch=1,
            grid=(nt,),
            in_specs=[pl.BlockSpec((1, 1, nb * tt), lambda i, ti: (i, 0, 0), memory_space=pltpu.SMEM),
                      pl.BlockSpec((nb, tt, ROW_W), lambda i, ti: (0, i + skip, 0))],
            out_specs=pl.BlockSpec(memory_space=pl.ANY),
            scratch_shapes=[pltpu.VMEM((TM_E, ROW_W), F32),
                            pltpu.SemaphoreType.DMA(()),
                            pltpu.SemaphoreType.DMA(())]),
        out_shape=jax.ShapeDtypeStruct((n_sorted, ROW_W), F32),
        compiler_params=_cparams(("arbitrary",)),
        name="moe_dispatch",
    )(tinfo, dest, xrow)


def _class_expert(c, hi):
    g = c // PAIRS_PER_GROUP
    p = c - PAIRS_PER_GROUP * g
    ge3 = (p >= 3).astype(jnp.int32)
    ge5 = (p >= 5).astype(jnp.int32)
    local = (p + 1 - 2 * ge3 - ge5) if hi else (ge3 + ge5)
    return EXPERTS_PER_GROUP * g + local


def _experts_kernel(tinfo_ref, xs_ref, wga_ref, wua_ref, wda_ref, wgb_ref, wub_ref, wdb_ref, ys_ref):
    @pl.when(pl.program_id(0) < tinfo_ref[1, 0])
    def _():
        x = xs_ref[:, :D_MODEL].astype(BF16)

        def ffn(wg_ref, wu_ref, wd_ref):
            hg = jnp.dot(x, wg_ref[0], preferred_element_type=F32)
            hu = jnp.dot(x, wu_ref[0], preferred_element_type=F32)
            hdn = (hg * jax.nn.sigmoid(hg)) * hu
            return jnp.dot(hdn.astype(BF16), wd_ref[0], preferred_element_type=F32)

        ys_ref[...] = (xs_ref[:, D_MODEL + 1:D_MODEL + 2] * ffn(wga_ref, wua_ref, wda_ref)
                       + xs_ref[:, D_MODEL + 2:D_MODEL + 3] * ffn(wgb_ref, wub_ref, wdb_ref))

    @pl.when(pl.program_id(0) >= tinfo_ref[1, 0])
    def _():
        ys_ref[...] = jnp.zeros_like(ys_ref)


def _experts(tinfo, xs, wg, wu, wd):
    n_sorted = xs.shape[0]
    nt = n_sorted // TM_E

    def tile(j, ti):
        return jnp.minimum(j, ti[1, 0] - 1)

    def wmap(hi):
        return lambda j, ti: (_class_expert(ti[0, tile(j, ti)], hi), 0, 0)

    up = lambda hi: pl.BlockSpec((1, D_MODEL, D_EXPERT), wmap(hi))
    down = lambda hi: pl.BlockSpec((1, D_EXPERT, D_MODEL), wmap(hi))
    return pl.pallas_call(
        _experts_kernel,
        grid_spec=pltpu.PrefetchScalarGridSpec(
            num_scalar_prefetch=1,
            grid=(nt,),
            in_specs=[pl.BlockSpec((TM_E, ROW_W), lambda j, ti: (tile(j, ti), 0)),
                      up(0), up(0), down(0), up(1), up(1), down(1)],
            out_specs=pl.BlockSpec((TM_E, D_MODEL), lambda j, ti: (j, 0))),
        out_shape=jax.ShapeDtypeStruct((n_sorted, D_MODEL), F32),
        compiler_params=_cparams(("arbitrary",)),
        name="moe_experts",
    )(tinfo, xs, wg, wu, wd, wg, wu, wd)


def _combine_kernel(dcur_ref, dnxt_ref, x1_ref, nf_ref, ys_ref, y_ref, buf, sem, *, nb, tt):
    i = pl.program_id(0)
    slot = i % 2

    def gather(dref, s):
        for b in range(nb):
            for t in range(tt):
                d = dref[0, 0, b * tt + t]
                pltpu.make_async_copy(ys_ref.at[pl.ds(d, 1), :], buf.at[s, b, pl.ds(t, 1), :],
                                      sem.at[s]).start()

    @pl.when(i == 0)
    def _():
        gather(dcur_ref, 0)

    @pl.when(i + 1 < pl.num_programs(0))
    def _():
        gather(dnxt_ref, 1 - slot)

    whole = ys_ref.at[pl.ds(0, nb * tt), :]
    pltpu.make_async_copy(whole, whole, sem.at[slot]).wait()
    x2 = x1_ref[...].reshape(nb * tt, D_MODEL) + buf[slot].reshape(nb * tt, D_MODEL)
    y_ref[...] = _rms(x2, nf_ref[...]).reshape(nb, tt, D_MODEL)


def _combine(dest, x1, nf, ys, *, tt, skip):
    nb, tl, _ = x1.shape
    nt = tl // tt - skip
    dspec = lambda f: pl.BlockSpec((1, 1, nb * tt), f, memory_space=pltpu.SMEM)
    return pl.pallas_call(
        functools.partial(_combine_kernel, nb=nb, tt=tt),
        grid=(nt,),
        in_specs=[dspec(lambda i: (i, 0, 0)),
                  dspec(lambda i: (jnp.minimum(i + 1, nt - 1), 0, 0)),
                  pl.BlockSpec((nb, tt, D_MODEL), lambda i: (0, i + skip, 0)),
                  pl.BlockSpec((1, D_MODEL), lambda i: (0, 0)),
                  pl.BlockSpec(memory_space=pl.ANY)],
        out_specs=pl.BlockSpec((nb, tt, D_MODEL), lambda i: (0, i, 0)),
        out_shape=jax.ShapeDtypeStruct((nb, nt * tt, D_MODEL), F32),
        scratch_shapes=[pltpu.VMEM((2, nb, tt, D_MODEL), F32),
                        pltpu.SemaphoreType.DMA((2,))],
        compiler_params=_cparams(("arbitrary",)),
        name="moe_combine",
    )(dest, dest, x1, nf, ys)


def _block_diag(w):
    per = RNN_SUPER // RNN_BLOCK
    w4 = w.reshape(N_SUPER, per, RNN_BLOCK, RNN_BLOCK).astype(BF16)
    rows = [jnp.pad(w4[:, q], ((0, 0), (0, 0), (q * RNN_BLOCK, (per - 1 - q) * RNN_BLOCK))) for q in range(per)]
    return jnp.concatenate(rows, axis=1)


def _mixers(x, meta, conv0, h0, c0, n0, m0, p, w_ready, *, has_head):
    nb, seq, _ = x.shape
    ready = w_ready is not None
    n_null = N_NULL if has_head else 0
    tt = CHUNK if has_head else seq
    tt_mm = MM_TILES * CHUNK if has_head else seq
    xn, gif = _prep(x, meta, p['norm1_w'], p['w_in'], p['b_gif'], has_head=has_head)
    w_bf16 = {}

    def in_proj(key, w, col0, ncols, tn, out_dtype, gelu=False):
        if ready:
            return _mm(xn, w_ready[key], col0=0, ncols=ncols, tn=tn, tt=tt_mm, out_dtype=out_dtype, gelu=gelu,
                       ready=True, name="in_" + key)
        out, w_bf16[key] = _mm(xn, w, col0=col0, ncols=ncols, tn=tn, tt=tt_mm, out_dtype=out_dtype, gelu=gelu,
                               name="in_" + key)
        return out

    xr = in_proj('xr', p['w_in'], 0, D_RNN, 512, F32)
    gr = in_proj('gr', p['w_in'], D_RNN, D_RNN, 512, BF16, gelu=True)
    qkvo = in_proj('qkvo', p['w_in'], 2 * D_RNN, QKVO, 1024, BF16)
    mg = in_proj('mg', p['w_in'], MG_OFF, 2 * D_MODEL, 1024, BF16)

    tail0 = jnp.transpose(conv0, (1, 0, 2)).reshape((CONV_W - 1) * nb, D_RNN)
    y_r, h_new, tail = _rglru(xr, gr, tail0, h0, p['conv_w'], p['conv_b'], p['rg_a_b'], p['rg_x_b'],
                              p['rg_lambda'], p['wa_bd'], p['wx_bd'], tt=tt, n_null=n_null)
    conv_new = jnp.transpose(tail.reshape(CONV_W - 1, nb, D_RNN), (1, 0, 2))

    y_m, c_new, n_new, m_new = _mlstm_fused(qkvo, gif, c0, n0, m0, p['mlstm_norm_w'], L=tt, n_null=n_null,
                                            nbk=MLSTM_NBK)

    if ready:
        mix = _merge(y_r, y_m, w_ready['proj_rnn'], w_ready['proj_mlstm'], mg, tn=512, tt=tt, ready=True)
    else:
        mix, w_bf16['proj_rnn'], w_bf16['proj_mlstm'] = _merge(y_r, y_m, p['w_proj_rnn'], p['w_proj_mlstm'], mg,
                                                               tn=512, tt=tt)
    states = (conv_new[None], h_new[None], c_new[None], n_new[None], m_new[None])
    return mix, states, (w_ready if ready else w_bf16)


def _ffn(mix, x, meta, p, expert_w, *, has_head):
    tt = CHUNK if has_head else x.shape[1]
    routed = has_head
    x1, xn2, gates = _outproj(mix, p['w_out'], x, meta, p['norm2_w'], p['w_router'], p['b_router'],
                              tt=tt, has_head=has_head, routed=routed)
    if routed:
        skip = 1
        dest, tinfo = _plan(gates, tt=tt, skip=skip)
        xs = _dispatch(tinfo, dest, xn2, tt=tt, skip=skip)
        ys = _experts(tinfo, xs, *expert_w)
        return _combine(dest, x1, p['norm_f_w'], ys, tt=tt, skip=skip), expert_w
    y, *expert_w_bf16 = _moe(xn2, gates, x1, *expert_w, p['norm_f_w'])
    return y, tuple(expert_w_bf16)


def kernel(x_prompt, x_sample, state_rglru_conv, state_rglru_h, state_mlstm_C, state_mlstm_n, state_mlstm_m, meta_tokens, norm1_w, w_in, b_gates, conv_w, conv_b, rg_a_w, rg_a_b, rg_x_w, rg_x_b, rg_lambda, mlstm_norm_w, w_proj_rnn, w_proj_mlstm, w_out, norm2_w, w_router_group, b_router_group, w_router_expert, b_router_expert, w_exp_gate, w_exp_up, w_exp_down, norm_f_w):
    l = 0
    w_in_t = jnp.swapaxes(w_in[l], 0, 1)
    pad_r = LANES - N_GROUPS - N_EXPERTS
    p = dict(
        norm1_w=norm1_w[l][None], w_in=w_in_t,
        b_gif=jnp.pad(b_gates[l], (0, LANES - 2 * N_HEADS_M))[None],
        conv_w=conv_w[l], conv_b=conv_b[l][None], rg_a_b=rg_a_b[l][None], rg_x_b=rg_x_b[l][None],
        rg_lambda=rg_lambda[l][None], wa_bd=_block_diag(rg_a_w[l]), wx_bd=_block_diag(rg_x_w[l]),
        mlstm_norm_w=mlstm_norm_w[l][None],
        w_proj_rnn=w_proj_rnn[l], w_proj_mlstm=w_proj_mlstm[l], w_out=w_out[l].astype(BF16),
        norm2_w=norm2_w[l][None],
        w_router=jnp.pad(jnp.concatenate([w_router_group[l], w_router_expert[l]], axis=1),
                         ((0, 0), (0, pad_r))).astype(BF16),
        b_router=jnp.pad(jnp.concatenate([b_router_group[l], b_router_expert[l]]), (0, pad_r))[None],
        norm_f_w=norm_f_w[None],
    )
    nbp = x_prompt.shape[0]
    dt = x_prompt.dtype
    mix_p, st_p, w_bf16 = _mixers(
        x_prompt, meta_tokens,
        jnp.zeros((nbp, CONV_W - 1, D_RNN), dt), jnp.zeros((nbp, D_RNN), F32),
        jnp.zeros((nbp, N_HEADS_M, DQK_M, DV_M), F32), jnp.zeros((nbp, N_HEADS_M, DQK_M), F32),
        jnp.zeros((nbp, N_HEADS_M), F32), p, None, has_head=True)
    mix_s, st_s, _ = _mixers(
        x_sample, meta_tokens, state_rglru_conv[l], state_rglru_h[l], state_mlstm_C[l],
        state_mlstm_n[l], state_mlstm_m[l], p, w_bf16, has_head=False)
    y_s, expert_w_bf16 = _ffn(mix_s, x_sample, meta_tokens, p, (w_exp_gate[l], w_exp_up[l], w_exp_down[l]),
                              has_head=False)
    y_p, _ = _ffn(mix_p, x_prompt, meta_tokens, p, expert_w_bf16, has_head=True)
    return (y_p, y_s) + st_p + st_s
```

```python
import functools

import jax
import jax.numpy as jnp
from jax import lax
from jax.experimental import pallas as pl
from jax.experimental.pallas import tpu as pltpu

F32 = jnp.float32
BF16 = jnp.bfloat16

D_MODEL = 2048
CHUNK = 64
N_META = 16
N_NULL = CHUNK - N_META
D_RNN = 2560
N_RNN_BLOCKS = 16
RNN_BLOCK = D_RNN // N_RNN_BLOCKS
RNN_SUPER = 640
N_SUPER = D_RNN // RNN_SUPER
CONV_W = 4
LRU_C = 8.0
N_HEADS_M = 8
DV_M = D_MODEL // N_HEADS_M
DQK_M = DV_M // 2
HQK = N_HEADS_M * DQK_M
HV = N_HEADS_M * DV_M
QKVO = 2 * HQK + 2 * HV
N_GROUPS = 4
EXPERTS_PER_GROUP = 4
N_EXPERTS = 16
D_EXPERT = 512
PAIRS_PER_GROUP = 6
N_CLASSES = N_GROUPS * PAIRS_PER_GROUP
MM_TILES = 3
MLSTM_NBK = 2
TM_E = 256
ROW_W = D_MODEL + 128
EPS = 1e-6
LANES = 128
GIF_OFF = 2 * D_RNN + QKVO
MG_OFF = GIF_OFF + 2 * N_HEADS_M
VMEM_LIMIT = 56 * 1024 * 1024
NEG_BIG = -1e30


def _cparams(sem):
    return pltpu.CompilerParams(dimension_semantics=sem, vmem_limit_bytes=VMEM_LIMIT)


def _rms(x, w):
    return x * lax.rsqrt(jnp.mean(x * x, axis=-1, keepdims=True) + EPS) * w


def _softplus(x):
    return jnp.maximum(x, 0.0) + jnp.log1p(jnp.exp(-jnp.abs(x)))


def _head_rows(meta_ref, nb):
    head = jnp.concatenate([jnp.zeros((N_NULL, D_MODEL), F32), meta_ref[...]], axis=0)
    return jnp.broadcast_to(head[None], (nb, CHUNK, D_MODEL)).reshape(nb * CHUNK, D_MODEL)


def _x_map(has_head):
    if has_head:
        return lambda ti: (0, jnp.maximum(ti - 1, 0), 0)
    return lambda ti: (0, ti, 0)


def _prep_kernel(x_ref, meta_ref, nw_ref, wg_ref, bg_ref, xn_ref, gif_ref, *, nb, tt, has_head):
    def emit(x):
        xnb = _rms(x, nw_ref[...]).astype(BF16)
        xn_ref[...] = xnb.reshape(nb, tt, D_MODEL)
        gif = lax.dot_general(xnb, wg_ref[...].astype(BF16), (((1,), (1,)), ((), ())),
                              preferred_element_type=F32) + bg_ref[...]
        gif_ref[...] = gif.reshape(nb, tt, LANES)

    if has_head:
        @pl.when(pl.program_id(0) == 0)
        def _():
            emit(_head_rows(meta_ref, nb))

        @pl.when(pl.program_id(0) > 0)
        def _():
            emit(x_ref[...].reshape(nb * tt, D_MODEL))
    else:
        emit(x_ref[...].reshape(nb * tt, D_MODEL))


def _prep(x, meta, norm_w, w_in_t, b_gif, *, has_head):
    nb, seq, _ = x.shape
    assert GIF_OFF % LANES == 0
    tt = CHUNK if has_head else seq
    n_tiles = seq // tt + (1 if has_head else 0)
    tl = n_tiles * tt
    const = lambda ti: (0, 0)
    return pl.pallas_call(
        functools.partial(_prep_kernel, nb=nb, tt=tt, has_head=has_head),
        grid=(n_tiles,),
        in_specs=[pl.BlockSpec((nb, tt, D_MODEL), _x_map(has_head)),
                  pl.BlockSpec((N_META, D_MODEL), const),
                  pl.BlockSpec((1, D_MODEL), const),
                  pl.BlockSpec((LANES, D_MODEL), lambda ti: (GIF_OFF // LANES, 0)),
                  pl.BlockSpec((1, LANES), const)],
        out_specs=[pl.BlockSpec((nb, tt, D_MODEL), lambda ti: (0, ti, 0)),
                   pl.BlockSpec((nb, tt, LANES), lambda ti: (0, ti, 0))],
        out_shape=[jax.ShapeDtypeStruct((nb, tl, D_MODEL), BF16),
                   jax.ShapeDtypeStruct((nb, tl, LANES), F32)],
        compiler_params=_cparams(("arbitrary",)),
        name="prep",
    )(x, meta, norm_w, w_in_t, b_gif)


def _gelu_tanh(g):
    return g * (0.5 * (1.0 + jnp.tanh(0.7978845608028654 * (g + 0.044715 * (g * g * g)))))


def _mm_kernel(lhs_ref, w_ref, out_ref, *wbf_ref, gelu):
    nb, tt, k = lhs_ref.shape
    tn = out_ref.shape[-1]
    if wbf_ref:
        w_bf = wbf_ref[0]

        @pl.when(pl.program_id(1) == 0)
        def _():
            w_bf[...] = w_ref[...].T.astype(BF16)
    else:
        w_bf = w_ref

    acc = jnp.dot(lhs_ref[...].reshape(nb * tt, k), w_bf[...], preferred_element_type=F32)
    if gelu:
        acc = _gelu_tanh(acc)
    out_ref[...] = acc.astype(out_ref.dtype).reshape(nb, tt, tn)


def _mm(lhs, w, *, col0, ncols, tn, tt, out_dtype, gelu=False, ready=False, name):
    nb, tl, k = lhs.shape
    assert tl % tt == 0 and ncols % tn == 0 and col0 % 8 == 0
    lhs_spec = pl.BlockSpec((nb, tt, k), lambda j, i: (0, i, 0))
    out_spec = pl.BlockSpec((nb, tt, tn), lambda j, i: (0, i, j))
    out_shape = jax.ShapeDtypeStruct((nb, tl, ncols), out_dtype)
    wbf_spec = pl.BlockSpec((k, tn), lambda j, i: (0, j))
    if ready:
        in_specs, out_specs, out_shapes = [lhs_spec, wbf_spec], out_spec, out_shape
    else:
        in_specs = [lhs_spec, pl.BlockSpec((pl.Element(tn), pl.Element(k)),
                                           lambda j, i: (pl.multiple_of(col0 + j * tn, 8), 0))]
        out_specs = [out_spec, wbf_spec]
        out_shapes = [out_shape, jax.ShapeDtypeStruct((k, ncols), BF16)]
    return pl.pallas_call(
        functools.partial(_mm_kernel, gelu=gelu),
        grid=(ncols // tn, tl // tt),
        in_specs=in_specs,
        out_specs=out_specs,
        out_shape=out_shapes,
        compiler_params=_cparams(("arbitrary", "arbitrary")),
        name=name,
    )(lhs, w)


def _merge_kernel(yr_ref, ym_ref, wr_ref, wm_ref, gr_ref, gm_ref, out_ref, *wbf_refs):
    nb, tt, tn = out_ref.shape
    rows = nb * tt
    if wbf_refs:
        wr_bf, wm_bf = wbf_refs

        @pl.when(pl.program_id(1) == 0)
        def _():
            wr_bf[...] = wr_ref[...].astype(BF16)
            wm_bf[...] = wm_ref[...].astype(BF16)
    else:
        wr_bf, wm_bf = wr_ref, wm_ref

    pr = jnp.dot(yr_ref[...].reshape(rows, D_RNN), wr_bf[...], preferred_element_type=F32)
    pm = jnp.dot(ym_ref[...].reshape(rows, HV), wm_bf[...], preferred_element_type=F32)
    mix = (jax.nn.sigmoid(gr_ref[...].reshape(rows, tn).astype(F32)) * pr
           + jax.nn.sigmoid(gm_ref[...].reshape(rows, tn).astype(F32)) * pm)
    out_ref[...] = mix.astype(BF16).reshape(nb, tt, tn)


def _merge(y_r, y_m, w_r, w_m, mg, *, tn, tt, ready=False):
    nb, tl, _ = y_r.shape
    nj = D_MODEL // tn
    wr_spec = pl.BlockSpec((D_RNN, tn), lambda j, i: (0, j))
    wm_spec = pl.BlockSpec((HV, tn), lambda j, i: (0, j))
    out_specs = pl.BlockSpec((nb, tt, tn), lambda j, i: (0, i, j))
    out_shape = jax.ShapeDtypeStruct((nb, tl, D_MODEL), BF16)
    if not ready:
        out_specs = [out_specs, wr_spec, wm_spec]
        out_shape = [out_shape, jax.ShapeDtypeStruct(w_r.shape, BF16), jax.ShapeDtypeStruct(w_m.shape, BF16)]
    return pl.pallas_call(
        _merge_kernel,
        grid=(nj, tl // tt),
        in_specs=[pl.BlockSpec((nb, tt, D_RNN), lambda j, i: (0, i, 0)),
                  pl.BlockSpec((nb, tt, HV), lambda j, i: (0, i, 0)),
                  wr_spec, wm_spec,
                  pl.BlockSpec((nb, tt, tn), lambda j, i: (0, i, j)),
                  pl.BlockSpec((nb, tt, tn), lambda j, i: (0, i, nj + j))],
        out_specs=out_specs,
        out_shape=out_shape,
        compiler_params=_cparams(("arbitrary", "arbitrary")),
        name="merge",
    )(y_r, y_m, w_r, w_m, mg, mg)


def _rglru_kernel(xr_ref, gr_ref, tail0_ref, h0_ref, cw_ref, cb_ref, ab_ref, xb_ref, lam_ref,
                  wa_ref, wx_ref, y_ref, hl_ref, tail_ref, xp_s, h_s, a_s, u_s, hs_s, *, nb, tt, n_null):
    ti = pl.program_id(1)
    rt = tt * nb
    tb = (CONV_W - 1) * nb

    def time_major(x):
        return jnp.swapaxes(x, 0, 1).reshape(rt, x.shape[-1])

    @pl.when(ti == 0)
    def _():
        xp_s[0:tb, :] = tail0_ref[...]
        h_s[...] = h0_ref[...]

    xp_s[tb:tb + rt, :] = time_major(xr_ref[...])
    cw = cw_ref[...]
    xc = cb_ref[...] + cw[0:1, :] * xp_s[0:rt, :]
    for j in range(1, CONV_W):
        xc = xc + cw[j:j + 1, :] * xp_s[j * nb:j * nb + rt, :]
    xcb = xc.astype(BF16)
    r = jax.nn.sigmoid(jnp.dot(xcb, wa_ref[0], preferred_element_type=F32) + ab_ref[...])
    i = jax.nn.sigmoid(jnp.dot(xcb, wx_ref[0], preferred_element_type=F32) + xb_ref[...])
    log_a = r * ((-LRU_C) * _softplus(-lam_ref[...]))
    a = jnp.exp(log_a)
    a_s[...] = a
    u = jnp.sqrt(-jnp.tanh(log_a) * (a * a + 1.0)) * (i * xc)
    u_s[...] = u
    if n_null:
        @pl.when(ti == 0)
        def _():
            a_s[n_null * nb:(n_null + 1) * nb, :] = jnp.zeros((nb, a_s.shape[-1]), F32)

    def step(t, h):
        o = pl.multiple_of(t * nb, nb)
        h = a_s[pl.ds(o, nb), :] * h + u_s[pl.ds(o, nb), :]
        hs_s[pl.ds(o, nb), :] = h
        return h

    h = lax.fori_loop(0, tt, step, h_s[...], unroll=8)
    h_s[...] = h
    h_bm = jnp.swapaxes(hs_s[...].reshape(tt, nb, hs_s.shape[-1]), 0, 1)
    y_ref[...] = (gr_ref[...].astype(F32) * h_bm).astype(BF16)
    xp_s[0:tb, :] = xp_s[rt:rt + tb, :]

    @pl.when(ti == pl.num_programs(1) - 1)
    def _():
        hl_ref[...] = h
        tail_ref[...] = xp_s[rt:rt + tb, :]


def _rglru(xr, gr, tail0, h0, cw, cb, ab, xb, lam, wa, wx, *, tt, n_null):
    nb, tl, _ = xr.shape
    rt = tt * nb
    tb = (CONV_W - 1) * nb
    blk = lambda s, ti: (0, ti, s)
    col = lambda s, ti: (0, s)
    return pl.pallas_call(
        functools.partial(_rglru_kernel, nb=nb, tt=tt, n_null=n_null),
        grid=(N_SUPER, tl // tt),
        in_specs=[pl.BlockSpec((nb, tt, RNN_SUPER), blk),
                  pl.BlockSpec((nb, tt, RNN_SUPER), blk),
                  pl.BlockSpec((tb, RNN_SUPER), col),
                  pl.BlockSpec((nb, RNN_SUPER), col),
                  pl.BlockSpec((CONV_W, RNN_SUPER), col),
                  pl.BlockSpec((1, RNN_SUPER), col),
                  pl.BlockSpec((1, RNN_SUPER), col),
                  pl.BlockSpec((1, RNN_SUPER), col),
                  pl.BlockSpec((1, RNN_SUPER), col),
                  pl.BlockSpec((1, RNN_SUPER, RNN_SUPER), lambda s, ti: (s, 0, 0)),
                  pl.BlockSpec((1, RNN_SUPER, RNN_SUPER), lambda s, ti: (s, 0, 0))],
        out_specs=[pl.BlockSpec((nb, tt, RNN_SUPER), blk),
                   pl.BlockSpec((nb, RNN_SUPER), col),
                   pl.BlockSpec((tb, RNN_SUPER), col)],
        out_shape=[jax.ShapeDtypeStruct((nb, tl, D_RNN), BF16),
                   jax.ShapeDtypeStruct((nb, D_RNN), F32),
                   jax.ShapeDtypeStruct((tb, D_RNN), F32)],
        scratch_shapes=[pltpu.VMEM((rt + tb, RNN_SUPER), F32),
                        pltpu.VMEM((nb, RNN_SUPER), F32),
                        pltpu.VMEM((rt, RNN_SUPER), F32),
                        pltpu.VMEM((rt, RNN_SUPER), F32),
                        pltpu.VMEM((rt, RNN_SUPER), F32)],
        compiler_params=_cparams(("arbitrary", "arbitrary")),
        name="rglru",
    )(xr, gr, tail0, h0, cw, cb, ab, xb, lam, wa, wx)


DVA = DV_M + 128


def _mlstm_gates(c, gif_ref, m_s, *, L, n_null):
    hm = N_HEADS_M
    g = gif_ref[...]
    if L < LANES:
        g = jnp.concatenate([g, jnp.zeros((LANES - L, LANES), F32)], axis=0)
    g_t = g.T
    lane = lax.broadcasted_iota(jnp.int32, (hm, LANES), 1)
    valid = lane < L
    li = g_t[0:hm, :]
    lfr = g_t[hm:2 * hm, :]
    lf = jnp.where(valid, jnp.minimum(lfr, 0.0) - jnp.log1p(jnp.exp(-jnp.abs(lfr))), 0.0)
    if n_null:
        null = (lane < n_null) & (c == 0)
        lf = jnp.where(null, 0.0, lf)
        li = jnp.where(null, NEG_BIG, li)
    r128 = lax.broadcasted_iota(jnp.int32, (LANES, LANES), 0)
    c128 = lax.broadcasted_iota(jnp.int32, (LANES, LANES), 1)
    upper = jnp.where(r128 <= c128, 1.0, 0.0)
    b = jnp.dot(lf, upper, preferred_element_type=F32, precision=lax.Precision.HIGHEST)
    a = li - b
    cm = jnp.where(valid, a, -jnp.inf)
    sh = 1
    while sh < L:
        cm = jnp.maximum(cm, jnp.where(lane >= sh, pltpu.roll(cm, sh, axis=1), -jnp.inf))
        sh *= 2
    m0 = m_s[...]
    big_m = jnp.maximum(cm, m0)
    m_last = jnp.max(jnp.where(valid, big_m, -jnp.inf), axis=1, keepdims=True)
    b_last = jnp.sum(lf, axis=1, keepdims=True)
    scale = DQK_M ** -0.5
    e_mv = jnp.exp(-(b + big_m))
    w_int = jnp.exp(m0 - big_m)
    w_end = jnp.exp(a - m_last) * scale
    decay = jnp.exp(m0 - m_last)
    m_s[...] = jnp.broadcast_to(b_last + m_last, (hm, LANES))
    cols = jnp.concatenate([big_m, e_mv, w_int, w_end, jnp.zeros((LANES - 4 * hm, LANES), F32)], axis=0).T
    return a, cols, decay


def _mlstm_heads(gates, q_ref, k_ref, v_ref, og_ref, nw_ref, y_ref, c_s, *, L):
    hm = N_HEADS_M
    scale = DQK_M ** -0.5
    rr = lax.broadcasted_iota(jnp.int32, (L, L), 0)
    cc = lax.broadcasted_iota(jnp.int32, (L, L), 1)
    causal = rr >= cc
    ones_col = jnp.where(lax.broadcasted_iota(jnp.int32, (L, 128), 1) == 0, 1.0, 0.0).astype(BF16)
    nw = nw_ref[...]
    dn_nt = (((1,), (1,)), ((), ()))
    dn_tn = (((0,), (0,)), ((), ()))
    bh = [(bi, h) for bi in range(len(gates)) for h in range(hm)]
    col = lambda bi, k, h: gates[bi][1][0:L, k * hm + h:k * hm + h + 1]
    qs = [q_ref[bi, :, h * DQK_M:(h + 1) * DQK_M] for bi, h in bh]
    ks = [k_ref[bi, :, h * DQK_M:(h + 1) * DQK_M] for bi, h in bh]
    v_augs = [jnp.concatenate([v_ref[bi, :, h * DV_M:(h + 1) * DV_M], ones_col], axis=1) for bi, h in bh]
    c_augs = [c_s[bi, h] for bi, h in bh]
    s_raw = [lax.dot_general(q, k, dn_nt, preferred_element_type=F32) for q, k in zip(qs, ks)]
    inter = [jnp.dot((qs[i].astype(F32) * col(bi, 2, h)).astype(BF16), c_augs[i].astype(BF16),
                     preferred_element_type=F32) for i, (bi, h) in enumerate(bh)]
    upd = [lax.dot_general((ks[i].astype(F32) * col(bi, 3, h)).astype(BF16), v_augs[i], dn_tn,
                           preferred_element_type=F32) for i, (bi, h) in enumerate(bh)]
    s_w = [(s_raw[i] * (scale * jnp.where(causal, jnp.exp(gates[bi][0][h:h + 1, 0:L] - col(bi, 0, h)), 0.0))
            ).astype(BF16) for i, (bi, h) in enumerate(bh)]
    res = [jnp.dot(s_w[i], v_augs[i], preferred_element_type=F32) + inter[i] for i in range(len(bh))]
    for i, (bi, h) in enumerate(bh):
        c_s[bi, h] = gates[bi][2][h:h + 1, 0:1] * c_augs[i] + upd[i]
    for i, (bi, h) in enumerate(bh):
        num = res[i][:, :DV_M]
        denom = jnp.maximum(jnp.abs(res[i][:, DV_M:DV_M + 1]), col(bi, 1, h))
        hh = num / denom
        hh = hh * lax.rsqrt(jnp.mean(hh * hh, axis=-1, keepdims=True) + EPS)
        hh = hh * nw[:, h * DV_M:(h + 1) * DV_M]
        og = og_ref[bi, :, h * DV_M:(h + 1) * DV_M].astype(F32)
        y_ref[bi, :, h * DV_M:(h + 1) * DV_M] = (hh * jax.nn.sigmoid(og)).astype(BF16)


def _mlstm_fused_kernel(q_ref, k_ref, v_ref, og_ref, gif_ref, c0_ref, n0_ref, m0_ref, nw_ref,
                        y_ref, co_ref, no_ref, mo_ref, c_s, m_s, *, L, n_null, nbk):
    c = pl.program_id(1)
    bh = [(bi, h) for bi in range(nbk) for h in range(N_HEADS_M)]

    @pl.when(c == 0)
    def _():
        m_s[...] = m0_ref[...]
        for bi, h in bh:
            c_s[bi, h, :, :DV_M] = c0_ref[bi, h]
            n_tile = jnp.concatenate([n0_ref[bi, h:h + 1, :], jnp.zeros((DQK_M - 1, DQK_M), F32)], axis=0)
            c_s[bi, h, :, DV_M:] = n_tile.T

    gates = [_mlstm_gates(c, gif_ref.at[bi], m_s.at[bi], L=L, n_null=n_null) for bi in range(nbk)]
    _mlstm_heads(gates, q_ref, k_ref, v_ref, og_ref, nw_ref, y_ref, c_s, L=L)

    @pl.when(c == pl.num_programs(1) - 1)
    def _():
        mo_ref[...] = m_s[...]
        for bi, h in bh:
            co_ref[bi, h] = c_s[bi, h, :, :DV_M]
            no_ref[bi, h:h + 1, :] = c_s[bi, h, :, DV_M:].T[0:1, :]


def _mlstm_fused(qkvo, gif, c0, n0, m0, nw, *, L, n_null, nbk):
    nb, tl, _ = qkvo.shape
    assert nb % nbk == 0
    hm = N_HEADS_M
    assert DQK_M == LANES and DVA - DV_M == DQK_M
    m_rep0 = jnp.broadcast_to(m0[..., None], (nb, hm, LANES))
    st4 = lambda b, c: (b, 0, 0, 0)
    st3 = lambda b, c: (b, 0, 0)
    y, c_new, n_new, m_rep = pl.pallas_call(
        functools.partial(_mlstm_fused_kernel, L=L, n_null=n_null, nbk=nbk),
        grid=(nb // nbk, tl // L),
        in_specs=[pl.BlockSpec((nbk, L, HQK), lambda b, c: (b, c, 0)),
                  pl.BlockSpec((nbk, L, HQK), lambda b, c: (b, c, 1)),
                  pl.BlockSpec((nbk, L, HV), lambda b, c: (b, c, 1)),
                  pl.BlockSpec((nbk, L, HV), lambda b, c: (b, c, 2)),
                  pl.BlockSpec((nbk, L, LANES), lambda b, c: (b, c, 0)),
                  pl.BlockSpec((nbk, hm, DQK_M, DV_M), st4),
                  pl.BlockSpec((nbk, hm, DQK_M), st3),
                  pl.BlockSpec((nbk, hm, LANES), st3),
                  pl.BlockSpec((1, HV), lambda b, c: (0, 0))],
        out_specs=[pl.BlockSpec((nbk, L, HV), lambda b, c: (b, c, 0)),
                   pl.BlockSpec((nbk, hm, DQK_M, DV_M), st4),
                   pl.BlockSpec((nbk, hm, DQK_M), st3),
                   pl.BlockSpec((nbk, hm, LANES), st3)],
        out_shape=[jax.ShapeDtypeStruct((nb, tl, HV), BF16),
                   jax.ShapeDtypeStruct((nb, hm, DQK_M, DV_M), F32),
                   jax.ShapeDtypeStruct((nb, hm, DQK_M), F32),
                   jax.ShapeDtypeStruct((nb, hm, LANES), F32)],
        scratch_shapes=[pltpu.VMEM((nbk, hm, DQK_M, DVA), F32),
                        pltpu.VMEM((nbk, hm, LANES), F32)],
        compiler_params=_cparams(("arbitrary", "arbitrary")),
        name="mlstm",
    )(qkvo, qkvo, qkvo, qkvo, gif, c0, n0, m_rep0, nw)
    return y, c_new, n_new, m_rep[..., 0]


def _route(logits, dense):
    tm = logits.shape[0]
    lane = lax.broadcasted_iota(jnp.int32, (tm, LANES), 1)
    lanef = lane.astype(F32)
    gl = jnp.where(lane < N_GROUPS, logits, -jnp.inf)
    gmax = jnp.max(gl, axis=1, keepdims=True)
    gidx = jnp.min(jnp.where(gl == gmax, lanef, float(LANES)), axis=1, keepdims=True)
    gw = 1.0 / jnp.sum(jnp.exp(gl - gmax), axis=1, keepdims=True)
    lo = float(N_GROUPS) + float(EXPERTS_PER_GROUP) * gidx
    sel = (lanef >= lo) & (lanef < lo + float(EXPERTS_PER_GROUP))
    ev = jnp.where(sel, logits, -jnp.inf)
    v1 = jnp.max(ev, axis=1, keepdims=True)
    i1 = jnp.min(jnp.where(ev == v1, lanef, float(LANES)), axis=1, keepdims=True)
    ev2 = jnp.where(lanef == i1, -jnp.inf, ev)
    v2 = jnp.max(ev2, axis=1, keepdims=True)
    i2 = jnp.min(jnp.where(ev2 == v2, lanef, float(LANES)), axis=1, keepdims=True)
    t = jnp.exp(v2 - v1)
    w1 = gw / (1.0 + t)
    w2 = gw * (t / (1.0 + t))
    if dense:
        return jnp.where(lanef == i1, w1, jnp.where(lanef == i2, w2, 0.0))
    l1 = i1 - lo
    l2 = i2 - lo
    first = l1 < l2
    a = jnp.minimum(l1, l2)
    b = jnp.maximum(l1, l2)
    pair = jnp.where(a == 0.0, b - 1.0, jnp.where(a == 1.0, b + 1.0, 5.0))
    cls = float(PAIRS_PER_GROUP) * gidx + pair
    return jnp.where(lane == 0, cls,
                     jnp.where(lane == 1, jnp.where(first, w1, w2),
                               jnp.where(lane == 2, jnp.where(first, w2, w1), 0.0)))


def _outproj_kernel(mix_ref, w_ref, x_ref, meta_ref, n2_ref, wr_ref, br_ref, x1_ref, xn2_ref, gates_ref,
                    *, nb, tt, has_head, routed):
    def emit(x):
        mix = mix_ref[...].reshape(nb * tt, D_MODEL)
        x1 = x + jnp.dot(mix, w_ref[...], preferred_element_type=F32)
        x1_ref[...] = x1.reshape(nb, tt, D_MODEL)
        xn2 = _rms(x1, n2_ref[...]).astype(BF16)
        logits = jnp.dot(xn2, wr_ref[...], preferred_element_type=F32) + br_ref[...]
        info = _route(logits, dense=not routed)
        gates_ref[...] = info.reshape(nb, tt, LANES)
        if routed:
            xn2_ref[:, :, :D_MODEL] = xn2.astype(F32).reshape(nb, tt, D_MODEL)
            xn2_ref[:, :, D_MODEL:] = info.reshape(nb, tt, LANES)
        else:
            xn2_ref[...] = xn2.reshape(nb, tt, D_MODEL)

    if has_head:
        @pl.when(pl.program_id(0) == 0)
        def _():
            emit(_head_rows(meta_ref, nb))

        @pl.when(pl.program_id(0) > 0)
        def _():
            emit(x_ref[...].reshape(nb * tt, D_MODEL))
    else:
        emit(x_ref[...].reshape(nb * tt, D_MODEL))


def _outproj(mix, w_out, x, meta, n2, w_router, b_router, *, tt, has_head, routed):
    nb, tl, _ = mix.shape
    const = lambda i: (0, 0)
    blk = lambda i: (0, i, 0)
    xn2_w, xn2_dt = (ROW_W, F32) if routed else (D_MODEL, BF16)
    return pl.pallas_call(
        functools.partial(_outproj_kernel, nb=nb, tt=tt, has_head=has_head, routed=routed),
        grid=(tl // tt,),
        in_specs=[pl.BlockSpec((nb, tt, D_MODEL), blk),
                  pl.BlockSpec((D_MODEL, D_MODEL), const, pipeline_mode=pl.Buffered(1)),
                  pl.BlockSpec((nb, tt, D_MODEL), _x_map(has_head)),
                  pl.BlockSpec((N_META, D_MODEL), const),
                  pl.BlockSpec((1, D_MODEL), const),
                  pl.BlockSpec((D_MODEL, LANES), const),
                  pl.BlockSpec((1, LANES), const)],
        out_specs=[pl.BlockSpec((nb, tt, D_MODEL), blk),
                   pl.BlockSpec((nb, tt, xn2_w), blk),
                   pl.BlockSpec((nb, tt, LANES), blk)],
        out_shape=[jax.ShapeDtypeStruct((nb, tl, D_MODEL), F32),
                   jax.ShapeDtypeStruct((nb, tl, xn2_w), xn2_dt),
                   jax.ShapeDtypeStruct((nb, tl, LANES), F32)],
        compiler_params=_cparams(("arbitrary",)),
        name="outproj",
    )(mix, w_out, x, meta, n2, w_router, b_router)


def _moe_kernel(xn_ref, gates_ref, x1_ref, wg_ref, wu_ref, wd_ref, nf_ref, y_ref, wgb_ref, wub_ref, wdb_ref,
                acc_s, *, nb, tt):
    e = pl.program_id(0)
    hf = pl.program_id(1)
    rows = nb * tt

    @pl.when((e == 0) & (hf == 0))
    def _():
        acc_s[...] = jnp.zeros_like(acc_s)

    wg = wg_ref[0].astype(BF16)
    wu = wu_ref[0].astype(BF16)
    wd = wd_ref[0].astype(BF16)
    wgb_ref[0] = wg
    wub_ref[0] = wu
    wdb_ref[0] = wd
    x = xn_ref[...].reshape(rows, D_MODEL)
    hg = jnp.dot(x, wg, preferred_element_type=F32)
    hu = jnp.dot(x, wu, preferred_element_type=F32)
    hdn = (hg * jax.nn.sigmoid(hg)) * hu
    yd = jnp.dot(hdn.astype(BF16), wd, preferred_element_type=F32)
    gates = gates_ref[...].reshape(rows, LANES)
    lane = lax.broadcasted_iota(jnp.int32, gates.shape, 1)
    gcol = jnp.sum(jnp.where(lane == e + N_GROUPS, gates, 0.0), axis=1, keepdims=True)
    acc_s[...] += gcol * yd

    @pl.when((e == pl.num_programs(0) - 1) & (hf == pl.num_programs(1) - 1))
    def _():
        x2 = x1_ref[...].reshape(rows, D_MODEL) + acc_s[...]
        y_ref[...] = _rms(x2, nf_ref[...]).reshape(nb, tt, D_MODEL)


def _moe(xn2, gates, x1, wg, wu, wd, nf):
    nb, tt, _ = xn2.shape
    halves = 2
    dh = D_EXPERT // halves
    blk = lambda e, hf: (0, 0, 0)
    up = pl.BlockSpec((1, D_MODEL, dh), lambda e, hf: (e, 0, hf))
    down = pl.BlockSpec((1, dh, D_MODEL), lambda e, hf: (e, hf, 0))
    return pl.pallas_call(
        functools.partial(_moe_kernel, nb=nb, tt=tt),
        grid=(N_EXPERTS, halves),
        in_specs=[pl.BlockSpec((nb, tt, D_MODEL), blk),
                  pl.BlockSpec((nb, tt, LANES), blk),
                  pl.BlockSpec((nb, tt, D_MODEL), blk),
                  up, up, down,
                  pl.BlockSpec((1, D_MODEL), lambda e, hf: (0, 0))],
        out_specs=[pl.BlockSpec((nb, tt, D_MODEL), blk), up, up, down],
        out_shape=[jax.ShapeDtypeStruct((nb, tt, D_MODEL), F32),
                   jax.ShapeDtypeStruct(wg.shape, BF16),
                   jax.ShapeDtypeStruct(wu.shape, BF16),
                   jax.ShapeDtypeStruct(wd.shape, BF16)],
        scratch_shapes=[pltpu.VMEM((nb * tt, D_MODEL), F32)],
        compiler_params=_cparams(("arbitrary", "arbitrary")),
        name="moe",
    )(xn2, gates, x1, wg, wu, wd, nf)


_NT = (((1,), (1,)), ((), ()))


def _plan_kernel(ri_ref, dest_ref, tinfo_ref, cnt_s, carry_s, offs_s, *, rows):
    ph = pl.program_id(0)
    i = pl.program_id(1)
    ri = ri_ref[...].reshape(rows, LANES)
    lanef = lax.broadcasted_iota(jnp.int32, (rows, LANES), 1).astype(F32)
    oh = jnp.where(lanef == ri[:, 0:1], 1.0, 0.0)
    colsum = jnp.sum(oh, axis=0, keepdims=True)
    r128 = lax.broadcasted_iota(jnp.int32, (LANES, LANES), 0)
    c128 = lax.broadcasted_iota(jnp.int32, (LANES, LANES), 1)
    eye = jnp.where(r128 == c128, 1.0, 0.0).astype(BF16)

    @pl.when((ph == 0) & (i == 0))
    def _():
        cnt_s[...] = jnp.zeros_like(cnt_s)

    @pl.when(ph == 0)
    def _():
        cnt_s[...] += colsum

    @pl.when((ph == 1) & (i == 0))
    def _():
        tiles = jnp.floor((cnt_s[...] + float(TM_E - 1)) * (1.0 / TM_E))
        upper = jnp.where(r128 <= c128, 1.0, 0.0).astype(BF16)
        cum_incl = jnp.dot(jnp.broadcast_to(tiles, (8, LANES)).astype(BF16), upper, preferred_element_type=F32)
        offs_s[...] = (cum_incl[0:1, :] - tiles) * float(TM_E)
        carry_s[...] = jnp.zeros_like(carry_s)
        cum_col = lax.dot_general(eye, cum_incl.astype(BF16), _NT, preferred_element_type=F32)
        ended = (cum_col[:, 0:1] <= c128.astype(F32)) & (r128 < N_CLASSES)
        tcls = jnp.sum(jnp.where(ended, 1.0, 0.0), axis=0, keepdims=True)
        total = cum_incl[0:1, LANES - 1:LANES]
        row8 = lax.broadcasted_iota(jnp.int32, (8, LANES), 0)
        tinfo_ref[...] = jnp.where(row8 == 0, tcls, total).astype(jnp.int32)

    @pl.when(ph == 1)
    def _():
        tri = jnp.where(lax.broadcasted_iota(jnp.int32, (rows, rows), 0)
                        > lax.broadcasted_iota(jnp.int32, (rows, rows), 1), 1.0, 0.0).astype(BF16)
        rank = jnp.dot(tri, oh.astype(BF16), preferred_element_type=F32)
        dest = jnp.sum(oh * (rank + (offs_s[...] + carry_s[...])), axis=1, keepdims=True)
        carry_s[...] += colsum
        dhi = jnp.floor(dest * (1.0 / 256.0))
        dlo = dest - 256.0 * dhi
        digits = jnp.where(lanef == 0.0, dlo, jnp.where(lanef == 1.0, dhi, 0.0)).astype(BF16)
        dt = lax.dot_general(eye, digits, _NT, preferred_element_type=F32)
        dest_ref[...] = (dt[0:1, :] + 256.0 * dt[1:2, :]).astype(jnp.int32).reshape(1, 1, rows)


def _plan(rinfo, *, tt, skip):
    nb, tl, _ = rinfo.shape
    rows = nb * tt
    nt = tl // tt - skip
    assert (nt * rows) // TM_E + N_CLASSES <= LANES and rows % LANES == 0
    return pl.pallas_call(
        functools.partial(_plan_kernel, rows=rows),
        grid=(2, nt),
        in_specs=[pl.BlockSpec((nb, tt, LANES), lambda ph, i: (0, i + skip, 0))],
        out_specs=[pl.BlockSpec((1, 1, rows), lambda ph, i: (i * ph, 0, 0)),
                   pl.BlockSpec((8, LANES), lambda ph, i: (0, 0))],
        out_shape=[jax.ShapeDtypeStruct((nt, 1, rows), jnp.int32),
                   jax.ShapeDtypeStruct((8, LANES), jnp.int32)],
        scratch_shapes=[pltpu.VMEM((1, LANES), F32)] * 3,
        compiler_params=_cparams(("arbitrary", "arbitrary")),
        name="moe_plan",
    )(rinfo)


def _dispatch_kernel(tinfo_ref, dest_ref, x_ref, xs_ref, zbuf, zsem, sem, *, nb, tt):
    i = pl.program_id(0)

    @pl.when(i == 0)
    def _():
        zbuf[...] = jnp.zeros_like(zbuf)
        n_used = tinfo_ref[1, 0]

        def fill(start):
            def body(j, carry):
                last = (j >= n_used - 1) | (tinfo_ref[0, jnp.minimum(j + 1, LANES - 1)] != tinfo_ref[0, j])

                @pl.when(last)
                def _():
                    cp = pltpu.make_async_copy(
                        zbuf, xs_ref.at[pl.ds(pl.multiple_of(j * TM_E, TM_E), TM_E), :], zsem)
                    if start:
                        cp.start()
                    else:
                        cp.wait()
                return carry
            lax.fori_loop(0, xs_ref.shape[0] // TM_E, body, 0)

        fill(True)
        fill(False)

    for b in range(nb):
        for t in range(tt):
            d = dest_ref[0, 0, b * tt + t]
            pltpu.make_async_copy(x_ref.at[b, pl.ds(t, 1), :], xs_ref.at[pl.ds(d, 1), :], sem).start()
    whole = xs_ref.at[pl.ds(0, nb * tt), :]
    pltpu.make_async_copy(whole, whole, sem).wait()


def _dispatch(tinfo, dest, xrow, *, tt, skip):
    nb, tl, _ = xrow.shape
    nt = tl // tt - skip
    n_sorted = ((nt * nb * tt) // TM_E + N_CLASSES) * TM_E
    return pl.pallas_call(
        functools.partial(_dispatch_kernel, nb=nb, tt=tt),
        grid_spec=pltpu.PrefetchScalarGridSpec(
            num_scalar_prefetch=1,
            grid=(nt,),
            in_specs=[pl.BlockSpec((1, 1, nb * tt), lambda i, ti: (i, 0, 0), memory_space=pltpu.SMEM),
                      pl.BlockSpec((nb, tt, ROW_W), lambda i, ti: (0, i + skip, 0))],
            out_specs=pl.BlockSpec(memory_space=pl.ANY),
            scratch_shapes=[pltpu.VMEM((TM_E, ROW_W), F32),
                            pltpu.SemaphoreType.DMA(()),
                            pltpu.SemaphoreType.DMA(())]),
        out_shape=jax.ShapeDtypeStruct((n_sorted, ROW_W), F32),
        compiler_params=_cparams(("arbitrary",)),
        name="moe_dispatch",
    )(tinfo, dest, xrow)


def _class_expert(c, hi):
    g = c // PAIRS_PER_GROUP
    p = c - PAIRS_PER_GROUP * g
    ge3 = (p >= 3).astype(jnp.int32)
    ge5 = (p >= 5).astype(jnp.int32)
    local = (p + 1 - 2 * ge3 - ge5) if hi else (ge3 + ge5)
    return EXPERTS_PER_GROUP * g + local


def _experts_kernel(tinfo_ref, xs_ref, wga_ref, wua_ref, wda_ref, wgb_ref, wub_ref, wdb_ref, ys_ref):
    @pl.when(pl.program_id(0) < tinfo_ref[1, 0])
    def _():
        x = xs_ref[:, :D_MODEL].astype(BF16)

        def ffn(wg_ref, wu_ref, wd_ref):
            hg = jnp.dot(x, wg_ref[0], preferred_element_type=F32)
            hu = jnp.dot(x, wu_ref[0], preferred_element_type=F32)
            hdn = (hg * jax.nn.sigmoid(hg)) * hu
            return jnp.dot(hdn.astype(BF16), wd_ref[0], preferred_element_type=F32)

        ys_ref[...] = (xs_ref[:, D_MODEL + 1:D_MODEL + 2] * ffn(wga_ref, wua_ref, wda_ref)
                       + xs_ref[:, D_MODEL + 2:D_MODEL + 3] * ffn(wgb_ref, wub_ref, wdb_ref))

    @pl.when(pl.program_id(0) >= tinfo_ref[1, 0])
    def _():
        ys_ref[...] = jnp.zeros_like(ys_ref)


def _experts(tinfo, xs, wg, wu, wd):
    n_sorted = xs.shape[0]
    nt = n_sorted // TM_E

    def tile(j, ti):
        return jnp.minimum(j, ti[1, 0] - 1)

    def wmap(hi):
        return lambda j, ti: (_class_expert(ti[0, tile(j, ti)], hi), 0, 0)

    up = lambda hi: pl.BlockSpec((1, D_MODEL, D_EXPERT), wmap(hi))
    down = lambda hi: pl.BlockSpec((1, D_EXPERT, D_MODEL), wmap(hi))
    return pl.pallas_call(
        _experts_kernel,
        grid_spec=pltpu.PrefetchScalarGridSpec(
            num_scalar_prefetch=1,
            grid=(nt,),
            in_specs=[pl.BlockSpec((TM_E, ROW_W), lambda j, ti: (tile(j, ti), 0)),
                      up(0), up(0), down(0), up(1), up(1), down(1)],
            out_specs=pl.BlockSpec((TM_E, D_MODEL), lambda j, ti: (j, 0))),
        out_shape=jax.ShapeDtypeStruct((n_sorted, D_MODEL), F32),
        compiler_params=_cparams(("arbitrary",)),
        name="moe_experts",
    )(tinfo, xs, wg, wu, wd, wg, wu, wd)


def _combine_kernel(dcur_ref, dnxt_ref, x1_ref, nf_ref, ys_ref, y_ref, buf, sem, *, nb, tt):
    i = pl.program_id(0)
    slot = i % 2

    def gather(dref, s):
        for b in range(nb):
            for t in range(tt):
                d = dref[0, 0, b * tt + t]
                pltpu.make_async_copy(ys_ref.at[pl.ds(d, 1), :], buf.at[s, b, pl.ds(t, 1), :],
                                      sem.at[s]).start()

    @pl.when(i == 0)
    def _():
        gather(dcur_ref, 0)

    @pl.when(i + 1 < pl.num_programs(0))
    def _():
        gather(dnxt_ref, 1 - slot)

    whole = ys_ref.at[pl.ds(0, nb * tt), :]
    pltpu.make_async_copy(whole, whole, sem.at[slot]).wait()
    x2 = x1_ref[...].reshape(nb * tt, D_MODEL) + buf[slot].reshape(nb * tt, D_MODEL)
    y_ref[...] = _rms(x2, nf_ref[...]).reshape(nb, tt, D_MODEL)


def _combine(dest, x1, nf, ys, *, tt, skip):
    nb, tl, _ = x1.shape
    nt = tl // tt - skip
    dspec = lambda f: pl.BlockSpec((1, 1, nb * tt), f, memory_space=pltpu.SMEM)
    return pl.pallas_call(
        functools.partial(_combine_kernel, nb=nb, tt=tt),
        grid=(nt,),
        in_specs=[dspec(lambda i: (i, 0, 0)),
                  dspec(lambda i: (jnp.minimum(i + 1, nt - 1), 0, 0)),
                  pl.BlockSpec((nb, tt, D_MODEL), lambda i: (0, i + skip, 0)),
                  pl.BlockSpec((1, D_MODEL), lambda i: (0, 0)),
                  pl.BlockSpec(memory_space=pl.ANY)],
        out_specs=pl.BlockSpec((nb, tt, D_MODEL), lambda i: (0, i, 0)),
        out_shape=jax.ShapeDtypeStruct((nb, nt * tt, D_MODEL), F32),
        scratch_shapes=[pltpu.VMEM((2, nb, tt, D_MODEL), F32),
                        pltpu.SemaphoreType.DMA((2,))],
        compiler_params=_cparams(("arbitrary",)),
        name="moe_combine",
    )(dest, dest, x1, nf, ys)


def _block_diag(w):
    per = RNN_SUPER // RNN_BLOCK
    w4 = w.reshape(N_SUPER, per, RNN_BLOCK, RNN_BLOCK).astype(BF16)
    rows = [jnp.pad(w4[:, q], ((0, 0), (0, 0), (q * RNN_BLOCK, (per - 1 - q) * RNN_BLOCK))) for q in range(per)]
    return jnp.concatenate(rows, axis=1)


def _mixers(x, meta, conv0, h0, c0, n0, m0, p, w_ready, *, has_head):
    nb, seq, _ = x.shape
    ready = w_ready is not None
    n_null = N_NULL if has_head else 0
    tt = CHUNK if has_head else seq
    tt_mm = MM_TILES * CHUNK if has_head else seq
    xn, gif = _prep(x, meta, p['norm1_w'], p['w_in'], p['b_gif'], has_head=has_head)
    w_bf16 = {}

    def in_proj(key, w, col0, ncols, tn, out_dtype, gelu=False):
        if ready:
            return _mm(xn, w_ready[key], col0=0, ncols=ncols, tn=tn, tt=tt_mm, out_dtype=out_dtype, gelu=gelu,
                       ready=True, name="in_" + key)
        out, w_bf16[key] = _mm(xn, w, col0=col0, ncols=ncols, tn=tn, tt=tt_mm, out_dtype=out_dtype, gelu=gelu,
                               name="in_" + key)
        return out

    tn_rnn = 1280 if ready else 512
    xr = in_proj('xr', p['w_in'], 0, D_RNN, tn_rnn, F32)
    gr = in_proj('gr', p['w_in'], D_RNN, D_RNN, tn_rnn, BF16, gelu=True)
    qkvo = in_proj('qkvo', p['w_in'], 2 * D_RNN, QKVO, 1024, BF16)
    mg = in_proj('mg', p['w_in'], MG_OFF, 2 * D_MODEL, 1024, BF16)

    tail0 = jnp.transpose(conv0, (1, 0, 2)).reshape((CONV_W - 1) * nb, D_RNN)
    y_r, h_new, tail = _rglru(xr, gr, tail0, h0, p['conv_w'], p['conv_b'], p['rg_a_b'], p['rg_x_b'],
                              p['rg_lambda'], p['wa_bd'], p['wx_bd'], tt=tt, n_null=n_null)
    conv_new = jnp.transpose(tail.reshape(CONV_W - 1, nb, D_RNN), (1, 0, 2))

    y_m, c_new, n_new, m_new = _mlstm_fused(qkvo, gif, c0, n0, m0, p['mlstm_norm_w'], L=tt, n_null=n_null,
                                            nbk=MLSTM_NBK)

    if ready:
        mix = _merge(y_r, y_m, w_ready['proj_rnn'], w_ready['proj_mlstm'], mg, tn=1024, tt=tt, ready=True)
    else:
        mix, w_bf16['proj_rnn'], w_bf16['proj_mlstm'] = _merge(y_r, y_m, p['w_proj_rnn'], p['w_proj_mlstm'], mg,
                                                               tn=512, tt=tt)
    states = (conv_new[None], h_new[None], c_new[None], n_new[None], m_new[None])
    return mix, states, (w_ready if ready else w_bf16)


def _ffn(mix, x, meta, p, expert_w, *, has_head):
    tt = CHUNK if has_head else x.shape[1]
    routed = has_head
    x1, xn2, gates = _outproj(mix, p['w_out'], x, meta, p['norm2_w'], p['w_router'], p['b_router'],
                              tt=tt, has_head=has_head, routed=routed)
    if routed:
        skip = 1
        dest, tinfo = _plan(gates, tt=tt, skip=skip)
        xs = _dispatch(tinfo, dest, xn2, tt=tt, skip=skip)
        ys = _experts(tinfo, xs, *expert_w)
        return _combine(dest, x1, p['norm_f_w'], ys, tt=tt, skip=skip), expert_w
    y, *expert_w_bf16 = _moe(xn2, gates, x1, *expert_w, p['norm_f_w'])
    return y, tuple(expert_w_bf16)


def kernel(x_prompt, x_sample, state_rglru_conv, state_rglru_h, state_mlstm_C, state_mlstm_n, state_mlstm_m, meta_tokens, norm1_w, w_in, b_gates, conv_w, conv_b, rg_a_w, rg_a_b, rg_x_w, rg_x_b, rg_lambda, mlstm_norm_w, w_proj_rnn, w_proj_mlstm, w_out, norm2_w, w_router_group, b_router_group, w_router_expert, b_router_expert, w_exp_gate, w_exp_up, w_exp_down, norm_f_w):
    l = 0
    w_in_t = jnp.swapaxes(w_in[l], 0, 1)
    pad_r = LANES - N_GROUPS - N_EXPERTS
    p = dict(
        norm1_w=norm1_w[l][None], w_in=w_in_t,
        b_gif=jnp.pad(b_gates[l], (0, LANES - 2 * N_HEADS_M))[None],
        conv_w=conv_w[l], conv_b=conv_b[l][None], rg_a_b=rg_a_b[l][None], rg_x_b=rg_x_b[l][None],
        rg_lambda=rg_lambda[l][None], wa_bd=_block_diag(rg_a_w[l]), wx_bd=_block_diag(rg_x_w[l]),
        mlstm_norm_w=mlstm_norm_w[l][None],
        w_proj_rnn=w_proj_rnn[l], w_proj_mlstm=w_proj_mlstm[l], w_out=w_out[l].astype(BF16),
        norm2_w=norm2_w[l][None],
        w_router=jnp.pad(jnp.concatenate([w_router_group[l], w_router_expert[l]], axis=1),
                         ((0, 0), (0, pad_r))).astype(BF16),
        b_router=jnp.pad(jnp.concatenate([b_router_group[l], b_router_expert[l]]), (0, pad_r))[None],
        norm_f_w=norm_f_w[None],
    )
    nbp = x_prompt.shape[0]
    dt = x_prompt.dtype
    mix_s, st_s, w_bf16 = _mixers(
        x_sample, meta_tokens, state_rglru_conv[l], state_rglru_h[l], state_mlstm_C[l],
        state_mlstm_n[l], state_mlstm_m[l], p, None, has_head=False)
    y_s, expert_w_bf16 = _ffn(mix_s, x_sample, meta_tokens, p, (w_exp_gate[l], w_exp_up[l], w_exp_down[l]),
                              has_head=False)
    mix_p, st_p, _ = _mixers(
        x_prompt, meta_tokens,
        jnp.zeros((nbp, CONV_W - 1, D_RNN), dt), jnp.zeros((nbp, D_RNN), F32),
        jnp.zeros((nbp, N_HEADS_M, DQK_M, DV_M), F32), jnp.zeros((nbp, N_HEADS_M, DQK_M), F32),
        jnp.zeros((nbp, N_HEADS_M), F32), p, w_bf16, has_head=True)
    y_p, _ = _ffn(mix_p, x_prompt, meta_tokens, p, expert_w_bf16, has_head=True)
    return (y_p, y_s) + st_p + st_s
```

```python
import functools

import jax
import jax.numpy as jnp
from jax import lax
from jax.experimental import pallas as pl
from jax.experimental.pallas import tpu as pltpu

F32 = jnp.float32
BF16 = jnp.bfloat16

D_MODEL = 2048
CHUNK = 64
N_META = 16
N_NULL = CHUNK - N_META
D_RNN = 2560
N_RNN_BLOCKS = 16
RNN_BLOCK = D_RNN // N_RNN_BLOCKS
RNN_SUPER = 640
N_SUPER = D_RNN // RNN_SUPER
CONV_W = 4
LRU_C = 8.0
N_HEADS_M = 8
DV_M = D_MODEL // N_HEADS_M
DQK_M = DV_M // 2
HQK = N_HEADS_M * DQK_M
HV = N_HEADS_M * DV_M
QKVO = 2 * HQK + 2 * HV
N_GROUPS = 4
EXPERTS_PER_GROUP = 4
N_EXPERTS = 16
D_EXPERT = 512
PAIRS_PER_GROUP = 6
N_CLASSES = N_GROUPS * PAIRS_PER_GROUP
MM_TILES = 3
MLSTM_NBK = 2
TM_E = 256
ROW_W = D_MODEL + 128
EPS = 1e-6
LANES = 128
GIF_OFF = 2 * D_RNN + QKVO
MG_OFF = GIF_OFF + 2 * N_HEADS_M
VMEM_LIMIT = 56 * 1024 * 1024
NEG_BIG = -1e30


def _cparams(sem):
    return pltpu.CompilerParams(dimension_semantics=sem, vmem_limit_bytes=VMEM_LIMIT)


def _rms(x, w):
    return x * lax.rsqrt(jnp.mean(x * x, axis=-1, keepdims=True) + EPS) * w


def _softplus(x):
    return jnp.maximum(x, 0.0) + jnp.log1p(jnp.exp(-jnp.abs(x)))


def _head_rows(meta_ref, nb):
    head = jnp.concatenate([jnp.zeros((N_NULL, D_MODEL), F32), meta_ref[...]], axis=0)
    return jnp.broadcast_to(head[None], (nb, CHUNK, D_MODEL)).reshape(nb * CHUNK, D_MODEL)


def _x_map(has_head):
    if has_head:
        return lambda ti: (0, jnp.maximum(ti - 1, 0), 0)
    return lambda ti: (0, ti, 0)


def _prep_kernel(x_ref, meta_ref, nw_ref, wg_ref, bg_ref, xn_ref, gif_ref, *, nb, tt, has_head):
    def emit(x):
        xnb = _rms(x, nw_ref[...]).astype(BF16)
        xn_ref[...] = xnb.reshape(nb, tt, D_MODEL)
        gif = lax.dot_general(xnb, wg_ref[...].astype(BF16), (((1,), (1,)), ((), ())),
                              preferred_element_type=F32) + bg_ref[...]
        gif_ref[...] = gif.reshape(nb, tt, LANES)

    if has_head:
        @pl.when(pl.program_id(0) == 0)
        def _():
            emit(_head_rows(meta_ref, nb))

        @pl.when(pl.program_id(0) > 0)
        def _():
            emit(x_ref[...].reshape(nb * tt, D_MODEL))
    else:
        emit(x_ref[...].reshape(nb * tt, D_MODEL))


def _prep(x, meta, norm_w, w_in_t, b_gif, *, has_head):
    nb, seq, _ = x.shape
    assert GIF_OFF % LANES == 0
    tt = CHUNK if has_head else seq
    n_tiles = seq // tt + (1 if has_head else 0)
    tl = n_tiles * tt
    const = lambda ti: (0, 0)
    return pl.pallas_call(
        functools.partial(_prep_kernel, nb=nb, tt=tt, has_head=has_head),
        grid=(n_tiles,),
        in_specs=[pl.BlockSpec((nb, tt, D_MODEL), _x_map(has_head)),
                  pl.BlockSpec((N_META, D_MODEL), const),
                  pl.BlockSpec((1, D_MODEL), const),
                  pl.BlockSpec((LANES, D_MODEL), lambda ti: (GIF_OFF // LANES, 0)),
                  pl.BlockSpec((1, LANES), const)],
        out_specs=[pl.BlockSpec((nb, tt, D_MODEL), lambda ti: (0, ti, 0)),
                   pl.BlockSpec((nb, tt, LANES), lambda ti: (0, ti, 0))],
        out_shape=[jax.ShapeDtypeStruct((nb, tl, D_MODEL), BF16),
                   jax.ShapeDtypeStruct((nb, tl, LANES), F32)],
        compiler_params=_cparams(("arbitrary",)),
        name="prep",
    )(x, meta, norm_w, w_in_t, b_gif)


def _gelu_tanh(g):
    return g * (0.5 * (1.0 + jnp.tanh(0.7978845608028654 * (g + 0.044715 * (g * g * g)))))


def _mm_kernel(lhs_ref, w_ref, out_ref, *wbf_ref, gelu):
    nb, tt, k = lhs_ref.shape
    tn = out_ref.shape[-1]
    if wbf_ref:
        w_bf = wbf_ref[0]

        @pl.when(pl.program_id(1) == 0)
        def _():
            w_bf[...] = w_ref[...].T.astype(BF16)
    else:
        w_bf = w_ref

    acc = jnp.dot(lhs_ref[...].reshape(nb * tt, k), w_bf[...], preferred_element_type=F32)
    if gelu:
        acc = _gelu_tanh(acc)
    out_ref[...] = acc.astype(out_ref.dtype).reshape(nb, tt, tn)


def _mm(lhs, w, *, col0, ncols, tn, tt, out_dtype, gelu=False, ready=False, name):
    nb, tl, k = lhs.shape
    assert tl % tt == 0 and ncols % tn == 0 and col0 % 8 == 0
    lhs_spec = pl.BlockSpec((nb, tt, k), lambda j, i: (0, i, 0))
    out_spec = pl.BlockSpec((nb, tt, tn), lambda j, i: (0, i, j))
    out_shape = jax.ShapeDtypeStruct((nb, tl, ncols), out_dtype)
    wbf_spec = pl.BlockSpec((k, tn), lambda j, i: (0, j))
    if ready:
        in_specs, out_specs, out_shapes = [lhs_spec, wbf_spec], out_spec, out_shape
    else:
        in_specs = [lhs_spec, pl.BlockSpec((pl.Element(tn), pl.Element(k)),
                                           lambda j, i: (pl.multiple_of(col0 + j * tn, 8), 0))]
        out_specs = [out_spec, wbf_spec]
        out_shapes = [out_shape, jax.ShapeDtypeStruct((k, ncols), BF16)]
    return pl.pallas_call(
        functools.partial(_mm_kernel, gelu=gelu),
        grid=(ncols // tn, tl // tt),
        in_specs=in_specs,
        out_specs=out_specs,
        out_shape=out_shapes,
        compiler_params=_cparams(("arbitrary", "arbitrary")),
        name=name,
    )(lhs, w)


def _merge_kernel(yr_ref, ym_ref, wr_ref, wm_ref, gr_ref, gm_ref, out_ref, *wbf_refs):
    nb, tt, tn = out_ref.shape
    rows = nb * tt
    if wbf_refs:
        wr_bf, wm_bf = wbf_refs

        @pl.when(pl.program_id(1) == 0)
        def _():
            wr_bf[...] = wr_ref[...].astype(BF16)
            wm_bf[...] = wm_ref[...].astype(BF16)
    else:
        wr_bf, wm_bf = wr_ref, wm_ref

    pr = jnp.dot(yr_ref[...].reshape(rows, D_RNN), wr_bf[...], preferred_element_type=F32)
    pm = jnp.dot(ym_ref[...].reshape(rows, HV), wm_bf[...], preferred_element_type=F32)
    mix = (jax.nn.sigmoid(gr_ref[...].reshape(rows, tn).astype(F32)) * pr
           + jax.nn.sigmoid(gm_ref[...].reshape(rows, tn).astype(F32)) * pm)
    out_ref[...] = mix.astype(BF16).reshape(nb, tt, tn)


def _merge(y_r, y_m, w_r, w_m, mg, *, tn, tt, ready=False):
    nb, tl, _ = y_r.shape
    nj = D_MODEL // tn
    wr_spec = pl.BlockSpec((D_RNN, tn), lambda j, i: (0, j))
    wm_spec = pl.BlockSpec((HV, tn), lambda j, i: (0, j))
    out_specs = pl.BlockSpec((nb, tt, tn), lambda j, i: (0, i, j))
    out_shape = jax.ShapeDtypeStruct((nb, tl, D_MODEL), BF16)
    if not ready:
        out_specs = [out_specs, wr_spec, wm_spec]
        out_shape = [out_shape, jax.ShapeDtypeStruct(w_r.shape, BF16), jax.ShapeDtypeStruct(w_m.shape, BF16)]
    return pl.pallas_call(
        _merge_kernel,
        grid=(nj, tl // tt),
        in_specs=[pl.BlockSpec((nb, tt, D_RNN), lambda j, i: (0, i, 0)),
                  pl.BlockSpec((nb, tt, HV), lambda j, i: (0, i, 0)),
                  wr_spec, wm_spec,
                  pl.BlockSpec((nb, tt, tn), lambda j, i: (0, i, j)),
                  pl.BlockSpec((nb, tt, tn), lambda j, i: (0, i, nj + j))],
        out_specs=out_specs,
        out_shape=out_shape,
        compiler_params=_cparams(("arbitrary", "arbitrary")),
        name="merge",
    )(y_r, y_m, w_r, w_m, mg, mg)


def _rglru_kernel(xr_ref, gr_ref, tail0_ref, h0_ref, cw_ref, cb_ref, ab_ref, xb_ref, lam_ref,
                  wa_ref, wx_ref, y_ref, hl_ref, tail_ref, xp_s, h_s, a_s, u_s, hs_s, *, nb, tt, n_null):
    ti = pl.program_id(1)
    rt = tt * nb
    tb = (CONV_W - 1) * nb

    def time_major(x):
        return jnp.swapaxes(x, 0, 1).reshape(rt, x.shape[-1])

    @pl.when(ti == 0)
    def _():
        xp_s[0:tb, :] = tail0_ref[...]
        h_s[...] = h0_ref[...]

    xp_s[tb:tb + rt, :] = time_major(xr_ref[...])
    cw = cw_ref[...]
    xc = cb_ref[...] + cw[0:1, :] * xp_s[0:rt, :]
    for j in range(1, CONV_W):
        xc = xc + cw[j:j + 1, :] * xp_s[j * nb:j * nb + rt, :]
    xcb = xc.astype(BF16)
    r = jax.nn.sigmoid(jnp.dot(xcb, wa_ref[0], preferred_element_type=F32) + ab_ref[...])
    i = jax.nn.sigmoid(jnp.dot(xcb, wx_ref[0], preferred_element_type=F32) + xb_ref[...])
    log_a = r * ((-LRU_C) * _softplus(-lam_ref[...]))
    a = jnp.exp(log_a)
    a_s[...] = a
    u = jnp.sqrt(-jnp.tanh(log_a) * (a * a + 1.0)) * (i * xc)
    u_s[...] = u
    if n_null:
        @pl.when(ti == 0)
        def _():
            a_s[n_null * nb:(n_null + 1) * nb, :] = jnp.zeros((nb, a_s.shape[-1]), F32)

    def step(t, h):
        o = pl.multiple_of(t * nb, nb)
        h = a_s[pl.ds(o, nb), :] * h + u_s[pl.ds(o, nb), :]
        hs_s[pl.ds(o, nb), :] = h
        return h

    h = lax.fori_loop(0, tt, step, h_s[...], unroll=8)
    h_s[...] = h
    h_bm = jnp.swapaxes(hs_s[...].reshape(tt, nb, hs_s.shape[-1]), 0, 1)
    y_ref[...] = (gr_ref[...].astype(F32) * h_bm).astype(BF16)
    xp_s[0:tb, :] = xp_s[rt:rt + tb, :]

    @pl.when(ti == pl.num_programs(1) - 1)
    def _():
        hl_ref[...] = h
        tail_ref[...] = xp_s[rt:rt + tb, :]


def _rglru(xr, gr, tail0, h0, cw, cb, ab, xb, lam, wa, wx, *, tt, n_null):
    nb, tl, _ = xr.shape
    rt = tt * nb
    tb = (CONV_W - 1) * nb
    blk = lambda s, ti: (0, ti, s)
    col = lambda s, ti: (0, s)
    return pl.pallas_call(
        functools.partial(_rglru_kernel, nb=nb, tt=tt, n_null=n_null),
        grid=(N_SUPER, tl // tt),
        in_specs=[pl.BlockSpec((nb, tt, RNN_SUPER), blk),
                  pl.BlockSpec((nb, tt, RNN_SUPER), blk),
                  pl.BlockSpec((tb, RNN_SUPER), col),
                  pl.BlockSpec((nb, RNN_SUPER), col),
                  pl.BlockSpec((CONV_W, RNN_SUPER), col),
                  pl.BlockSpec((1, RNN_SUPER), col),
                  pl.BlockSpec((1, RNN_SUPER), col),
                  pl.BlockSpec((1, RNN_SUPER), col),
                  pl.BlockSpec((1, RNN_SUPER), col),
                  pl.BlockSpec((1, RNN_SUPER, RNN_SUPER), lambda s, ti: (s, 0, 0)),
                  pl.BlockSpec((1, RNN_SUPER, RNN_SUPER), lambda s, ti: (s, 0, 0))],
        out_specs=[pl.BlockSpec((nb, tt, RNN_SUPER), blk),
                   pl.BlockSpec((nb, RNN_SUPER), col),
                   pl.BlockSpec((tb, RNN_SUPER), col)],
        out_shape=[jax.ShapeDtypeStruct((nb, tl, D_RNN), BF16),
                   jax.ShapeDtypeStruct((nb, D_RNN), F32),
                   jax.ShapeDtypeStruct((tb, D_RNN), F32)],
        scratch_shapes=[pltpu.VMEM((rt + tb, RNN_SUPER), F32),
                        pltpu.VMEM((nb, RNN_SUPER), F32),
                        pltpu.VMEM((rt, RNN_SUPER), F32),
                        pltpu.VMEM((rt, RNN_SUPER), F32),
                        pltpu.VMEM((rt, RNN_SUPER), F32)],
        compiler_params=_cparams(("arbitrary", "arbitrary")),
        name="rglru",
    )(xr, gr, tail0, h0, cw, cb, ab, xb, lam, wa, wx)


DVA = DV_M + 128


def _mlstm_gates(c, gif_ref, m_s, *, L, n_null):
    hm = N_HEADS_M
    g = gif_ref[...]
    if L < LANES:
        g = jnp.concatenate([g, jnp.zeros((LANES - L, LANES), F32)], axis=0)
    g_t = g.T
    lane = lax.broadcasted_iota(jnp.int32, (hm, LANES), 1)
    valid = lane < L
    li = g_t[0:hm, :]
    lfr = g_t[hm:2 * hm, :]
    lf = jnp.where(valid, jnp.minimum(lfr, 0.0) - jnp.log1p(jnp.exp(-jnp.abs(lfr))), 0.0)
    if n_null:
        null = (lane < n_null) & (c == 0)
        lf = jnp.where(null, 0.0, lf)
        li = jnp.where(null, NEG_BIG, li)
    r128 = lax.broadcasted_iota(jnp.int32, (LANES, LANES), 0)
    c128 = lax.broadcasted_iota(jnp.int32, (LANES, LANES), 1)
    upper = jnp.where(r128 <= c128, 1.0, 0.0)
    b = jnp.dot(lf, upper, preferred_element_type=F32, precision=lax.Precision.HIGHEST)
    a = li - b
    cm = jnp.where(valid, a, -jnp.inf)
    sh = 1
    while sh < L:
        cm = jnp.maximum(cm, jnp.where(lane >= sh, pltpu.roll(cm, sh, axis=1), -jnp.inf))
        sh *= 2
    m0 = m_s[...]
    big_m = jnp.maximum(cm, m0)
    m_last = jnp.max(jnp.where(valid, big_m, -jnp.inf), axis=1, keepdims=True)
    b_last = jnp.sum(lf, axis=1, keepdims=True)
    scale = DQK_M ** -0.5
    e_mv = jnp.exp(-(b + big_m))
    w_int = jnp.exp(m0 - big_m)
    w_end = jnp.exp(a - m_last) * scale
    decay = jnp.exp(m0 - m_last)
    m_s[...] = jnp.broadcast_to(b_last + m_last, (hm, LANES))
    cols = jnp.concatenate([big_m, e_mv, w_int, w_end, jnp.zeros((LANES - 4 * hm, LANES), F32)], axis=0).T
    return a, cols, decay


def _mlstm_heads(gates, q_ref, k_ref, v_ref, og_ref, nw_ref, y_ref, c_s, *, L):
    hm = N_HEADS_M
    scale = DQK_M ** -0.5
    rr = lax.broadcasted_iota(jnp.int32, (L, L), 0)
    cc = lax.broadcasted_iota(jnp.int32, (L, L), 1)
    causal = rr >= cc
    ones_col = jnp.where(lax.broadcasted_iota(jnp.int32, (L, 128), 1) == 0, 1.0, 0.0).astype(BF16)
    nw = nw_ref[...]
    dn_nt = (((1,), (1,)), ((), ()))
    dn_tn = (((0,), (0,)), ((), ()))
    bh = [(bi, h) for bi in range(len(gates)) for h in range(hm)]
    col = lambda bi, k, h: gates[bi][1][0:L, k * hm + h:k * hm + h + 1]
    qs = [q_ref[bi, :, h * DQK_M:(h + 1) * DQK_M] for bi, h in bh]
    ks = [k_ref[bi, :, h * DQK_M:(h + 1) * DQK_M] for bi, h in bh]
    v_augs = [jnp.concatenate([v_ref[bi, :, h * DV_M:(h + 1) * DV_M], ones_col], axis=1) for bi, h in bh]
    c_augs = [c_s[bi, h] for bi, h in bh]
    s_raw = [lax.dot_general(q, k, dn_nt, preferred_element_type=F32) for q, k in zip(qs, ks)]
    inter = [jnp.dot((qs[i].astype(F32) * col(bi, 2, h)).astype(BF16), c_augs[i].astype(BF16),
                     preferred_element_type=F32) for i, (bi, h) in enumerate(bh)]
    s_w = [(s_raw[i] * (scale * jnp.where(causal, jnp.exp(gates[bi][0][h:h + 1, 0:L] - col(bi, 0, h)), 0.0))
            ).astype(BF16) for i, (bi, h) in enumerate(bh)]
    res = [jnp.dot(s_w[i], v_augs[i], preferred_element_type=F32) + inter[i] for i in range(len(bh))]
    upd = [lax.dot_general((ks[i].astype(F32) * col(bi, 3, h)).astype(BF16), v_augs[i], dn_tn,
                           preferred_element_type=F32) for i, (bi, h) in enumerate(bh)]
    for i, (bi, h) in enumerate(bh):
        c_s[bi, h] = gates[bi][2][h:h + 1, 0:1] * c_augs[i] + upd[i]
    for i, (bi, h) in enumerate(bh):
        num = res[i][:, :DV_M]
        denom = jnp.maximum(jnp.abs(res[i][:, DV_M:DV_M + 1]), col(bi, 1, h))
        hh = num / denom
        hh = hh * lax.rsqrt(jnp.mean(hh * hh, axis=-1, keepdims=True) + EPS)
        hh = hh * nw[:, h * DV_M:(h + 1) * DV_M]
        og = og_ref[bi, :, h * DV_M:(h + 1) * DV_M].astype(F32)
        y_ref[bi, :, h * DV_M:(h + 1) * DV_M] = (hh * jax.nn.sigmoid(og)).astype(BF16)


def _mlstm_fused_kernel(q_ref, k_ref, v_ref, og_ref, gif_ref, c0_ref, n0_ref, m0_ref, nw_ref,
                        y_ref, co_ref, no_ref, mo_ref, c_s, m_s, *, L, n_null, nbk):
    c = pl.program_id(1)
    bh = [(bi, h) for bi in range(nbk) for h in range(N_HEADS_M)]

    @pl.when(c == 0)
    def _():
        m_s[...] = m0_ref[...]
        for bi, h in bh:
            c_s[bi, h, :, :DV_M] = c0_ref[bi, h]
            n_tile = jnp.concatenate([n0_ref[bi, h:h + 1, :], jnp.zeros((DQK_M - 1, DQK_M), F32)], axis=0)
            c_s[bi, h, :, DV_M:] = n_tile.T

    gates = [_mlstm_gates(c, gif_ref.at[bi], m_s.at[bi], L=L, n_null=n_null) for bi in range(nbk)]
    _mlstm_heads(gates, q_ref, k_ref, v_ref, og_ref, nw_ref, y_ref, c_s, L=L)

    @pl.when(c == pl.num_programs(1) - 1)
    def _():
        mo_ref[...] = m_s[...]
        for bi, h in bh:
            co_ref[bi, h] = c_s[bi, h, :, :DV_M]
            no_ref[bi, h:h + 1, :] = c_s[bi, h, :, DV_M:].T[0:1, :]


def _mlstm_fused(qkvo, gif, c0, n0, m0, nw, *, L, n_null, nbk):
    nb, tl, _ = qkvo.shape
    assert nb % nbk == 0
    hm = N_HEADS_M
    assert DQK_M == LANES and DVA - DV_M == DQK_M
    m_rep0 = jnp.broadcast_to(m0[..., None], (nb, hm, LANES))
    st4 = lambda b, c: (b, 0, 0, 0)
    st3 = lambda b, c: (b, 0, 0)
    y, c_new, n_new, m_rep = pl.pallas_call(
        functools.partial(_mlstm_fused_kernel, L=L, n_null=n_null, nbk=nbk),
        grid=(nb // nbk, tl // L),
        in_specs=[pl.BlockSpec((nbk, L, HQK), lambda b, c: (b, c, 0)),
                  pl.BlockSpec((nbk, L, HQK), lambda b, c: (b, c, 1)),
                  pl.BlockSpec((nbk, L, HV), lambda b, c: (b, c, 1)),
                  pl.BlockSpec((nbk, L, HV), lambda b, c: (b, c, 2)),
                  pl.BlockSpec((nbk, L, LANES), lambda b, c: (b, c, 0)),
                  pl.BlockSpec((nbk, hm, DQK_M, DV_M), st4),
                  pl.BlockSpec((nbk, hm, DQK_M), st3),
                  pl.BlockSpec((nbk, hm, LANES), st3),
                  pl.BlockSpec((1, HV), lambda b, c: (0, 0))],
        out_specs=[pl.BlockSpec((nbk, L, HV), lambda b, c: (b, c, 0)),
                   pl.BlockSpec((nbk, hm, DQK_M, DV_M), st4),
                   pl.BlockSpec((nbk, hm, DQK_M), st3),
                   pl.BlockSpec((nbk, hm, LANES), st3)],
        out_shape=[jax.ShapeDtypeStruct((nb, tl, HV), BF16),
                   jax.ShapeDtypeStruct((nb, hm, DQK_M, DV_M), F32),
                   jax.ShapeDtypeStruct((nb, hm, DQK_M), F32),
                   jax.ShapeDtypeStruct((nb, hm, LANES), F32)],
        scratch_shapes=[pltpu.VMEM((nbk, hm, DQK_M, DVA), F32),
                        pltpu.VMEM((nbk, hm, LANES), F32)],
        compiler_params=_cparams(("arbitrary", "arbitrary")),
        name="mlstm",
    )(qkvo, qkvo, qkvo, qkvo, gif, c0, n0, m_rep0, nw)
    return y, c_new, n_new, m_rep[..., 0]


def _route(logits, dense):
    tm = logits.shape[0]
    lane = lax.broadcasted_iota(jnp.int32, (tm, LANES), 1)
    lanef = lane.astype(F32)
    gl = jnp.where(lane < N_GROUPS, logits, -jnp.inf)
    gmax = jnp.max(gl, axis=1, keepdims=True)
    gidx = jnp.min(jnp.where(gl == gmax, lanef, float(LANES)), axis=1, keepdims=True)
    gw = 1.0 / jnp.sum(jnp.exp(gl - gmax), axis=1, keepdims=True)
    lo = float(N_GROUPS) + float(EXPERTS_PER_GROUP) * gidx
    sel = (lanef >= lo) & (lanef < lo + float(EXPERTS_PER_GROUP))
    ev = jnp.where(sel, logits, -jnp.inf)
    v1 = jnp.max(ev, axis=1, keepdims=True)
    i1 = jnp.min(jnp.where(ev == v1, lanef, float(LANES)), axis=1, keepdims=True)
    ev2 = jnp.where(lanef == i1, -jnp.inf, ev)
    v2 = jnp.max(ev2, axis=1, keepdims=True)
    i2 = jnp.min(jnp.where(ev2 == v2, lanef, float(LANES)), axis=1, keepdims=True)
    t = jnp.exp(v2 - v1)
    w1 = gw / (1.0 + t)
    w2 = gw * (t / (1.0 + t))
    if dense:
        return jnp.where(lanef == i1, w1, jnp.where(lanef == i2, w2, 0.0))
    l1 = i1 - lo
    l2 = i2 - lo
    first = l1 < l2
    a = jnp.minimum(l1, l2)
    b = jnp.maximum(l1, l2)
    pair = jnp.where(a == 0.0, b - 1.0, jnp.where(a == 1.0, b + 1.0, 5.0))
    cls = float(PAIRS_PER_GROUP) * gidx + pair
    return jnp.where(lane == 0, cls,
                     jnp.where(lane == 1, jnp.where(first, w1, w2),
                               jnp.where(lane == 2, jnp.where(first, w2, w1), 0.0)))


def _outproj_kernel(mix_ref, w_ref, x_ref, meta_ref, n2_ref, wr_ref, br_ref, x1_ref, xn2_ref, gates_ref,
                    *, nb, tt, has_head, routed):
    def emit(x):
        mix = mix_ref[...].reshape(nb * tt, D_MODEL)
        x1 = x + jnp.dot(mix, w_ref[...], preferred_element_type=F32)
        x1_ref[...] = x1.reshape(nb, tt, D_MODEL)
        xn2 = _rms(x1, n2_ref[...]).astype(BF16)
        logits = jnp.dot(xn2, wr_ref[...], preferred_element_type=F32) + br_ref[...]
        info = _route(logits, dense=not routed)
        gates_ref[...] = info.reshape(nb, tt, LANES)
        if routed:
            xn2_ref[:, :, :D_MODEL] = xn2.astype(F32).reshape(nb, tt, D_MODEL)
            xn2_ref[:, :, D_MODEL:] = info.reshape(nb, tt, LANES)
        else:
            xn2_ref[...] = xn2.reshape(nb, tt, D_MODEL)

    if has_head:
        @pl.when(pl.program_id(0) == 0)
        def _():
            emit(_head_rows(meta_ref, nb))

        @pl.when(pl.program_id(0) > 0)
        def _():
            emit(x_ref[...].reshape(nb * tt, D_MODEL))
    else:
        emit(x_ref[...].reshape(nb * tt, D_MODEL))


def _outproj(mix, w_out, x, meta, n2, w_router, b_router, *, tt, has_head, routed):
    nb, tl, _ = mix.shape
    const = lambda i: (0, 0)
    blk = lambda i: (0, i, 0)
    xn2_w, xn2_dt = (ROW_W, F32) if routed else (D_MODEL, BF16)
    return pl.pallas_call(
        functools.partial(_outproj_kernel, nb=nb, tt=tt, has_head=has_head, routed=routed),
        grid=(tl // tt,),
        in_specs=[pl.BlockSpec((nb, tt, D_MODEL), blk),
                  pl.BlockSpec((D_MODEL, D_MODEL), const, pipeline_mode=pl.Buffered(1)),
                  pl.BlockSpec((nb, tt, D_MODEL), _x_map(has_head)),
                  pl.BlockSpec((N_META, D_MODEL), const),
                  pl.BlockSpec((1, D_MODEL), const),
                  pl.BlockSpec((D_MODEL, LANES), const),
                  pl.BlockSpec((1, LANES), const)],
        out_specs=[pl.BlockSpec((nb, tt, D_MODEL), blk),
                   pl.BlockSpec((nb, tt, xn2_w), blk),
                   pl.BlockSpec((nb, tt, LANES), blk)],
        out_shape=[jax.ShapeDtypeStruct((nb, tl, D_MODEL), F32),
                   jax.ShapeDtypeStruct((nb, tl, xn2_w), xn2_dt),
                   jax.ShapeDtypeStruct((nb, tl, LANES), F32)],
        compiler_params=_cparams(("arbitrary",)),
        name="outproj",
    )(mix, w_out, x, meta, n2, w_router, b_router)


def _moe_kernel(xn_ref, gates_ref, x1_ref, wg_ref, wu_ref, wd_ref, nf_ref, y_ref, wgb_ref, wub_ref, wdb_ref,
                acc_s, *, nb, tt):
    e = pl.program_id(0)
    hf = pl.program_id(1)
    rows = nb * tt

    @pl.when((e == 0) & (hf == 0))
    def _():
        acc_s[...] = jnp.zeros_like(acc_s)

    wg = wg_ref[0].astype(BF16)
    wu = wu_ref[0].astype(BF16)
    wd = wd_ref[0].astype(BF16)
    wgb_ref[0] = wg
    wub_ref[0] = wu
    wdb_ref[0] = wd
    x = xn_ref[...].reshape(rows, D_MODEL)
    hg = jnp.dot(x, wg, preferred_element_type=F32)
    hu = jnp.dot(x, wu, preferred_element_type=F32)
    hdn = (hg * jax.nn.sigmoid(hg)) * hu
    yd = jnp.dot(hdn.astype(BF16), wd, preferred_element_type=F32)
    gates = gates_ref[...].reshape(rows, LANES)
    lane = lax.broadcasted_iota(jnp.int32, gates.shape, 1)
    gcol = jnp.sum(jnp.where(lane == e + N_GROUPS, gates, 0.0), axis=1, keepdims=True)
    acc_s[...] += gcol * yd

    @pl.when((e == pl.num_programs(0) - 1) & (hf == pl.num_programs(1) - 1))
    def _():
        x2 = x1_ref[...].reshape(rows, D_MODEL) + acc_s[...]
        y_ref[...] = _rms(x2, nf_ref[...]).reshape(nb, tt, D_MODEL)


def _moe(xn2, gates, x1, wg, wu, wd, nf):
    nb, tt, _ = xn2.shape
    halves = 2
    dh = D_EXPERT // halves
    blk = lambda e, hf: (0, 0, 0)
    up = pl.BlockSpec((1, D_MODEL, dh), lambda e, hf: (e, 0, hf))
    down = pl.BlockSpec((1, dh, D_MODEL), lambda e, hf: (e, hf, 0))
    return pl.pallas_call(
        functools.partial(_moe_kernel, nb=nb, tt=tt),
        grid=(N_EXPERTS, halves),
        in_specs=[pl.BlockSpec((nb, tt, D_MODEL), blk),
                  pl.BlockSpec((nb, tt, LANES), blk),
                  pl.BlockSpec((nb, tt, D_MODEL), blk),
                  up, up, down,
                  pl.BlockSpec((1, D_MODEL), lambda e, hf: (0, 0))],
        out_specs=[pl.BlockSpec((nb, tt, D_MODEL), blk), up, up, down],
        out_shape=[jax.ShapeDtypeStruct((nb, tt, D_MODEL), F32),
                   jax.ShapeDtypeStruct(wg.shape, BF16),
                   jax.ShapeDtypeStruct(wu.shape, BF16),
                   jax.ShapeDtypeStruct(wd.shape, BF16)],
        scratch_shapes=[pltpu.VMEM((nb * tt, D_MODEL), F32)],
        compiler_params=_cparams(("arbitrary", "arbitrary")),
        name="moe",
    )(xn2, gates, x1, wg, wu, wd, nf)


_NT = (((1,), (1,)), ((), ()))


def _plan_kernel(ri_ref, dest_ref, tinfo_ref, cnt_s, carry_s, offs_s, *, rows):
    ph = pl.program_id(0)
    i = pl.program_id(1)
    ri = ri_ref[...].reshape(rows, LANES)
    lanef = lax.broadcasted_iota(jnp.int32, (rows, LANES), 1).astype(F32)
    oh = jnp.where(lanef == ri[:, 0:1], 1.0, 0.0)
    colsum = jnp.sum(oh, axis=0, keepdims=True)
    r128 = lax.broadcasted_iota(jnp.int32, (LANES, LANES), 0)
    c128 = lax.broadcasted_iota(jnp.int32, (LANES, LANES), 1)
    eye = jnp.where(r128 == c128, 1.0, 0.0).astype(BF16)

    @pl.when((ph == 0) & (i == 0))
    def _():
        cnt_s[...] = jnp.zeros_like(cnt_s)

    @pl.when(ph == 0)
    def _():
        cnt_s[...] += colsum

    @pl.when((ph == 1) & (i == 0))
    def _():
        tiles = jnp.floor((cnt_s[...] + float(TM_E - 1)) * (1.0 / TM_E))
        upper = jnp.where(r128 <= c128, 1.0, 0.0).astype(BF16)
        cum_incl = jnp.dot(jnp.broadcast_to(tiles, (8, LANES)).astype(BF16), upper, preferred_element_type=F32)
        offs_s[...] = (cum_incl[0:1, :] - tiles) * float(TM_E)
        carry_s[...] = jnp.zeros_like(carry_s)
        cum_col = lax.dot_general(eye, cum_incl.astype(BF16), _NT, preferred_element_type=F32)
        ended = (cum_col[:, 0:1] <= c128.astype(F32)) & (r128 < N_CLASSES)
        tcls = jnp.sum(jnp.where(ended, 1.0, 0.0), axis=0, keepdims=True)
        total = cum_incl[0:1, LANES - 1:LANES]
        row8 = lax.broadcasted_iota(jnp.int32, (8, LANES), 0)
        tinfo_ref[...] = jnp.where(row8 == 0, tcls, total).astype(jnp.int32)

    @pl.when(ph == 1)
    def _():
        tri = jnp.where(lax.broadcasted_iota(jnp.int32, (rows, rows), 0)
                        > lax.broadcasted_iota(jnp.int32, (rows, rows), 1), 1.0, 0.0).astype(BF16)
        rank = jnp.dot(tri, oh.astype(BF16), preferred_element_type=F32)
        dest = jnp.sum(oh * (rank + (offs_s[...] + carry_s[...])), axis=1, keepdims=True)
        carry_s[...] += colsum
        dhi = jnp.floor(dest * (1.0 / 256.0))
        dlo = dest - 256.0 * dhi
        digits = jnp.where(lanef == 0.0, dlo, jnp.where(lanef == 1.0, dhi, 0.0)).astype(BF16)
        dt = lax.dot_general(eye, digits, _NT, preferred_element_type=F32)
        dest_ref[...] = (dt[0:1, :] + 256.0 * dt[1:2, :]).astype(jnp.int32).reshape(1, 1, rows)


def _plan(rinfo, *, tt, skip):
    nb, tl, _ = rinfo.shape
    rows = nb * tt
    nt = tl // tt - skip
    assert (nt * rows) // TM_E + N_CLASSES <= LANES and rows % LANES == 0
    return pl.pallas_call(
        functools.partial(_plan_kernel, rows=rows),
        grid=(2, nt),
        in_specs=[pl.BlockSpec((nb, tt, LANES), lambda ph, i: (0, i + skip, 0))],
        out_specs=[pl.BlockSpec((1, 1, rows), lambda ph, i: (i * ph, 0, 0)),
                   pl.BlockSpec((8, LANES), lambda ph, i: (0, 0))],
        out_shape=[jax.ShapeDtypeStruct((nt, 1, rows), jnp.int32),
                   jax.ShapeDtypeStruct((8, LANES), jnp.int32)],
        scratch_shapes=[pltpu.VMEM((1, LANES), F32)] * 3,
        compiler_params=_cparams(("arbitrary", "arbitrary")),
        name="moe_plan",
    )(rinfo)


def _dispatch_kernel(tinfo_ref, dest_ref, x_ref, xs_ref, zbuf, zsem, sem, *, nb, tt):
    i = pl.program_id(0)

    @pl.when(i == 0)
    def _():
        zbuf[...] = jnp.zeros_like(zbuf)
        n_used = tinfo_ref[1, 0]

        def fill(start):
            def body(j, carry):
                last = (j >= n_used - 1) | (tinfo_ref[0, jnp.minimum(j + 1, LANES - 1)] != tinfo_ref[0, j])

                @pl.when(last)
                def _():
                    cp = pltpu.make_async_copy(
                        zbuf, xs_ref.at[pl.ds(pl.multiple_of(j * TM_E, TM_E), TM_E), :], zsem)
                    if start:
                        cp.start()
                    else:
                        cp.wait()
                return carry
            lax.fori_loop(0, xs_ref.shape[0] // TM_E, body, 0)

        fill(True)
        fill(False)

    for b in range(nb):
        for t in range(tt):
            d = dest_ref[0, 0, b * tt + t]
            pltpu.make_async_copy(x_ref.at[b, pl.ds(t, 1), :], xs_ref.at[pl.ds(d, 1), :], sem).start()
    whole = xs_ref.at[pl.ds(0, nb * tt), :]
    pltpu.make_async_copy(whole, whole, sem).wait()


def _dispatch(tinfo, dest, xrow, *, tt, skip):
    nb, tl, _ = xrow.shape
    nt = tl // tt - skip
    n_sorted = ((nt * nb * tt) // TM_E + N_CLASSES) * TM_E
    return pl.pallas_call(
        functools.partial(_dispatch_kernel, nb=nb, tt=tt),
        grid_spec=pltpu.PrefetchScalarGridSpec(
            num_scalar_prefetch=1,
            grid=(nt,),
            in_specs=[pl.BlockSpec((1, 1, nb * tt), lambda i, ti: (i, 0, 0), memory_space=pltpu.SMEM),
                      pl.BlockSpec((nb, tt, ROW_W), lambda i, ti: (0, i + skip, 0))],
            out_specs=pl.BlockSpec(memory_space=pl.ANY),
            scratch_shapes=[pltpu.VMEM((TM_E, ROW_W), F32),
                            pltpu.SemaphoreType.DMA(()),
                            pltpu.SemaphoreType.DMA(())]),
        out_shape=jax.ShapeDtypeStruct((n_sorted, ROW_W), F32),
        compiler_params=_cparams(("arbitrary",)),
        name="moe_dispatch",
    )(tinfo, dest, xrow)


def _class_expert(c, hi):
    g = c // PAIRS_PER_GROUP
    p = c - PAIRS_PER_GROUP * g
    ge3 = (p >= 3).astype(jnp.int32)
    ge5 = (p >= 5).astype(jnp.int32)
    local = (p + 1 - 2 * ge3 - ge5) if hi else (ge3 + ge5)
    return EXPERTS_PER_GROUP * g + local


def _experts_kernel(tinfo_ref, xs_ref, wga_ref, wua_ref, wda_ref, wgb_ref, wub_ref, wdb_ref, ys_ref):
    @pl.when(pl.program_id(0) < tinfo_ref[1, 0])
    def _():
        x = xs_ref[:, :D_MODEL].astype(BF16)

        def ffn(wg_ref, wu_ref, wd_ref):
            hg = jnp.dot(x, wg_ref[0], preferred_element_type=F32)
            hu = jnp.dot(x, wu_ref[0], preferred_element_type=F32)
            hdn = (hg * jax.nn.sigmoid(hg)) * hu
            return jnp.dot(hdn.astype(BF16), wd_ref[0], preferred_element_type=F32)

        ys_ref[...] = (xs_ref[:, D_MODEL + 1:D_MODEL + 2] * ffn(wga_ref, wua_ref, wda_ref)
                       + xs_ref[:, D_MODEL + 2:D_MODEL + 3] * ffn(wgb_ref, wub_ref, wdb_ref))

    @pl.when(pl.program_id(0) >= tinfo_ref[1, 0])
    def _():
        ys_ref[...] = jnp.zeros_like(ys_ref)


def _experts(tinfo, xs, wg, wu, wd):
    n_sorted = xs.shape[0]
    nt = n_sorted // TM_E

    def tile(j, ti):
        return jnp.minimum(j, ti[1, 0] - 1)

    def wmap(hi):
        return lambda j, ti: (_class_expert(ti[0, tile(j, ti)], hi), 0, 0)

    up = lambda hi: pl.BlockSpec((1, D_MODEL, D_EXPERT), wmap(hi))
    down = lambda hi: pl.BlockSpec((1, D_EXPERT, D_MODEL), wmap(hi))
    return pl.pallas_call(
        _experts_kernel,
        grid_spec=pltpu.PrefetchScalarGridSpec(
            num_scalar_prefetch=1,
            grid=(nt,),
            in_specs=[pl.BlockSpec((TM_E, ROW_W), lambda j, ti: (tile(j, ti), 0)),
                      up(0), up(0), down(0), up(1), up(1), down(1)],
            out_specs=pl.BlockSpec((TM_E, D_MODEL), lambda j, ti: (j, 0))),
        out_shape=jax.ShapeDtypeStruct((n_sorted, D_MODEL), F32),
        compiler_params=_cparams(("arbitrary",)),
        name="moe_experts",
    )(tinfo, xs, wg, wu, wd, wg, wu, wd)


def _combine_kernel(dcur_ref, dnxt_ref, x1_ref, nf_ref, ys_ref, y_ref, buf, sem, *, nb, tt):
    i = pl.program_id(0)
    slot = i % 2

    def gather(dref, s):
        for b in range(nb):
            for t in range(tt):
                d = dref[0, 0, b * tt + t]
                pltpu.make_async_copy(ys_ref.at[pl.ds(d, 1), :], buf.at[s, b, pl.ds(t, 1), :],
                                      sem.at[s]).start()

    @pl.when(i == 0)
    def _():
        gather(dcur_ref, 0)

    @pl.when(i + 1 < pl.num_programs(0))
    def _():
        gather(dnxt_ref, 1 - slot)

    whole = ys_ref.at[pl.ds(0, nb * tt), :]
    pltpu.make_async_copy(whole, whole, sem.at[slot]).wait()
    x2 = x1_ref[...].reshape(nb * tt, D_MODEL) + buf[slot].reshape(nb * tt, D_MODEL)
    y_ref[...] = _rms(x2, nf_ref[...]).reshape(nb, tt, D_MODEL)


def _combine(dest, x1, nf, ys, *, tt, skip):
    nb, tl, _ = x1.shape
    nt = tl // tt - skip
    dspec = lambda f: pl.BlockSpec((1, 1, nb * tt), f, memory_space=pltpu.SMEM)
    return pl.pallas_call(
        functools.partial(_combine_kernel, nb=nb, tt=tt),
        grid=(nt,),
        in_specs=[dspec(lambda i: (i, 0, 0)),
                  dspec(lambda i: (jnp.minimum(i + 1, nt - 1), 0, 0)),
                  pl.BlockSpec((nb, tt, D_MODEL), lambda i: (0, i + skip, 0)),
                  pl.BlockSpec((1, D_MODEL), lambda i: (0, 0)),
                  pl.BlockSpec(memory_space=pl.ANY)],
        out_specs=pl.BlockSpec((nb, tt, D_MODEL), lambda i: (0, i, 0)),
        out_shape=jax.ShapeDtypeStruct((nb, nt * tt, D_MODEL), F32),
        scratch_shapes=[pltpu.VMEM((2, nb, tt, D_MODEL), F32),
                        pltpu.SemaphoreType.DMA((2,))],
        compiler_params=_cparams(("arbitrary",)),
        name="moe_combine",
    )(dest, dest, x1, nf, ys)


def _block_diag(w):
    per = RNN_SUPER // RNN_BLOCK
    w4 = w.reshape(N_SUPER, per, RNN_BLOCK, RNN_BLOCK).astype(BF16)
    rows = [jnp.pad(w4[:, q], ((0, 0), (0, 0), (q * RNN_BLOCK, (per - 1 - q) * RNN_BLOCK))) for q in range(per)]
    return jnp.concatenate(rows, axis=1)


def _mixers(x, meta, conv0, h0, c0, n0, m0, p, w_ready, *, has_head):
    nb, seq, _ = x.shape
    ready = w_ready is not None
    n_null = N_NULL if has_head else 0
    tt = CHUNK if has_head else seq
    tt_mm = MM_TILES * CHUNK if has_head else seq
    xn, gif = _prep(x, meta, p['norm1_w'], p['w_in'], p['b_gif'], has_head=has_head)
    w_bf16 = {}

    def in_proj(key, w, col0, ncols, tn, out_dtype, gelu=False):
        if ready:
            return _mm(xn, w_ready[key], col0=0, ncols=ncols, tn=tn, tt=tt_mm, out_dtype=out_dtype, gelu=gelu,
                       ready=True, name="in_" + key)
        out, w_bf16[key] = _mm(xn, w, col0=col0, ncols=ncols, tn=tn, tt=tt_mm, out_dtype=out_dtype, gelu=gelu,
                               name="in_" + key)
        return out

    tn_rnn = 1280 if ready else 512
    xr = in_proj('xr', p['w_in'], 0, D_RNN, tn_rnn, F32)
    gr = in_proj('gr', p['w_in'], D_RNN, D_RNN, tn_rnn, BF16, gelu=True)
    qkvo = in_proj('qkvo', p['w_in'], 2 * D_RNN, QKVO, 1536 if ready else 1024, BF16)
    mg = in_proj('mg', p['w_in'], MG_OFF, 2 * D_MODEL, 1024, BF16)

    tail0 = jnp.transpose(conv0, (1, 0, 2)).reshape((CONV_W - 1) * nb, D_RNN)
    y_r, h_new, tail = _rglru(xr, gr, tail0, h0, p['conv_w'], p['conv_b'], p['rg_a_b'], p['rg_x_b'],
                              p['rg_lambda'], p['wa_bd'], p['wx_bd'], tt=tt, n_null=n_null)
    conv_new = jnp.transpose(tail.reshape(CONV_W - 1, nb, D_RNN), (1, 0, 2))

    y_m, c_new, n_new, m_new = _mlstm_fused(qkvo, gif, c0, n0, m0, p['mlstm_norm_w'], L=tt, n_null=n_null,
                                            nbk=MLSTM_NBK)

    if ready:
        mix = _merge(y_r, y_m, w_ready['proj_rnn'], w_ready['proj_mlstm'], mg, tn=1024, tt=tt, ready=True)
    else:
        mix, w_bf16['proj_rnn'], w_bf16['proj_mlstm'] = _merge(y_r, y_m, p['w_proj_rnn'], p['w_proj_mlstm'], mg,
                                                               tn=512, tt=tt)
    states = (conv_new[None], h_new[None], c_new[None], n_new[None], m_new[None])
    return mix, states, (w_ready if ready else w_bf16)


def _ffn(mix, x, meta, p, expert_w, *, has_head):
    tt = CHUNK if has_head else x.shape[1]
    routed = has_head
    x1, xn2, gates = _outproj(mix, p['w_out'], x, meta, p['norm2_w'], p['w_router'], p['b_router'],
                              tt=tt, has_head=has_head, routed=routed)
    if routed:
        skip = 1
        dest, tinfo = _plan(gates, tt=tt, skip=skip)
        xs = _dispatch(tinfo, dest, xn2, tt=tt, skip=skip)
        ys = _experts(tinfo, xs, *expert_w)
        return _combine(dest, x1, p['norm_f_w'], ys, tt=tt, skip=skip), expert_w
    y, *expert_w_bf16 = _moe(xn2, gates, x1, *expert_w, p['norm_f_w'])
    return y, tuple(expert_w_bf16)


def kernel(x_prompt, x_sample, state_rglru_conv, state_rglru_h, state_mlstm_C, state_mlstm_n, state_mlstm_m, meta_tokens, norm1_w, w_in, b_gates, conv_w, conv_b, rg_a_w, rg_a_b, rg_x_w, rg_x_b, rg_lambda, mlstm_norm_w, w_proj_rnn, w_proj_mlstm, w_out, norm2_w, w_router_group, b_router_group, w_router_expert, b_router_expert, w_exp_gate, w_exp_up, w_exp_down, norm_f_w):
    l = 0
    w_in_t = jnp.swapaxes(w_in[l], 0, 1)
    pad_r = LANES - N_GROUPS - N_EXPERTS
    p = dict(
        norm1_w=norm1_w[l][None], w_in=w_in_t,
        b_gif=jnp.pad(b_gates[l], (0, LANES - 2 * N_HEADS_M))[None],
        conv_w=conv_w[l], conv_b=conv_b[l][None], rg_a_b=rg_a_b[l][None], rg_x_b=rg_x_b[l][None],
        rg_lambda=rg_lambda[l][None], wa_bd=_block_diag(rg_a_w[l]), wx_bd=_block_diag(rg_x_w[l]),
        mlstm_norm_w=mlstm_norm_w[l][None],
        w_proj_rnn=w_proj_rnn[l], w_proj_mlstm=w_proj_mlstm[l], w_out=w_out[l].astype(BF16),
        norm2_w=norm2_w[l][None],
        w_router=jnp.pad(jnp.concatenate([w_router_group[l], w_router_expert[l]], axis=1),
                         ((0, 0), (0, pad_r))).astype(BF16),
        b_router=jnp.pad(jnp.concatenate([b_router_group[l], b_router_expert[l]]), (0, pad_r))[None],
        norm_f_w=norm_f_w[None],
    )
    nbp = x_prompt.shape[0]
    dt = x_prompt.dtype
    mix_s, st_s, w_bf16 = _mixers(
        x_sample, meta_tokens, state_rglru_conv[l], state_rglru_h[l], state_mlstm_C[l],
        state_mlstm_n[l], state_mlstm_m[l], p, None, has_head=False)
    y_s, expert_w_bf16 = _ffn(mix_s, x_sample, meta_tokens, p, (w_exp_gate[l], w_exp_up[l], w_exp_down[l]),
                              has_head=False)
    mix_p, st_p, _ = _mixers(
        x_prompt, meta_tokens,
        jnp.zeros((nbp, CONV_W - 1, D_RNN), dt), jnp.zeros((nbp, D_RNN), F32),
        jnp.zeros((nbp, N_HEADS_M, DQK_M, DV_M), F32), jnp.zeros((nbp, N_HEADS_M, DQK_M), F32),
        jnp.zeros((nbp, N_HEADS_M), F32), p, w_bf16, has_head=True)
    y_p, _ = _ffn(mix_p, x_prompt, meta_tokens, p, expert_w_bf16, has_head=True)
    return (y_p, y_s) + st_p + st_s
```

```python
import functools

import jax
import jax.numpy as jnp
from jax import lax
from jax.experimental import pallas as pl
from jax.experimental.pallas import tpu as pltpu

F32 = jnp.float32
BF16 = jnp.bfloat16

D_MODEL = 2048
CHUNK = 64
N_META = 16
N_NULL = CHUNK - N_META
D_RNN = 2560
N_RNN_BLOCKS = 16
RNN_BLOCK = D_RNN // N_RNN_BLOCKS
RNN_SUPER = 640
N_SUPER = D_RNN // RNN_SUPER
CONV_W = 4
LRU_C = 8.0
N_HEADS_M = 8
DV_M = D_MODEL // N_HEADS_M
DQK_M = DV_M // 2
HQK = N_HEADS_M * DQK_M
HV = N_HEADS_M * DV_M
QKVO = 2 * HQK + 2 * HV
N_GROUPS = 4
EXPERTS_PER_GROUP = 4
N_EXPERTS = 16
D_EXPERT = 512
PAIRS_PER_GROUP = 6
N_CLASSES = N_GROUPS * PAIRS_PER_GROUP
MM_TILES = 3
MLSTM_NBK = 2
TM_E = 256
ROW_W = D_MODEL + 128
EPS = 1e-6
LANES = 128
GIF_OFF = 2 * D_RNN + QKVO
MG_OFF = GIF_OFF + 2 * N_HEADS_M
VMEM_LIMIT = 56 * 1024 * 1024
NEG_BIG = -1e30


def _cparams(sem):
    return pltpu.CompilerParams(dimension_semantics=sem, vmem_limit_bytes=VMEM_LIMIT)


def _rms(x, w):
    return x * lax.rsqrt(jnp.mean(x * x, axis=-1, keepdims=True) + EPS) * w


def _softplus(x):
    return jnp.maximum(x, 0.0) + jnp.log1p(jnp.exp(-jnp.abs(x)))


def _head_rows(meta_ref, nb):
    head = jnp.concatenate([jnp.zeros((N_NULL, D_MODEL), F32), meta_ref[...]], axis=0)
    return jnp.broadcast_to(head[None], (nb, CHUNK, D_MODEL)).reshape(nb * CHUNK, D_MODEL)


def _x_map(has_head):
    if has_head:
        return lambda ti: (0, jnp.maximum(ti - 1, 0), 0)
    return lambda ti: (0, ti, 0)


def _prep_kernel(x_ref, meta_ref, nw_ref, wg_ref, bg_ref, xn_ref, gif_ref, *, nb, tt, has_head):
    def emit(x):
        xnb = _rms(x, nw_ref[...]).astype(BF16)
        xn_ref[...] = xnb.reshape(nb, tt, D_MODEL)
        gif = lax.dot_general(xnb, wg_ref[...].astype(BF16), (((1,), (1,)), ((), ())),
                              preferred_element_type=F32) + bg_ref[...]
        gif_ref[...] = gif.reshape(nb, tt, LANES)

    if has_head:
        @pl.when(pl.program_id(0) == 0)
        def _():
            emit(_head_rows(meta_ref, nb))

        @pl.when(pl.program_id(0) > 0)
        def _():
            emit(x_ref[...].reshape(nb * tt, D_MODEL))
    else:
        emit(x_ref[...].reshape(nb * tt, D_MODEL))


def _prep(x, meta, norm_w, w_in_t, b_gif, *, has_head):
    nb, seq, _ = x.shape
    assert GIF_OFF % LANES == 0
    tt = CHUNK if has_head else seq
    n_tiles = seq // tt + (1 if has_head else 0)
    tl = n_tiles * tt
    const = lambda ti: (0, 0)
    return pl.pallas_call(
        functools.partial(_prep_kernel, nb=nb, tt=tt, has_head=has_head),
        grid=(n_tiles,),
        in_specs=[pl.BlockSpec((nb, tt, D_MODEL), _x_map(has_head)),
                  pl.BlockSpec((N_META, D_MODEL), const),
                  pl.BlockSpec((1, D_MODEL), const),
                  pl.BlockSpec((LANES, D_MODEL), lambda ti: (GIF_OFF // LANES, 0)),
                  pl.BlockSpec((1, LANES), const)],
        out_specs=[pl.BlockSpec((nb, tt, D_MODEL), lambda ti: (0, ti, 0)),
                   pl.BlockSpec((nb, tt, LANES), lambda ti: (0, ti, 0))],
        out_shape=[jax.ShapeDtypeStruct((nb, tl, D_MODEL), BF16),
                   jax.ShapeDtypeStruct((nb, tl, LANES), F32)],
        compiler_params=_cparams(("arbitrary",)),
        name="prep",
    )(x, meta, norm_w, w_in_t, b_gif)


def _gelu_tanh(g):
    return g * (0.5 * (1.0 + jnp.tanh(0.7978845608028654 * (g + 0.044715 * (g * g * g)))))


def _mm_kernel(lhs_ref, w_ref, out_ref, *wbf_ref, gelu):
    nb, tt, k = lhs_ref.shape
    tn = out_ref.shape[-1]
    if wbf_ref:
        w_bf = wbf_ref[0]

        @pl.when(pl.program_id(1) == 0)
        def _():
            w_bf[...] = w_ref[...].T.astype(BF16)
    else:
        w_bf = w_ref

    acc = jnp.dot(lhs_ref[...].reshape(nb * tt, k), w_bf[...], preferred_element_type=F32)
    if gelu:
        acc = _gelu_tanh(acc)
    out_ref[...] = acc.astype(out_ref.dtype).reshape(nb, tt, tn)


def _mm(lhs, w, *, col0, ncols, tn, tt, out_dtype, gelu=False, ready=False, name):
    nb, tl, k = lhs.shape
    assert tl % tt == 0 and ncols % tn == 0 and col0 % 8 == 0
    lhs_spec = pl.BlockSpec((nb, tt, k), lambda j, i: (0, i, 0))
    out_spec = pl.BlockSpec((nb, tt, tn), lambda j, i: (0, i, j))
    out_shape = jax.ShapeDtypeStruct((nb, tl, ncols), out_dtype)
    wbf_spec = pl.BlockSpec((k, tn), lambda j, i: (0, j))
    if ready:
        in_specs, out_specs, out_shapes = [lhs_spec, wbf_spec], out_spec, out_shape
    else:
        in_specs = [lhs_spec, pl.BlockSpec((pl.Element(tn), pl.Element(k)),
                                           lambda j, i: (pl.multiple_of(col0 + j * tn, 8), 0))]
        out_specs = [out_spec, wbf_spec]
        out_shapes = [out_shape, jax.ShapeDtypeStruct((k, ncols), BF16)]
    return pl.pallas_call(
        functools.partial(_mm_kernel, gelu=gelu),
        grid=(ncols // tn, tl // tt),
        in_specs=in_specs,
        out_specs=out_specs,
        out_shape=out_shapes,
        compiler_params=_cparams(("arbitrary", "arbitrary")),
        name=name,
    )(lhs, w)


def _merge_kernel(yr_ref, ym_ref, wr_ref, wm_ref, gr_ref, gm_ref, out_ref, *wbf_refs):
    nb, tt, tn = out_ref.shape
    rows = nb * tt
    if wbf_refs:
        wr_bf, wm_bf = wbf_refs

        @pl.when(pl.program_id(1) == 0)
        def _():
            wr_bf[...] = wr_ref[...].astype(BF16)
            wm_bf[...] = wm_ref[...].astype(BF16)
    else:
        wr_bf, wm_bf = wr_ref, wm_ref

    pr = jnp.dot(yr_ref[...].reshape(rows, D_RNN), wr_bf[...], preferred_element_type=F32)
    pm = jnp.dot(ym_ref[...].reshape(rows, HV), wm_bf[...], preferred_element_type=F32)
    mix = (jax.nn.sigmoid(gr_ref[...].reshape(rows, tn).astype(F32)) * pr
           + jax.nn.sigmoid(gm_ref[...].reshape(rows, tn).astype(F32)) * pm)
    out_ref[...] = mix.astype(BF16).reshape(nb, tt, tn)


def _merge(y_r, y_m, w_r, w_m, mg, *, tn, tt, ready=False):
    nb, tl, _ = y_r.shape
    nj = D_MODEL // tn
    wr_spec = pl.BlockSpec((D_RNN, tn), lambda j, i: (0, j))
    wm_spec = pl.BlockSpec((HV, tn), lambda j, i: (0, j))
    out_specs = pl.BlockSpec((nb, tt, tn), lambda j, i: (0, i, j))
    out_shape = jax.ShapeDtypeStruct((nb, tl, D_MODEL), BF16)
    if not ready:
        out_specs = [out_specs, wr_spec, wm_spec]
        out_shape = [out_shape, jax.ShapeDtypeStruct(w_r.shape, BF16), jax.ShapeDtypeStruct(w_m.shape, BF16)]
    return pl.pallas_call(
        _merge_kernel,
        grid=(nj, tl // tt),
        in_specs=[pl.BlockSpec((nb, tt, D_RNN), lambda j, i: (0, i, 0)),
                  pl.BlockSpec((nb, tt, HV), lambda j, i: (0, i, 0)),
                  wr_spec, wm_spec,
                  pl.BlockSpec((nb, tt, tn), lambda j, i: (0, i, j)),
                  pl.BlockSpec((nb, tt, tn), lambda j, i: (0, i, nj + j))],
        out_specs=out_specs,
        out_shape=out_shape,
        compiler_params=_cparams(("arbitrary", "arbitrary")),
        name="merge",
    )(y_r, y_m, w_r, w_m, mg, mg)


def _rglru_kernel(xr_ref, gr_ref, tail0_ref, h0_ref, cw_ref, cb_ref, ab_ref, xb_ref, lam_ref,
                  wa_ref, wx_ref, y_ref, hl_ref, tail_ref, xp_s, h_s, a_s, u_s, hs_s, *, nb, tt, n_null):
    ti = pl.program_id(1)
    rt = tt * nb
    tb = (CONV_W - 1) * nb

    def time_major(x):
        return jnp.swapaxes(x, 0, 1).reshape(rt, x.shape[-1])

    @pl.when(ti == 0)
    def _():
        xp_s[0:tb, :] = tail0_ref[...]
        h_s[...] = h0_ref[...]

    xp_s[tb:tb + rt, :] = time_major(xr_ref[...])
    cw = cw_ref[...]
    xc = cb_ref[...] + cw[0:1, :] * xp_s[0:rt, :]
    for j in range(1, CONV_W):
        xc = xc + cw[j:j + 1, :] * xp_s[j * nb:j * nb + rt, :]
    xcb = xc.astype(BF16)
    r = jax.nn.sigmoid(jnp.dot(xcb, wa_ref[0], preferred_element_type=F32) + ab_ref[...])
    i = jax.nn.sigmoid(jnp.dot(xcb, wx_ref[0], preferred_element_type=F32) + xb_ref[...])
    log_a = r * ((-LRU_C) * _softplus(-lam_ref[...]))
    a = jnp.exp(log_a)
    a_s[...] = a
    u = jnp.sqrt(-jnp.tanh(log_a) * (a * a + 1.0)) * (i * xc)
    u_s[...] = u
    if n_null:
        @pl.when(ti == 0)
        def _():
            a_s[n_null * nb:(n_null + 1) * nb, :] = jnp.zeros((nb, a_s.shape[-1]), F32)

    def step(t, h):
        o = pl.multiple_of(t * nb, nb)
        h = a_s[pl.ds(o, nb), :] * h + u_s[pl.ds(o, nb), :]
        hs_s[pl.ds(o, nb), :] = h
        return h

    h = lax.fori_loop(0, tt, step, h_s[...], unroll=8)
    h_s[...] = h
    h_bm = jnp.swapaxes(hs_s[...].reshape(tt, nb, hs_s.shape[-1]), 0, 1)
    y_ref[...] = (gr_ref[...].astype(F32) * h_bm).astype(BF16)
    xp_s[0:tb, :] = xp_s[rt:rt + tb, :]

    @pl.when(ti == pl.num_programs(1) - 1)
    def _():
        hl_ref[...] = h
        tail_ref[...] = xp_s[rt:rt + tb, :]


def _rglru(xr, gr, tail0, h0, cw, cb, ab, xb, lam, wa, wx, *, tt, n_null):
    nb, tl, _ = xr.shape
    rt = tt * nb
    tb = (CONV_W - 1) * nb
    blk = lambda s, ti: (0, ti, s)
    col = lambda s, ti: (0, s)
    return pl.pallas_call(
        functools.partial(_rglru_kernel, nb=nb, tt=tt, n_null=n_null),
        grid=(N_SUPER, tl // tt),
        in_specs=[pl.BlockSpec((nb, tt, RNN_SUPER), blk),
                  pl.BlockSpec((nb, tt, RNN_SUPER), blk),
                  pl.BlockSpec((tb, RNN_SUPER), col),
                  pl.BlockSpec((nb, RNN_SUPER), col),
                  pl.BlockSpec((CONV_W, RNN_SUPER), col),
                  pl.BlockSpec((1, RNN_SUPER), col),
                  pl.BlockSpec((1, RNN_SUPER), col),
                  pl.BlockSpec((1, RNN_SUPER), col),
                  pl.BlockSpec((1, RNN_SUPER), col),
                  pl.BlockSpec((1, RNN_SUPER, RNN_SUPER), lambda s, ti: (s, 0, 0)),
                  pl.BlockSpec((1, RNN_SUPER, RNN_SUPER), lambda s, ti: (s, 0, 0))],
        out_specs=[pl.BlockSpec((nb, tt, RNN_SUPER), blk),
                   pl.BlockSpec((nb, RNN_SUPER), col),
                   pl.BlockSpec((tb, RNN_SUPER), col)],
        out_shape=[jax.ShapeDtypeStruct((nb, tl, D_RNN), BF16),
                   jax.ShapeDtypeStruct((nb, D_RNN), F32),
                   jax.ShapeDtypeStruct((tb, D_RNN), F32)],
        scratch_shapes=[pltpu.VMEM((rt + tb, RNN_SUPER), F32),
                        pltpu.VMEM((nb, RNN_SUPER), F32),
                        pltpu.VMEM((rt, RNN_SUPER), F32),
                        pltpu.VMEM((rt, RNN_SUPER), F32),
                        pltpu.VMEM((rt, RNN_SUPER), F32)],
        compiler_params=_cparams(("arbitrary", "arbitrary")),
        name="rglru",
    )(xr, gr, tail0, h0, cw, cb, ab, xb, lam, wa, wx)


DVA = DV_M + 128


def _mlstm_gates(c, gif_ref, m_s, *, L, n_null):
    hm = N_HEADS_M
    g = gif_ref[...]
    if L < LANES:
        g = jnp.concatenate([g, jnp.zeros((LANES - L, LANES), F32)], axis=0)
    g_t = g.T
    lane = lax.broadcasted_iota(jnp.int32, (hm, LANES), 1)
    valid = lane < L
    li = g_t[0:hm, :]
    lfr = g_t[hm:2 * hm, :]
    lf = jnp.where(valid, jnp.minimum(lfr, 0.0) - jnp.log1p(jnp.exp(-jnp.abs(lfr))), 0.0)
    if n_null:
        null = (lane < n_null) & (c == 0)
        lf = jnp.where(null, 0.0, lf)
        li = jnp.where(null, NEG_BIG, li)
    r128 = lax.broadcasted_iota(jnp.int32, (LANES, LANES), 0)
    c128 = lax.broadcasted_iota(jnp.int32, (LANES, LANES), 1)
    upper = jnp.where(r128 <= c128, 1.0, 0.0)
    b = jnp.dot(lf, upper, preferred_element_type=F32, precision=lax.Precision.HIGHEST)
    a = li - b
    cm = jnp.where(valid, a, -jnp.inf)
    sh = 1
    while sh < L:
        cm = jnp.maximum(cm, jnp.where(lane >= sh, pltpu.roll(cm, sh, axis=1), -jnp.inf))
        sh *= 2
    m0 = m_s[...]
    big_m = jnp.maximum(cm, m0)
    m_last = jnp.max(jnp.where(valid, big_m, -jnp.inf), axis=1, keepdims=True)
    b_last = jnp.sum(lf, axis=1, keepdims=True)
    scale = DQK_M ** -0.5
    e_mv = jnp.exp(-(b + big_m))
    w_int = jnp.exp(m0 - big_m)
    w_end = jnp.exp(a - m_last) * scale
    decay = jnp.exp(m0 - m_last)
    m_s[...] = jnp.broadcast_to(b_last + m_last, (hm, LANES))
    cols = jnp.concatenate([big_m, e_mv, w_int, w_end, jnp.zeros((LANES - 4 * hm, LANES), F32)], axis=0).T
    return a, cols, decay


def _mlstm_heads(gates, q_ref, k_ref, v_ref, og_ref, nw_ref, y_ref, c_s, *, L):
    hm = N_HEADS_M
    scale = DQK_M ** -0.5
    rr = lax.broadcasted_iota(jnp.int32, (L, L), 0)
    cc = lax.broadcasted_iota(jnp.int32, (L, L), 1)
    causal = rr >= cc
    ones_col = jnp.where(lax.broadcasted_iota(jnp.int32, (L, 128), 1) == 0, 1.0, 0.0).astype(BF16)
    nw = nw_ref[...]
    dn_nt = (((1,), (1,)), ((), ()))
    dn_tn = (((0,), (0,)), ((), ()))
    bh = [(bi, h) for bi in range(len(gates)) for h in range(hm)]
    col = lambda bi, k, h: gates[bi][1][0:L, k * hm + h:k * hm + h + 1]
    qs = [q_ref[bi, :, h * DQK_M:(h + 1) * DQK_M] for bi, h in bh]
    ks = [k_ref[bi, :, h * DQK_M:(h + 1) * DQK_M] for bi, h in bh]
    v_augs = [jnp.concatenate([v_ref[bi, :, h * DV_M:(h + 1) * DV_M], ones_col], axis=1) for bi, h in bh]
    c_augs = [c_s[bi, h] for bi, h in bh]
    s_raw = [lax.dot_general(q, k, dn_nt, preferred_element_type=F32) for q, k in zip(qs, ks)]
    inter = [jnp.dot((qs[i].astype(F32) * col(bi, 2, h)).astype(BF16), c_augs[i].astype(BF16),
                     preferred_element_type=F32) for i, (bi, h) in enumerate(bh)]
    s_w = [(s_raw[i] * (scale * jnp.where(causal, jnp.exp(gates[bi][0][h:h + 1, 0:L] - col(bi, 0, h)), 0.0))
            ).astype(BF16) for i, (bi, h) in enumerate(bh)]
    res = [jnp.dot(s_w[i], v_augs[i], preferred_element_type=F32) + inter[i] for i in range(len(bh))]
    upd = [lax.dot_general((ks[i].astype(F32) * col(bi, 3, h)).astype(BF16), v_augs[i], dn_tn,
                           preferred_element_type=F32) for i, (bi, h) in enumerate(bh)]
    for i, (bi, h) in enumerate(bh):
        c_s[bi, h] = gates[bi][2][h:h + 1, 0:1] * c_augs[i] + upd[i]
    for i, (bi, h) in enumerate(bh):
        num = res[i][:, :DV_M]
        denom = jnp.maximum(jnp.abs(res[i][:, DV_M:DV_M + 1]), col(bi, 1, h))
        hh = num / denom
        hh = hh * lax.rsqrt(jnp.mean(hh * hh, axis=-1, keepdims=True) + EPS)
        hh = hh * nw[:, h * DV_M:(h + 1) * DV_M]
        og = og_ref[bi, :, h * DV_M:(h + 1) * DV_M].astype(F32)
        y_ref[bi, :, h * DV_M:(h + 1) * DV_M] = (hh * jax.nn.sigmoid(og)).astype(BF16)


def _mlstm_fused_kernel(q_ref, k_ref, v_ref, og_ref, gif_ref, c0_ref, n0_ref, m0_ref, nw_ref,
                        y_ref, co_ref, no_ref, mo_ref, c_s, m_s, *, L, n_null, nbk):
    c = pl.program_id(1)
    bh = [(bi, h) for bi in range(nbk) for h in range(N_HEADS_M)]

    @pl.when(c == 0)
    def _():
        m_s[...] = m0_ref[...]
        for bi, h in bh:
            c_s[bi, h, :, :DV_M] = c0_ref[bi, h]
            n_tile = jnp.concatenate([n0_ref[bi, h:h + 1, :], jnp.zeros((DQK_M - 1, DQK_M), F32)], axis=0)
            c_s[bi, h, :, DV_M:] = n_tile.T

    gates = [_mlstm_gates(c, gif_ref.at[bi], m_s.at[bi], L=L, n_null=n_null) for bi in range(nbk)]
    _mlstm_heads(gates, q_ref, k_ref, v_ref, og_ref, nw_ref, y_ref, c_s, L=L)

    @pl.when(c == pl.num_programs(1) - 1)
    def _():
        mo_ref[...] = m_s[...]
        for bi, h in bh:
            co_ref[bi, h] = c_s[bi, h, :, :DV_M]
            no_ref[bi, h:h + 1, :] = c_s[bi, h, :, DV_M:].T[0:1, :]


def _mlstm_fused(qkvo, gif, c0, n0, m0, nw, *, L, n_null, nbk):
    nb, tl, _ = qkvo.shape
    assert nb % nbk == 0
    hm = N_HEADS_M
    assert DQK_M == LANES and DVA - DV_M == DQK_M
    m_rep0 = jnp.broadcast_to(m0[..., None], (nb, hm, LANES))
    st4 = lambda b, c: (b, 0, 0, 0)
    st3 = lambda b, c: (b, 0, 0)
    y, c_new, n_new, m_rep = pl.pallas_call(
        functools.partial(_mlstm_fused_kernel, L=L, n_null=n_null, nbk=nbk),
        grid=(nb // nbk, tl // L),
        in_specs=[pl.BlockSpec((nbk, L, HQK), lambda b, c: (b, c, 0)),
                  pl.BlockSpec((nbk, L, HQK), lambda b, c: (b, c, 1)),
                  pl.BlockSpec((nbk, L, HV), lambda b, c: (b, c, 1)),
                  pl.BlockSpec((nbk, L, HV), lambda b, c: (b, c, 2)),
                  pl.BlockSpec((nbk, L, LANES), lambda b, c: (b, c, 0)),
                  pl.BlockSpec((nbk, hm, DQK_M, DV_M), st4),
                  pl.BlockSpec((nbk, hm, DQK_M), st3),
                  pl.BlockSpec((nbk, hm, LANES), st3),
                  pl.BlockSpec((1, HV), lambda b, c: (0, 0))],
        out_specs=[pl.BlockSpec((nbk, L, HV), lambda b, c: (b, c, 0)),
                   pl.BlockSpec((nbk, hm, DQK_M, DV_M), st4),
                   pl.BlockSpec((nbk, hm, DQK_M), st3),
                   pl.BlockSpec((nbk, hm, LANES), st3)],
        out_shape=[jax.ShapeDtypeStruct((nb, tl, HV), BF16),
                   jax.ShapeDtypeStruct((nb, hm, DQK_M, DV_M), F32),
                   jax.ShapeDtypeStruct((nb, hm, DQK_M), F32),
                   jax.ShapeDtypeStruct((nb, hm, LANES), F32)],
        scratch_shapes=[pltpu.VMEM((nbk, hm, DQK_M, DVA), F32),
                        pltpu.VMEM((nbk, hm, LANES), F32)],
        compiler_params=_cparams(("arbitrary", "arbitrary")),
        name="mlstm",
    )(qkvo, qkvo, qkvo, qkvo, gif, c0, n0, m_rep0, nw)
    return y, c_new, n_new, m_rep[..., 0]


def _route(logits, dense):
    tm = logits.shape[0]
    lane = lax.broadcasted_iota(jnp.int32, (tm, LANES), 1)
    lanef = lane.astype(F32)
    gl = jnp.where(lane < N_GROUPS, logits, -jnp.inf)
    gmax = jnp.max(gl, axis=1, keepdims=True)
    gidx = jnp.min(jnp.where(gl == gmax, lanef, float(LANES)), axis=1, keepdims=True)
    gw = 1.0 / jnp.sum(jnp.exp(gl - gmax), axis=1, keepdims=True)
    lo = float(N_GROUPS) + float(EXPERTS_PER_GROUP) * gidx
    sel = (lanef >= lo) & (lanef < lo + float(EXPERTS_PER_GROUP))
    ev = jnp.where(sel, logits, -jnp.inf)
    v1 = jnp.max(ev, axis=1, keepdims=True)
    i1 = jnp.min(jnp.where(ev == v1, lanef, float(LANES)), axis=1, keepdims=True)
    ev2 = jnp.where(lanef == i1, -jnp.inf, ev)
    v2 = jnp.max(ev2, axis=1, keepdims=True)
    i2 = jnp.min(jnp.where(ev2 == v2, lanef, float(LANES)), axis=1, keepdims=True)
    t = jnp.exp(v2 - v1)
    w1 = gw / (1.0 + t)
    w2 = gw * (t / (1.0 + t))
    if dense:
        return jnp.where(lanef == i1, w1, jnp.where(lanef == i2, w2, 0.0))
    l1 = i1 - lo
    l2 = i2 - lo
    first = l1 < l2
    a = jnp.minimum(l1, l2)
    b = jnp.maximum(l1, l2)
    pair = jnp.where(a == 0.0, b - 1.0, jnp.where(a == 1.0, b + 1.0, 5.0))
    cls = float(PAIRS_PER_GROUP) * gidx + pair
    return jnp.where(lane == 0, cls,
                     jnp.where(lane == 1, jnp.where(first, w1, w2),
                               jnp.where(lane == 2, jnp.where(first, w2, w1), 0.0)))


def _outproj_kernel(mix_ref, w_ref, x_ref, meta_ref, n2_ref, wr_ref, br_ref, x1_ref, xn2_ref, gates_ref,
                    *, nb, tt, has_head, routed):
    def emit(x):
        mix = mix_ref[...].reshape(nb * tt, D_MODEL)
        x1 = x + jnp.dot(mix, w_ref[...], preferred_element_type=F32)
        x1_ref[...] = x1.reshape(nb, tt, D_MODEL)
        xn2 = _rms(x1, n2_ref[...]).astype(BF16)
        logits = jnp.dot(xn2, wr_ref[...], preferred_element_type=F32) + br_ref[...]
        info = _route(logits, dense=not routed)
        gates_ref[...] = info.reshape(nb, tt, LANES)
        if routed:
            xn2_ref[:, :, :D_MODEL] = xn2.astype(F32).reshape(nb, tt, D_MODEL)
            xn2_ref[:, :, D_MODEL:] = info.reshape(nb, tt, LANES)
        else:
            xn2_ref[...] = xn2.reshape(nb, tt, D_MODEL)

    if has_head:
        @pl.when(pl.program_id(0) == 0)
        def _():
            emit(_head_rows(meta_ref, nb))

        @pl.when(pl.program_id(0) > 0)
        def _():
            emit(x_ref[...].reshape(nb * tt, D_MODEL))
    else:
        emit(x_ref[...].reshape(nb * tt, D_MODEL))


def _outproj(mix, w_out, x, meta, n2, w_router, b_router, *, tt, has_head, routed):
    nb, tl, _ = mix.shape
    const = lambda i: (0, 0)
    blk = lambda i: (0, i, 0)
    xn2_w, xn2_dt = (ROW_W, F32) if routed else (D_MODEL, BF16)
    return pl.pallas_call(
        functools.partial(_outproj_kernel, nb=nb, tt=tt, has_head=has_head, routed=routed),
        grid=(tl // tt,),
        in_specs=[pl.BlockSpec((nb, tt, D_MODEL), blk),
                  pl.BlockSpec((D_MODEL, D_MODEL), const, pipeline_mode=pl.Buffered(1)),
                  pl.BlockSpec((nb, tt, D_MODEL), _x_map(has_head)),
                  pl.BlockSpec((N_META, D_MODEL), const),
                  pl.BlockSpec((1, D_MODEL), const),
                  pl.BlockSpec((D_MODEL, LANES), const),
                  pl.BlockSpec((1, LANES), const)],
        out_specs=[pl.BlockSpec((nb, tt, D_MODEL), blk),
                   pl.BlockSpec((nb, tt, xn2_w), blk),
                   pl.BlockSpec((nb, tt, LANES), blk)],
        out_shape=[jax.ShapeDtypeStruct((nb, tl, D_MODEL), F32),
                   jax.ShapeDtypeStruct((nb, tl, xn2_w), xn2_dt),
                   jax.ShapeDtypeStruct((nb, tl, LANES), F32)],
        compiler_params=_cparams(("arbitrary",)),
        name="outproj",
    )(mix, w_out, x, meta, n2, w_router, b_router)


def _moe_kernel(xn_ref, gates_ref, x1_ref, wg_ref, wu_ref, wd_ref, nf_ref, y_ref, wgb_ref, wub_ref, wdb_ref,
                acc_s, *, nb, tt):
    e = pl.program_id(0)
    hf = pl.program_id(1)
    rows = nb * tt

    @pl.when((e == 0) & (hf == 0))
    def _():
        acc_s[...] = jnp.zeros_like(acc_s)

    wg = wg_ref[0].astype(BF16)
    wu = wu_ref[0].astype(BF16)
    wd = wd_ref[0].astype(BF16)
    wgb_ref[0] = wg
    wub_ref[0] = wu
    wdb_ref[0] = wd
    x = xn_ref[...].reshape(rows, D_MODEL)
    hg = jnp.dot(x, wg, preferred_element_type=F32)
    hu = jnp.dot(x, wu, preferred_element_type=F32)
    hdn = (hg * jax.nn.sigmoid(hg)) * hu
    yd = jnp.dot(hdn.astype(BF16), wd, preferred_element_type=F32)
    gates = gates_ref[...].reshape(rows, LANES)
    lane = lax.broadcasted_iota(jnp.int32, gates.shape, 1)
    gcol = jnp.sum(jnp.where(lane == e + N_GROUPS, gates, 0.0), axis=1, keepdims=True)
    acc_s[...] += gcol * yd

    @pl.when((e == pl.num_programs(0) - 1) & (hf == pl.num_programs(1) - 1))
    def _():
        x2 = x1_ref[...].reshape(rows, D_MODEL) + acc_s[...]
        y_ref[...] = _rms(x2, nf_ref[...]).reshape(nb, tt, D_MODEL)


def _moe(xn2, gates, x1, wg, wu, wd, nf):
    nb, tt, _ = xn2.shape
    halves = 2
    dh = D_EXPERT // halves
    blk = lambda e, hf: (0, 0, 0)
    up = pl.BlockSpec((1, D_MODEL, dh), lambda e, hf: (e, 0, hf))
    down = pl.BlockSpec((1, dh, D_MODEL), lambda e, hf: (e, hf, 0))
    return pl.pallas_call(
        functools.partial(_moe_kernel, nb=nb, tt=tt),
        grid=(N_EXPERTS, halves),
        in_specs=[pl.BlockSpec((nb, tt, D_MODEL), blk),
                  pl.BlockSpec((nb, tt, LANES), blk),
                  pl.BlockSpec((nb, tt, D_MODEL), blk),
                  up, up, down,
                  pl.BlockSpec((1, D_MODEL), lambda e, hf: (0, 0))],
        out_specs=[pl.BlockSpec((nb, tt, D_MODEL), blk), up, up, down],
        out_shape=[jax.ShapeDtypeStruct((nb, tt, D_MODEL), F32),
                   jax.ShapeDtypeStruct(wg.shape, BF16),
                   jax.ShapeDtypeStruct(wu.shape, BF16),
                   jax.ShapeDtypeStruct(wd.shape, BF16)],
        scratch_shapes=[pltpu.VMEM((nb * tt, D_MODEL), F32)],
        compiler_params=_cparams(("arbitrary", "arbitrary")),
        name="moe",
    )(xn2, gates, x1, wg, wu, wd, nf)


_NT = (((1,), (1,)), ((), ()))


def _plan_kernel(ri_ref, dest_ref, tinfo_ref, cnt_s, carry_s, offs_s, *, rows):
    ph = pl.program_id(0)
    i = pl.program_id(1)
    ri = ri_ref[...].reshape(rows, LANES)
    lanef = lax.broadcasted_iota(jnp.int32, (rows, LANES), 1).astype(F32)
    oh = jnp.where(lanef == ri[:, 0:1], 1.0, 0.0)
    colsum = jnp.sum(oh, axis=0, keepdims=True)
    r128 = lax.broadcasted_iota(jnp.int32, (LANES, LANES), 0)
    c128 = lax.broadcasted_iota(jnp.int32, (LANES, LANES), 1)
    eye = jnp.where(r128 == c128, 1.0, 0.0).astype(BF16)

    @pl.when((ph == 0) & (i == 0))
    def _():
        cnt_s[...] = jnp.zeros_like(cnt_s)

    @pl.when(ph == 0)
    def _():
        cnt_s[...] += colsum

    @pl.when((ph == 1) & (i == 0))
    def _():
        tiles = jnp.floor((cnt_s[...] + float(TM_E - 1)) * (1.0 / TM_E))
        upper = jnp.where(r128 <= c128, 1.0, 0.0).astype(BF16)
        cum_incl = jnp.dot(jnp.broadcast_to(tiles, (8, LANES)).astype(BF16), upper, preferred_element_type=F32)
        offs_s[...] = (cum_incl[0:1, :] - tiles) * float(TM_E)
        carry_s[...] = jnp.zeros_like(carry_s)
        cum_col = lax.dot_general(eye, cum_incl.astype(BF16), _NT, preferred_element_type=F32)
        ended = (cum_col[:, 0:1] <= c128.astype(F32)) & (r128 < N_CLASSES)
        tcls = jnp.sum(jnp.where(ended, 1.0, 0.0), axis=0, keepdims=True)
        total = cum_incl[0:1, LANES - 1:LANES]
        row8 = lax.broadcasted_iota(jnp.int32, (8, LANES), 0)
        tinfo_ref[...] = jnp.where(row8 == 0, tcls, total).astype(jnp.int32)

    @pl.when(ph == 1)
    def _():
        tri = jnp.where(lax.broadcasted_iota(jnp.int32, (rows, rows), 0)
                        > lax.broadcasted_iota(jnp.int32, (rows, rows), 1), 1.0, 0.0).astype(BF16)
        rank = jnp.dot(tri, oh.astype(BF16), preferred_element_type=F32)
        dest = jnp.sum(oh * (rank + (offs_s[...] + carry_s[...])), axis=1, keepdims=True)
        carry_s[...] += colsum
        dhi = jnp.floor(dest * (1.0 / 256.0))
        dlo = dest - 256.0 * dhi
        digits = jnp.where(lanef == 0.0, dlo, jnp.where(lanef == 1.0, dhi, 0.0)).astype(BF16)
        dt = lax.dot_general(eye, digits, _NT, preferred_element_type=F32)
        dest_ref[...] = (dt[0:1, :] + 256.0 * dt[1:2, :]).astype(jnp.int32).reshape(1, 1, rows)


def _plan(rinfo, *, tt, skip):
    nb, tl, _ = rinfo.shape
    rows = nb * tt
    nt = tl // tt - skip
    assert (nt * rows) // TM_E + N_CLASSES <= LANES and rows % LANES == 0
    return pl.pallas_call(
        functools.partial(_plan_kernel, rows=rows),
        grid=(2, nt),
        in_specs=[pl.BlockSpec((nb, tt, LANES), lambda ph, i: (0, i + skip, 0))],
        out_specs=[pl.BlockSpec((1, 1, rows), lambda ph, i: (i * ph, 0, 0)),
                   pl.BlockSpec((8, LANES), lambda ph, i: (0, 0))],
        out_shape=[jax.ShapeDtypeStruct((nt, 1, rows), jnp.int32),
                   jax.ShapeDtypeStruct((8, LANES), jnp.int32)],
        scratch_shapes=[pltpu.VMEM((1, LANES), F32)] * 3,
        compiler_params=_cparams(("arbitrary", "arbitrary")),
        name="moe_plan",
    )(rinfo)


def _dispatch_kernel(tinfo_ref, dest_ref, x_ref, xs_ref, zbuf, zsem, sem, *, nb, tt):
    i = pl.program_id(0)

    @pl.when(i == 0)
    def _():
        zbuf[...] = jnp.zeros_like(zbuf)
        n_used = tinfo_ref[1, 0]

        def fill(start):
            def body(j, carry):
                last = (j >= n_used - 1) | (tinfo_ref[0, jnp.minimum(j + 1, LANES - 1)] != tinfo_ref[0, j])

                @pl.when(last)
                def _():
                    cp = pltpu.make_async_copy(
                        zbuf, xs_ref.at[pl.ds(pl.multiple_of(j * TM_E, TM_E), TM_E), :], zsem)
                    if start:
                        cp.start()
                    else:
                        cp.wait()
                return carry
            lax.fori_loop(0, xs_ref.shape[0] // TM_E, body, 0)

        fill(True)
        fill(False)

    for b in range(nb):
        for t in range(tt):
            d = dest_ref[0, 0, b * tt + t]
            pltpu.make_async_copy(x_ref.at[b, pl.ds(t, 1), :], xs_ref.at[pl.ds(d, 1), :], sem).start()
    whole = xs_ref.at[pl.ds(0, nb * tt), :]
    pltpu.make_async_copy(whole, whole, sem).wait()


def _dispatch(tinfo, dest, xrow, *, tt, skip):
    nb, tl, _ = xrow.shape
    nt = tl // tt - skip
    n_sorted = ((nt * nb * tt) // TM_E + N_CLASSES) * TM_E
    return pl.pallas_call(
        functools.partial(_dispatch_kernel, nb=nb, tt=tt),
        grid_spec=pltpu.PrefetchScalarGridSpec(
            num_scalar_prefetch=1,
            grid=(nt,),
            in_specs=[pl.BlockSpec((1, 1, nb * tt), lambda i, ti: (i, 0, 0), memory_space=pltpu.SMEM),
                      pl.BlockSpec((nb, tt, ROW_W), lambda i, ti: (0, i + skip, 0))],
            out_specs=pl.BlockSpec(memory_space=pl.ANY),
            scratch_shapes=[pltpu.VMEM((TM_E, ROW_W), F32),
                            pltpu.SemaphoreType.DMA(()),
                            pltpu.SemaphoreType.DMA(())]),
        out_shape=jax.ShapeDtypeStruct((n_sorted, ROW_W), F32),
        compiler_params=_cparams(("arbitrary",)),
        name="moe_dispatch",
    )(tinfo, dest, xrow)


def _class_expert(c, hi):
    g = c // PAIRS_PER_GROUP
    p = c - PAIRS_PER_GROUP * g
    ge3 = (p >= 3).astype(jnp.int32)
    ge5 = (p >= 5).astype(jnp.int32)
    local = (p + 1 - 2 * ge3 - ge5) if hi else (ge3 + ge5)
    return EXPERTS_PER_GROUP * g + local


def _experts_kernel(tinfo_ref, xs_ref, wga_ref, wua_ref, wda_ref, wgb_ref, wub_ref, wdb_ref, ys_ref):
    @pl.when(pl.program_id(0) < tinfo_ref[1, 0])
    def _():
        x = xs_ref[:, :D_MODEL].astype(BF16)

        def ffn(wg_ref, wu_ref, wd_ref):
            hg = jnp.dot(x, wg_ref[0], preferred_element_type=F32)
            hu = jnp.dot(x, wu_ref[0], preferred_element_type=F32)
            hdn = (hg * jax.nn.sigmoid(hg)) * hu
            return jnp.dot(hdn.astype(BF16), wd_ref[0], preferred_element_type=F32)

        y = (xs_ref[:, D_MODEL + 1:D_MODEL + 2] * ffn(wga_ref, wua_ref, wda_ref)
             + xs_ref[:, D_MODEL + 2:D_MODEL + 3] * ffn(wgb_ref, wub_ref, wdb_ref))
        tiles = jnp.stack([y[:, c * LANES:(c + 1) * LANES] for c in range(D_MODEL // LANES)], axis=0)
        ys_ref[...] = jnp.swapaxes(tiles, 0, 1)

    @pl.when(pl.program_id(0) >= tinfo_ref[1, 0])
    def _():
        ys_ref[...] = jnp.zeros_like(ys_ref)


def _experts(tinfo, xs, wg, wu, wd):
    n_sorted = xs.shape[0]
    nt = n_sorted // TM_E

    def tile(j, ti):
        return jnp.minimum(j, ti[1, 0] - 1)

    def wmap(hi):
        return lambda j, ti: (_class_expert(ti[0, tile(j, ti)], hi), 0, 0)

    up = lambda hi: pl.BlockSpec((1, D_MODEL, D_EXPERT), wmap(hi))
    down = lambda hi: pl.BlockSpec((1, D_EXPERT, D_MODEL), wmap(hi))
    return pl.pallas_call(
        _experts_kernel,
        grid_spec=pltpu.PrefetchScalarGridSpec(
            num_scalar_prefetch=1,
            grid=(nt,),
            in_specs=[pl.BlockSpec((TM_E, ROW_W), lambda j, ti: (tile(j, ti), 0)),
                      up(0), up(0), down(0), up(1), up(1), down(1)],
            out_specs=pl.BlockSpec((TM_E, D_MODEL // LANES, LANES), lambda j, ti: (j, 0, 0))),
        out_shape=jax.ShapeDtypeStruct((n_sorted, D_MODEL // LANES, LANES), F32),
        compiler_params=_cparams(("arbitrary",)),
        name="moe_experts",
    )(tinfo, xs, wg, wu, wd, wg, wu, wd)


def _combine_kernel(dcur_ref, dnxt_ref, x1_ref, nf_ref, ys_ref, y_ref, buf, sem, *, nb, tt):
    i = pl.program_id(0)
    slot = i % 2

    def gather(dref, s):
        for b in range(nb):
            for t in range(tt):
                d = dref[0, 0, b * tt + t]
                pltpu.make_async_copy(ys_ref.at[pl.ds(d, 1)], buf.at[s, pl.ds(b * tt + t, 1)], sem.at[s]).start()

    @pl.when(i == 0)
    def _():
        gather(dcur_ref, 0)

    @pl.when(i + 1 < pl.num_programs(0))
    def _():
        gather(dnxt_ref, 1 - slot)

    whole = ys_ref.at[pl.ds(0, nb * tt)]
    pltpu.make_async_copy(whole, whole, sem.at[slot]).wait()
    tiles = jnp.swapaxes(buf[slot], 0, 1)
    moe = jnp.concatenate([tiles[c] for c in range(D_MODEL // LANES)], axis=1)
    x2 = x1_ref[...].reshape(nb * tt, D_MODEL) + moe
    y_ref[...] = _rms(x2, nf_ref[...]).reshape(nb, tt, D_MODEL)


def _combine(dest, x1, nf, ys, *, tt, skip):
    nb, tl, _ = x1.shape
    nt = tl // tt - skip
    dspec = lambda f: pl.BlockSpec((1, 1, nb * tt), f, memory_space=pltpu.SMEM)
    return pl.pallas_call(
        functools.partial(_combine_kernel, nb=nb, tt=tt),
        grid=(nt,),
        in_specs=[dspec(lambda i: (i, 0, 0)),
                  dspec(lambda i: (jnp.minimum(i + 1, nt - 1), 0, 0)),
                  pl.BlockSpec((nb, tt, D_MODEL), lambda i: (0, i + skip, 0)),
                  pl.BlockSpec((1, D_MODEL), lambda i: (0, 0)),
                  pl.BlockSpec(memory_space=pl.ANY)],
        out_specs=pl.BlockSpec((nb, tt, D_MODEL), lambda i: (0, i, 0)),
        out_shape=jax.ShapeDtypeStruct((nb, nt * tt, D_MODEL), F32),
        scratch_shapes=[pltpu.VMEM((2, nb * tt, D_MODEL // LANES, LANES), F32),
                        pltpu.SemaphoreType.DMA((2,))],
        compiler_params=_cparams(("arbitrary",)),
        name="moe_combine",
    )(dest, dest, x1, nf, ys)


def _block_diag(w):
    per = RNN_SUPER // RNN_BLOCK
    w4 = w.reshape(N_SUPER, per, RNN_BLOCK, RNN_BLOCK).astype(BF16)
    rows = [jnp.pad(w4[:, q], ((0, 0), (0, 0), (q * RNN_BLOCK, (per - 1 - q) * RNN_BLOCK))) for q in range(per)]
    return jnp.concatenate(rows, axis=1)


def _mixers(x, meta, conv0, h0, c0, n0, m0, p, w_ready, *, has_head):
    nb, seq, _ = x.shape
    ready = w_ready is not None
    n_null = N_NULL if has_head else 0
    tt = CHUNK if has_head else seq
    tt_mm = MM_TILES * CHUNK if has_head else seq
    xn, gif = _prep(x, meta, p['norm1_w'], p['w_in'], p['b_gif'], has_head=has_head)
    w_bf16 = {}

    def in_proj(key, w, col0, ncols, tn, out_dtype, gelu=False):
        if ready:
            return _mm(xn, w_ready[key], col0=0, ncols=ncols, tn=tn, tt=tt_mm, out_dtype=out_dtype, gelu=gelu,
                       ready=True, name="in_" + key)
        out, w_bf16[key] = _mm(xn, w, col0=col0, ncols=ncols, tn=tn, tt=tt_mm, out_dtype=out_dtype, gelu=gelu,
                               name="in_" + key)
        return out

    tn_rnn = 1280 if ready else 512
    xr = in_proj('xr', p['w_in'], 0, D_RNN, tn_rnn, F32)
    gr = in_proj('gr', p['w_in'], D_RNN, D_RNN, tn_rnn, BF16, gelu=True)
    qkvo = in_proj('qkvo', p['w_in'], 2 * D_RNN, QKVO, 1536 if ready else 1024, BF16)
    mg = in_proj('mg', p['w_in'], MG_OFF, 2 * D_MODEL, 1024, BF16)

    tail0 = jnp.transpose(conv0, (1, 0, 2)).reshape((CONV_W - 1) * nb, D_RNN)
    y_r, h_new, tail = _rglru(xr, gr, tail0, h0, p['conv_w'], p['conv_b'], p['rg_a_b'], p['rg_x_b'],
                              p['rg_lambda'], p['wa_bd'], p['wx_bd'], tt=tt, n_null=n_null)
    conv_new = jnp.transpose(tail.reshape(CONV_W - 1, nb, D_RNN), (1, 0, 2))

    y_m, c_new, n_new, m_new = _mlstm_fused(qkvo, gif, c0, n0, m0, p['mlstm_norm_w'], L=tt, n_null=n_null,
                                            nbk=MLSTM_NBK)

    if ready:
        mix = _merge(y_r, y_m, w_ready['proj_rnn'], w_ready['proj_mlstm'], mg, tn=1024, tt=tt, ready=True)
    else:
        mix, w_bf16['proj_rnn'], w_bf16['proj_mlstm'] = _merge(y_r, y_m, p['w_proj_rnn'], p['w_proj_mlstm'], mg,
                                                               tn=512, tt=tt)
    states = (conv_new[None], h_new[None], c_new[None], n_new[None], m_new[None])
    return mix, states, (w_ready if ready else w_bf16)


def _ffn(mix, x, meta, p, expert_w, *, has_head):
    tt = CHUNK if has_head else x.shape[1]
    routed = has_head
    x1, xn2, gates = _outproj(mix, p['w_out'], x, meta, p['norm2_w'], p['w_router'], p['b_router'],
                              tt=tt, has_head=has_head, routed=routed)
    if routed:
        skip = 1
        dest, tinfo = _plan(gates, tt=tt, skip=skip)
        xs = _dispatch(tinfo, dest, xn2, tt=tt, skip=skip)
        ys = _experts(tinfo, xs, *expert_w)
        return _combine(dest, x1, p['norm_f_w'], ys, tt=tt, skip=skip), expert_w
    y, *expert_w_bf16 = _moe(xn2, gates, x1, *expert_w, p['norm_f_w'])
    return y, tuple(expert_w_bf16)


def kernel(x_prompt, x_sample, state_rglru_conv, state_rglru_h, state_mlstm_C, state_mlstm_n, state_mlstm_m, meta_tokens, norm1_w, w_in, b_gates, conv_w, conv_b, rg_a_w, rg_a_b, rg_x_w, rg_x_b, rg_lambda, mlstm_norm_w, w_proj_rnn, w_proj_mlstm, w_out, norm2_w, w_router_group, b_router_group, w_router_expert, b_router_expert, w_exp_gate, w_exp_up, w_exp_down, norm_f_w):
    l = 0
    w_in_t = jnp.swapaxes(w_in[l], 0, 1)
    pad_r = LANES - N_GROUPS - N_EXPERTS
    p = dict(
        norm1_w=norm1_w[l][None], w_in=w_in_t,
        b_gif=jnp.pad(b_gates[l], (0, LANES - 2 * N_HEADS_M))[None],
        conv_w=conv_w[l], conv_b=conv_b[l][None], rg_a_b=rg_a_b[l][None], rg_x_b=rg_x_b[l][None],
        rg_lambda=rg_lambda[l][None], wa_bd=_block_diag(rg_a_w[l]), wx_bd=_block_diag(rg_x_w[l]),
        mlstm_norm_w=mlstm_norm_w[l][None],
        w_proj_rnn=w_proj_rnn[l], w_proj_mlstm=w_proj_mlstm[l], w_out=w_out[l].astype(BF16),
        norm2_w=norm2_w[l][None],
        w_router=jnp.pad(jnp.concatenate([w_router_group[l], w_router_expert[l]], axis=1),
                         ((0, 0), (0, pad_r))).astype(BF16),
        b_router=jnp.pad(jnp.concatenate([b_router_group[l], b_router_expert[l]]), (0, pad_r))[None],
        norm_f_w=norm_f_w[None],
    )
    nbp = x_prompt.shape[0]
    dt = x_prompt.dtype
    mix_s, st_s, w_bf16 = _mixers(
        x_sample, meta_tokens, state_rglru_conv[l], state_rglru_h[l], state_mlstm_C[l],
        state_mlstm_n[l], state_mlstm_m[l], p, None, has_head=False)
    y_s, expert_w_bf16 = _ffn(mix_s, x_sample, meta_tokens, p, (w_exp_gate[l], w_exp_up[l], w_exp_down[l]),
                              has_head=False)
    mix_p, st_p, _ = _mixers(
        x_prompt, meta_tokens,
        jnp.zeros((nbp, CONV_W - 1, D_RNN), dt), jnp.zeros((nbp, D_RNN), F32),
        jnp.zeros((nbp, N_HEADS_M, DQK_M, DV_M), F32), jnp.zeros((nbp, N_HEADS_M, DQK_M), F32),
        jnp.zeros((nbp, N_HEADS_M), F32), p, w_bf16, has_head=True)
    y_p, _ = _ffn(mix_p, x_prompt, meta_tokens, p, expert_w_bf16, has_head=True)
    return (y_p, y_s) + st_p + st_s
```

```python
import functools

import jax
import jax.numpy as jnp
from jax import lax
from jax.experimental import pallas as pl
from jax.experimental.pallas import tpu as pltpu

F32 = jnp.float32
BF16 = jnp.bfloat16

D_MODEL = 2048
CHUNK = 64
N_META = 16
N_NULL = CHUNK - N_META
D_RNN = 2560
N_RNN_BLOCKS = 16
RNN_BLOCK = D_RNN // N_RNN_BLOCKS
RNN_SUPER = 640
N_SUPER = D_RNN // RNN_SUPER
CONV_W = 4
LRU_C = 8.0
N_HEADS_M = 8
DV_M = D_MODEL // N_HEADS_M
DQK_M = DV_M // 2
HQK = N_HEADS_M * DQK_M
HV = N_HEADS_M * DV_M
QKVO = 2 * HQK + 2 * HV
N_GROUPS = 4
EXPERTS_PER_GROUP = 4
N_EXPERTS = 16
D_EXPERT = 512
PAIRS_PER_GROUP = 6
N_CLASSES = N_GROUPS * PAIRS_PER_GROUP
MM_TILES = 3
MLSTM_NBK = 2
TM_E = 256
ROW_W = D_MODEL + 128
EPS = 1e-6
LANES = 128
GIF_OFF = 2 * D_RNN + QKVO
MG_OFF = GIF_OFF + 2 * N_HEADS_M
VMEM_LIMIT = 56 * 1024 * 1024
NEG_BIG = -1e30


def _cparams(sem):
    return pltpu.CompilerParams(dimension_semantics=sem, vmem_limit_bytes=VMEM_LIMIT)


def _rms(x, w):
    return x * lax.rsqrt(jnp.mean(x * x, axis=-1, keepdims=True) + EPS) * w


def _softplus(x):
    return jnp.maximum(x, 0.0) + jnp.log1p(jnp.exp(-jnp.abs(x)))


def _head_rows(meta_ref, nb):
    head = jnp.concatenate([jnp.zeros((N_NULL, D_MODEL), F32), meta_ref[...]], axis=0)
    return jnp.broadcast_to(head[None], (nb, CHUNK, D_MODEL)).reshape(nb * CHUNK, D_MODEL)


def _x_map(has_head):
    if has_head:
        return lambda ti: (0, jnp.maximum(ti - 1, 0), 0)
    return lambda ti: (0, ti, 0)


def _prep_kernel(x_ref, meta_ref, nw_ref, wg_ref, bg_ref, xn_ref, gif_ref, *, nb, tt, has_head):
    def emit(x):
        xnb = _rms(x, nw_ref[...]).astype(BF16)
        xn_ref[...] = xnb.reshape(nb, tt, D_MODEL)
        gif = lax.dot_general(xnb, wg_ref[...].astype(BF16), (((1,), (1,)), ((), ())),
                              preferred_element_type=F32) + bg_ref[...]
        gif_ref[...] = gif.reshape(nb, tt, LANES)

    if has_head:
        @pl.when(pl.program_id(0) == 0)
        def _():
            emit(_head_rows(meta_ref, nb))

        @pl.when(pl.program_id(0) > 0)
        def _():
            emit(x_ref[...].reshape(nb * tt, D_MODEL))
    else:
        emit(x_ref[...].reshape(nb * tt, D_MODEL))


def _prep(x, meta, norm_w, w_in_t, b_gif, *, has_head):
    nb, seq, _ = x.shape
    assert GIF_OFF % LANES == 0
    tt = CHUNK if has_head else seq
    n_tiles = seq // tt + (1 if has_head else 0)
    tl = n_tiles * tt
    const = lambda ti: (0, 0)
    return pl.pallas_call(
        functools.partial(_prep_kernel, nb=nb, tt=tt, has_head=has_head),
        grid=(n_tiles,),
        in_specs=[pl.BlockSpec((nb, tt, D_MODEL), _x_map(has_head)),
                  pl.BlockSpec((N_META, D_MODEL), const),
                  pl.BlockSpec((1, D_MODEL), const),
                  pl.BlockSpec((LANES, D_MODEL), lambda ti: (GIF_OFF // LANES, 0)),
                  pl.BlockSpec((1, LANES), const)],
        out_specs=[pl.BlockSpec((nb, tt, D_MODEL), lambda ti: (0, ti, 0)),
                   pl.BlockSpec((nb, tt, LANES), lambda ti: (0, ti, 0))],
        out_shape=[jax.ShapeDtypeStruct((nb, tl, D_MODEL), BF16),
                   jax.ShapeDtypeStruct((nb, tl, LANES), F32)],
        compiler_params=_cparams(("arbitrary",)),
        name="prep",
    )(x, meta, norm_w, w_in_t, b_gif)


def _gelu_tanh(g):
    return g * (0.5 * (1.0 + jnp.tanh(0.7978845608028654 * (g + 0.044715 * (g * g * g)))))


def _mm_kernel(lhs_ref, w_ref, out_ref, *wbf_ref, gelu):
    nb, tt, k = lhs_ref.shape
    tn = out_ref.shape[-1]
    if wbf_ref:
        w_bf = wbf_ref[0]

        @pl.when(pl.program_id(1) == 0)
        def _():
            w_bf[...] = w_ref[...].T.astype(BF16)
    else:
        w_bf = w_ref

    acc = jnp.dot(lhs_ref[...].reshape(nb * tt, k), w_bf[...], preferred_element_type=F32)
    if gelu:
        acc = _gelu_tanh(acc)
    out_ref[...] = acc.astype(out_ref.dtype).reshape(nb, tt, tn)


def _mm(lhs, w, *, col0, ncols, tn, tt, out_dtype, gelu=False, ready=False, name):
    nb, tl, k = lhs.shape
    assert tl % tt == 0 and ncols % tn == 0 and col0 % 8 == 0
    lhs_spec = pl.BlockSpec((nb, tt, k), lambda j, i: (0, i, 0))
    out_spec = pl.BlockSpec((nb, tt, tn), lambda j, i: (0, i, j))
    out_shape = jax.ShapeDtypeStruct((nb, tl, ncols), out_dtype)
    wbf_spec = pl.BlockSpec((k, tn), lambda j, i: (0, j))
    if ready:
        in_specs, out_specs, out_shapes = [lhs_spec, wbf_spec], out_spec, out_shape
    else:
        in_specs = [lhs_spec, pl.BlockSpec((pl.Element(tn), pl.Element(k)),
                                           lambda j, i: (pl.multiple_of(col0 + j * tn, 8), 0))]
        out_specs = [out_spec, wbf_spec]
        out_shapes = [out_shape, jax.ShapeDtypeStruct((k, ncols), BF16)]
    return pl.pallas_call(
        functools.partial(_mm_kernel, gelu=gelu),
        grid=(ncols // tn, tl // tt),
        in_specs=in_specs,
        out_specs=out_specs,
        out_shape=out_shapes,
        compiler_params=_cparams(("arbitrary", "arbitrary")),
        name=name,
    )(lhs, w)


def _merge_kernel(yr_ref, ym_ref, wr_ref, wm_ref, gr_ref, gm_ref, out_ref, *wbf_refs):
    nb, tt, tn = out_ref.shape
    rows = nb * tt
    if wbf_refs:
        wr_bf, wm_bf = wbf_refs

        @pl.when(pl.program_id(1) == 0)
        def _():
            wr_bf[...] = wr_ref[...].astype(BF16)
            wm_bf[...] = wm_ref[...].astype(BF16)
    else:
        wr_bf, wm_bf = wr_ref, wm_ref

    pr = jnp.dot(yr_ref[...].reshape(rows, D_RNN), wr_bf[...], preferred_element_type=F32)
    pm = jnp.dot(ym_ref[...].reshape(rows, HV), wm_bf[...], preferred_element_type=F32)
    mix = (jax.nn.sigmoid(gr_ref[...].reshape(rows, tn).astype(F32)) * pr
           + jax.nn.sigmoid(gm_ref[...].reshape(rows, tn).astype(F32)) * pm)
    out_ref[...] = mix.astype(BF16).reshape(nb, tt, tn)


def _merge(y_r, y_m, w_r, w_m, mg, *, tn, tt, ready=False):
    nb, tl, _ = y_r.shape
    nj = D_MODEL // tn
    wr_spec = pl.BlockSpec((D_RNN, tn), lambda j, i: (0, j))
    wm_spec = pl.BlockSpec((HV, tn), lambda j, i: (0, j))
    out_specs = pl.BlockSpec((nb, tt, tn), lambda j, i: (0, i, j))
    out_shape = jax.ShapeDtypeStruct((nb, tl, D_MODEL), BF16)
    if not ready:
        out_specs = [out_specs, wr_spec, wm_spec]
        out_shape = [out_shape, jax.ShapeDtypeStruct(w_r.shape, BF16), jax.ShapeDtypeStruct(w_m.shape, BF16)]
    return pl.pallas_call(
        _merge_kernel,
        grid=(nj, tl // tt),
        in_specs=[pl.BlockSpec((nb, tt, D_RNN), lambda j, i: (0, i, 0)),
                  pl.BlockSpec((nb, tt, HV), lambda j, i: (0, i, 0)),
                  wr_spec, wm_spec,
                  pl.BlockSpec((nb, tt, tn), lambda j, i: (0, i, j)),
                  pl.BlockSpec((nb, tt, tn), lambda j, i: (0, i, nj + j))],
        out_specs=out_specs,
        out_shape=out_shape,
        compiler_params=_cparams(("arbitrary", "arbitrary")),
        name="merge",
    )(y_r, y_m, w_r, w_m, mg, mg)


def _rglru_kernel(xr_ref, gr_ref, tail0_ref, h0_ref, cw_ref, cb_ref, ab_ref, xb_ref, lam_ref,
                  wa_ref, wx_ref, y_ref, hl_ref, tail_ref, xp_s, h_s, a_s, u_s, hs_s, *, nb, tt, n_null):
    ti = pl.program_id(1)
    rt = tt * nb
    tb = (CONV_W - 1) * nb

    def time_major(x):
        return jnp.swapaxes(x, 0, 1).reshape(rt, x.shape[-1])

    @pl.when(ti == 0)
    def _():
        xp_s[0:tb, :] = tail0_ref[...]
        h_s[...] = h0_ref[...]

    xp_s[tb:tb + rt, :] = time_major(xr_ref[...])
    cw = cw_ref[...]
    xc = cb_ref[...] + cw[0:1, :] * xp_s[0:rt, :]
    for j in range(1, CONV_W):
        xc = xc + cw[j:j + 1, :] * xp_s[j * nb:j * nb + rt, :]
    xcb = xc.astype(BF16)
    r = jax.nn.sigmoid(jnp.dot(xcb, wa_ref[0], preferred_element_type=F32) + ab_ref[...])
    i = jax.nn.sigmoid(jnp.dot(xcb, wx_ref[0], preferred_element_type=F32) + xb_ref[...])
    log_a = r * ((-LRU_C) * _softplus(-lam_ref[...]))
    a = jnp.exp(log_a)
    a_s[...] = a
    u = jnp.sqrt(-jnp.tanh(log_a) * (a * a + 1.0)) * (i * xc)
    u_s[...] = u
    if n_null:
        @pl.when(ti == 0)
        def _():
            a_s[n_null * nb:(n_null + 1) * nb, :] = jnp.zeros((nb, a_s.shape[-1]), F32)

    def step(t, h):
        o = pl.multiple_of(t * nb, nb)
        h = a_s[pl.ds(o, nb), :] * h + u_s[pl.ds(o, nb), :]
        hs_s[pl.ds(o, nb), :] = h
        return h

    h = lax.fori_loop(0, tt, step, h_s[...], unroll=8)
    h_s[...] = h
    h_bm = jnp.swapaxes(hs_s[...].reshape(tt, nb, hs_s.shape[-1]), 0, 1)
    y_ref[...] = (gr_ref[...].astype(F32) * h_bm).astype(BF16)
    xp_s[0:tb, :] = xp_s[rt:rt + tb, :]

    @pl.when(ti == pl.num_programs(1) - 1)
    def _():
        hl_ref[...] = h
        tail_ref[...] = xp_s[rt:rt + tb, :]


def _rglru(xr, gr, tail0, h0, cw, cb, ab, xb, lam, wa, wx, *, tt, n_null):
    nb, tl, _ = xr.shape
    rt = tt * nb
    tb = (CONV_W - 1) * nb
    blk = lambda s, ti: (0, ti, s)
    col = lambda s, ti: (0, s)
    return pl.pallas_call(
        functools.partial(_rglru_kernel, nb=nb, tt=tt, n_null=n_null),
        grid=(N_SUPER, tl // tt),
        in_specs=[pl.BlockSpec((nb, tt, RNN_SUPER), blk),
                  pl.BlockSpec((nb, tt, RNN_SUPER), blk),
                  pl.BlockSpec((tb, RNN_SUPER), col),
                  pl.BlockSpec((nb, RNN_SUPER), col),
                  pl.BlockSpec((CONV_W, RNN_SUPER), col),
                  pl.BlockSpec((1, RNN_SUPER), col),
                  pl.BlockSpec((1, RNN_SUPER), col),
                  pl.BlockSpec((1, RNN_SUPER), col),
                  pl.BlockSpec((1, RNN_SUPER), col),
                  pl.BlockSpec((1, RNN_SUPER, RNN_SUPER), lambda s, ti: (s, 0, 0)),
                  pl.BlockSpec((1, RNN_SUPER, RNN_SUPER), lambda s, ti: (s, 0, 0))],
        out_specs=[pl.BlockSpec((nb, tt, RNN_SUPER), blk),
                   pl.BlockSpec((nb, RNN_SUPER), col),
                   pl.BlockSpec((tb, RNN_SUPER), col)],
        out_shape=[jax.ShapeDtypeStruct((nb, tl, D_RNN), BF16),
                   jax.ShapeDtypeStruct((nb, D_RNN), F32),
                   jax.ShapeDtypeStruct((tb, D_RNN), F32)],
        scratch_shapes=[pltpu.VMEM((rt + tb, RNN_SUPER), F32),
                        pltpu.VMEM((nb, RNN_SUPER), F32),
                        pltpu.VMEM((rt, RNN_SUPER), F32),
                        pltpu.VMEM((rt, RNN_SUPER), F32),
                        pltpu.VMEM((rt, RNN_SUPER), F32)],
        compiler_params=_cparams(("arbitrary", "arbitrary")),
        name="rglru",
    )(xr, gr, tail0, h0, cw, cb, ab, xb, lam, wa, wx)


DVA = DV_M + 128


def _mlstm_gates(c, gif_ref, m_s, *, L, n_null):
    hm = N_HEADS_M
    g = gif_ref[...]
    if L < LANES:
        g = jnp.concatenate([g, jnp.zeros((LANES - L, LANES), F32)], axis=0)
    g_t = g.T
    lane = lax.broadcasted_iota(jnp.int32, (hm, LANES), 1)
    valid = lane < L
    li = g_t[0:hm, :]
    lfr = g_t[hm:2 * hm, :]
    lf = jnp.where(valid, jnp.minimum(lfr, 0.0) - jnp.log1p(jnp.exp(-jnp.abs(lfr))), 0.0)
    if n_null:
        null = (lane < n_null) & (c == 0)
        lf = jnp.where(null, 0.0, lf)
        li = jnp.where(null, NEG_BIG, li)
    r128 = lax.broadcasted_iota(jnp.int32, (LANES, LANES), 0)
    c128 = lax.broadcasted_iota(jnp.int32, (LANES, LANES), 1)
    upper = jnp.where(r128 <= c128, 1.0, 0.0)
    b = jnp.dot(lf, upper, preferred_element_type=F32, precision=lax.Precision.HIGHEST)
    a = li - b
    cm = jnp.where(valid, a, -jnp.inf)
    sh = 1
    while sh < L:
        cm = jnp.maximum(cm, jnp.where(lane >= sh, pltpu.roll(cm, sh, axis=1), -jnp.inf))
        sh *= 2
    m0 = m_s[...]
    big_m = jnp.maximum(cm, m0)
    m_last = jnp.max(jnp.where(valid, big_m, -jnp.inf), axis=1, keepdims=True)
    b_last = jnp.sum(lf, axis=1, keepdims=True)
    scale = DQK_M ** -0.5
    e_mv = jnp.exp(-(b + big_m))
    w_int = jnp.exp(m0 - big_m)
    w_end = jnp.exp(a - m_last) * scale
    decay = jnp.exp(m0 - m_last)
    m_s[...] = jnp.broadcast_to(b_last + m_last, (hm, LANES))
    cols = jnp.concatenate([big_m, e_mv, w_int, w_end, jnp.zeros((LANES - 4 * hm, LANES), F32)], axis=0).T
    return a, cols, decay


def _mlstm_heads(gates, q_ref, k_ref, v_ref, og_ref, nw_ref, y_ref, c_s, *, L):
    hm = N_HEADS_M
    scale = DQK_M ** -0.5
    rr = lax.broadcasted_iota(jnp.int32, (L, L), 0)
    cc = lax.broadcasted_iota(jnp.int32, (L, L), 1)
    causal = rr >= cc
    ones_col = jnp.where(lax.broadcasted_iota(jnp.int32, (L, 128), 1) == 0, 1.0, 0.0).astype(BF16)
    nw = nw_ref[...]
    dn_nt = (((1,), (1,)), ((), ()))
    dn_tn = (((0,), (0,)), ((), ()))
    bh = [(bi, h) for bi in range(len(gates)) for h in range(hm)]
    col = lambda bi, k, h: gates[bi][1][0:L, k * hm + h:k * hm + h + 1]
    qs = [q_ref[bi, :, h * DQK_M:(h + 1) * DQK_M] for bi, h in bh]
    ks = [k_ref[bi, :, h * DQK_M:(h + 1) * DQK_M] for bi, h in bh]
    v_augs = [jnp.concatenate([v_ref[bi, :, h * DV_M:(h + 1) * DV_M], ones_col], axis=1) for bi, h in bh]
    c_augs = [c_s[bi, h] for bi, h in bh]
    s_raw = [lax.dot_general(q, k, dn_nt, preferred_element_type=F32) for q, k in zip(qs, ks)]
    inter = [jnp.dot((qs[i].astype(F32) * col(bi, 2, h)).astype(BF16), c_augs[i].astype(BF16),
                     preferred_element_type=F32) for i, (bi, h) in enumerate(bh)]
    s_w = [(s_raw[i] * (scale * jnp.where(causal, jnp.exp(gates[bi][0][h:h + 1, 0:L] - col(bi, 0, h)), 0.0))
            ).astype(BF16) for i, (bi, h) in enumerate(bh)]
    res = [jnp.dot(s_w[i], v_augs[i], preferred_element_type=F32) + inter[i] for i in range(len(bh))]
    upd = [lax.dot_general((ks[i].astype(F32) * col(bi, 3, h)).astype(BF16), v_augs[i], dn_tn,
                           preferred_element_type=F32) for i, (bi, h) in enumerate(bh)]
    for i, (bi, h) in enumerate(bh):
        c_s[bi, h] = gates[bi][2][h:h + 1, 0:1] * c_augs[i] + upd[i]
    for i, (bi, h) in enumerate(bh):
        num = res[i][:, :DV_M]
        denom = jnp.maximum(jnp.abs(res[i][:, DV_M:DV_M + 1]), col(bi, 1, h))
        hh = num / denom
        hh = hh * lax.rsqrt(jnp.mean(hh * hh, axis=-1, keepdims=True) + EPS)
        hh = hh * nw[:, h * DV_M:(h + 1) * DV_M]
        og = og_ref[bi, :, h * DV_M:(h + 1) * DV_M].astype(F32)
        y_ref[bi, :, h * DV_M:(h + 1) * DV_M] = (hh * jax.nn.sigmoid(og)).astype(BF16)


def _mlstm_fused_kernel(q_ref, k_ref, v_ref, og_ref, gif_ref, c0_ref, n0_ref, m0_ref, nw_ref,
                        y_ref, co_ref, no_ref, mo_ref, c_s, m_s, *, L, n_null, nbk):
    c = pl.program_id(1)
    bh = [(bi, h) for bi in range(nbk) for h in range(N_HEADS_M)]

    @pl.when(c == 0)
    def _():
        m_s[...] = m0_ref[...]
        for bi, h in bh:
            c_s[bi, h, :, :DV_M] = c0_ref[bi, h]
            n_tile = jnp.concatenate([n0_ref[bi, h:h + 1, :], jnp.zeros((DQK_M - 1, DQK_M), F32)], axis=0)
            c_s[bi, h, :, DV_M:] = n_tile.T

    gates = [_mlstm_gates(c, gif_ref.at[bi], m_s.at[bi], L=L, n_null=n_null) for bi in range(nbk)]
    _mlstm_heads(gates, q_ref, k_ref, v_ref, og_ref, nw_ref, y_ref, c_s, L=L)

    @pl.when(c == pl.num_programs(1) - 1)
    def _():
        mo_ref[...] = m_s[...]
        for bi, h in bh:
            co_ref[bi, h] = c_s[bi, h, :, :DV_M]
            no_ref[bi, h:h + 1, :] = c_s[bi, h, :, DV_M:].T[0:1, :]


def _mlstm_fused(qkvo, gif, c0, n0, m0, nw, *, L, n_null, nbk):
    nb, tl, _ = qkvo.shape
    assert nb % nbk == 0
    hm = N_HEADS_M
    assert DQK_M == LANES and DVA - DV_M == DQK_M
    m_rep0 = jnp.broadcast_to(m0[..., None], (nb, hm, LANES))
    st4 = lambda b, c: (b, 0, 0, 0)
    st3 = lambda b, c: (b, 0, 0)
    y, c_new, n_new, m_rep = pl.pallas_call(
        functools.partial(_mlstm_fused_kernel, L=L, n_null=n_null, nbk=nbk),
        grid=(nb // nbk, tl // L),
        in_specs=[pl.BlockSpec((nbk, L, HQK), lambda b, c: (b, c, 0)),
                  pl.BlockSpec((nbk, L, HQK), lambda b, c: (b, c, 1)),
                  pl.BlockSpec((nbk, L, HV), lambda b, c: (b, c, 1)),
                  pl.BlockSpec((nbk, L, HV), lambda b, c: (b, c, 2)),
                  pl.BlockSpec((nbk, L, LANES), lambda b, c: (b, c, 0)),
                  pl.BlockSpec((nbk, hm, DQK_M, DV_M), st4),
                  pl.BlockSpec((nbk, hm, DQK_M), st3),
                  pl.BlockSpec((nbk, hm, LANES), st3),
                  pl.BlockSpec((1, HV), lambda b, c: (0, 0))],
        out_specs=[pl.BlockSpec((nbk, L, HV), lambda b, c: (b, c, 0)),
                   pl.BlockSpec((nbk, hm, DQK_M, DV_M), st4),
                   pl.BlockSpec((nbk, hm, DQK_M), st3),
                   pl.BlockSpec((nbk, hm, LANES), st3)],
        out_shape=[jax.ShapeDtypeStruct((nb, tl, HV), BF16),
                   jax.ShapeDtypeStruct((nb, hm, DQK_M, DV_M), F32),
                   jax.ShapeDtypeStruct((nb, hm, DQK_M), F32),
                   jax.ShapeDtypeStruct((nb, hm, LANES), F32)],
        scratch_shapes=[pltpu.VMEM((nbk, hm, DQK_M, DVA), F32),
                        pltpu.VMEM((nbk, hm, LANES), F32)],
        compiler_params=_cparams(("arbitrary", "arbitrary")),
        name="mlstm",
    )(qkvo, qkvo, qkvo, qkvo, gif, c0, n0, m_rep0, nw)
    return y, c_new, n_new, m_rep[..., 0]


def _route(logits, dense):
    tm = logits.shape[0]
    lane = lax.broadcasted_iota(jnp.int32, (tm, LANES), 1)
    lanef = lane.astype(F32)
    gl = jnp.where(lane < N_GROUPS, logits, -jnp.inf)
    gmax = jnp.max(gl, axis=1, keepdims=True)
    gidx = jnp.min(jnp.where(gl == gmax, lanef, float(LANES)), axis=1, keepdims=True)
    gw = 1.0 / jnp.sum(jnp.exp(gl - gmax), axis=1, keepdims=True)
    lo = float(N_GROUPS) + float(EXPERTS_PER_GROUP) * gidx
    sel = (lanef >= lo) & (lanef < lo + float(EXPERTS_PER_GROUP))
    ev = jnp.where(sel, logits, -jnp.inf)
    v1 = jnp.max(ev, axis=1, keepdims=True)
    i1 = jnp.min(jnp.where(ev == v1, lanef, float(LANES)), axis=1, keepdims=True)
    ev2 = jnp.where(lanef == i1, -jnp.inf, ev)
    v2 = jnp.max(ev2, axis=1, keepdims=True)
    i2 = jnp.min(jnp.where(ev2 == v2, lanef, float(LANES)), axis=1, keepdims=True)
    t = jnp.exp(v2 - v1)
    w1 = gw / (1.0 + t)
    w2 = gw * (t / (1.0 + t))
    if dense:
        return jnp.where(lanef == i1, w1, jnp.where(lanef == i2, w2, 0.0))
    l1 = i1 - lo
    l2 = i2 - lo
    first = l1 < l2
    a = jnp.minimum(l1, l2)
    b = jnp.maximum(l1, l2)
    pair = jnp.where(a == 0.0, b - 1.0, jnp.where(a == 1.0, b + 1.0, 5.0))
    cls = float(PAIRS_PER_GROUP) * gidx + pair
    return jnp.where(lane == 0, cls,
                     jnp.where(lane == 1, jnp.where(first, w1, w2),
                               jnp.where(lane == 2, jnp.where(first, w2, w1), 0.0)))


def _outproj_kernel(mix_ref, w_ref, x_ref, meta_ref, n2_ref, wr_ref, br_ref, x1_ref, xn2_ref, gates_ref,
                    *, nb, tt, has_head, routed):
    def emit(x):
        mix = mix_ref[...].reshape(nb * tt, D_MODEL)
        x1 = x + jnp.dot(mix, w_ref[...], preferred_element_type=F32)
        x1_ref[...] = x1.reshape(nb, tt, D_MODEL)
        xn2 = _rms(x1, n2_ref[...]).astype(BF16)
        logits = jnp.dot(xn2, wr_ref[...], preferred_element_type=F32) + br_ref[...]
        info = _route(logits, dense=not routed)
        gates_ref[...] = info.reshape(nb, tt, LANES)
        if routed:
            xn2_ref[:, :, :D_MODEL] = xn2.astype(F32).reshape(nb, tt, D_MODEL)
            xn2_ref[:, :, D_MODEL:] = info.reshape(nb, tt, LANES)
        else:
            xn2_ref[...] = xn2.reshape(nb, tt, D_MODEL)

    if has_head:
        @pl.when(pl.program_id(0) == 0)
        def _():
            emit(_head_rows(meta_ref, nb))

        @pl.when(pl.program_id(0) > 0)
        def _():
            emit(x_ref[...].reshape(nb * tt, D_MODEL))
    else:
        emit(x_ref[...].reshape(nb * tt, D_MODEL))


def _outproj(mix, w_out, x, meta, n2, w_router, b_router, *, tt, has_head, routed):
    nb, tl, _ = mix.shape
    skip = 1 if has_head else 0
    tl = tl - skip * tt
    const = lambda i: (0, 0)
    blk = lambda i: (0, i, 0)
    xn2_w, xn2_dt = (ROW_W, F32) if routed else (D_MODEL, BF16)
    return pl.pallas_call(
        functools.partial(_outproj_kernel, nb=nb, tt=tt, has_head=False, routed=routed),
        grid=(tl // tt,),
        in_specs=[pl.BlockSpec((nb, tt, D_MODEL), lambda i: (0, i + skip, 0)),
                  pl.BlockSpec((D_MODEL, D_MODEL), const, pipeline_mode=pl.Buffered(1)),
                  pl.BlockSpec((nb, tt, D_MODEL), blk),
                  pl.BlockSpec((N_META, D_MODEL), const),
                  pl.BlockSpec((1, D_MODEL), const),
                  pl.BlockSpec((D_MODEL, LANES), const),
                  pl.BlockSpec((1, LANES), const)],
        out_specs=[pl.BlockSpec((nb, tt, D_MODEL), blk),
                   pl.BlockSpec((nb, tt, xn2_w), blk),
                   pl.BlockSpec((nb, tt, LANES), blk)],
        out_shape=[jax.ShapeDtypeStruct((nb, tl, D_MODEL), F32),
                   jax.ShapeDtypeStruct((nb, tl, xn2_w), xn2_dt),
                   jax.ShapeDtypeStruct((nb, tl, LANES), F32)],
        compiler_params=_cparams(("arbitrary",)),
        name="outproj",
    )(mix, w_out, x, meta, n2, w_router, b_router)


def _moe_kernel(xn_ref, gates_ref, x1_ref, wg_ref, wu_ref, wd_ref, nf_ref, y_ref, wgb_ref, wub_ref, wdb_ref,
                acc_s, *, nb, tt):
    e = pl.program_id(0)
    hf = pl.program_id(1)
    rows = nb * tt

    @pl.when((e == 0) & (hf == 0))
    def _():
        acc_s[...] = jnp.zeros_like(acc_s)

    wg = wg_ref[0].astype(BF16)
    wu = wu_ref[0].astype(BF16)
    wd = wd_ref[0].astype(BF16)
    wgb_ref[0] = wg
    wub_ref[0] = wu
    wdb_ref[0] = wd
    x = xn_ref[...].reshape(rows, D_MODEL)
    hg = jnp.dot(x, wg, preferred_element_type=F32)
    hu = jnp.dot(x, wu, preferred_element_type=F32)
    hdn = (hg * jax.nn.sigmoid(hg)) * hu
    yd = jnp.dot(hdn.astype(BF16), wd, preferred_element_type=F32)
    gates = gates_ref[...].reshape(rows, LANES)
    lane = lax.broadcasted_iota(jnp.int32, gates.shape, 1)
    gcol = jnp.sum(jnp.where(lane == e + N_GROUPS, gates, 0.0), axis=1, keepdims=True)
    acc_s[...] += gcol * yd

    @pl.when((e == pl.num_programs(0) - 1) & (hf == pl.num_programs(1) - 1))
    def _():
        x2 = x1_ref[...].reshape(rows, D_MODEL) + acc_s[...]
        y_ref[...] = _rms(x2, nf_ref[...]).reshape(nb, tt, D_MODEL)


def _moe(xn2, gates, x1, wg, wu, wd, nf):
    nb, tt, _ = xn2.shape
    halves = 2
    dh = D_EXPERT // halves
    blk = lambda e, hf: (0, 0, 0)
    up = pl.BlockSpec((1, D_MODEL, dh), lambda e, hf: (e, 0, hf))
    down = pl.BlockSpec((1, dh, D_MODEL), lambda e, hf: (e, hf, 0))
    return pl.pallas_call(
        functools.partial(_moe_kernel, nb=nb, tt=tt),
        grid=(N_EXPERTS, halves),
        in_specs=[pl.BlockSpec((nb, tt, D_MODEL), blk),
                  pl.BlockSpec((nb, tt, LANES), blk),
                  pl.BlockSpec((nb, tt, D_MODEL), blk),
                  up, up, down,
                  pl.BlockSpec((1, D_MODEL), lambda e, hf: (0, 0))],
        out_specs=[pl.BlockSpec((nb, tt, D_MODEL), blk), up, up, down],
        out_shape=[jax.ShapeDtypeStruct((nb, tt, D_MODEL), F32),
                   jax.ShapeDtypeStruct(wg.shape, BF16),
                   jax.ShapeDtypeStruct(wu.shape, BF16),
                   jax.ShapeDtypeStruct(wd.shape, BF16)],
        scratch_shapes=[pltpu.VMEM((nb * tt, D_MODEL), F32)],
        compiler_params=_cparams(("arbitrary", "arbitrary")),
        name="moe",
    )(xn2, gates, x1, wg, wu, wd, nf)


_NT = (((1,), (1,)), ((), ()))


def _plan_kernel(ri_ref, dest_ref, tinfo_ref, cnt_s, carry_s, offs_s, *, rows):
    ph = pl.program_id(0)
    i = pl.program_id(1)
    ri = ri_ref[...].reshape(rows, LANES)
    lanef = lax.broadcasted_iota(jnp.int32, (rows, LANES), 1).astype(F32)
    oh = jnp.where(lanef == ri[:, 0:1], 1.0, 0.0)
    colsum = jnp.sum(oh, axis=0, keepdims=True)
    r128 = lax.broadcasted_iota(jnp.int32, (LANES, LANES), 0)
    c128 = lax.broadcasted_iota(jnp.int32, (LANES, LANES), 1)
    eye = jnp.where(r128 == c128, 1.0, 0.0).astype(BF16)

    @pl.when((ph == 0) & (i == 0))
    def _():
        cnt_s[...] = jnp.zeros_like(cnt_s)

    @pl.when(ph == 0)
    def _():
        cnt_s[...] += colsum

    @pl.when((ph == 1) & (i == 0))
    def _():
        tiles = jnp.floor((cnt_s[...] + float(TM_E - 1)) * (1.0 / TM_E))
        upper = jnp.where(r128 <= c128, 1.0, 0.0).astype(BF16)
        cum_incl = jnp.dot(jnp.broadcast_to(tiles, (8, LANES)).astype(BF16), upper, preferred_element_type=F32)
        offs_s[...] = (cum_incl[0:1, :] - tiles) * float(TM_E)
        carry_s[...] = jnp.zeros_like(carry_s)
        cum_col = lax.dot_general(eye, cum_incl.astype(BF16), _NT, preferred_element_type=F32)
        ended = (cum_col[:, 0:1] <= c128.astype(F32)) & (r128 < N_CLASSES)
        tcls = jnp.sum(jnp.where(ended, 1.0, 0.0), axis=0, keepdims=True)
        total = cum_incl[0:1, LANES - 1:LANES]
        row8 = lax.broadcasted_iota(jnp.int32, (8, LANES), 0)
        tinfo_ref[...] = jnp.where(row8 == 0, tcls, total).astype(jnp.int32)

    @pl.when(ph == 1)
    def _():
        tri = jnp.where(lax.broadcasted_iota(jnp.int32, (rows, rows), 0)
                        > lax.broadcasted_iota(jnp.int32, (rows, rows), 1), 1.0, 0.0).astype(BF16)
        rank = jnp.dot(tri, oh.astype(BF16), preferred_element_type=F32)
        dest = jnp.sum(oh * (rank + (offs_s[...] + carry_s[...])), axis=1, keepdims=True)
        carry_s[...] += colsum
        dhi = jnp.floor(dest * (1.0 / 256.0))
        dlo = dest - 256.0 * dhi
        digits = jnp.where(lanef == 0.0, dlo, jnp.where(lanef == 1.0, dhi, 0.0)).astype(BF16)
        dt = lax.dot_general(eye, digits, _NT, preferred_element_type=F32)
        dest_ref[...] = (dt[0:1, :] + 256.0 * dt[1:2, :]).astype(jnp.int32).reshape(1, 1, rows)


def _plan(rinfo, *, tt, skip):
    nb, tl, _ = rinfo.shape
    rows = nb * tt
    nt = tl // tt - skip
    assert (nt * rows) // TM_E + N_CLASSES <= LANES and rows % LANES == 0
    return pl.pallas_call(
        functools.partial(_plan_kernel, rows=rows),
        grid=(2, nt),
        in_specs=[pl.BlockSpec((nb, tt, LANES), lambda ph, i: (0, i + skip, 0))],
        out_specs=[pl.BlockSpec((1, 1, rows), lambda ph, i: (i * ph, 0, 0)),
                   pl.BlockSpec((8, LANES), lambda ph, i: (0, 0))],
        out_shape=[jax.ShapeDtypeStruct((nt, 1, rows), jnp.int32),
                   jax.ShapeDtypeStruct((8, LANES), jnp.int32)],
        scratch_shapes=[pltpu.VMEM((1, LANES), F32)] * 3,
        compiler_params=_cparams(("arbitrary", "arbitrary")),
        name="moe_plan",
    )(rinfo)


def _dispatch_kernel(tinfo_ref, dest_ref, x_ref, xs_ref, zbuf, zsem, sem, *, nb, tt):
    i = pl.program_id(0)

    @pl.when(i == 0)
    def _():
        zbuf[...] = jnp.zeros_like(zbuf)
        n_used = tinfo_ref[1, 0]

        def fill(start):
            def body(j, carry):
                last = (j >= n_used - 1) | (tinfo_ref[0, jnp.minimum(j + 1, LANES - 1)] != tinfo_ref[0, j])

                @pl.when(last)
                def _():
                    cp = pltpu.make_async_copy(
                        zbuf, xs_ref.at[pl.ds(pl.multiple_of(j * TM_E, TM_E), TM_E), :], zsem)
                    if start:
                        cp.start()
                    else:
                        cp.wait()
                return carry
            lax.fori_loop(0, xs_ref.shape[0] // TM_E, body, 0)

        fill(True)
        fill(False)

    for b in range(nb):
        for t in range(tt):
            d = dest_ref[0, 0, b * tt + t]
            pltpu.make_async_copy(x_ref.at[b, pl.ds(t, 1), :], xs_ref.at[pl.ds(d, 1), :], sem).start()
    whole = xs_ref.at[pl.ds(0, nb * tt), :]
    pltpu.make_async_copy(whole, whole, sem).wait()


def _dispatch(tinfo, dest, xrow, *, tt, skip):
    nb, tl, _ = xrow.shape
    nt = tl // tt - skip
    n_sorted = ((nt * nb * tt) // TM_E + N_CLASSES) * TM_E
    return pl.pallas_call(
        functools.partial(_dispatch_kernel, nb=nb, tt=tt),
        grid_spec=pltpu.PrefetchScalarGridSpec(
            num_scalar_prefetch=1,
            grid=(nt,),
            in_specs=[pl.BlockSpec((1, 1, nb * tt), lambda i, ti: (i, 0, 0), memory_space=pltpu.SMEM),
                      pl.BlockSpec((nb, tt, ROW_W), lambda i, ti: (0, i + skip, 0))],
            out_specs=pl.BlockSpec(memory_space=pl.ANY),
            scratch_shapes=[pltpu.VMEM((TM_E, ROW_W), F32),
                            pltpu.SemaphoreType.DMA(()),
                            pltpu.SemaphoreType.DMA(())]),
        out_shape=jax.ShapeDtypeStruct((n_sorted, ROW_W), F32),
        compiler_params=_cparams(("arbitrary",)),
        name="moe_dispatch",
    )(tinfo, dest, xrow)


def _class_expert(c, hi):
    g = c // PAIRS_PER_GROUP
    p = c - PAIRS_PER_GROUP * g
    ge3 = (p >= 3).astype(jnp.int32)
    ge5 = (p >= 5).astype(jnp.int32)
    local = (p + 1 - 2 * ge3 - ge5) if hi else (ge3 + ge5)
    return EXPERTS_PER_GROUP * g + local


def _experts_kernel(tinfo_ref, xs_ref, wga_ref, wua_ref, wda_ref, wgb_ref, wub_ref, wdb_ref, ys_ref):
    @pl.when(pl.program_id(0) < tinfo_ref[1, 0])
    def _():
        x = xs_ref[:, :D_MODEL].astype(BF16)

        def ffn(wg_ref, wu_ref, wd_ref):
            hg = jnp.dot(x, wg_ref[0], preferred_element_type=F32)
            hu = jnp.dot(x, wu_ref[0], preferred_element_type=F32)
            hdn = (hg * jax.nn.sigmoid(hg)) * hu
            return jnp.dot(hdn.astype(BF16), wd_ref[0], preferred_element_type=F32)

        ys_ref[...] = (xs_ref[:, D_MODEL + 1:D_MODEL + 2] * ffn(wga_ref, wua_ref, wda_ref)
                       + xs_ref[:, D_MODEL + 2:D_MODEL + 3] * ffn(wgb_ref, wub_ref, wdb_ref))

    @pl.when(pl.program_id(0) >= tinfo_ref[1, 0])
    def _():
        ys_ref[...] = jnp.zeros_like(ys_ref)


def _experts(tinfo, xs, wg, wu, wd):
    n_sorted = xs.shape[0]
    nt = n_sorted // TM_E

    def tile(j, ti):
        return jnp.minimum(j, ti[1, 0] - 1)

    def wmap(hi):
        return lambda j, ti: (_class_expert(ti[0, tile(j, ti)], hi), 0, 0)

    up = lambda hi: pl.BlockSpec((1, D_MODEL, D_EXPERT), wmap(hi))
    down = lambda hi: pl.BlockSpec((1, D_EXPERT, D_MODEL), wmap(hi))
    return pl.pallas_call(
        _experts_kernel,
        grid_spec=pltpu.PrefetchScalarGridSpec(
            num_scalar_prefetch=1,
            grid=(nt,),
            in_specs=[pl.BlockSpec((TM_E, ROW_W), lambda j, ti: (tile(j, ti), 0)),
                      up(0), up(0), down(0), up(1), up(1), down(1)],
            out_specs=pl.BlockSpec((TM_E, D_MODEL), lambda j, ti: (j, 0))),
        out_shape=jax.ShapeDtypeStruct((n_sorted, D_MODEL), F32),
        compiler_params=_cparams(("arbitrary",)),
        name="moe_experts",
    )(tinfo, xs, wg, wu, wd, wg, wu, wd)


def _combine_kernel(dcur_ref, dnxt_ref, x1_ref, nf_ref, ys_ref, y_ref, buf, sem, *, nb, tt):
    i = pl.program_id(0)
    slot = i % 2

    def gather(dref, s):
        for b in range(nb):
            for t in range(tt):
                d = dref[0, 0, b * tt + t]
                pltpu.make_async_copy(ys_ref.at[pl.ds(d, 1), :], buf.at[s, b, pl.ds(t, 1), :],
                                      sem.at[s]).start()

    @pl.when(i == 0)
    def _():
        gather(dcur_ref, 0)

    @pl.when(i + 1 < pl.num_programs(0))
    def _():
        gather(dnxt_ref, 1 - slot)

    whole = ys_ref.at[pl.ds(0, nb * tt), :]
    pltpu.make_async_copy(whole, whole, sem.at[slot]).wait()
    x2 = x1_ref[...].reshape(nb * tt, D_MODEL) + buf[slot].reshape(nb * tt, D_MODEL)
    y_ref[...] = _rms(x2, nf_ref[...]).reshape(nb, tt, D_MODEL)


def _combine(dest, x1, nf, ys, *, tt, skip):
    nb, tl, _ = x1.shape
    nt = tl // tt - skip
    dspec = lambda f: pl.BlockSpec((1, 1, nb * tt), f, memory_space=pltpu.SMEM)
    return pl.pallas_call(
        functools.partial(_combine_kernel, nb=nb, tt=tt),
        grid=(nt,),
        in_specs=[dspec(lambda i: (i, 0, 0)),
                  dspec(lambda i: (jnp.minimum(i + 1, nt - 1), 0, 0)),
                  pl.BlockSpec((nb, tt, D_MODEL), lambda i: (0, i + skip, 0)),
                  pl.BlockSpec((1, D_MODEL), lambda i: (0, 0)),
                  pl.BlockSpec(memory_space=pl.ANY)],
        out_specs=pl.BlockSpec((nb, tt, D_MODEL), lambda i: (0, i, 0)),
        out_shape=jax.ShapeDtypeStruct((nb, nt * tt, D_MODEL), F32),
        scratch_shapes=[pltpu.VMEM((2, nb, tt, D_MODEL), F32),
                        pltpu.SemaphoreType.DMA((2,))],
        compiler_params=_cparams(("arbitrary",)),
        name="moe_combine",
    )(dest, dest, x1, nf, ys)


def _block_diag(w):
    per = RNN_SUPER // RNN_BLOCK
    w4 = w.reshape(N_SUPER, per, RNN_BLOCK, RNN_BLOCK).astype(BF16)
    rows = [jnp.pad(w4[:, q], ((0, 0), (0, 0), (q * RNN_BLOCK, (per - 1 - q) * RNN_BLOCK))) for q in range(per)]
    return jnp.concatenate(rows, axis=1)


def _mixers(x, meta, conv0, h0, c0, n0, m0, p, w_ready, *, has_head):
    nb, seq, _ = x.shape
    ready = w_ready is not None
    n_null = N_NULL if has_head else 0
    tt = CHUNK if has_head else seq
    tt_mm = MM_TILES * CHUNK if has_head else seq
    xn, gif = _prep(x, meta, p['norm1_w'], p['w_in'], p['b_gif'], has_head=has_head)
    w_bf16 = {}

    def in_proj(key, w, col0, ncols, tn, out_dtype, gelu=False):
        if ready:
            return _mm(xn, w_ready[key], col0=0, ncols=ncols, tn=tn, tt=tt_mm, out_dtype=out_dtype, gelu=gelu,
                       ready=True, name="in_" + key)
        out, w_bf16[key] = _mm(xn, w, col0=col0, ncols=ncols, tn=tn, tt=tt_mm, out_dtype=out_dtype, gelu=gelu,
                               name="in_" + key)
        return out

    tn_rnn = 1280 if ready else 512
    xr = in_proj('xr', p['w_in'], 0, D_RNN, tn_rnn, F32)
    gr = in_proj('gr', p['w_in'], D_RNN, D_RNN, tn_rnn, BF16, gelu=True)
    qkvo = in_proj('qkvo', p['w_in'], 2 * D_RNN, QKVO, 1536 if ready else 1024, BF16)
    mg = in_proj('mg', p['w_in'], MG_OFF, 2 * D_MODEL, 1024, BF16)

    tail0 = jnp.transpose(conv0, (1, 0, 2)).reshape((CONV_W - 1) * nb, D_RNN)
    y_r, h_new, tail = _rglru(xr, gr, tail0, h0, p['conv_w'], p['conv_b'], p['rg_a_b'], p['rg_x_b'],
                              p['rg_lambda'], p['wa_bd'], p['wx_bd'], tt=tt, n_null=n_null)
    conv_new = jnp.transpose(tail.reshape(CONV_W - 1, nb, D_RNN), (1, 0, 2))

    y_m, c_new, n_new, m_new = _mlstm_fused(qkvo, gif, c0, n0, m0, p['mlstm_norm_w'], L=tt, n_null=n_null,
                                            nbk=MLSTM_NBK)

    if ready:
        mix = _merge(y_r, y_m, w_ready['proj_rnn'], w_ready['proj_mlstm'], mg, tn=1024, tt=tt, ready=True)
    else:
        mix, w_bf16['proj_rnn'], w_bf16['proj_mlstm'] = _merge(y_r, y_m, p['w_proj_rnn'], p['w_proj_mlstm'], mg,
                                                               tn=512, tt=tt)
    states = (conv_new[None], h_new[None], c_new[None], n_new[None], m_new[None])
    return mix, states, (w_ready if ready else w_bf16)


def _ffn(mix, x, meta, p, expert_w, *, has_head):
    tt = CHUNK if has_head else x.shape[1]
    routed = has_head
    x1, xn2, gates = _outproj(mix, p['w_out'], x, meta, p['norm2_w'], p['w_router'], p['b_router'],
                              tt=tt, has_head=has_head, routed=routed)
    if routed:
        skip = 0
        dest, tinfo = _plan(gates, tt=tt, skip=skip)
        xs = _dispatch(tinfo, dest, xn2, tt=tt, skip=skip)
        ys = _experts(tinfo, xs, *expert_w)
        return _combine(dest, x1, p['norm_f_w'], ys, tt=tt, skip=skip), expert_w
    y, *expert_w_bf16 = _moe(xn2, gates, x1, *expert_w, p['norm_f_w'])
    return y, tuple(expert_w_bf16)


def kernel(x_prompt, x_sample, state_rglru_conv, state_rglru_h, state_mlstm_C, state_mlstm_n, state_mlstm_m, meta_tokens, norm1_w, w_in, b_gates, conv_w, conv_b, rg_a_w, rg_a_b, rg_x_w, rg_x_b, rg_lambda, mlstm_norm_w, w_proj_rnn, w_proj_mlstm, w_out, norm2_w, w_router_group, b_router_group, w_router_expert, b_router_expert, w_exp_gate, w_exp_up, w_exp_down, norm_f_w):
    l = 0
    w_in_t = jnp.swapaxes(w_in[l], 0, 1)
    pad_r = LANES - N_GROUPS - N_EXPERTS
    p = dict(
        norm1_w=norm1_w[l][None], w_in=w_in_t,
        b_gif=jnp.pad(b_gates[l], (0, LANES - 2 * N_HEADS_M))[None],
        conv_w=conv_w[l], conv_b=conv_b[l][None], rg_a_b=rg_a_b[l][None], rg_x_b=rg_x_b[l][None],
        rg_lambda=rg_lambda[l][None], wa_bd=_block_diag(rg_a_w[l]), wx_bd=_block_diag(rg_x_w[l]),
        mlstm_norm_w=mlstm_norm_w[l][None],
        w_proj_rnn=w_proj_rnn[l], w_proj_mlstm=w_proj_mlstm[l], w_out=w_out[l].astype(BF16),
        norm2_w=norm2_w[l][None],
        w_router=jnp.pad(jnp.concatenate([w_router_group[l], w_router_expert[l]], axis=1),
                         ((0, 0), (0, pad_r))).astype(BF16),
        b_router=jnp.pad(jnp.concatenate([b_router_group[l], b_router_expert[l]]), (0, pad_r))[None],
        norm_f_w=norm_f_w[None],
    )
    nbp = x_prompt.shape[0]
    dt = x_prompt.dtype
    mix_s, st_s, w_bf16 = _mixers(
        x_sample, meta_tokens, state_rglru_conv[l], state_rglru_h[l], state_mlstm_C[l],
        state_mlstm_n[l], state_mlstm_m[l], p, None, has_head=False)
    y_s, expert_w_bf16 = _ffn(mix_s, x_sample, meta_tokens, p, (w_exp_gate[l], w_exp_up[l], w_exp_down[l]),
                              has_head=False)
    mix_p, st_p, _ = _mixers(
        x_prompt, meta_tokens,
        jnp.zeros((nbp, CONV_W - 1, D_RNN), dt), jnp.zeros((nbp, D_RNN), F32),
        jnp.zeros((nbp, N_HEADS_M, DQK_M, DV_M), F32), jnp.zeros((nbp, N_HEADS_M, DQK_M), F32),
        jnp.zeros((nbp, N_HEADS_M), F32), p, w_bf16, has_head=True)
    y_p, _ = _ffn(mix_p, x_prompt, meta_tokens, p, expert_w_bf16, has_head=True)
    return (y_p, y_s) + st_p + st_s
```
